```python
import math
import jax
import jax.numpy as jnp
from jax import lax
import numpy as np

D_MODEL = 2048
BATCH = 1
SEQ = 8192
DEPTH = 4

GRID_W = 64
CTX_LEN = 256
NORM_EPS = 1e-6
N_MOD = 6

DN_HEADS = 8
DN_DIM = 128
DN_WIDTH = DN_HEADS * DN_DIM
DN_CONV = 5
DN_CHUNK = 64
DN_IN = 4 * DN_WIDTH + 4 * DN_HEADS

HY_WIDTH = D_MODEL - DN_WIDTH
HY_SHORT = 3
HY_BANDS = 16
HY_EMB = 1 + 2 * HY_BANDS
HY_HID = 64
HY_DECAY_TARGET = 1e-2
HY_STRONG_DECAY_PCT = 0.3
HY_WEAK_DECAY_PCT = 1.5
HY_FILTER_OUT_STD = 0.02
HY_IN = 3 * HY_WIDTH
EVEN_IN = DN_IN + HY_IN

NA_HEADS = D_MODEL // 128
NA_DIM = D_MODEL // NA_HEADS
WIN_R = 8
WIN_C = 16

FFN_HIDDEN = -(-8 * D_MODEL // (3 * 256)) * 256

N_EVEN = (DEPTH + 1) // 2
N_ODD = DEPTH // 2

kernel_name = 'hybrid_deltanet_hyena_natten_dit'


def rms_norm(x, w):
    xf = x.astype(jnp.float32)
    y = xf * lax.rsqrt(jnp.mean(xf * xf, axis=-1, keepdims=True) + NORM_EPS)
    return (y * w.astype(jnp.float32)).astype(x.dtype)


def l2_normalize(x):
    xf = x.astype(jnp.float32)
    return xf * lax.rsqrt(jnp.sum(xf * xf, axis=-1, keepdims=True) + NORM_EPS)


def dwconv_centred(x, w, b=None):
    k = w.shape[0]
    y = lax.conv_general_dilated(x, w[:, None, :].astype(x.dtype), (1,), [(k // 2, k // 2)],
                                 dimension_numbers=('NWC', 'WIO', 'NWC'),
                                 feature_group_count=x.shape[-1])
    return y if b is None else y + b.astype(x.dtype)


def swiglu(h, w_gate, w_up, w_down):
    return (jax.nn.silu(h @ w_gate) * (h @ w_up)) @ w_down


def gdn_chunk_scan(q, k, v, g, beta, s0):
    b, h, l, _ = q.shape
    dv = v.shape[-1]
    cs = DN_CHUNK
    n = l // cs
    q, k, v = [t.astype(jnp.float32).reshape(b, h, n, cs, t.shape[-1]) for t in (q, k, v)]
    g = g.astype(jnp.float32).reshape(b, h, n, cs)
    beta = beta.astype(jnp.float32).reshape(b, h, n, cs)
    gc = jnp.cumsum(g, axis=-1)
    lower = jnp.tril(jnp.ones((cs, cs), dtype=bool))
    decay = jnp.exp(jnp.where(lower, gc[..., :, None] - gc[..., None, :], -jnp.inf))
    kb = k * beta[..., None]
    eye = jnp.eye(cs, dtype=jnp.float32)
    m = jnp.einsum('bhncd,bhnsd->bhncs', kb, k) * decay * (1.0 - eye)
    t_inv = lax.linalg.triangular_solve(eye + m, jnp.broadcast_to(eye, m.shape), left_side=True,
                                        lower=True, unit_diagonal=True)
    u = jnp.einsum('bhncs,bhnsd->bhncd', t_inv, v * beta[..., None])
    w = jnp.einsum('bhncs,bhnsd->bhncd', t_inv, kb * jnp.exp(gc)[..., None])
    qk = jnp.einsum('bhncd,bhnsd->bhncs', q, k) * decay
    qg = q * jnp.exp(gc)[..., None]
    kd = k * jnp.exp(gc[..., -1:] - gc)[..., None]
    gl = jnp.exp(gc[..., -1])

    def step(s, inp):
        u_n, w_n, qk_n, qg_n, kd_n, gl_n = inp
        v_new = u_n - jnp.einsum('bhck,bhkv->bhcv', w_n, s)
        o = jnp.einsum('bhck,bhkv->bhcv', qg_n, s) + jnp.einsum('bhcs,bhsv->bhcv', qk_n, v_new)
        s = s * gl_n[..., None, None] + jnp.einsum('bhck,bhcv->bhkv', kd_n, v_new)
        return s, o

    xs = tuple(jnp.moveaxis(t, 2, 0) for t in (u, w, qk, qg, kd, gl))
    s_fin, o = lax.scan(step, s0.astype(jnp.float32), xs)
    return jnp.moveaxis(o, 0, 2).reshape(b, h, l, dv), s_fin


def dn_prepare(z, conv_w, a_log, dt_bias):
    bsz, l, _ = z.shape
    qkv = jax.nn.silu(dwconv_centred(z[..., :3 * DN_WIDTH], conv_w))
    q, k, v = jnp.split(qkv, 3, axis=-1)

    def heads(t):
        return t.reshape(bsz, l, DN_HEADS, DN_DIM).transpose(0, 2, 1, 3)

    q = l2_normalize(heads(q)) * DN_DIM ** -0.5
    k = l2_normalize(heads(k))
    v = heads(v)
    gate = z[..., 3 * DN_WIDTH:4 * DN_WIDTH]
    a = z[..., 4 * DN_WIDTH:4 * DN_WIDTH + 2 * DN_HEADS].reshape(bsz, l, 2, DN_HEADS).astype(jnp.float32)
    bb = z[..., 4 * DN_WIDTH + 2 * DN_HEADS:].reshape(bsz, l, 2, DN_HEADS).astype(jnp.float32)
    g = -jnp.exp(a_log.astype(jnp.float32)) * jax.nn.softplus(a + dt_bias.astype(jnp.float32))
    beta = jax.nn.sigmoid(bb)
    return q, k, v, gate, g.transpose(2, 0, 3, 1), beta.transpose(2, 0, 3, 1)


def dn_output(o, gate, onorm_w):
    bsz, h, l, d = o.shape
    o = rms_norm(o.transpose(0, 2, 1, 3), onorm_w)
    y = o * jax.nn.silu(gate.astype(jnp.float32).reshape(bsz, l, h, d))
    return y.reshape(bsz, l, h * d).astype(gate.dtype)


def gated_deltanet(zc, zl, conv_w, a_log, dt_bias, onorm_w, need_ctx):
    qc, kc, vc, gate_c, gc, bc = dn_prepare(zc, conv_w, a_log, dt_bias)
    ql, kl, vl, gate_l, gl, bl = dn_prepare(zl, conv_w, a_log, dt_bias)
    s0 = jnp.zeros((zl.shape[0], DN_HEADS, DN_DIM, DN_DIM), jnp.float32)
    o_ctx, o_lat = 0.0, 0.0
    for d in range(2):
        f = (lambda t: t) if d == 0 else (lambda t: jnp.flip(t, axis=2))
        oc, s_ctx = gdn_chunk_scan(f(qc), f(kc), f(vc), f(gc[d]), f(bc[d]), s0)
        ol, _ = gdn_chunk_scan(f(ql), f(kl), f(vl), f(gl[d]), f(bl[d]), s_ctx)
        o_ctx = o_ctx + f(oc)
        o_lat = o_lat + f(ol)
    y_lat = dn_output(o_lat, gate_l, onorm_w)
    y_ctx = dn_output(o_ctx, gate_c, onorm_w) if need_ctx else None
    return y_ctx, y_lat


def hyena_filters(l, w1, b1, w2, b2, w3, b3, w4, freq):
    f32 = jnp.float32
    pos = jnp.arange(l, dtype=f32)
    t = (pos / max(l - 1, 1))[:, None]
    wpos = (2.0 * math.pi * pos / l)[:, None]
    bands = jnp.linspace(1e-4, HY_BANDS - 1, HY_BANDS, dtype=f32)
    z = jnp.concatenate([t, jnp.cos(bands * wpos), -jnp.sin(bands * wpos)], axis=-1)
    fr = freq.astype(f32)
    h = jnp.sin(fr * (z @ w1.astype(f32) + b1.astype(f32)))
    h = jnp.sin(fr * (h @ w2.astype(f32) + b2.astype(f32)))
    h = jnp.sin(fr * (h @ w3.astype(f32) + b3.astype(f32)))
    h = h @ w4.astype(f32)
    min_decay = math.log(HY_DECAY_TARGET) / HY_WEAK_DECAY_PCT
    max_decay = math.log(HY_DECAY_TARGET) / HY_STRONG_DECAY_PCT
    deltas = jnp.abs(jnp.linspace(min_decay, max_decay, HY_WIDTH, dtype=f32))
    window = jnp.exp(-t * deltas)
    return h[:, :HY_WIDTH] * window, h[:, HY_WIDTH:] * window


def hyena(z, short_w, short_b, filt, bias):
    bsz, l, _ = z.shape
    z = dwconv_centred(z, short_w, short_b)
    x0, x1, v = jnp.split(z, 3, axis=-1)
    h_f, h_b = hyena_filters(l, *filt)
    filt_circ = jnp.concatenate([h_f[:1] + h_b[:1], h_f[1:], jnp.zeros_like(h_f[:1]), h_b[:0:-1]], axis=0)
    u = (v * x1).astype(jnp.float32)
    y = jnp.fft.irfft(jnp.fft.rfft(u, n=2 * l, axis=1) * jnp.fft.rfft(filt_circ, axis=0)[None],
                      n=2 * l, axis=1)[:, :l]
    y = y + u * bias.astype(jnp.float32)
    return (y * x0.astype(jnp.float32)).astype(z.dtype)


def even_mixer(hc, hl, w_in, w_out, conv_w, a_log, dt_bias, onorm_w, short_w, short_b, filt, hy_b,
               need_ctx):
    zl = hl @ w_in
    zc = hc @ (w_in if need_ctx else w_in[:, :DN_IN])
    dn_c, dn_l = gated_deltanet(zc[..., :DN_IN], zl[..., :DN_IN], conv_w, a_log, dt_bias, onorm_w, need_ctx)
    hy_l = hyena(zl[..., DN_IN:], short_w, short_b, filt, hy_b)
    y_lat = jnp.concatenate([dn_l, hy_l], axis=-1) @ w_out
    y_ctx = None
    if need_ctx:
        hy_c = hyena(zc[..., DN_IN:], short_w, short_b, filt, hy_b)
        y_ctx = jnp.concatenate([dn_c, hy_c], axis=-1) @ w_out
    return y_ctx, y_lat


def na_mixer(hc, hl, w_qkv, rpb, w_out, need_ctx):
    bsz, l, _ = hl.shape
    n_ctx = hc.shape[1]
    rows = l // GRID_W
    kr = min(WIN_R, rows)
    scale = NA_DIM ** -0.5
    zl = (hl @ w_qkv).reshape(bsz, rows, GRID_W, 3, NA_HEADS, NA_DIM)
    ql, kl, vl = zl[:, :, :, 0], zl[:, :, :, 1], zl[:, :, :, 2]
    if need_ctx:
        zc = (hc @ w_qkv).reshape(bsz, n_ctx, 3, NA_HEADS, NA_DIM)
        qc, kc, vc = zc[:, :, 0], zc[:, :, 1], zc[:, :, 2]
    else:
        zc = (hc @ w_qkv[:, D_MODEL:]).reshape(bsz, n_ctx, 2, NA_HEADS, NA_DIM)
        kc, vc = zc[:, :, 0], zc[:, :, 1]
    cols = np.arange(GRID_W)
    col_start = np.clip(cols - WIN_C // 2, 0, GRID_W - WIN_C)
    col_idx = col_start[:, None] + np.arange(WIN_C)[None, :]
    col_off = col_idx - cols[:, None] + (WIN_C - 1)

    def row_block(r):
        rs = jnp.clip(r - kr // 2, 0, rows - kr)
        q = lax.dynamic_index_in_dim(ql, r, axis=1, keepdims=False)
        k_win = lax.dynamic_slice_in_dim(kl, rs, kr, axis=1)[:, :, col_idx]
        v_win = lax.dynamic_slice_in_dim(vl, rs, kr, axis=1)[:, :, col_idx]
        row_off = rs + jnp.arange(kr) - r + (WIN_R - 1)
        bias = jnp.take(jnp.take(rpb, row_off, axis=1), col_off, axis=2)
        s_loc = jnp.einsum('bqhd,brqjhd->bhqrj', q, k_win) * scale + bias.transpose(0, 2, 1, 3)[None]
        s_ctx = jnp.einsum('bqhd,bchd->bhqc', q, kc) * scale
        s = jnp.concatenate([s_loc.reshape(bsz, NA_HEADS, GRID_W, kr * WIN_C), s_ctx], axis=-1)
        p = jax.nn.softmax(s.astype(jnp.float32), axis=-1).astype(vl.dtype)
        p_loc = p[..., :kr * WIN_C].reshape(bsz, NA_HEADS, GRID_W, kr, WIN_C)
        return (jnp.einsum('bhqrj,brqjhd->bqhd', p_loc, v_win)
                + jnp.einsum('bhqc,bchd->bqhd', p[..., kr * WIN_C:], vc))

    o = lax.map(row_block, jnp.arange(rows))
    y_lat = o.transpose(1, 0, 2, 3, 4).reshape(bsz, l, D_MODEL) @ w_out
    y_ctx = None
    if need_ctx:
        s = jnp.einsum('bqhd,bchd->bhqc', qc, kc) * scale
        p = jax.nn.softmax(s.astype(jnp.float32), axis=-1).astype(vc.dtype)
        y_ctx = jnp.einsum('bhqc,bchd->bqhd', p, vc).reshape(bsz, n_ctx, D_MODEL) @ w_out
    return y_ctx, y_lat


def setup_inputs(seed: int = 0) -> dict:
    key = jax.random.key(seed)
    ks = list(jax.random.split(key, 40))

    def nrm(i, shape, s):
        return jax.random.normal(ks[i], shape, jnp.float32) * s

    d, f = D_MODEL, FFN_HIDDEN
    ne, no = N_EVEN, N_ODD
    dt = jnp.exp(jax.random.uniform(ks[13], (ne, 2, DN_HEADS), jnp.float32, math.log(1e-3), math.log(1e-1)))
    return {
        'x': nrm(0, (BATCH, SEQ, d), 1.0),
        'c': nrm(1, (BATCH, d), 1.0),
        'ctx': nrm(2, (BATCH, CTX_LEN, d), 1.0),
        'c_ctx': nrm(3, (d,), 1.0),
        'w_mod': nrm(4, (DEPTH, d, N_MOD * d), 0.5 * d ** -0.5),
        'b_mod': nrm(5, (DEPTH, N_MOD * d), 0.02),
        'norm1_w': 1.0 + nrm(6, (DEPTH, d), 0.05),
        'norm2_w': 1.0 + nrm(7, (DEPTH, d), 0.05),
        'ffn_w_gate': nrm(8, (DEPTH, d, f), d ** -0.5),
        'ffn_w_up': nrm(9, (DEPTH, d, f), d ** -0.5),
        'ffn_w_down': nrm(10, (DEPTH, f, d), f ** -0.5),
        'even_w_in': nrm(11, (ne, d, EVEN_IN), d ** -0.5),
        'even_w_out': nrm(12, (ne, d, d), d ** -0.5),
        'dn_conv_w': nrm(14, (ne, DN_CONV, 3 * DN_WIDTH), DN_CONV ** -0.5),
        'dn_a_log': jnp.log(jax.random.uniform(ks[15], (ne, 2, DN_HEADS), jnp.float32, 1.0, 16.0)),
        'dn_dt_bias': dt + jnp.log(-jnp.expm1(-dt)),
        'dn_onorm_w': 1.0 + nrm(16, (ne, DN_DIM), 0.05),
        'hy_short_w': nrm(17, (ne, HY_SHORT, HY_IN), HY_SHORT ** -0.5),
        'hy_short_b': nrm(18, (ne, HY_IN), 0.02),
        'hy_f_w1': nrm(19, (ne, HY_EMB, HY_HID), HY_EMB ** -0.5),
        'hy_f_b1': nrm(20, (ne, HY_HID), 0.1),
        'hy_f_w2': nrm(21, (ne, HY_HID, HY_HID), HY_HID ** -0.5),
        'hy_f_b2': nrm(22, (ne, HY_HID), 0.1),
        'hy_f_w3': nrm(23, (ne, HY_HID, HY_HID), HY_HID ** -0.5),
        'hy_f_b3': nrm(24, (ne, HY_HID), 0.1),
        'hy_f_w4': nrm(25, (ne, HY_HID, 2 * HY_WIDTH), HY_FILTER_OUT_STD),
        'hy_f_freq': 1.0 + nrm(26, (ne, HY_HID), 0.05),
        'hy_bias': nrm(27, (ne, HY_WIDTH), 0.5),
        'na_w_qkv': nrm(28, (no, d, 3 * d), d ** -0.5),
        'na_rpb': nrm(29, (no, NA_HEADS, 2 * WIN_R - 1, 2 * WIN_C - 1), 0.1),
        'na_w_out': nrm(30, (no, d, d), d ** -0.5),
        'final_norm_w': 1.0 + nrm(31, (d,), 0.05),
    }


def reference(x, c, ctx, c_ctx, w_mod, b_mod, norm1_w, norm2_w, ffn_w_gate, ffn_w_up, ffn_w_down,
              even_w_in, even_w_out, dn_conv_w, dn_a_log, dn_dt_bias, dn_onorm_w, hy_short_w, hy_short_b,
              hy_f_w1, hy_f_b1, hy_f_w2, hy_f_b2, hy_f_w3, hy_f_b3, hy_f_w4, hy_f_freq, hy_bias,
              na_w_qkv, na_rpb, na_w_out, final_norm_w):
    xl, xc = x, ctx
    act_lat = jax.nn.silu(c)
    act_ctx = jax.nn.silu(c_ctx)
    for layer in range(DEPTH):
        need_ctx = layer < DEPTH - 1
        n_cm = N_MOD if need_ctx else 2
        mod_l = (act_lat @ w_mod[layer] + b_mod[layer])[:, None, :]
        sh1, sc1, g1, sh2, sc2, g2 = jnp.split(mod_l, N_MOD, axis=-1)
        mc = jnp.split(act_ctx @ w_mod[layer][:, :n_cm * D_MODEL] + b_mod[layer][:n_cm * D_MODEL], n_cm, axis=-1)
        hl = rms_norm(xl, norm1_w[layer]) * (1.0 + sc1) + sh1
        hc = rms_norm(xc, norm1_w[layer]) * (1.0 + mc[1]) + mc[0]
        i = layer // 2
        if layer % 2 == 0:
            filt = (hy_f_w1[i], hy_f_b1[i], hy_f_w2[i], hy_f_b2[i], hy_f_w3[i], hy_f_b3[i], hy_f_w4[i], hy_f_freq[i])
            yc, yl = even_mixer(hc, hl, even_w_in[i], even_w_out[i], dn_conv_w[i], dn_a_log[i], dn_dt_bias[i],
                                dn_onorm_w[i], hy_short_w[i], hy_short_b[i], filt, hy_bias[i], need_ctx)
        else:
            yc, yl = na_mixer(hc, hl, na_w_qkv[i], na_rpb[i], na_w_out[i], need_ctx)
        xl = xl + g1 * yl
        hl = rms_norm(xl, norm2_w[layer]) * (1.0 + sc2) + sh2
        xl = xl + g2 * swiglu(hl, ffn_w_gate[layer], ffn_w_up[layer], ffn_w_down[layer])
        if need_ctx:
            xc = xc + mc[2] * yc
            hc = rms_norm(xc, norm2_w[layer]) * (1.0 + mc[4]) + mc[3]
            xc = xc + mc[5] * swiglu(hc, ffn_w_gate[layer], ffn_w_up[layer], ffn_w_down[layer])
    return rms_norm(xl, final_norm_w)
```

```python
import functools
import math

import jax
import jax.numpy as jnp
import numpy as np
from jax import lax
from jax.experimental import pallas as pl
from jax.experimental.pallas import tpu as pltpu

D_MODEL = 2048
SEQ = 8192
DEPTH = 4
GRID_W = 64
ROWS = SEQ // GRID_W
CTX_LEN = 256
NORM_EPS = 1e-6
N_MOD = 6

DN_HEADS = 8
DN_DIM = 128
DN_WIDTH = DN_HEADS * DN_DIM
DN_CONV = 5
DN_CHUNK = 64
DN_IN = 4 * DN_WIDTH + 4 * DN_HEADS

HY_WIDTH = D_MODEL - DN_WIDTH
HY_BANDS = 16
HY_DECAY_TARGET = 1e-2
HY_STRONG_DECAY_PCT = 0.3
HY_WEAK_DECAY_PCT = 1.5
HY_IN = 3 * HY_WIDTH
EVEN_IN = DN_IN + HY_IN

NA_HEADS = D_MODEL // 128
NA_DIM = 128
WIN_R = 8
WIN_C = 16
FFN_HIDDEN = -(-8 * D_MODEL // (3 * 256)) * 256

LANES = 128
VMEM_LIMIT = 56 * 1024 * 1024
NEG = -1e30

F32 = jnp.float32
BF16 = jnp.bfloat16


def _cparams(*sem):
    return pltpu.CompilerParams(dimension_semantics=sem, vmem_limit_bytes=VMEM_LIMIT)


def _sigmoid(x):
    return 1.0 / (1.0 + jnp.exp(-x))


MOD_TN = 1024


def _mod_kernel(c_ref, w_ref, b_ref, o_ref):
    rows = []
    for r in range(2):
        c = c_ref[r]
        a = c * _sigmoid(c)
        cols = [jnp.sum(w_ref[0, :, j * LANES:(j + 1) * LANES] * a, axis=0, keepdims=True)
                for j in range(MOD_TN // LANES)]
        rows.append(jnp.concatenate(cols, axis=1))
    o_ref[0] = jnp.concatenate(rows, axis=0) + b_ref[0]


def modulation(c_pair, w_mod, b_mod):
    n = N_MOD * D_MODEL
    cb = jnp.broadcast_to(c_pair[:, :, None], (2, D_MODEL, LANES))
    return pl.pallas_call(
        _mod_kernel,
        grid=(DEPTH, n // MOD_TN),
        in_specs=[pl.BlockSpec((2, D_MODEL, LANES), lambda l, j: (0, 0, 0)),
                  pl.BlockSpec((1, D_MODEL, MOD_TN), lambda l, j: (l, 0, j)),
                  pl.BlockSpec((1, 1, MOD_TN), lambda l, j: (l, 0, j))],
        out_specs=pl.BlockSpec((1, 2, MOD_TN), lambda l, j: (l, 0, j)),
        out_shape=jax.ShapeDtypeStruct((DEPTH, 2, n), F32),
        compiler_params=_cparams("parallel", "parallel"),
        name="modulation",
    )(cb, w_mod, b_mod.reshape(DEPTH, 1, n))


NORM_ROWS = 256


def _norm_mod_to(h_ref, x_ref, nw_ref, sc_ref, sh_ref):
    tm = x_ref.shape[0]
    gain = nw_ref[...]
    scale = 1.0 + sc_ref[...]
    shift = sh_ref[...]

    def body(i, carry):
        rows = pl.ds(pl.multiple_of(i * NORM_ROWS, NORM_ROWS), NORM_ROWS)
        x = x_ref[rows, :]
        y = x * lax.rsqrt(jnp.mean(x * x, axis=-1, keepdims=True) + NORM_EPS)
        h_ref[rows, :] = (y * gain * scale + shift).astype(BF16)
        return carry

    lax.fori_loop(0, tm // NORM_ROWS, body, 0)


def _nmm_kernel(x_ref, nw_ref, sc_ref, sh_ref, w_ref, o_ref, h_ref):
    @pl.when(pl.program_id(1) == 0)
    def _():
        _norm_mod_to(h_ref, x_ref, nw_ref, sc_ref, sh_ref)

    o_ref[...] = jnp.dot(h_ref[...], w_ref[...].astype(BF16),
                         preferred_element_type=F32).astype(o_ref.dtype)


def norm_mod_matmul(x, nw, sc, sh, w, out_dtype, tn=512):
    m, d = x.shape
    n = w.shape[1]
    tm = min(m, 1024)
    vec = pl.BlockSpec((1, d), lambda i, j: (0, 0))
    return pl.pallas_call(
        _nmm_kernel,
        grid=(m // tm, pl.cdiv(n, tn)),
        in_specs=[pl.BlockSpec((tm, d), lambda i, j: (i, 0)), vec, vec, vec,
                  pl.BlockSpec((d, tn), lambda i, j: (0, j))],
        out_specs=pl.BlockSpec((tm, tn), lambda i, j: (i, j)),
        out_shape=jax.ShapeDtypeStruct((m, n), out_dtype),
        scratch_shapes=[pltpu.VMEM((tm, d), BF16)],
        compiler_params=_cparams("parallel", "arbitrary"),
        name="norm_mod_matmul",
    )(x, nw, sc, sh, w)


FFN_TF = 256


def _ffn_kernel(x_ref, nw_ref, sc_ref, sh_ref, g_ref, wg_ref, wu_ref, wd_ref, o_ref, h_ref):
    f = pl.program_id(1)

    @pl.when(f == 0)
    def _():
        _norm_mod_to(h_ref, x_ref, nw_ref, sc_ref, sh_ref)

    h = h_ref[...]
    gate = jnp.dot(h, wg_ref[...].astype(BF16), preferred_element_type=F32)
    up = jnp.dot(h, wu_ref[...].astype(BF16), preferred_element_type=F32)
    a = (gate * _sigmoid(gate) * up).astype(BF16)
    part = jnp.dot(a, wd_ref[...].astype(BF16), preferred_element_type=F32)

    @pl.when(f == 0)
    def _():
        o_ref[...] = part

    @pl.when(f > 0)
    def _():
        o_ref[...] += part

    @pl.when(f == pl.num_programs(1) - 1)
    def _():
        o_ref[...] = x_ref[...] + g_ref[...] * o_ref[...]


def ffn_block(x, nw, sc, sh, g, wg, wu, wd):
    m, d = x.shape
    hid = wg.shape[1]
    tm = min(m, 1024)
    vec = pl.BlockSpec((1, d), lambda i, f: (0, 0))
    return pl.pallas_call(
        _ffn_kernel,
        grid=(m // tm, hid // FFN_TF),
        in_specs=[pl.BlockSpec((tm, d), lambda i, f: (i, 0), pipeline_mode=pl.Buffered(1)), vec, vec, vec, vec,
                  pl.BlockSpec((d, FFN_TF), lambda i, f: (0, f)),
                  pl.BlockSpec((d, FFN_TF), lambda i, f: (0, f)),
                  pl.BlockSpec((FFN_TF, d), lambda i, f: (f, 0))],
        out_specs=pl.BlockSpec((tm, d), lambda i, f: (i, 0)),
        out_shape=jax.ShapeDtypeStruct((m, d), F32),
        scratch_shapes=[pltpu.VMEM((tm, d), BF16)],
        compiler_params=_cparams("parallel", "arbitrary"),
        name="ffn_block",
    )(x, nw, sc, sh, g, wg, wu, wd)


def _proj_res_kernel(y_ref, w_ref, x_ref, g_ref, o_ref):
    o_ref[...] = x_ref[...] + g_ref[...] * jnp.dot(y_ref[...], w_ref[...], preferred_element_type=F32)


def proj_residual(y, w_bf16, x, g):
    m, d = x.shape
    tm = min(m, 512)
    return pl.pallas_call(
        _proj_res_kernel,
        grid=(m // tm,),
        in_specs=[pl.BlockSpec((tm, d), lambda i: (i, 0)),
                  pl.BlockSpec((d, d), lambda i: (0, 0)),
                  pl.BlockSpec((tm, d), lambda i: (i, 0)),
                  pl.BlockSpec((1, d), lambda i: (0, 0))],
        out_specs=pl.BlockSpec((tm, d), lambda i: (i, 0)),
        out_shape=jax.ShapeDtypeStruct((m, d), F32),
        compiler_params=_cparams("parallel"),
        name="proj_residual",
    )(y, w_bf16, x, g)


def _final_norm_kernel(x_ref, w_ref, o_ref):
    x = x_ref[...]
    o_ref[...] = x * lax.rsqrt(jnp.mean(x * x, axis=-1, keepdims=True) + NORM_EPS) * w_ref[...]


def final_norm(x, w):
    m, d = x.shape
    tm = 512
    return pl.pallas_call(
        _final_norm_kernel,
        grid=(m // tm,),
        in_specs=[pl.BlockSpec((tm, d), lambda i: (i, 0)), pl.BlockSpec((1, d), lambda i: (0, 0))],
        out_specs=pl.BlockSpec((tm, d), lambda i: (i, 0)),
        out_shape=jax.ShapeDtypeStruct((m, d), F32),
        compiler_params=_cparams("parallel"),
        name="final_norm",
    )(x, w)


NA_R = 8
NA_KD = (NA_R + WIN_R) // 2
NA_T = 2 * WIN_R


def _na_bias_kernel(rpb_ref, o_ref):
    h = pl.program_id(0)
    c = lax.broadcasted_iota(jnp.int32, (GRID_W, LANES), 0)
    lane = lax.broadcasted_iota(jnp.int32, (GRID_W, LANES), 1)
    left = lane < GRID_W
    kc = jnp.where(left, lane, lane - GRID_W)
    cs = jnp.clip(c - WIN_C // 2, 0, GRID_W - WIN_C)
    inwin = (kc >= cs) & (kc < cs + WIN_C)
    diff = kc - c + (WIN_C - 1)
    n_ro, n_co = 2 * WIN_R - 1, 2 * WIN_C - 1
    for t in range(NA_T):
        def body(d, acc, t=t):
            vl = rpb_ref[(h * n_ro + (t - 1)) * n_co + d] if t >= 1 else jnp.float32(NEG)
            vr = rpb_ref[(h * n_ro + t) * n_co + d] if t < n_ro else jnp.float32(NEG)
            return jnp.where(diff == d, jnp.where(left, vl, vr), acc)

        acc = lax.fori_loop(0, n_co, body, jnp.full((GRID_W, LANES), NEG, F32))
        o_ref[0, t] = jnp.where(inwin, acc, NEG)


def na_bias_tiles(rpb):
    return pl.pallas_call(
        _na_bias_kernel,
        grid=(NA_HEADS,),
        in_specs=[pl.BlockSpec(memory_space=pltpu.SMEM)],
        out_specs=pl.BlockSpec((1, NA_T, GRID_W, LANES), lambda h: (h, 0, 0, 0)),
        out_shape=jax.ShapeDtypeStruct((NA_HEADS, NA_T, GRID_W, LANES), F32),
        compiler_params=_cparams("parallel"),
        name="na_bias_tiles",
    )(rpb.reshape(-1))


def _na_kernel(q_ref, k_ref, v_ref, kc_ref, vc_ref, tt_ref, o_ref):
    scale = NA_DIM ** -0.5
    r0 = pl.program_id(1) * NA_R
    ks = jnp.clip(r0 - WIN_R // 2, 0, ROWS - 2 * NA_KD)
    kstart = pl.multiple_of(ks * GRID_W, LANES)
    kspan = k_ref[pl.ds(kstart, NA_KD * LANES), :]
    vspan = v_ref[pl.ds(kstart, NA_KD * LANES), :]
    q = q_ref[...]
    contract_last = (((1,), (1,)), ((), ()))
    s_loc = lax.dot_general(q, kspan, contract_last, preferred_element_type=F32) * scale
    s_ctx = lax.dot_general(q, kc_ref[...], contract_last, preferred_element_type=F32) * scale

    left = lax.broadcasted_iota(jnp.int32, (GRID_W, LANES), 1) < GRID_W
    bias_rows = []
    for qi in range(NA_R):
        r = r0 + qi
        rs = jnp.clip(r - WIN_R // 2, 0, ROWS - WIN_R)
        tiles = []
        for dj in range(NA_KD):
            kl = ks + 2 * dj
            vl = ((kl >= rs) & (kl < rs + WIN_R)).astype(jnp.int32)
            vr = ((kl + 1 >= rs) & (kl + 1 < rs + WIN_R)).astype(jnp.int32)
            t = jnp.clip(kl - r + WIN_R, 0, NA_T - 1)
            valid = jnp.where(left, vl, vr) > 0
            tiles.append(jnp.where(valid, tt_ref[0, t], NEG))
        bias_rows.append(jnp.concatenate(tiles, axis=1))
    s_loc = s_loc + jnp.concatenate(bias_rows, axis=0)

    m = jnp.maximum(jnp.max(s_loc, axis=-1, keepdims=True), jnp.max(s_ctx, axis=-1, keepdims=True))
    p_loc = jnp.exp(s_loc - m)
    p_ctx = jnp.exp(s_ctx - m)
    denom = jnp.sum(p_loc, axis=-1, keepdims=True) + jnp.sum(p_ctx, axis=-1, keepdims=True)
    o = (jnp.dot(p_loc.astype(BF16), vspan, preferred_element_type=F32)
         + jnp.dot(p_ctx.astype(BF16), vc_ref[...], preferred_element_type=F32))
    o_ref[...] = (o / denom).astype(o_ref.dtype)


def na_attention(zl, zc, tiles):
    tq = NA_R * GRID_W
    nh = NA_HEADS
    return pl.pallas_call(
        _na_kernel,
        grid=(nh, ROWS // NA_R),
        in_specs=[pl.BlockSpec((tq, NA_DIM), lambda h, i: (i, h)),
                  pl.BlockSpec((SEQ, NA_DIM), lambda h, i: (0, nh + h)),
                  pl.BlockSpec((SEQ, NA_DIM), lambda h, i: (0, 2 * nh + h)),
                  pl.BlockSpec((CTX_LEN, NA_DIM), lambda h, i: (0, nh + h)),
                  pl.BlockSpec((CTX_LEN, NA_DIM), lambda h, i: (0, 2 * nh + h)),
                  pl.BlockSpec((1, NA_T, GRID_W, LANES), lambda h, i: (h, 0, 0, 0))],
        out_specs=pl.BlockSpec((tq, NA_DIM), lambda h, i: (i, h)),
        out_shape=jax.ShapeDtypeStruct((SEQ, D_MODEL), BF16),
        compiler_params=_cparams("parallel", "arbitrary"),
        name="na_attention",
    )(zl, zl, zl, zc, zc, tiles)


def _ctx_attn_kernel(q_ref, k_ref, v_ref, o_ref):
    scale = NA_DIM ** -0.5
    s = lax.dot_general(q_ref[...], k_ref[...], (((1,), (1,)), ((), ())), preferred_element_type=F32) * scale
    p = jnp.exp(s - jnp.max(s, axis=-1, keepdims=True))
    o = jnp.dot(p.astype(BF16), v_ref[...], preferred_element_type=F32)
    o_ref[...] = (o / jnp.sum(p, axis=-1, keepdims=True)).astype(o_ref.dtype)


def ctx_attention(zc):
    nh = NA_HEADS
    blk = lambda off: pl.BlockSpec((CTX_LEN, NA_DIM), lambda h: (0, off + h))
    return pl.pallas_call(
        _ctx_attn_kernel,
        grid=(nh,),
        in_specs=[blk(0), blk(nh), blk(2 * nh)],
        out_specs=blk(0),
        out_shape=jax.ShapeDtypeStruct((CTX_LEN, D_MODEL), BF16),
        compiler_params=_cparams("parallel"),
        name="ctx_attention",
    )(zc, zc, zc)


def _dwconv_centred(x, w, b=None):
    k = w.shape[0]
    y = lax.conv_general_dilated(x, w[:, None, :].astype(x.dtype), (1,), [(k // 2, k // 2)],
                                 dimension_numbers=('NWC', 'WIO', 'NWC'),
                                 feature_group_count=x.shape[-1])
    return y if b is None else y + b.astype(x.dtype)


def _l2_normalize(x):
    return x * lax.rsqrt(jnp.sum(x * x, axis=-1, keepdims=True) + NORM_EPS)


def _gdn_chunk_scan(q, k, v, g, beta, s0):
    b, h, l, _ = q.shape
    dv = v.shape[-1]
    cs = DN_CHUNK
    n = l // cs
    q, k, v = [t.reshape(b, h, n, cs, t.shape[-1]) for t in (q, k, v)]
    g = g.reshape(b, h, n, cs)
    beta = beta.reshape(b, h, n, cs)
    gc = jnp.cumsum(g, axis=-1)
    lower = jnp.tril(jnp.ones((cs, cs), dtype=bool))
    decay = jnp.exp(jnp.where(lower, gc[..., :, None] - gc[..., None, :], -jnp.inf))
    kb = k * beta[..., None]
    eye = jnp.eye(cs, dtype=F32)
    m = jnp.einsum('bhncd,bhnsd->bhncs', kb, k) * decay * (1.0 - eye)
    t_inv = lax.linalg.triangular_solve(eye + m, jnp.broadcast_to(eye, m.shape), left_side=True,
                                        lower=True, unit_diagonal=True)
    u = jnp.einsum('bhncs,bhnsd->bhncd', t_inv, v * beta[..., None])
    w = jnp.einsum('bhncs,bhnsd->bhncd', t_inv, kb * jnp.exp(gc)[..., None])
    qk = jnp.einsum('bhncd,bhnsd->bhncs', q, k) * decay
    qg = q * jnp.exp(gc)[..., None]
    kd = k * jnp.exp(gc[..., -1:] - gc)[..., None]
    gl = jnp.exp(gc[..., -1])

    def step(s, inp):
        u_n, w_n, qk_n, qg_n, kd_n, gl_n = inp
        v_new = u_n - jnp.einsum('bhck,bhkv->bhcv', w_n, s)
        o = jnp.einsum('bhck,bhkv->bhcv', qg_n, s) + jnp.einsum('bhcs,bhsv->bhcv', qk_n, v_new)
        s = s * gl_n[..., None, None] + jnp.einsum('bhck,bhcv->bhkv', kd_n, v_new)
        return s, o

    xs = tuple(jnp.moveaxis(t, 2, 0) for t in (u, w, qk, qg, kd, gl))
    s_fin, o = lax.scan(step, s0, xs)
    return jnp.moveaxis(o, 0, 2).reshape(b, h, l, dv), s_fin


def _dn_prepare(z, conv_w, a_log, dt_bias):
    bsz, l, _ = z.shape
    qkv = jax.nn.silu(_dwconv_centred(z[..., :3 * DN_WIDTH], conv_w))
    q, k, v = jnp.split(qkv, 3, axis=-1)

    def heads(t):
        return t.reshape(bsz, l, DN_HEADS, DN_DIM).transpose(0, 2, 1, 3)

    q = _l2_normalize(heads(q)) * DN_DIM ** -0.5
    k = _l2_normalize(heads(k))
    v = heads(v)
    gate = z[..., 3 * DN_WIDTH:4 * DN_WIDTH]
    a = z[..., 4 * DN_WIDTH:4 * DN_WIDTH + 2 * DN_HEADS].reshape(bsz, l, 2, DN_HEADS)
    bb = z[..., 4 * DN_WIDTH + 2 * DN_HEADS:].reshape(bsz, l, 2, DN_HEADS)
    g = -jnp.exp(a_log) * jax.nn.softplus(a + dt_bias)
    beta = jax.nn.sigmoid(bb)
    return q, k, v, gate, g.transpose(2, 0, 3, 1), beta.transpose(2, 0, 3, 1)


def _dn_output(o, gate, onorm_w):
    bsz, h, l, d = o.shape
    o = o.transpose(0, 2, 1, 3)
    o = o * lax.rsqrt(jnp.mean(o * o, axis=-1, keepdims=True) + NORM_EPS) * onorm_w
    y = o * jax.nn.silu(gate.reshape(bsz, l, h, d))
    return y.reshape(bsz, l, h * d)


def _gated_deltanet(zc, zl, conv_w, a_log, dt_bias, onorm_w):
    qc, kc, vc, gate_c, gc, bc = _dn_prepare(zc, conv_w, a_log, dt_bias)
    ql, kl, vl, gate_l, gl, bl = _dn_prepare(zl, conv_w, a_log, dt_bias)
    s0 = jnp.zeros((zl.shape[0], DN_HEADS, DN_DIM, DN_DIM), F32)
    o_ctx, o_lat = 0.0, 0.0
    for d in range(2):
        f = (lambda t: t) if d == 0 else (lambda t: jnp.flip(t, axis=2))
        oc, s_ctx = _gdn_chunk_scan(f(qc), f(kc), f(vc), f(gc[d]), f(bc[d]), s0)
        ol, _ = _gdn_chunk_scan(f(ql), f(kl), f(vl), f(gl[d]), f(bl[d]), s_ctx)
        o_ctx = o_ctx + f(oc)
        o_lat = o_lat + f(ol)
    return _dn_output(o_ctx, gate_c, onorm_w), _dn_output(o_lat, gate_l, onorm_w)


def _hyena_filters(l, w1, b1, w2, b2, w3, b3, w4, freq):
    pos = jnp.arange(l, dtype=F32)
    t = (pos / max(l - 1, 1))[:, None]
    wpos = (2.0 * math.pi * pos / l)[:, None]
    bands = jnp.linspace(1e-4, HY_BANDS - 1, HY_BANDS, dtype=F32)
    z = jnp.concatenate([t, jnp.cos(bands * wpos), -jnp.sin(bands * wpos)], axis=-1)
    h = jnp.sin(freq * (z @ w1 + b1))
    h = jnp.sin(freq * (h @ w2 + b2))
    h = jnp.sin(freq * (h @ w3 + b3))
    h = h @ w4
    min_decay = math.log(HY_DECAY_TARGET) / HY_WEAK_DECAY_PCT
    max_decay = math.log(HY_DECAY_TARGET) / HY_STRONG_DECAY_PCT
    deltas = jnp.abs(jnp.linspace(min_decay, max_decay, HY_WIDTH, dtype=F32))
    window = jnp.exp(-t * deltas)
    return h[:, :HY_WIDTH] * window, h[:, HY_WIDTH:] * window


def _hyena(z, short_w, short_b, filt, bias):
    bsz, l, _ = z.shape
    z = _dwconv_centred(z, short_w, short_b)
    x0, x1, v = jnp.split(z, 3, axis=-1)
    h_f, h_b = _hyena_filters(l, *filt)
    filt_circ = jnp.concatenate([h_f[:1] + h_b[:1], h_f[1:], jnp.zeros_like(h_f[:1]), h_b[:0:-1]], axis=0)
    u = v * x1
    y = jnp.fft.irfft(jnp.fft.rfft(u, n=2 * l, axis=1) * jnp.fft.rfft(filt_circ, axis=0)[None],
                      n=2 * l, axis=1)[:, :l]
    y = y + u * bias
    return y * x0


def kernel(x, c, ctx, c_ctx, w_mod, b_mod, norm1_w, norm2_w, ffn_w_gate, ffn_w_up, ffn_w_down, even_w_in, even_w_out, dn_conv_w, dn_a_log, dn_dt_bias, dn_onorm_w, hy_short_w, hy_short_b, hy_f_w1, hy_f_b1, hy_f_w2, hy_f_b2, hy_f_w3, hy_f_b3, hy_f_w4, hy_f_freq, hy_bias, na_w_qkv, na_rpb, na_w_out, final_norm_w):
    d = D_MODEL
    xl, xc = x[0], ctx[0]
    mod = modulation(jnp.concatenate([c, c_ctx[None]], axis=0), w_mod, b_mod)

    def chunk(layer, row, idx):
        return mod[layer, row:row + 1, idx * d:(idx + 1) * d]

    for layer in range(DEPTH):
        need_ctx = layer < DEPTH - 1
        ml = [chunk(layer, 0, j) for j in range(N_MOD)]
        mc = [chunk(layer, 1, j) for j in range(N_MOD)]
        n1 = norm1_w[layer][None]
        n2 = norm2_w[layer][None]
        i = layer // 2
        if layer % 2 == 0:
            w_in = even_w_in[i]
            zl = norm_mod_matmul(xl, n1, ml[1], ml[0], w_in, F32)
            zc = norm_mod_matmul(xc, n1, mc[1], mc[0], w_in, F32)
            dn_c, dn_l = _gated_deltanet(zc[None, :, :DN_IN], zl[None, :, :DN_IN], dn_conv_w[i], dn_a_log[i],
                                         dn_dt_bias[i], dn_onorm_w[i])
            filt = (hy_f_w1[i], hy_f_b1[i], hy_f_w2[i], hy_f_b2[i], hy_f_w3[i], hy_f_b3[i], hy_f_w4[i],
                    hy_f_freq[i])
            hy_l = _hyena(zl[None, :, DN_IN:], hy_short_w[i], hy_short_b[i], filt, hy_bias[i])
            yl = jnp.concatenate([dn_l[0], hy_l[0]], axis=-1).astype(BF16)
            w_out = even_w_out[i].astype(BF16)
            if need_ctx:
                hy_c = _hyena(zc[None, :, DN_IN:], hy_short_w[i], hy_short_b[i], filt, hy_bias[i])
                yc = jnp.concatenate([dn_c[0], hy_c[0]], axis=-1).astype(BF16)
        else:
            w_qkv = na_w_qkv[i]
            zl = norm_mod_matmul(xl, n1, ml[1], ml[0], w_qkv, BF16)
            zc = norm_mod_matmul(xc, n1, mc[1], mc[0], w_qkv, BF16)
            yl = na_attention(zl, zc, na_bias_tiles(na_rpb[i]))
            w_out = na_w_out[i].astype(BF16)
            if need_ctx:
                yc = ctx_attention(zc)
        xl = proj_residual(yl, w_out, xl, ml[2])
        xl = ffn_block(xl, n2, ml[4], ml[3], ml[5], ffn_w_gate[layer], ffn_w_up[layer], ffn_w_down[layer])
        if need_ctx:
            xc = proj_residual(yc, w_out, xc, mc[2])
            xc = ffn_block(xc, n2, mc[4], mc[3], mc[5], ffn_w_gate[layer], ffn_w_up[layer], ffn_w_down[layer])
    return final_norm(xl, final_norm_w[None])[None]
```

```python
import functools
import math

import jax
import jax.numpy as jnp
import numpy as np
from jax import lax
from jax.experimental import pallas as pl
from jax.experimental.pallas import tpu as pltpu

D_MODEL = 2048
SEQ = 8192
DEPTH = 4
GRID_W = 64
ROWS = SEQ // GRID_W
CTX_LEN = 256
NORM_EPS = 1e-6
N_MOD = 6

DN_HEADS = 8
DN_DIM = 128
DN_WIDTH = DN_HEADS * DN_DIM
DN_CONV = 5
DN_CHUNK = 64
DN_IN = 4 * DN_WIDTH + 4 * DN_HEADS

HY_WIDTH = D_MODEL - DN_WIDTH
HY_BANDS = 16
HY_DECAY_TARGET = 1e-2
HY_STRONG_DECAY_PCT = 0.3
HY_WEAK_DECAY_PCT = 1.5
HY_IN = 3 * HY_WIDTH
EVEN_IN = DN_IN + HY_IN

NA_HEADS = D_MODEL // 128
NA_DIM = 128
WIN_R = 8
WIN_C = 16
FFN_HIDDEN = -(-8 * D_MODEL // (3 * 256)) * 256

LANES = 128
VMEM_LIMIT = 56 * 1024 * 1024
NEG = -1e30

F32 = jnp.float32
BF16 = jnp.bfloat16


def _cparams(*sem):
    return pltpu.CompilerParams(dimension_semantics=sem, vmem_limit_bytes=VMEM_LIMIT)


def _sigmoid(x):
    return 1.0 / (1.0 + jnp.exp(-x))


MOD_TN = 1024


def _mod_kernel(c_ref, w_ref, b_ref, o_ref):
    rows = []
    for r in range(2):
        c = c_ref[r]
        a = c * _sigmoid(c)
        cols = [jnp.sum(w_ref[0, :, j * LANES:(j + 1) * LANES] * a, axis=0, keepdims=True)
                for j in range(MOD_TN // LANES)]
        rows.append(jnp.concatenate(cols, axis=1))
    o_ref[0] = jnp.concatenate(rows, axis=0) + b_ref[0]


def modulation(c_pair, w_mod, b_mod):
    n = N_MOD * D_MODEL
    cb = jnp.broadcast_to(c_pair[:, :, None], (2, D_MODEL, LANES))
    return pl.pallas_call(
        _mod_kernel,
        grid=(DEPTH, n // MOD_TN),
        in_specs=[pl.BlockSpec((2, D_MODEL, LANES), lambda l, j: (0, 0, 0)),
                  pl.BlockSpec((1, D_MODEL, MOD_TN), lambda l, j: (l, 0, j)),
                  pl.BlockSpec((1, 1, MOD_TN), lambda l, j: (l, 0, j))],
        out_specs=pl.BlockSpec((1, 2, MOD_TN), lambda l, j: (l, 0, j)),
        out_shape=jax.ShapeDtypeStruct((DEPTH, 2, n), F32),
        compiler_params=_cparams("parallel", "parallel"),
        name="modulation",
    )(cb, w_mod, b_mod.reshape(DEPTH, 1, n))


NORM_ROWS = 256


def _norm_mod_to(h_ref, x_ref, nw_ref, sc_ref, sh_ref):
    tm = x_ref.shape[0]
    gain = nw_ref[...]
    scale = 1.0 + sc_ref[...]
    shift = sh_ref[...]

    def body(i, carry):
        rows = pl.ds(pl.multiple_of(i * NORM_ROWS, NORM_ROWS), NORM_ROWS)
        x = x_ref[rows, :]
        y = x * lax.rsqrt(jnp.mean(x * x, axis=-1, keepdims=True) + NORM_EPS)
        h_ref[rows, :] = (y * gain * scale + shift).astype(BF16)
        return carry

    lax.fori_loop(0, tm // NORM_ROWS, body, 0)


def _nmm_kernel(x_ref, nw_ref, sc_ref, sh_ref, w_ref, o_ref, h_ref):
    @pl.when(pl.program_id(1) == 0)
    def _():
        _norm_mod_to(h_ref, x_ref, nw_ref, sc_ref, sh_ref)

    o_ref[...] = jnp.dot(h_ref[...], w_ref[...].astype(BF16),
                         preferred_element_type=F32).astype(o_ref.dtype)


def norm_mod_matmul(x, nw, sc, sh, w, out_dtype, tn=512):
    m, d = x.shape
    n = w.shape[1]
    tm = min(m, 1024)
    vec = pl.BlockSpec((1, d), lambda i, j: (0, 0))
    return pl.pallas_call(
        _nmm_kernel,
        grid=(m // tm, pl.cdiv(n, tn)),
        in_specs=[pl.BlockSpec((tm, d), lambda i, j: (i, 0)), vec, vec, vec,
                  pl.BlockSpec((d, tn), lambda i, j: (0, j))],
        out_specs=pl.BlockSpec((tm, tn), lambda i, j: (i, j)),
        out_shape=jax.ShapeDtypeStruct((m, n), out_dtype),
        scratch_shapes=[pltpu.VMEM((tm, d), BF16)],
        compiler_params=_cparams("parallel", "arbitrary"),
        name="norm_mod_matmul",
    )(x, nw, sc, sh, w)


FFN_TF = 256


def _ffn_kernel(x_ref, nw_ref, sc_ref, sh_ref, g_ref, wg_ref, wu_ref, wd_ref, o_ref, h_ref):
    f = pl.program_id(1)

    @pl.when(f == 0)
    def _():
        _norm_mod_to(h_ref, x_ref, nw_ref, sc_ref, sh_ref)

    h = h_ref[...]
    gate = jnp.dot(h, wg_ref[...].astype(BF16), preferred_element_type=F32)
    up = jnp.dot(h, wu_ref[...].astype(BF16), preferred_element_type=F32)
    a = (gate * _sigmoid(gate) * up).astype(BF16)
    part = jnp.dot(a, wd_ref[...].astype(BF16), preferred_element_type=F32)

    @pl.when(f == 0)
    def _():
        o_ref[...] = part

    @pl.when(f > 0)
    def _():
        o_ref[...] += part

    @pl.when(f == pl.num_programs(1) - 1)
    def _():
        o_ref[...] = x_ref[...] + g_ref[...] * o_ref[...]


def ffn_block(x, nw, sc, sh, g, wg, wu, wd):
    m, d = x.shape
    hid = wg.shape[1]
    tm = min(m, 1024)
    vec = pl.BlockSpec((1, d), lambda i, f: (0, 0))
    return pl.pallas_call(
        _ffn_kernel,
        grid=(m // tm, hid // FFN_TF),
        in_specs=[pl.BlockSpec((tm, d), lambda i, f: (i, 0), pipeline_mode=pl.Buffered(1)), vec, vec, vec, vec,
                  pl.BlockSpec((d, FFN_TF), lambda i, f: (0, f)),
                  pl.BlockSpec((d, FFN_TF), lambda i, f: (0, f)),
                  pl.BlockSpec((FFN_TF, d), lambda i, f: (f, 0))],
        out_specs=pl.BlockSpec((tm, d), lambda i, f: (i, 0)),
        out_shape=jax.ShapeDtypeStruct((m, d), F32),
        scratch_shapes=[pltpu.VMEM((tm, d), BF16)],
        compiler_params=_cparams("parallel", "arbitrary"),
        name="ffn_block",
    )(x, nw, sc, sh, g, wg, wu, wd)


def _proj_res_kernel(ya_ref, yb_ref, w_ref, x_ref, g_ref, o_ref):
    ka = ya_ref.shape[1]
    y = (jnp.dot(ya_ref[...], w_ref[:ka, :], preferred_element_type=F32)
         + jnp.dot(yb_ref[...], w_ref[ka:, :], preferred_element_type=F32))
    o_ref[...] = x_ref[...] + g_ref[...] * y


def proj_residual(ya, yb, w_bf16, x, g):
    m, d = x.shape
    ka = kb = d // 2
    second = 1 if yb is ya else 0
    tm = min(m, 512)
    return pl.pallas_call(
        _proj_res_kernel,
        grid=(m // tm,),
        in_specs=[pl.BlockSpec((tm, ka), lambda i: (i, 0)),
                  pl.BlockSpec((tm, kb), lambda i: (i, second)),
                  pl.BlockSpec((d, d), lambda i: (0, 0)),
                  pl.BlockSpec((tm, d), lambda i: (i, 0)),
                  pl.BlockSpec((1, d), lambda i: (0, 0))],
        out_specs=pl.BlockSpec((tm, d), lambda i: (i, 0)),
        out_shape=jax.ShapeDtypeStruct((m, d), F32),
        compiler_params=_cparams("parallel"),
        name="proj_residual",
    )(ya, yb, w_bf16, x, g)


def _final_norm_kernel(x_ref, w_ref, o_ref):
    x = x_ref[...]
    o_ref[...] = x * lax.rsqrt(jnp.mean(x * x, axis=-1, keepdims=True) + NORM_EPS) * w_ref[...]


def final_norm(x, w):
    m, d = x.shape
    tm = 512
    return pl.pallas_call(
        _final_norm_kernel,
        grid=(m // tm,),
        in_specs=[pl.BlockSpec((tm, d), lambda i: (i, 0)), pl.BlockSpec((1, d), lambda i: (0, 0))],
        out_specs=pl.BlockSpec((tm, d), lambda i: (i, 0)),
        out_shape=jax.ShapeDtypeStruct((m, d), F32),
        compiler_params=_cparams("parallel"),
        name="final_norm",
    )(x, w)


NA_R = 8
NA_KD = (NA_R + WIN_R) // 2
NA_T = 2 * WIN_R


def _na_bias_kernel(rpb_ref, o_ref):
    h = pl.program_id(0)
    c = lax.broadcasted_iota(jnp.int32, (GRID_W, LANES), 0)
    lane = lax.broadcasted_iota(jnp.int32, (GRID_W, LANES), 1)
    left = lane < GRID_W
    kc = jnp.where(left, lane, lane - GRID_W)
    cs = jnp.clip(c - WIN_C // 2, 0, GRID_W - WIN_C)
    inwin = (kc >= cs) & (kc < cs + WIN_C)
    diff = kc - c + (WIN_C - 1)
    n_ro, n_co = 2 * WIN_R - 1, 2 * WIN_C - 1
    for t in range(NA_T):
        def body(d, acc, t=t):
            vl = rpb_ref[(h * n_ro + (t - 1)) * n_co + d] if t >= 1 else jnp.float32(NEG)
            vr = rpb_ref[(h * n_ro + t) * n_co + d] if t < n_ro else jnp.float32(NEG)
            return jnp.where(diff == d, jnp.where(left, vl, vr), acc)

        acc = lax.fori_loop(0, n_co, body, jnp.full((GRID_W, LANES), NEG, F32))
        o_ref[0, t] = jnp.where(inwin, acc, NEG)


def na_bias_tiles(rpb):
    return pl.pallas_call(
        _na_bias_kernel,
        grid=(NA_HEADS,),
        in_specs=[pl.BlockSpec(memory_space=pltpu.SMEM)],
        out_specs=pl.BlockSpec((1, NA_T, GRID_W, LANES), lambda h: (h, 0, 0, 0)),
        out_shape=jax.ShapeDtypeStruct((NA_HEADS, NA_T, GRID_W, LANES), F32),
        compiler_params=_cparams("parallel"),
        name="na_bias_tiles",
    )(rpb.reshape(-1))


def _na_kernel(q_ref, k_ref, v_ref, kc_ref, vc_ref, tt_ref, o_ref):
    scale = NA_DIM ** -0.5
    r0 = pl.program_id(1) * NA_R
    ks = jnp.clip(r0 - WIN_R // 2, 0, ROWS - 2 * NA_KD)
    kstart = pl.multiple_of(ks * GRID_W, LANES)
    kspan = k_ref[pl.ds(kstart, NA_KD * LANES), :]
    vspan = v_ref[pl.ds(kstart, NA_KD * LANES), :]
    q = q_ref[...]
    contract_last = (((1,), (1,)), ((), ()))
    s_loc = lax.dot_general(q, kspan, contract_last, preferred_element_type=F32) * scale
    s_ctx = lax.dot_general(q, kc_ref[...], contract_last, preferred_element_type=F32) * scale

    left = lax.broadcasted_iota(jnp.int32, (GRID_W, LANES), 1) < GRID_W
    bias_rows = []
    for qi in range(NA_R):
        r = r0 + qi
        rs = jnp.clip(r - WIN_R // 2, 0, ROWS - WIN_R)
        tiles = []
        for dj in range(NA_KD):
            kl = ks + 2 * dj
            vl = ((kl >= rs) & (kl < rs + WIN_R)).astype(jnp.int32)
            vr = ((kl + 1 >= rs) & (kl + 1 < rs + WIN_R)).astype(jnp.int32)
            t = jnp.clip(kl - r + WIN_R, 0, NA_T - 1)
            valid = jnp.where(left, vl, vr) > 0
            tiles.append(jnp.where(valid, tt_ref[0, t], NEG))
        bias_rows.append(jnp.concatenate(tiles, axis=1))
    s_loc = s_loc + jnp.concatenate(bias_rows, axis=0)

    m = jnp.maximum(jnp.max(s_loc, axis=-1, keepdims=True), jnp.max(s_ctx, axis=-1, keepdims=True))
    p_loc = jnp.exp(s_loc - m)
    p_ctx = jnp.exp(s_ctx - m)
    denom = jnp.sum(p_loc, axis=-1, keepdims=True) + jnp.sum(p_ctx, axis=-1, keepdims=True)
    o = (jnp.dot(p_loc.astype(BF16), vspan, preferred_element_type=F32)
         + jnp.dot(p_ctx.astype(BF16), vc_ref[...], preferred_element_type=F32))
    o_ref[...] = (o / denom).astype(o_ref.dtype)


def na_attention(zl, zc, tiles):
    tq = NA_R * GRID_W
    nh = NA_HEADS
    return pl.pallas_call(
        _na_kernel,
        grid=(nh, ROWS // NA_R),
        in_specs=[pl.BlockSpec((tq, NA_DIM), lambda h, i: (i, h)),
                  pl.BlockSpec((SEQ, NA_DIM), lambda h, i: (0, nh + h)),
                  pl.BlockSpec((SEQ, NA_DIM), lambda h, i: (0, 2 * nh + h)),
                  pl.BlockSpec((CTX_LEN, NA_DIM), lambda h, i: (0, nh + h)),
                  pl.BlockSpec((CTX_LEN, NA_DIM), lambda h, i: (0, 2 * nh + h)),
                  pl.BlockSpec((1, NA_T, GRID_W, LANES), lambda h, i: (h, 0, 0, 0))],
        out_specs=pl.BlockSpec((tq, NA_DIM), lambda h, i: (i, h)),
        out_shape=jax.ShapeDtypeStruct((SEQ, D_MODEL), BF16),
        compiler_params=_cparams("parallel", "arbitrary"),
        name="na_attention",
    )(zl, zl, zl, zc, zc, tiles)


def _ctx_attn_kernel(q_ref, k_ref, v_ref, o_ref):
    scale = NA_DIM ** -0.5
    s = lax.dot_general(q_ref[...], k_ref[...], (((1,), (1,)), ((), ())), preferred_element_type=F32) * scale
    p = jnp.exp(s - jnp.max(s, axis=-1, keepdims=True))
    o = jnp.dot(p.astype(BF16), v_ref[...], preferred_element_type=F32)
    o_ref[...] = (o / jnp.sum(p, axis=-1, keepdims=True)).astype(o_ref.dtype)


def ctx_attention(zc):
    nh = NA_HEADS
    blk = lambda off: pl.BlockSpec((CTX_LEN, NA_DIM), lambda h: (0, off + h))
    return pl.pallas_call(
        _ctx_attn_kernel,
        grid=(nh,),
        in_specs=[blk(0), blk(nh), blk(2 * nh)],
        out_specs=blk(0),
        out_shape=jax.ShapeDtypeStruct((CTX_LEN, D_MODEL), BF16),
        compiler_params=_cparams("parallel"),
        name="ctx_attention",
    )(zc, zc, zc)


DN_SC = 256
HALO = 8
N_GB = 2 * DN_HEADS


def _softplus(x):
    return jnp.maximum(x, 0.0) + jnp.log1p(jnp.exp(-jnp.abs(x)))


def _chunk_masks(n):
    r = lax.broadcasted_iota(jnp.int32, (n, n), 0)
    c = lax.broadcasted_iota(jnp.int32, (n, n), 1)
    return r, c, (r // DN_CHUNK) == (c // DN_CHUNK)


def _dn_scalars_kernel(za_ref, zb_ref, zat_ref, alog_ref, dtb_ref, alogt_ref, dtbt_ref,
                       gc_ref, gt_ref, beta_ref, gct_ref):
    hi = lax.Precision.HIGHEST
    g = -jnp.exp(alog_ref[...]) * _softplus(za_ref[...] + dtb_ref[...])
    beta_ref[...] = _sigmoid(zb_ref[...])
    r, c, same = _chunk_masks(DN_SC)
    lo = (same & (c <= r)).astype(F32)
    up = (same & (c >= r)).astype(F32)
    fwd_col = lax.broadcasted_iota(jnp.int32, (DN_SC, N_GB), 1) < DN_HEADS
    gc_ref[...] = jnp.where(fwd_col, jnp.dot(lo, g, precision=hi), jnp.dot(up, g, precision=hi))
    gt_ref[...] = jnp.dot(same.astype(F32), g, precision=hi)
    g_t = -jnp.exp(alogt_ref[...]) * _softplus(zat_ref[...] + dtbt_ref[...])
    fwd_row = lax.broadcasted_iota(jnp.int32, (N_GB, DN_SC), 0) < DN_HEADS
    gct_ref[...] = jnp.where(fwd_row, jnp.dot(g_t, up, precision=hi), jnp.dot(g_t, lo, precision=hi))


def dn_scalars(za, zb, a_log, dt_bias):
    l = za.shape[0]
    col = pl.BlockSpec((DN_SC, N_GB), lambda i: (i, 0))
    row = pl.BlockSpec((N_GB, DN_SC), lambda i: (0, i))
    prow = pl.BlockSpec((1, N_GB), lambda i: (0, 0))
    pcol = pl.BlockSpec((N_GB, 1), lambda i: (0, 0))
    cshape = jax.ShapeDtypeStruct((l, N_GB), F32)
    return pl.pallas_call(
        _dn_scalars_kernel,
        grid=(l // DN_SC,),
        in_specs=[col, col, row, prow, prow, pcol, pcol],
        out_specs=[col, col, col, row],
        out_shape=[cshape, cshape, cshape, jax.ShapeDtypeStruct((N_GB, l), F32)],
        compiler_params=_cparams("parallel"),
        name="dn_scalars",
    )(za, zb, za.T, a_log.reshape(1, N_GB), dt_bias.reshape(1, N_GB), a_log.reshape(N_GB, 1),
      dt_bias.reshape(N_GB, 1))


def _gdn_kernel(zq_ref, zqp_ref, zqn_ref, zk_ref, zkp_ref, zkn_ref, zv_ref, zvp_ref, zvn_ref, cw_ref,
                gc_ref, gt_ref, beta_ref, gct_ref, s0_ref, o_ref, sfin_ref, state_ref, *, bwd, nblk):
    i = pl.program_id(0)
    blk = nblk - 1 - i if bwd else i

    @pl.when(i == 0)
    def _():
        state_ref[...] = s0_ref[...]

    def conv_silu(main_ref, prev_ref, next_ref, part):
        prev = jnp.where(blk > 0, prev_ref[...], 0.0)
        nxt = jnp.where(blk < nblk - 1, next_ref[...], 0.0)
        xf = jnp.concatenate([prev, main_ref[...], nxt], axis=0)
        n = xf.shape[0]
        acc = None
        for j in range(DN_CONV):
            sh = (DN_CONV // 2 - j) % n
            xs = xf if sh == 0 else pltpu.roll(xf, sh, 0)
            term = xs[HALO:HALO + DN_SC] * cw_ref[j:j + 1, part * DN_WIDTH:(part + 1) * DN_WIDTH]
            acc = term if acc is None else acc + term
        return acc * _sigmoid(acc)

    q_all = conv_silu(zq_ref, zqp_ref, zqn_ref, 0)
    k_all = conv_silu(zk_ref, zkp_ref, zkn_ref, 1)
    v_all = conv_silu(zv_ref, zvp_ref, zvn_ref, 2)

    r, c, same64 = _chunk_masks(DN_SC)
    tri = same64 & ((c >= r) if bwd else (c <= r))
    offdiag = r != c
    same16 = (r // 16) == (c // 16)
    same32 = (r // 32) == (c // 32)
    contract_last = (((1,), (1,)), ((), ()))
    contract_first = (((0,), (0,)), ((), ()))
    nc = DN_SC // DN_CHUNK
    dot = functools.partial(jnp.dot, preferred_element_type=F32)

    for h in range(DN_HEADS):
        sl = slice(h * DN_DIM, (h + 1) * DN_DIM)
        q = q_all[:, sl]
        q = q * (lax.rsqrt(jnp.sum(q * q, axis=-1, keepdims=True) + NORM_EPS) * DN_DIM ** -0.5)
        k = k_all[:, sl]
        k = k * lax.rsqrt(jnp.sum(k * k, axis=-1, keepdims=True) + NORM_EPS)
        v = v_all[:, sl]
        cc = h + (DN_HEADS if bwd else 0)
        gc_c = gc_ref[:, cc:cc + 1]
        gt_c = gt_ref[:, cc:cc + 1]
        beta_c = beta_ref[:, cc:cc + 1]
        gc_r = gct_ref[cc:cc + 1, :]
        dec = jnp.exp(jnp.where(tri, gc_c - gc_r, NEG))
        k16 = k.astype(BF16)
        kk = lax.dot_general(k16, k16, contract_last, preferred_element_type=F32)
        qk16 = (lax.dot_general(q.astype(BF16), k16, contract_last, preferred_element_type=F32) * dec).astype(BF16)
        m = jnp.where(offdiag, kk * beta_c * dec, 0.0)

        p = jnp.where(same16, -m, 0.0)
        e = p
        for _ in range(3):
            p16 = p.astype(BF16)
            p = dot(p16, p16)
            e = e + p + dot(e.astype(BF16), p.astype(BF16))
        for inner, outer in ((same16, same32), (same32, None)):
            cm = jnp.where(~inner if outer is None else (outer & ~inner), m, 0.0)
            e16 = e.astype(BF16)
            y = cm + dot(cm.astype(BF16), e16)
            e = e - y - dot(e16, y.astype(BF16))

        eg = jnp.exp(gc_c)
        rhs = jnp.concatenate([v * beta_c, k * (beta_c * eg)], axis=1)
        uw = rhs + dot(e.astype(BF16), rhs.astype(BF16))
        u = uw[:, :DN_DIM]
        w16 = uw[:, DN_DIM:].astype(BF16)
        qg16 = (q * eg).astype(BF16)
        kd16 = (k * jnp.exp(gt_c - gc_c)).astype(BF16)

        s = state_ref[h]
        outs = [None] * nc
        for ci in (range(nc - 1, -1, -1) if bwd else range(nc)):
            rows = slice(ci * DN_CHUNK, (ci + 1) * DN_CHUNK)
            ws = dot(jnp.concatenate([w16[rows], qg16[rows]], axis=0), s.astype(BF16))
            v_new = u[rows] - ws[:DN_CHUNK]
            vn16 = v_new.astype(BF16)
            outs[ci] = ws[DN_CHUNK:] + dot(qk16[rows, ci * DN_CHUNK:(ci + 1) * DN_CHUNK], vn16)
            gl = jnp.exp(gt_c[ci * DN_CHUNK:ci * DN_CHUNK + 1, :])
            s = s * gl + lax.dot_general(kd16[rows], vn16, contract_first, preferred_element_type=F32)
        state_ref[h] = s
        o_ref[:, sl] = jnp.concatenate(outs, axis=0)

    @pl.when(i == nblk - 1)
    def _():
        sfin_ref[...] = state_ref[...]


def gdn_scan(z, conv_w, gc, gt, beta, gct, s0, bwd):
    l = z.shape[0]
    nblk = l // DN_SC
    per = DN_SC // HALO

    def b(i):
        return nblk - 1 - i if bwd else i

    def part_specs(part):
        return [pl.BlockSpec((DN_SC, DN_WIDTH), lambda i: (b(i), part)),
                pl.BlockSpec((HALO, DN_WIDTH), lambda i: (jnp.maximum(b(i) * per - 1, 0), part)),
                pl.BlockSpec((HALO, DN_WIDTH), lambda i: (jnp.minimum((b(i) + 1) * per, l // HALO - 1), part))]

    col = pl.BlockSpec((DN_SC, N_GB), lambda i: (b(i), 0))
    st = pl.BlockSpec((DN_HEADS, DN_DIM, DN_DIM), lambda i: (0, 0, 0))
    return pl.pallas_call(
        functools.partial(_gdn_kernel, bwd=bwd, nblk=nblk),
        grid=(nblk,),
        in_specs=part_specs(0) + part_specs(1) + part_specs(2) + [
            pl.BlockSpec((DN_CONV, 3 * DN_WIDTH), lambda i: (0, 0)), col, col, col,
            pl.BlockSpec((N_GB, DN_SC), lambda i: (0, b(i))), st],
        out_specs=[pl.BlockSpec((DN_SC, DN_WIDTH), lambda i: (b(i), 0)), st],
        out_shape=[jax.ShapeDtypeStruct((l, DN_WIDTH), F32),
                   jax.ShapeDtypeStruct((DN_HEADS, DN_DIM, DN_DIM), F32)],
        scratch_shapes=[pltpu.VMEM((DN_HEADS, DN_DIM, DN_DIM), F32)],
        compiler_params=_cparams("arbitrary"),
        name="gdn_scan_bwd" if bwd else "gdn_scan_fwd",
    )(z, z, z, z, z, z, z, z, z, conv_w, gc, gt, beta, gct, s0)


def _dn_out_kernel(of_ref, ob_ref, gate_ref, nw_ref, y_ref):
    for h in range(DN_HEADS):
        sl = slice(h * DN_DIM, (h + 1) * DN_DIM)
        o = of_ref[:, sl] + ob_ref[:, sl]
        o = o * lax.rsqrt(jnp.mean(o * o, axis=-1, keepdims=True) + NORM_EPS) * nw_ref[...]
        gate = gate_ref[:, sl]
        y_ref[:, sl] = (o * (gate * _sigmoid(gate))).astype(y_ref.dtype)


def dn_output(o_f, o_b, z, onorm_w):
    l = z.shape[0]
    tm = min(l, 512)
    blk = pl.BlockSpec((tm, DN_WIDTH), lambda i: (i, 0))
    return pl.pallas_call(
        _dn_out_kernel,
        grid=(l // tm,),
        in_specs=[blk, blk, pl.BlockSpec((tm, DN_WIDTH), lambda i: (i, 3)),
                  pl.BlockSpec((1, DN_DIM), lambda i: (0, 0))],
        out_specs=blk,
        out_shape=jax.ShapeDtypeStruct((l, DN_WIDTH), BF16),
        compiler_params=_cparams("parallel"),
        name="dn_output",
    )(o_f, o_b, z, onorm_w.reshape(1, DN_DIM))


def gated_deltanet(zc, zl, conv_w, a_log, dt_bias, onorm_w):
    a0, b0 = 4 * DN_WIDTH, 4 * DN_WIDTH + N_GB
    sc_c = dn_scalars(zc[:, a0:b0], zc[:, b0:b0 + N_GB], a_log, dt_bias)
    sc_l = dn_scalars(zl[:, a0:b0], zl[:, b0:b0 + N_GB], a_log, dt_bias)
    s0 = jnp.zeros((DN_HEADS, DN_DIM, DN_DIM), F32)
    outs_c, outs_l = [], []
    for bwd in (False, True):
        oc, s_ctx = gdn_scan(zc, conv_w, *sc_c, s0, bwd)
        ol, _ = gdn_scan(zl, conv_w, *sc_l, s_ctx, bwd)
        outs_c.append(oc)
        outs_l.append(ol)
    return dn_output(*outs_c, zc, onorm_w), dn_output(*outs_l, zl, onorm_w)


def _dwconv_centred(x, w, b=None):
    k = w.shape[0]
    y = lax.conv_general_dilated(x, w[:, None, :].astype(x.dtype), (1,), [(k // 2, k // 2)],
                                 dimension_numbers=('NWC', 'WIO', 'NWC'),
                                 feature_group_count=x.shape[-1])
    return y if b is None else y + b.astype(x.dtype)


def _hyena_filters(l, w1, b1, w2, b2, w3, b3, w4, freq):
    pos = jnp.arange(l, dtype=F32)
    t = (pos / max(l - 1, 1))[:, None]
    wpos = (2.0 * math.pi * pos / l)[:, None]
    bands = jnp.linspace(1e-4, HY_BANDS - 1, HY_BANDS, dtype=F32)
    z = jnp.concatenate([t, jnp.cos(bands * wpos), -jnp.sin(bands * wpos)], axis=-1)
    h = jnp.sin(freq * (z @ w1 + b1))
    h = jnp.sin(freq * (h @ w2 + b2))
    h = jnp.sin(freq * (h @ w3 + b3))
    h = h @ w4
    min_decay = math.log(HY_DECAY_TARGET) / HY_WEAK_DECAY_PCT
    max_decay = math.log(HY_DECAY_TARGET) / HY_STRONG_DECAY_PCT
    deltas = jnp.abs(jnp.linspace(min_decay, max_decay, HY_WIDTH, dtype=F32))
    window = jnp.exp(-t * deltas)
    return h[:, :HY_WIDTH] * window, h[:, HY_WIDTH:] * window


def _hyena(z, short_w, short_b, filt, bias):
    bsz, l, _ = z.shape
    z = _dwconv_centred(z, short_w, short_b)
    x0, x1, v = jnp.split(z, 3, axis=-1)
    h_f, h_b = _hyena_filters(l, *filt)
    filt_circ = jnp.concatenate([h_f[:1] + h_b[:1], h_f[1:], jnp.zeros_like(h_f[:1]), h_b[:0:-1]], axis=0)
    u = v * x1
    y = jnp.fft.irfft(jnp.fft.rfft(u, n=2 * l, axis=1) * jnp.fft.rfft(filt_circ, axis=0)[None],
                      n=2 * l, axis=1)[:, :l]
    y = y + u * bias
    return y * x0


def kernel(x, c, ctx, c_ctx, w_mod, b_mod, norm1_w, norm2_w, ffn_w_gate, ffn_w_up, ffn_w_down, even_w_in, even_w_out, dn_conv_w, dn_a_log, dn_dt_bias, dn_onorm_w, hy_short_w, hy_short_b, hy_f_w1, hy_f_b1, hy_f_w2, hy_f_b2, hy_f_w3, hy_f_b3, hy_f_w4, hy_f_freq, hy_bias, na_w_qkv, na_rpb, na_w_out, final_norm_w):
    d = D_MODEL
    xl, xc = x[0], ctx[0]
    mod = modulation(jnp.concatenate([c, c_ctx[None]], axis=0), w_mod, b_mod)

    def chunk(layer, row, idx):
        return mod[layer, row:row + 1, idx * d:(idx + 1) * d]

    for layer in range(DEPTH):
        need_ctx = layer < DEPTH - 1
        ml = [chunk(layer, 0, j) for j in range(N_MOD)]
        mc = [chunk(layer, 1, j) for j in range(N_MOD)]
        n1 = norm1_w[layer][None]
        n2 = norm2_w[layer][None]
        i = layer // 2
        if layer % 2 == 0:
            w_in = even_w_in[i]
            zl = norm_mod_matmul(xl, n1, ml[1], ml[0], w_in, F32)
            zc = norm_mod_matmul(xc, n1, mc[1], mc[0], w_in, F32)
            dn_c, dn_l = gated_deltanet(zc, zl, dn_conv_w[i], dn_a_log[i], dn_dt_bias[i], dn_onorm_w[i])
            filt = (hy_f_w1[i], hy_f_b1[i], hy_f_w2[i], hy_f_b2[i], hy_f_w3[i], hy_f_b3[i], hy_f_w4[i],
                    hy_f_freq[i])
            hy_l = _hyena(zl[None, :, DN_IN:], hy_short_w[i], hy_short_b[i], filt, hy_bias[i])
            yl = (dn_l, hy_l[0].astype(BF16))
            w_out = even_w_out[i].astype(BF16)
            if need_ctx:
                hy_c = _hyena(zc[None, :, DN_IN:], hy_short_w[i], hy_short_b[i], filt, hy_bias[i])
                yc = (dn_c, hy_c[0].astype(BF16))
        else:
            w_qkv = na_w_qkv[i]
            zl = norm_mod_matmul(xl, n1, ml[1], ml[0], w_qkv, BF16)
            zc = norm_mod_matmul(xc, n1, mc[1], mc[0], w_qkv, BF16)
            ya = na_attention(zl, zc, na_bias_tiles(na_rpb[i]))
            yl = (ya, ya)
            w_out = na_w_out[i].astype(BF16)
            if need_ctx:
                ya = ctx_attention(zc)
                yc = (ya, ya)
        xl = proj_residual(*yl, w_out, xl, ml[2])
        xl = ffn_block(xl, n2, ml[4], ml[3], ml[5], ffn_w_gate[layer], ffn_w_up[layer], ffn_w_down[layer])
        if need_ctx:
            xc = proj_residual(*yc, w_out, xc, mc[2])
            xc = ffn_block(xc, n2, mc[4], mc[3], mc[5], ffn_w_gate[layer], ffn_w_up[layer], ffn_w_down[layer])
    return final_norm(xl, final_norm_w[None])[None]
```

```python
import functools
import math

import jax
import jax.numpy as jnp
import numpy as np
from jax import lax
from jax.experimental import pallas as pl
from jax.experimental.pallas import tpu as pltpu

D_MODEL = 2048
SEQ = 8192
DEPTH = 4
GRID_W = 64
ROWS = SEQ // GRID_W
CTX_LEN = 256
NORM_EPS = 1e-6
N_MOD = 6

DN_HEADS = 8
DN_DIM = 128
DN_WIDTH = DN_HEADS * DN_DIM
DN_CONV = 5
DN_CHUNK = 64
DN_IN = 4 * DN_WIDTH + 4 * DN_HEADS

HY_WIDTH = D_MODEL - DN_WIDTH
HY_BANDS = 16
HY_DECAY_TARGET = 1e-2
HY_STRONG_DECAY_PCT = 0.3
HY_WEAK_DECAY_PCT = 1.5
HY_IN = 3 * HY_WIDTH
EVEN_IN = DN_IN + HY_IN

NA_HEADS = D_MODEL // 128
NA_DIM = 128
WIN_R = 8
WIN_C = 16
FFN_HIDDEN = -(-8 * D_MODEL // (3 * 256)) * 256

LANES = 128
VMEM_LIMIT = 56 * 1024 * 1024
NEG = -1e30

F32 = jnp.float32
BF16 = jnp.bfloat16


def _cparams(*sem):
    return pltpu.CompilerParams(dimension_semantics=sem, vmem_limit_bytes=VMEM_LIMIT)


def _sigmoid(x):
    return 1.0 / (1.0 + jnp.exp(-x))


MOD_TN = 1024


def _mod_kernel(c_ref, w_ref, b_ref, o_ref):
    rows = []
    for r in range(2):
        c = c_ref[r]
        a = c * _sigmoid(c)
        cols = [jnp.sum(w_ref[0, :, j * LANES:(j + 1) * LANES] * a, axis=0, keepdims=True)
                for j in range(MOD_TN // LANES)]
        rows.append(jnp.concatenate(cols, axis=1))
    o_ref[0] = jnp.concatenate(rows, axis=0) + b_ref[0]


def modulation(c_pair, w_mod, b_mod):
    n = N_MOD * D_MODEL
    cb = jnp.broadcast_to(c_pair[:, :, None], (2, D_MODEL, LANES))
    return pl.pallas_call(
        _mod_kernel,
        grid=(DEPTH, n // MOD_TN),
        in_specs=[pl.BlockSpec((2, D_MODEL, LANES), lambda l, j: (0, 0, 0)),
                  pl.BlockSpec((1, D_MODEL, MOD_TN), lambda l, j: (l, 0, j)),
                  pl.BlockSpec((1, 1, MOD_TN), lambda l, j: (l, 0, j))],
        out_specs=pl.BlockSpec((1, 2, MOD_TN), lambda l, j: (l, 0, j)),
        out_shape=jax.ShapeDtypeStruct((DEPTH, 2, n), F32),
        compiler_params=_cparams("parallel", "parallel"),
        name="modulation",
    )(cb, w_mod, b_mod.reshape(DEPTH, 1, n))


NORM_ROWS = 256


def _norm_mod_to(h_ref, x_ref, nw_ref, sc_ref, sh_ref):
    tm = x_ref.shape[0]
    gain = nw_ref[...]
    scale = 1.0 + sc_ref[...]
    shift = sh_ref[...]

    def body(i, carry):
        rows = pl.ds(pl.multiple_of(i * NORM_ROWS, NORM_ROWS), NORM_ROWS)
        x = x_ref[rows, :]
        y = x * lax.rsqrt(jnp.mean(x * x, axis=-1, keepdims=True) + NORM_EPS)
        h_ref[rows, :] = (y * gain * scale + shift).astype(BF16)
        return carry

    lax.fori_loop(0, tm // NORM_ROWS, body, 0)


def _nmm_kernel(x_ref, nw_ref, sc_ref, sh_ref, w_ref, o_ref, h_ref):
    @pl.when(pl.program_id(1) == 0)
    def _():
        _norm_mod_to(h_ref, x_ref, nw_ref, sc_ref, sh_ref)

    o_ref[...] = jnp.dot(h_ref[...], w_ref[...].astype(BF16),
                         preferred_element_type=F32).astype(o_ref.dtype)


def norm_mod_matmul(x, nw, sc, sh, w, out_dtype, tn=512):
    m, d = x.shape
    n = w.shape[1]
    tm = min(m, 1024)
    vec = pl.BlockSpec((1, d), lambda i, j: (0, 0))
    return pl.pallas_call(
        _nmm_kernel,
        grid=(m // tm, pl.cdiv(n, tn)),
        in_specs=[pl.BlockSpec((tm, d), lambda i, j: (i, 0)), vec, vec, vec,
                  pl.BlockSpec((d, tn), lambda i, j: (0, j))],
        out_specs=pl.BlockSpec((tm, tn), lambda i, j: (i, j)),
        out_shape=jax.ShapeDtypeStruct((m, n), out_dtype),
        scratch_shapes=[pltpu.VMEM((tm, d), BF16)],
        compiler_params=_cparams("parallel", "arbitrary"),
        name="norm_mod_matmul",
    )(x, nw, sc, sh, w)


FFN_TF = 256


def _ffn_kernel(x_ref, nw_ref, sc_ref, sh_ref, g_ref, wg_ref, wu_ref, wd_ref, o_ref, h_ref):
    f = pl.program_id(1)

    @pl.when(f == 0)
    def _():
        _norm_mod_to(h_ref, x_ref, nw_ref, sc_ref, sh_ref)

    h = h_ref[...]
    gate = jnp.dot(h, wg_ref[...].astype(BF16), preferred_element_type=F32)
    up = jnp.dot(h, wu_ref[...].astype(BF16), preferred_element_type=F32)
    a = (gate * _sigmoid(gate) * up).astype(BF16)
    part = jnp.dot(a, wd_ref[...].astype(BF16), preferred_element_type=F32)

    @pl.when(f == 0)
    def _():
        o_ref[...] = part

    @pl.when(f > 0)
    def _():
        o_ref[...] += part

    @pl.when(f == pl.num_programs(1) - 1)
    def _():
        o_ref[...] = x_ref[...] + g_ref[...] * o_ref[...]


def ffn_block(x, nw, sc, sh, g, wg, wu, wd):
    m, d = x.shape
    hid = wg.shape[1]
    tm = min(m, 1024)
    vec = pl.BlockSpec((1, d), lambda i, f: (0, 0))
    return pl.pallas_call(
        _ffn_kernel,
        grid=(m // tm, hid // FFN_TF),
        in_specs=[pl.BlockSpec((tm, d), lambda i, f: (i, 0), pipeline_mode=pl.Buffered(1)), vec, vec, vec, vec,
                  pl.BlockSpec((d, FFN_TF), lambda i, f: (0, f)),
                  pl.BlockSpec((d, FFN_TF), lambda i, f: (0, f)),
                  pl.BlockSpec((FFN_TF, d), lambda i, f: (f, 0))],
        out_specs=pl.BlockSpec((tm, d), lambda i, f: (i, 0)),
        out_shape=jax.ShapeDtypeStruct((m, d), F32),
        scratch_shapes=[pltpu.VMEM((tm, d), BF16)],
        compiler_params=_cparams("parallel", "arbitrary"),
        name="ffn_block",
    )(x, nw, sc, sh, g, wg, wu, wd)


def _proj_res_kernel(ya_ref, yb_ref, w_ref, x_ref, g_ref, o_ref):
    ka = ya_ref.shape[1]
    y = (jnp.dot(ya_ref[...], w_ref[:ka, :], preferred_element_type=F32)
         + jnp.dot(yb_ref[...], w_ref[ka:, :], preferred_element_type=F32))
    o_ref[...] = x_ref[...] + g_ref[...] * y


def proj_residual(ya, yb, w_bf16, x, g):
    m, d = x.shape
    ka = kb = d // 2
    second = 1 if yb is ya else 0
    tm = min(m, 512)
    return pl.pallas_call(
        _proj_res_kernel,
        grid=(m // tm,),
        in_specs=[pl.BlockSpec((tm, ka), lambda i: (i, 0)),
                  pl.BlockSpec((tm, kb), lambda i: (i, second)),
                  pl.BlockSpec((d, d), lambda i: (0, 0)),
                  pl.BlockSpec((tm, d), lambda i: (i, 0)),
                  pl.BlockSpec((1, d), lambda i: (0, 0))],
        out_specs=pl.BlockSpec((tm, d), lambda i: (i, 0)),
        out_shape=jax.ShapeDtypeStruct((m, d), F32),
        compiler_params=_cparams("parallel"),
        name="proj_residual",
    )(ya, yb, w_bf16, x, g)


def _final_norm_kernel(x_ref, w_ref, o_ref):
    x = x_ref[...]
    o_ref[...] = x * lax.rsqrt(jnp.mean(x * x, axis=-1, keepdims=True) + NORM_EPS) * w_ref[...]


def final_norm(x, w):
    m, d = x.shape
    tm = 512
    return pl.pallas_call(
        _final_norm_kernel,
        grid=(m // tm,),
        in_specs=[pl.BlockSpec((tm, d), lambda i: (i, 0)), pl.BlockSpec((1, d), lambda i: (0, 0))],
        out_specs=pl.BlockSpec((tm, d), lambda i: (i, 0)),
        out_shape=jax.ShapeDtypeStruct((m, d), F32),
        compiler_params=_cparams("parallel"),
        name="final_norm",
    )(x, w)


NA_R = 8
NA_KD = (NA_R + WIN_R) // 2
NA_T = 2 * WIN_R


def _na_bias_kernel(rpb_ref, o_ref):
    h = pl.program_id(0)
    c = lax.broadcasted_iota(jnp.int32, (GRID_W, LANES), 0)
    lane = lax.broadcasted_iota(jnp.int32, (GRID_W, LANES), 1)
    left = lane < GRID_W
    kc = jnp.where(left, lane, lane - GRID_W)
    cs = jnp.clip(c - WIN_C // 2, 0, GRID_W - WIN_C)
    inwin = (kc >= cs) & (kc < cs + WIN_C)
    diff = kc - c + (WIN_C - 1)
    n_ro, n_co = 2 * WIN_R - 1, 2 * WIN_C - 1
    for t in range(NA_T):
        def body(d, acc, t=t):
            vl = rpb_ref[(h * n_ro + (t - 1)) * n_co + d] if t >= 1 else jnp.float32(NEG)
            vr = rpb_ref[(h * n_ro + t) * n_co + d] if t < n_ro else jnp.float32(NEG)
            return jnp.where(diff == d, jnp.where(left, vl, vr), acc)

        acc = lax.fori_loop(0, n_co, body, jnp.full((GRID_W, LANES), NEG, F32))
        o_ref[0, t] = jnp.where(inwin, acc, NEG)


def na_bias_tiles(rpb):
    return pl.pallas_call(
        _na_bias_kernel,
        grid=(NA_HEADS,),
        in_specs=[pl.BlockSpec(memory_space=pltpu.SMEM)],
        out_specs=pl.BlockSpec((1, NA_T, GRID_W, LANES), lambda h: (h, 0, 0, 0)),
        out_shape=jax.ShapeDtypeStruct((NA_HEADS, NA_T, GRID_W, LANES), F32),
        compiler_params=_cparams("parallel"),
        name="na_bias_tiles",
    )(rpb.reshape(-1))


def _na_kernel(q_ref, k_ref, v_ref, kc_ref, vc_ref, tt_ref, o_ref):
    scale = NA_DIM ** -0.5
    r0 = pl.program_id(1) * NA_R
    ks = jnp.clip(r0 - WIN_R // 2, 0, ROWS - 2 * NA_KD)
    kstart = pl.multiple_of(ks * GRID_W, LANES)
    kspan = k_ref[pl.ds(kstart, NA_KD * LANES), :]
    vspan = v_ref[pl.ds(kstart, NA_KD * LANES), :]
    q = q_ref[...]
    contract_last = (((1,), (1,)), ((), ()))
    s_loc = lax.dot_general(q, kspan, contract_last, preferred_element_type=F32) * scale
    s_ctx = lax.dot_general(q, kc_ref[...], contract_last, preferred_element_type=F32) * scale

    left = lax.broadcasted_iota(jnp.int32, (GRID_W, LANES), 1) < GRID_W
    bias_rows = []
    for qi in range(NA_R):
        r = r0 + qi
        rs = jnp.clip(r - WIN_R // 2, 0, ROWS - WIN_R)
        tiles = []
        for dj in range(NA_KD):
            kl = ks + 2 * dj
            vl = ((kl >= rs) & (kl < rs + WIN_R)).astype(jnp.int32)
            vr = ((kl + 1 >= rs) & (kl + 1 < rs + WIN_R)).astype(jnp.int32)
            t = jnp.clip(kl - r + WIN_R, 0, NA_T - 1)
            valid = jnp.where(left, vl, vr) > 0
            tiles.append(jnp.where(valid, tt_ref[0, t], NEG))
        bias_rows.append(jnp.concatenate(tiles, axis=1))
    s_loc = s_loc + jnp.concatenate(bias_rows, axis=0)

    m = jnp.maximum(jnp.max(s_loc, axis=-1, keepdims=True), jnp.max(s_ctx, axis=-1, keepdims=True))
    p_loc = jnp.exp(s_loc - m)
    p_ctx = jnp.exp(s_ctx - m)
    denom = jnp.sum(p_loc, axis=-1, keepdims=True) + jnp.sum(p_ctx, axis=-1, keepdims=True)
    o = (jnp.dot(p_loc.astype(BF16), vspan, preferred_element_type=F32)
         + jnp.dot(p_ctx.astype(BF16), vc_ref[...], preferred_element_type=F32))
    o_ref[...] = (o / denom).astype(o_ref.dtype)


def na_attention(zl, zc, tiles):
    tq = NA_R * GRID_W
    nh = NA_HEADS
    return pl.pallas_call(
        _na_kernel,
        grid=(nh, ROWS // NA_R),
        in_specs=[pl.BlockSpec((tq, NA_DIM), lambda h, i: (i, h)),
                  pl.BlockSpec((SEQ, NA_DIM), lambda h, i: (0, nh + h)),
                  pl.BlockSpec((SEQ, NA_DIM), lambda h, i: (0, 2 * nh + h)),
                  pl.BlockSpec((CTX_LEN, NA_DIM), lambda h, i: (0, nh + h)),
                  pl.BlockSpec((CTX_LEN, NA_DIM), lambda h, i: (0, 2 * nh + h)),
                  pl.BlockSpec((1, NA_T, GRID_W, LANES), lambda h, i: (h, 0, 0, 0))],
        out_specs=pl.BlockSpec((tq, NA_DIM), lambda h, i: (i, h)),
        out_shape=jax.ShapeDtypeStruct((SEQ, D_MODEL), BF16),
        compiler_params=_cparams("parallel", "arbitrary"),
        name="na_attention",
    )(zl, zl, zl, zc, zc, tiles)


def _ctx_attn_kernel(q_ref, k_ref, v_ref, o_ref):
    scale = NA_DIM ** -0.5
    s = lax.dot_general(q_ref[...], k_ref[...], (((1,), (1,)), ((), ())), preferred_element_type=F32) * scale
    p = jnp.exp(s - jnp.max(s, axis=-1, keepdims=True))
    o = jnp.dot(p.astype(BF16), v_ref[...], preferred_element_type=F32)
    o_ref[...] = (o / jnp.sum(p, axis=-1, keepdims=True)).astype(o_ref.dtype)


def ctx_attention(zc):
    nh = NA_HEADS
    blk = lambda off: pl.BlockSpec((CTX_LEN, NA_DIM), lambda h: (0, off + h))
    return pl.pallas_call(
        _ctx_attn_kernel,
        grid=(nh,),
        in_specs=[blk(0), blk(nh), blk(2 * nh)],
        out_specs=blk(0),
        out_shape=jax.ShapeDtypeStruct((CTX_LEN, D_MODEL), BF16),
        compiler_params=_cparams("parallel"),
        name="ctx_attention",
    )(zc, zc, zc)


DN_SC = 256
HALO = 8
N_GB = 2 * DN_HEADS


def _softplus(x):
    return jnp.maximum(x, 0.0) + jnp.log1p(jnp.exp(-jnp.abs(x)))


def _short_conv(main_ref, prev_ref, next_ref, w_ref, col0, blk, nblk):
    ksize = w_ref.shape[0]
    tb, width = main_ref.shape
    prev = jnp.where(blk > 0, prev_ref[...], 0.0)
    nxt = jnp.where(blk < nblk - 1, next_ref[...], 0.0)
    xf = jnp.concatenate([prev, main_ref[...], nxt], axis=0)
    n = xf.shape[0]
    acc = None
    for j in range(ksize):
        sh = (ksize // 2 - j) % n
        xs = xf if sh == 0 else pltpu.roll(xf, sh, 0)
        term = xs[HALO:HALO + tb] * w_ref[j:j + 1, col0:col0 + width]
        acc = term if acc is None else acc + term
    return acc


def _chunk_masks(n):
    r = lax.broadcasted_iota(jnp.int32, (n, n), 0)
    c = lax.broadcasted_iota(jnp.int32, (n, n), 1)
    return r, c, (r // DN_CHUNK) == (c // DN_CHUNK)


def _dn_scalars_kernel(za_ref, zb_ref, zat_ref, alog_ref, dtb_ref, alogt_ref, dtbt_ref,
                       gc_ref, gt_ref, beta_ref, gct_ref):
    hi = lax.Precision.HIGHEST
    g = -jnp.exp(alog_ref[...]) * _softplus(za_ref[...] + dtb_ref[...])
    beta_ref[...] = _sigmoid(zb_ref[...])
    r, c, same = _chunk_masks(DN_SC)
    lo = (same & (c <= r)).astype(F32)
    up = (same & (c >= r)).astype(F32)
    fwd_col = lax.broadcasted_iota(jnp.int32, (DN_SC, N_GB), 1) < DN_HEADS
    gc_ref[...] = jnp.where(fwd_col, jnp.dot(lo, g, precision=hi), jnp.dot(up, g, precision=hi))
    gt_ref[...] = jnp.dot(same.astype(F32), g, precision=hi)
    g_t = -jnp.exp(alogt_ref[...]) * _softplus(zat_ref[...] + dtbt_ref[...])
    fwd_row = lax.broadcasted_iota(jnp.int32, (N_GB, DN_SC), 0) < DN_HEADS
    gct_ref[...] = jnp.where(fwd_row, jnp.dot(g_t, up, precision=hi), jnp.dot(g_t, lo, precision=hi))


def dn_scalars(za, zb, a_log, dt_bias):
    l = za.shape[0]
    col = pl.BlockSpec((DN_SC, N_GB), lambda i: (i, 0))
    row = pl.BlockSpec((N_GB, DN_SC), lambda i: (0, i))
    prow = pl.BlockSpec((1, N_GB), lambda i: (0, 0))
    pcol = pl.BlockSpec((N_GB, 1), lambda i: (0, 0))
    cshape = jax.ShapeDtypeStruct((l, N_GB), F32)
    return pl.pallas_call(
        _dn_scalars_kernel,
        grid=(l // DN_SC,),
        in_specs=[col, col, row, prow, prow, pcol, pcol],
        out_specs=[col, col, col, row],
        out_shape=[cshape, cshape, cshape, jax.ShapeDtypeStruct((N_GB, l), F32)],
        compiler_params=_cparams("parallel"),
        name="dn_scalars",
    )(za, zb, za.T, a_log.reshape(1, N_GB), dt_bias.reshape(1, N_GB), a_log.reshape(N_GB, 1),
      dt_bias.reshape(N_GB, 1))


def _gdn_kernel(zq_ref, zqp_ref, zqn_ref, zk_ref, zkp_ref, zkn_ref, zv_ref, zvp_ref, zvn_ref, cw_ref,
                gc_ref, gt_ref, beta_ref, gct_ref, s0_ref, o_ref, sfin_ref, state_ref, *, bwd, nblk):
    i = pl.program_id(0)
    blk = nblk - 1 - i if bwd else i

    @pl.when(i == 0)
    def _():
        state_ref[...] = s0_ref[...]

    def conv_silu(main_ref, prev_ref, next_ref, part):
        acc = _short_conv(main_ref, prev_ref, next_ref, cw_ref, part * DN_WIDTH, blk, nblk)
        return acc * _sigmoid(acc)

    q_all = conv_silu(zq_ref, zqp_ref, zqn_ref, 0)
    k_all = conv_silu(zk_ref, zkp_ref, zkn_ref, 1)
    v_all = conv_silu(zv_ref, zvp_ref, zvn_ref, 2)

    r, c, same64 = _chunk_masks(DN_SC)
    tri = same64 & ((c >= r) if bwd else (c <= r))
    offdiag = r != c
    same16 = (r // 16) == (c // 16)
    same32 = (r // 32) == (c // 32)
    contract_last = (((1,), (1,)), ((), ()))
    contract_first = (((0,), (0,)), ((), ()))
    nc = DN_SC // DN_CHUNK
    dot = functools.partial(jnp.dot, preferred_element_type=F32)

    for h in range(DN_HEADS):
        sl = slice(h * DN_DIM, (h + 1) * DN_DIM)
        q = q_all[:, sl]
        q = q * (lax.rsqrt(jnp.sum(q * q, axis=-1, keepdims=True) + NORM_EPS) * DN_DIM ** -0.5)
        k = k_all[:, sl]
        k = k * lax.rsqrt(jnp.sum(k * k, axis=-1, keepdims=True) + NORM_EPS)
        v = v_all[:, sl]
        cc = h + (DN_HEADS if bwd else 0)
        gc_c = gc_ref[:, cc:cc + 1]
        gt_c = gt_ref[:, cc:cc + 1]
        beta_c = beta_ref[:, cc:cc + 1]
        gc_r = gct_ref[cc:cc + 1, :]
        dec = jnp.exp(jnp.where(tri, gc_c - gc_r, NEG))
        k16 = k.astype(BF16)
        kk = lax.dot_general(k16, k16, contract_last, preferred_element_type=F32)
        qk16 = (lax.dot_general(q.astype(BF16), k16, contract_last, preferred_element_type=F32) * dec).astype(BF16)
        m = jnp.where(offdiag, kk * beta_c * dec, 0.0)

        p = jnp.where(same16, -m, 0.0)
        e = p
        for _ in range(3):
            p16 = p.astype(BF16)
            p = dot(p16, p16)
            e = e + p + dot(e.astype(BF16), p.astype(BF16))
        for inner, outer in ((same16, same32), (same32, None)):
            cm = jnp.where(~inner if outer is None else (outer & ~inner), m, 0.0)
            e16 = e.astype(BF16)
            y = cm + dot(cm.astype(BF16), e16)
            e = e - y - dot(e16, y.astype(BF16))

        eg = jnp.exp(gc_c)
        rhs = jnp.concatenate([v * beta_c, k * (beta_c * eg)], axis=1)
        uw = rhs + dot(e.astype(BF16), rhs.astype(BF16))
        u = uw[:, :DN_DIM]
        w16 = uw[:, DN_DIM:].astype(BF16)
        qg16 = (q * eg).astype(BF16)
        kd16 = (k * jnp.exp(gt_c - gc_c)).astype(BF16)

        s = state_ref[h]
        outs = [None] * nc
        for ci in (range(nc - 1, -1, -1) if bwd else range(nc)):
            rows = slice(ci * DN_CHUNK, (ci + 1) * DN_CHUNK)
            ws = dot(jnp.concatenate([w16[rows], qg16[rows]], axis=0), s.astype(BF16))
            v_new = u[rows] - ws[:DN_CHUNK]
            vn16 = v_new.astype(BF16)
            outs[ci] = ws[DN_CHUNK:] + dot(qk16[rows, ci * DN_CHUNK:(ci + 1) * DN_CHUNK], vn16)
            gl = jnp.exp(gt_c[ci * DN_CHUNK:ci * DN_CHUNK + 1, :])
            s = s * gl + lax.dot_general(kd16[rows], vn16, contract_first, preferred_element_type=F32)
        state_ref[h] = s
        o_ref[:, sl] = jnp.concatenate(outs, axis=0)

    @pl.when(i == nblk - 1)
    def _():
        sfin_ref[...] = state_ref[...]


def gdn_scan(z, conv_w, gc, gt, beta, gct, s0, bwd):
    l = z.shape[0]
    nblk = l // DN_SC
    per = DN_SC // HALO

    def b(i):
        return nblk - 1 - i if bwd else i

    def part_specs(part):
        return [pl.BlockSpec((DN_SC, DN_WIDTH), lambda i: (b(i), part)),
                pl.BlockSpec((HALO, DN_WIDTH), lambda i: (jnp.maximum(b(i) * per - 1, 0), part)),
                pl.BlockSpec((HALO, DN_WIDTH), lambda i: (jnp.minimum((b(i) + 1) * per, l // HALO - 1), part))]

    col = pl.BlockSpec((DN_SC, N_GB), lambda i: (b(i), 0))
    st = pl.BlockSpec((DN_HEADS, DN_DIM, DN_DIM), lambda i: (0, 0, 0))
    return pl.pallas_call(
        functools.partial(_gdn_kernel, bwd=bwd, nblk=nblk),
        grid=(nblk,),
        in_specs=part_specs(0) + part_specs(1) + part_specs(2) + [
            pl.BlockSpec((DN_CONV, 3 * DN_WIDTH), lambda i: (0, 0)), col, col, col,
            pl.BlockSpec((N_GB, DN_SC), lambda i: (0, b(i))), st],
        out_specs=[pl.BlockSpec((DN_SC, DN_WIDTH), lambda i: (b(i), 0)), st],
        out_shape=[jax.ShapeDtypeStruct((l, DN_WIDTH), F32),
                   jax.ShapeDtypeStruct((DN_HEADS, DN_DIM, DN_DIM), F32)],
        scratch_shapes=[pltpu.VMEM((DN_HEADS, DN_DIM, DN_DIM), F32)],
        compiler_params=_cparams("arbitrary"),
        name="gdn_scan_bwd" if bwd else "gdn_scan_fwd",
    )(z, z, z, z, z, z, z, z, z, conv_w, gc, gt, beta, gct, s0)


def _dn_out_kernel(of_ref, ob_ref, gate_ref, nw_ref, y_ref):
    for h in range(DN_HEADS):
        sl = slice(h * DN_DIM, (h + 1) * DN_DIM)
        o = of_ref[:, sl] + ob_ref[:, sl]
        o = o * lax.rsqrt(jnp.mean(o * o, axis=-1, keepdims=True) + NORM_EPS) * nw_ref[...]
        gate = gate_ref[:, sl]
        y_ref[:, sl] = (o * (gate * _sigmoid(gate))).astype(y_ref.dtype)


def dn_output(o_f, o_b, z, onorm_w):
    l = z.shape[0]
    tm = min(l, 512)
    blk = pl.BlockSpec((tm, DN_WIDTH), lambda i: (i, 0))
    return pl.pallas_call(
        _dn_out_kernel,
        grid=(l // tm,),
        in_specs=[blk, blk, pl.BlockSpec((tm, DN_WIDTH), lambda i: (i, 3)),
                  pl.BlockSpec((1, DN_DIM), lambda i: (0, 0))],
        out_specs=blk,
        out_shape=jax.ShapeDtypeStruct((l, DN_WIDTH), BF16),
        compiler_params=_cparams("parallel"),
        name="dn_output",
    )(o_f, o_b, z, onorm_w.reshape(1, DN_DIM))


def gated_deltanet(zc, zl, conv_w, a_log, dt_bias, onorm_w):
    a0, b0 = EVEN_IN - 2 * N_GB, EVEN_IN - N_GB
    sc_c = dn_scalars(zc[:, a0:b0], zc[:, b0:b0 + N_GB], a_log, dt_bias)
    sc_l = dn_scalars(zl[:, a0:b0], zl[:, b0:b0 + N_GB], a_log, dt_bias)
    s0 = jnp.zeros((DN_HEADS, DN_DIM, DN_DIM), F32)
    outs_c, outs_l = [], []
    for bwd in (False, True):
        oc, s_ctx = gdn_scan(zc, conv_w, *sc_c, s0, bwd)
        ol, _ = gdn_scan(zl, conv_w, *sc_l, s_ctx, bwd)
        outs_c.append(oc)
        outs_l.append(ol)
    return dn_output(*outs_c, zc, onorm_w), dn_output(*outs_l, zl, onorm_w)


HY_SHORT = 3
HY_EMB = 1 + 2 * HY_BANDS
HY_EMB_PAD = 40
HY_TB = 256
FFT_B = 128
HY_CB = 16
HY_COL0 = 4


def _hy_pre_kernel(x0_ref, x0p_ref, x0n_ref, x1_ref, x1p_ref, x1n_ref, v_ref, vp_ref, vn_ref, w_ref, b_ref,
                   u_ref, x0c_ref, *, nblk, transpose_u):
    blk = pl.program_id(0)
    w = HY_WIDTH
    x0 = _short_conv(x0_ref, x0p_ref, x0n_ref, w_ref, 0, blk, nblk) + b_ref[:, :w]
    x1 = _short_conv(x1_ref, x1p_ref, x1n_ref, w_ref, w, blk, nblk) + b_ref[:, w:2 * w]
    v = _short_conv(v_ref, vp_ref, vn_ref, w_ref, 2 * w, blk, nblk) + b_ref[:, 2 * w:]
    u = v * x1
    x0c_ref[...] = x0
    u_ref[...] = u.T if transpose_u else u


def hyena_pre(z, short_w, short_b, transpose_u):
    l = z.shape[0]
    tb = min(HY_TB, l)
    nblk = l // tb
    per = tb // HALO
    w = HY_WIDTH

    def part_specs(part):
        cb = HY_COL0 + part
        return [pl.BlockSpec((tb, w), lambda i: (i, cb)),
                pl.BlockSpec((HALO, w), lambda i: (jnp.maximum(i * per - 1, 0), cb)),
                pl.BlockSpec((HALO, w), lambda i: (jnp.minimum((i + 1) * per, l // HALO - 1), cb))]

    u_shape, u_spec = ((w, l), pl.BlockSpec((w, tb), lambda i: (0, i))) if transpose_u else (
        (l, w), pl.BlockSpec((tb, w), lambda i: (i, 0)))
    return pl.pallas_call(
        functools.partial(_hy_pre_kernel, nblk=nblk, transpose_u=transpose_u),
        grid=(nblk,),
        in_specs=part_specs(0) + part_specs(1) + part_specs(2) + [
            pl.BlockSpec((HY_SHORT, HY_IN), lambda i: (0, 0)), pl.BlockSpec((1, HY_IN), lambda i: (0, 0))],
        out_specs=[u_spec, pl.BlockSpec((tb, w), lambda i: (i, 0))],
        out_shape=[jax.ShapeDtypeStruct(u_shape, F32), jax.ShapeDtypeStruct((l, w), F32)],
        compiler_params=_cparams("parallel"),
        name="hyena_pre",
    )(z, z, z, z, z, z, z, z, z, short_w, short_b.reshape(1, HY_IN))


def _hy_filter_kernel(w1t_ref, b1_ref, w2t_ref, b2_ref, w3t_ref, b3_ref, w4t_ref, fr_ref, band_ref, dl_ref,
                      f_ref, hb0_ref, *, l, fb):
    hi = lax.Precision.HIGHEST
    j = pl.program_id(0)
    second = j >= l // fb
    n = j * fb + lax.broadcasted_iota(jnp.int32, (1, fb), 1)
    pos = jnp.where(second, 2 * l - n, n).astype(F32)
    t = pos / max(l - 1, 1)
    wpos = 2.0 * math.pi * pos / l
    arg = band_ref[...] * wpos
    row = lax.broadcasted_iota(jnp.int32, (HY_EMB_PAD, fb), 0)
    feat = jnp.where(row == 0, t, jnp.where(row <= HY_BANDS, jnp.cos(arg),
                                            jnp.where(row <= 2 * HY_BANDS, -jnp.sin(arg), 0.0)))
    fr = fr_ref[...]
    h = jnp.sin(fr * (jnp.dot(w1t_ref[...], feat, precision=hi) + b1_ref[...]))
    h = jnp.sin(fr * (jnp.dot(w2t_ref[...], h, precision=hi) + b2_ref[...]))
    h = jnp.sin(fr * (jnp.dot(w3t_ref[...], h, precision=hi) + b3_ref[...]))
    window = jnp.exp(-t * dl_ref[...])
    half = pl.multiple_of(jnp.where(second, HY_WIDTH, 0), HY_WIDTH)
    f = jnp.dot(w4t_ref[pl.ds(half, HY_WIDTH), :], h, precision=hi) * window
    f_ref[...] = jnp.where(n == l, 0.0, f)

    @pl.when(j == 0)
    def _():
        hb0_ref[...] = (jnp.dot(w4t_ref[HY_WIDTH:, :], h[:, :LANES], precision=hi) * window[:, :LANES])


def hyena_filter(l, w1, b1, w2, b2, w3, b3, w4, freq):
    fb = min(1024, l)
    colv = lambda v: v.reshape(-1, 1)
    bands = np.zeros((HY_EMB_PAD, 1), np.float32)
    bands[1:1 + HY_BANDS, 0] = bands[1 + HY_BANDS:HY_EMB, 0] = np.linspace(1e-4, HY_BANDS - 1, HY_BANDS,
                                                                            dtype=np.float32)
    min_decay = math.log(HY_DECAY_TARGET) / HY_WEAK_DECAY_PCT
    max_decay = math.log(HY_DECAY_TARGET) / HY_STRONG_DECAY_PCT
    deltas = np.abs(np.linspace(min_decay, max_decay, HY_WIDTH, dtype=np.float32)).reshape(-1, 1)
    w1t = jnp.pad(w1.T, ((0, 0), (0, HY_EMB_PAD - HY_EMB)))
    full = lambda a: pl.BlockSpec(a.shape, lambda j: (0,) * a.ndim)
    args = (w1t, colv(b1), w2.T, colv(b2), w3.T, colv(b3), w4.T, colv(freq), jnp.asarray(bands), jnp.asarray(deltas))
    filt, hb0 = pl.pallas_call(
        functools.partial(_hy_filter_kernel, l=l, fb=fb),
        grid=(2 * l // fb,),
        in_specs=[full(a) for a in args],
        out_specs=[pl.BlockSpec((HY_WIDTH, fb), lambda j: (0, j)), pl.BlockSpec((HY_WIDTH, LANES), lambda j: (0, 0))],
        out_shape=[jax.ShapeDtypeStruct((HY_WIDTH, 2 * l), F32), jax.ShapeDtypeStruct((HY_WIDTH, LANES), F32)],
        compiler_params=_cparams("arbitrary"),
        name="hyena_filter",
    )(*args)
    return filt, hb0[:, :1]


def _dft_constants():
    b = FFT_B
    n = b * b
    idx = np.arange(b)
    ang = 2.0 * np.pi * np.outer(idx, idx) / b
    c, s = np.cos(ang), np.sin(ang)
    tw = 2.0 * np.pi * np.outer(idx, idx) / n
    fwd_b = np.concatenate([c, -s], axis=0)
    cs = np.concatenate([c, s], axis=1)
    inv_b = np.concatenate([c[:b // 2], -s[:b // 2]], axis=1) / n
    return (jnp.asarray(fwd_b, BF16), jnp.asarray(cs, BF16), jnp.asarray(np.cos(tw), F32),
            jnp.asarray(-np.sin(tw), F32), jnp.asarray(inv_b, BF16))


def _hy_conv_kernel(u_ref, f_ref, fwd_ref, cs_ref, tr_ref, ti_ref, inv_ref, y_ref, ur_s, ui_s, fr_s, fi_s):
    cb = u_ref.shape[0]
    b = FFT_B
    m = cb * b
    dot = functools.partial(jnp.dot, preferred_element_type=F32)
    fwd = fwd_ref[...]
    fwd_half = fwd[:, :b // 2]
    tr, ti = tr_ref[...], ti_ref[...]

    def first_stage(c, carry):
        for src, lhs, re_s, im_s in ((u_ref, fwd_half, ur_s, ui_s), (f_ref, fwd, fr_s, fi_s)):
            p = dot(lhs, src[c].astype(BF16))
            pr, pi = p[:b], p[b:]
            re_s[c] = (pr * tr - pi * ti).astype(BF16)
            im_s[c] = (pr * ti + pi * tr).astype(BF16)
        return carry

    lax.fori_loop(0, cb, first_stage, 0, unroll=2)

    cs = cs_ref[...]

    def times_cs(re, im):
        big = dot(jnp.concatenate([re, im], axis=0), cs)
        return big[:m, :b], big[:m, b:], big[m:, :b], big[m:, b:]

    def second_stage(re_s, im_s):
        rc, rs, ic, is_ = times_cs(re_s[...].reshape(m, b), im_s[...].reshape(m, b))
        return rc + is_, ic - rs

    xr, xi = second_stage(ur_s, ui_s)
    hr, hi = second_stage(fr_s, fi_s)
    zr = (xr * hr - xi * hi).astype(BF16)
    zi = (xr * hi + xi * hr).astype(BF16)
    rc, rs, ic, is_ = times_cs(zr, zi)
    gr = (rc - is_).reshape(cb, b, b)
    gi = (rs + ic).reshape(cb, b, b)
    ur_s[...] = (gr * tr + gi * ti).astype(BF16)
    ui_s[...] = (gi * tr - gr * ti).astype(BF16)
    inv = inv_ref[...]

    def last_stage(c, carry):
        y_ref[c] = dot(inv, jnp.concatenate([ur_s[c], ui_s[c]], axis=0))
        return carry

    lax.fori_loop(0, cb, last_stage, 0, unroll=2)


def hyena_long_conv(u_t, filt_t):
    c, l = u_t.shape
    b = FFT_B
    consts = _dft_constants()
    full = lambda a: pl.BlockSpec(a.shape, lambda i: (0,) * a.ndim)
    y = pl.pallas_call(
        _hy_conv_kernel,
        grid=(c // HY_CB,),
        in_specs=[pl.BlockSpec((HY_CB, b // 2, b), lambda i: (i, 0, 0)),
                  pl.BlockSpec((HY_CB, b, b), lambda i: (i, 0, 0))] + [full(a) for a in consts],
        out_specs=pl.BlockSpec((HY_CB, b // 2, b), lambda i: (i, 0, 0)),
        out_shape=jax.ShapeDtypeStruct((c, b // 2, b), F32),
        scratch_shapes=[pltpu.VMEM((HY_CB, b, b), BF16) for _ in range(4)],
        compiler_params=_cparams("parallel"),
        name="hyena_long_conv",
    )(u_t.reshape(c, b // 2, b), filt_t.reshape(c, b, b), *consts)
    return y.reshape(c, l)


def _hy_ctx_conv_kernel(u_ref, f_ref, fwdu_ref, fwdf_ref, inv_ref, y_ref):
    dot = functools.partial(jnp.dot, preferred_element_type=F32)
    n = f_ref.shape[0]
    x = dot(fwdu_ref[...], u_ref[...].astype(BF16))
    h = dot(fwdf_ref[...], f_ref[...].astype(BF16))
    xr, xi, hr, hi = x[:n], x[n:], h[:n], h[n:]
    z = jnp.concatenate([xr * hr - xi * hi, xr * hi + xi * hr], axis=0).astype(BF16)
    y_ref[...] = dot(inv_ref[...], z)


def hyena_ctx_conv(u, filt):
    l, c = u.shape
    n = 2 * l
    idx = np.arange(n)
    ang = 2.0 * np.pi * np.outer(idx, idx) / n
    cm, sm = np.cos(ang), np.sin(ang)
    fwdf = np.concatenate([cm, -sm], axis=0)
    inv = np.concatenate([cm[:l], -sm[:l]], axis=1) / n
    consts = (jnp.asarray(fwdf[:, :l], BF16), jnp.asarray(fwdf, BF16), jnp.asarray(inv, BF16))
    tc = 256
    full = lambda a: pl.BlockSpec(a.shape, lambda i: (0,) * a.ndim)
    return pl.pallas_call(
        _hy_ctx_conv_kernel,
        grid=(c // tc,),
        in_specs=[pl.BlockSpec((l, tc), lambda i: (0, i)), pl.BlockSpec((n, tc), lambda i: (0, i))]
        + [full(a) for a in consts],
        out_specs=pl.BlockSpec((l, tc), lambda i: (0, i)),
        out_shape=jax.ShapeDtypeStruct((l, c), F32),
        compiler_params=_cparams("parallel"),
        name="hyena_ctx_conv",
    )(u, filt, *consts)


def _hy_post_kernel(y_ref, u_ref, x0_ref, b_ref, o_ref, *, transposed):
    w = y_ref[...] + u_ref[...] * b_ref[...]
    if transposed:
        w = w.T
    o_ref[...] = (w * x0_ref[...]).astype(o_ref.dtype)


def hyena_post(y, u, x0, bias, transposed):
    l, w = x0.shape
    tb = min(HY_TB, l)
    tm = pl.BlockSpec((tb, w), lambda i: (i, 0))
    yu = pl.BlockSpec((w, tb), lambda i: (0, i)) if transposed else tm
    return pl.pallas_call(
        functools.partial(_hy_post_kernel, transposed=transposed),
        grid=(l // tb,),
        in_specs=[yu, yu, tm, pl.BlockSpec(bias.shape, lambda i: (0, 0))],
        out_specs=tm,
        out_shape=jax.ShapeDtypeStruct((l, w), BF16),
        compiler_params=_cparams("parallel"),
        name="hyena_post",
    )(y, u, x0, bias)


def hyena(z, short_w, short_b, filt, bias):
    l = z.shape[0]
    filt_t, hb0 = hyena_filter(l, *filt)
    if l == FFT_B * FFT_B // 2:
        u_t, x0 = hyena_pre(z, short_w, short_b, True)
        y_t = hyena_long_conv(u_t, filt_t)
        return hyena_post(y_t, u_t, x0, bias.reshape(-1, 1) + hb0, True)
    u, x0 = hyena_pre(z, short_w, short_b, False)
    y = hyena_ctx_conv(u, filt_t.T)
    return hyena_post(y, u, x0, (bias.reshape(-1, 1) + hb0).T, False)


def kernel(x, c, ctx, c_ctx, w_mod, b_mod, norm1_w, norm2_w, ffn_w_gate, ffn_w_up, ffn_w_down, even_w_in, even_w_out, dn_conv_w, dn_a_log, dn_dt_bias, dn_onorm_w, hy_short_w, hy_short_b, hy_f_w1, hy_f_b1, hy_f_w2, hy_f_b2, hy_f_w3, hy_f_b3, hy_f_w4, hy_f_freq, hy_bias, na_w_qkv, na_rpb, na_w_out, final_norm_w):
    d = D_MODEL
    xl, xc = x[0], ctx[0]
    mod = modulation(jnp.concatenate([c, c_ctx[None]], axis=0), w_mod, b_mod)

    def chunk(layer, row, idx):
        return mod[layer, row:row + 1, idx * d:(idx + 1) * d]

    for layer in range(DEPTH):
        need_ctx = layer < DEPTH - 1
        ml = [chunk(layer, 0, j) for j in range(N_MOD)]
        mc = [chunk(layer, 1, j) for j in range(N_MOD)]
        n1 = norm1_w[layer][None]
        n2 = norm2_w[layer][None]
        i = layer // 2
        if layer % 2 == 0:
            w_in = even_w_in[i]
            w_in = jnp.concatenate([w_in[:, :4 * DN_WIDTH], w_in[:, DN_IN:], w_in[:, 4 * DN_WIDTH:DN_IN]],
                                   axis=1).astype(BF16)
            zl = norm_mod_matmul(xl, n1, ml[1], ml[0], w_in, F32)
            zc = norm_mod_matmul(xc, n1, mc[1], mc[0], w_in, F32)
            dn_c, dn_l = gated_deltanet(zc, zl, dn_conv_w[i], dn_a_log[i], dn_dt_bias[i], dn_onorm_w[i])
            filt = (hy_f_w1[i], hy_f_b1[i], hy_f_w2[i], hy_f_b2[i], hy_f_w3[i], hy_f_b3[i], hy_f_w4[i],
                    hy_f_freq[i])
            yl = (dn_l, hyena(zl, hy_short_w[i], hy_short_b[i], filt, hy_bias[i]))
            w_out = even_w_out[i].astype(BF16)
            if need_ctx:
                yc = (dn_c, hyena(zc, hy_short_w[i], hy_short_b[i], filt, hy_bias[i]))
        else:
            w_qkv = na_w_qkv[i]
            zl = norm_mod_matmul(xl, n1, ml[1], ml[0], w_qkv, BF16)
            zc = norm_mod_matmul(xc, n1, mc[1], mc[0], w_qkv, BF16)
            ya = na_attention(zl, zc, na_bias_tiles(na_rpb[i]))
            yl = (ya, ya)
            w_out = na_w_out[i].astype(BF16)
            if need_ctx:
                ya = ctx_attention(zc)
                yc = (ya, ya)
        xl = proj_residual(*yl, w_out, xl, ml[2])
        xl = ffn_block(xl, n2, ml[4], ml[3], ml[5], ffn_w_gate[layer], ffn_w_up[layer], ffn_w_down[layer])
        if need_ctx:
            xc = proj_residual(*yc, w_out, xc, mc[2])
            xc = ffn_block(xc, n2, mc[4], mc[3], mc[5], ffn_w_gate[layer], ffn_w_up[layer], ffn_w_down[layer])
    return final_norm(xl, final_norm_w[None])[None]
```

```python
import functools
import math

import jax
import jax.numpy as jnp
import numpy as np
from jax import lax
from jax.experimental import pallas as pl
from jax.experimental.pallas import tpu as pltpu

D_MODEL = 2048
SEQ = 8192
DEPTH = 4
GRID_W = 64
ROWS = SEQ // GRID_W
CTX_LEN = 256
NORM_EPS = 1e-6
N_MOD = 6

DN_HEADS = 8
DN_DIM = 128
DN_WIDTH = DN_HEADS * DN_DIM
DN_CONV = 5
DN_CHUNK = 64
DN_IN = 4 * DN_WIDTH + 4 * DN_HEADS

HY_WIDTH = D_MODEL - DN_WIDTH
HY_BANDS = 16
HY_DECAY_TARGET = 1e-2
HY_STRONG_DECAY_PCT = 0.3
HY_WEAK_DECAY_PCT = 1.5
HY_IN = 3 * HY_WIDTH
EVEN_IN = DN_IN + HY_IN

NA_HEADS = D_MODEL // 128
NA_DIM = 128
WIN_R = 8
WIN_C = 16
FFN_HIDDEN = -(-8 * D_MODEL // (3 * 256)) * 256

LANES = 128
VMEM_LIMIT = 56 * 1024 * 1024
NEG = -1e30

F32 = jnp.float32
BF16 = jnp.bfloat16


def _cparams(*sem):
    return pltpu.CompilerParams(dimension_semantics=sem, vmem_limit_bytes=VMEM_LIMIT)


def _sigmoid(x):
    return 1.0 / (1.0 + jnp.exp(-x))


MOD_TN = 1024


def _mod_kernel(c_ref, w_ref, b_ref, o_ref):
    rows = []
    for r in range(2):
        c = c_ref[r]
        a = c * _sigmoid(c)
        cols = [jnp.sum(w_ref[0, :, j * LANES:(j + 1) * LANES] * a, axis=0, keepdims=True)
                for j in range(MOD_TN // LANES)]
        rows.append(jnp.concatenate(cols, axis=1))
    o_ref[0] = jnp.concatenate(rows, axis=0) + b_ref[0]


def modulation(c_pair, w_mod, b_mod):
    n = N_MOD * D_MODEL
    cb = jnp.broadcast_to(c_pair[:, :, None], (2, D_MODEL, LANES))
    return pl.pallas_call(
        _mod_kernel,
        grid=(DEPTH, n // MOD_TN),
        in_specs=[pl.BlockSpec((2, D_MODEL, LANES), lambda l, j: (0, 0, 0)),
                  pl.BlockSpec((1, D_MODEL, MOD_TN), lambda l, j: (l, 0, j)),
                  pl.BlockSpec((1, 1, MOD_TN), lambda l, j: (l, 0, j))],
        out_specs=pl.BlockSpec((1, 2, MOD_TN), lambda l, j: (l, 0, j)),
        out_shape=jax.ShapeDtypeStruct((DEPTH, 2, n), F32),
        compiler_params=_cparams("parallel", "parallel"),
        name="modulation",
    )(cb, w_mod, b_mod.reshape(DEPTH, 1, n))


NORM_ROWS = 256


def _norm_mod_to(h_ref, x_ref, nw_ref, sc_ref, sh_ref):
    tm = x_ref.shape[0]
    gain = nw_ref[...]
    scale = 1.0 + sc_ref[...]
    shift = sh_ref[...]

    def body(i, carry):
        rows = pl.ds(pl.multiple_of(i * NORM_ROWS, NORM_ROWS), NORM_ROWS)
        x = x_ref[rows, :]
        y = x * lax.rsqrt(jnp.mean(x * x, axis=-1, keepdims=True) + NORM_EPS)
        h_ref[rows, :] = (y * gain * scale + shift).astype(BF16)
        return carry

    lax.fori_loop(0, tm // NORM_ROWS, body, 0)


def _nmm_kernel(x_ref, nw_ref, sc_ref, sh_ref, w_ref, o_ref, h_ref):
    @pl.when(pl.program_id(1) == 0)
    def _():
        _norm_mod_to(h_ref, x_ref, nw_ref, sc_ref, sh_ref)

    o_ref[...] = jnp.dot(h_ref[...], w_ref[...].astype(BF16),
                         preferred_element_type=F32).astype(o_ref.dtype)


def norm_mod_matmul(x, nw, sc, sh, w, out_dtype, layer=None, tn=512):
    m, d = x.shape
    n = w.shape[-1]
    tm = min(m, 1024)
    vec = pl.BlockSpec((1, d), lambda i, j: (0, 0))
    w_spec = (pl.BlockSpec((d, tn), lambda i, j: (0, j)) if layer is None
              else pl.BlockSpec((None, d, tn), lambda i, j: (layer, 0, j)))
    return pl.pallas_call(
        _nmm_kernel,
        grid=(m // tm, pl.cdiv(n, tn)),
        in_specs=[pl.BlockSpec((tm, d), lambda i, j: (i, 0)), vec, vec, vec, w_spec],
        out_specs=pl.BlockSpec((tm, tn), lambda i, j: (i, j)),
        out_shape=jax.ShapeDtypeStruct((m, n), out_dtype),
        scratch_shapes=[pltpu.VMEM((tm, d), BF16)],
        compiler_params=_cparams("parallel", "arbitrary"),
        name="norm_mod_matmul",
    )(x, nw, sc, sh, w)


FFN_TF = 512
FFN_TN = 256


def _ffn_up_kernel(x_ref, nw_ref, sc_ref, sh_ref, wg_ref, wu_ref, a_ref, h_ref):
    @pl.when(pl.program_id(1) == 0)
    def _():
        _norm_mod_to(h_ref, x_ref, nw_ref, sc_ref, sh_ref)

    h = h_ref[...]
    gate = jnp.dot(h, wg_ref[...].astype(BF16), preferred_element_type=F32)
    up = jnp.dot(h, wu_ref[...].astype(BF16), preferred_element_type=F32)
    a_ref[...] = (gate * _sigmoid(gate) * up).astype(a_ref.dtype)


def _ffn_down_kernel(a_ref, wd_ref, x_ref, g_ref, o_ref):
    y = jnp.dot(a_ref[...], wd_ref[...].astype(BF16), preferred_element_type=F32)
    o_ref[...] = x_ref[...] + g_ref[...] * y


def ffn_block(x, nw, sc, sh, g, wg, wu, wd, layer):
    m, d = x.shape
    hid = wg.shape[2]
    tm = min(m, 1024)
    vec = pl.BlockSpec((1, d), lambda i, f: (0, 0))
    a = pl.pallas_call(
        _ffn_up_kernel,
        grid=(m // tm, hid // FFN_TF),
        in_specs=[pl.BlockSpec((tm, d), lambda i, f: (i, 0)), vec, vec, vec,
                  pl.BlockSpec((None, d, FFN_TF), lambda i, f: (layer, 0, f)),
                  pl.BlockSpec((None, d, FFN_TF), lambda i, f: (layer, 0, f))],
        out_specs=pl.BlockSpec((tm, FFN_TF), lambda i, f: (i, f)),
        out_shape=jax.ShapeDtypeStruct((m, hid), BF16),
        scratch_shapes=[pltpu.VMEM((tm, d), BF16)],
        compiler_params=_cparams("parallel", "arbitrary"),
        name="ffn_up",
    )(x, nw, sc, sh, wg, wu)
    return pl.pallas_call(
        _ffn_down_kernel,
        grid=(m // tm, d // FFN_TN),
        in_specs=[pl.BlockSpec((tm, hid), lambda i, j: (i, 0)),
                  pl.BlockSpec((None, hid, FFN_TN), lambda i, j: (layer, 0, j)),
                  pl.BlockSpec((tm, FFN_TN), lambda i, j: (i, j)),
                  pl.BlockSpec((1, FFN_TN), lambda i, j: (0, j))],
        out_specs=pl.BlockSpec((tm, FFN_TN), lambda i, j: (i, j)),
        out_shape=jax.ShapeDtypeStruct((m, d), F32),
        compiler_params=_cparams("parallel", "arbitrary"),
        name="ffn_down",
    )(a, wd, x, g)


def _proj_res_kernel(ya_ref, yb_ref, w_ref, x_ref, g_ref, o_ref):
    ka = ya_ref.shape[1]
    y = (jnp.dot(ya_ref[...], w_ref[:ka, :], preferred_element_type=F32)
         + jnp.dot(yb_ref[...], w_ref[ka:, :], preferred_element_type=F32))
    o_ref[...] = x_ref[...] + g_ref[...] * y


def proj_residual(ya, yb, w_bf16, x, g):
    m, d = x.shape
    ka = kb = d // 2
    second = 1 if yb is ya else 0
    tm = min(m, 512)
    return pl.pallas_call(
        _proj_res_kernel,
        grid=(m // tm,),
        in_specs=[pl.BlockSpec((tm, ka), lambda i: (i, 0)),
                  pl.BlockSpec((tm, kb), lambda i: (i, second)),
                  pl.BlockSpec((d, d), lambda i: (0, 0)),
                  pl.BlockSpec((tm, d), lambda i: (i, 0)),
                  pl.BlockSpec((1, d), lambda i: (0, 0))],
        out_specs=pl.BlockSpec((tm, d), lambda i: (i, 0)),
        out_shape=jax.ShapeDtypeStruct((m, d), F32),
        compiler_params=_cparams("parallel"),
        name="proj_residual",
    )(ya, yb, w_bf16, x, g)


def _final_norm_kernel(x_ref, w_ref, o_ref):
    x = x_ref[...]
    o_ref[...] = x * lax.rsqrt(jnp.mean(x * x, axis=-1, keepdims=True) + NORM_EPS) * w_ref[...]


def final_norm(x, w):
    m, d = x.shape
    tm = 512
    return pl.pallas_call(
        _final_norm_kernel,
        grid=(m // tm,),
        in_specs=[pl.BlockSpec((tm, d), lambda i: (i, 0)), pl.BlockSpec((1, d), lambda i: (0, 0))],
        out_specs=pl.BlockSpec((tm, d), lambda i: (i, 0)),
        out_shape=jax.ShapeDtypeStruct((m, d), F32),
        compiler_params=_cparams("parallel"),
        name="final_norm",
    )(x, w)


NA_R = 8
NA_KD = (NA_R + WIN_R) // 2
NA_T = 2 * WIN_R


def _na_bias_kernel(rpb_ref, o_ref):
    h = pl.program_id(0)
    c = lax.broadcasted_iota(jnp.int32, (GRID_W, LANES), 0)
    lane = lax.broadcasted_iota(jnp.int32, (GRID_W, LANES), 1)
    left = lane < GRID_W
    kc = jnp.where(left, lane, lane - GRID_W)
    cs = jnp.clip(c - WIN_C // 2, 0, GRID_W - WIN_C)
    inwin = (kc >= cs) & (kc < cs + WIN_C)
    diff = kc - c + (WIN_C - 1)
    n_ro, n_co = 2 * WIN_R - 1, 2 * WIN_C - 1
    for t in range(NA_T):
        def body(d, acc, t=t):
            vl = rpb_ref[(h * n_ro + (t - 1)) * n_co + d] if t >= 1 else jnp.float32(NEG)
            vr = rpb_ref[(h * n_ro + t) * n_co + d] if t < n_ro else jnp.float32(NEG)
            return jnp.where(diff == d, jnp.where(left, vl, vr), acc)

        acc = lax.fori_loop(0, n_co, body, jnp.full((GRID_W, LANES), NEG, F32))
        o_ref[0, t] = jnp.where(inwin, acc, NEG)


def na_bias_tiles(rpb):
    return pl.pallas_call(
        _na_bias_kernel,
        grid=(NA_HEADS,),
        in_specs=[pl.BlockSpec(memory_space=pltpu.SMEM)],
        out_specs=pl.BlockSpec((1, NA_T, GRID_W, LANES), lambda h: (h, 0, 0, 0)),
        out_shape=jax.ShapeDtypeStruct((NA_HEADS, NA_T, GRID_W, LANES), F32),
        compiler_params=_cparams("parallel"),
        name="na_bias_tiles",
    )(rpb.reshape(-1))


def _na_kernel(q_ref, k_ref, v_ref, kc_ref, vc_ref, tt_ref, o_ref):
    scale = NA_DIM ** -0.5
    r0 = pl.program_id(1) * NA_R
    ks = jnp.clip(r0 - WIN_R // 2, 0, ROWS - 2 * NA_KD)
    kstart = pl.multiple_of(ks * GRID_W, LANES)
    kspan = k_ref[pl.ds(kstart, NA_KD * LANES), :]
    vspan = v_ref[pl.ds(kstart, NA_KD * LANES), :]
    q = q_ref[...]
    contract_last = (((1,), (1,)), ((), ()))
    s_loc = lax.dot_general(q, kspan, contract_last, preferred_element_type=F32) * scale
    s_ctx = lax.dot_general(q, kc_ref[...], contract_last, preferred_element_type=F32) * scale

    left = lax.broadcasted_iota(jnp.int32, (GRID_W, LANES), 1) < GRID_W
    bias_rows = []
    for qi in range(NA_R):
        r = r0 + qi
        rs = jnp.clip(r - WIN_R // 2, 0, ROWS - WIN_R)
        tiles = []
        for dj in range(NA_KD):
            kl = ks + 2 * dj
            vl = ((kl >= rs) & (kl < rs + WIN_R)).astype(jnp.int32)
            vr = ((kl + 1 >= rs) & (kl + 1 < rs + WIN_R)).astype(jnp.int32)
            t = jnp.clip(kl - r + WIN_R, 0, NA_T - 1)
            valid = jnp.where(left, vl, vr) > 0
            tiles.append(jnp.where(valid, tt_ref[0, t], NEG))
        bias_rows.append(jnp.concatenate(tiles, axis=1))
    s_loc = s_loc + jnp.concatenate(bias_rows, axis=0)

    m = jnp.maximum(jnp.max(s_loc, axis=-1, keepdims=True), jnp.max(s_ctx, axis=-1, keepdims=True))
    p_loc = jnp.exp(s_loc - m)
    p_ctx = jnp.exp(s_ctx - m)
    denom = jnp.sum(p_loc, axis=-1, keepdims=True) + jnp.sum(p_ctx, axis=-1, keepdims=True)
    o = (jnp.dot(p_loc.astype(BF16), vspan, preferred_element_type=F32)
         + jnp.dot(p_ctx.astype(BF16), vc_ref[...], preferred_element_type=F32))
    o_ref[...] = (o / denom).astype(o_ref.dtype)


def na_attention(zl, zc, tiles):
    tq = NA_R * GRID_W
    nh = NA_HEADS
    return pl.pallas_call(
        _na_kernel,
        grid=(nh, ROWS // NA_R),
        in_specs=[pl.BlockSpec((tq, NA_DIM), lambda h, i: (i, h)),
                  pl.BlockSpec((SEQ, NA_DIM), lambda h, i: (0, nh + h)),
                  pl.BlockSpec((SEQ, NA_DIM), lambda h, i: (0, 2 * nh + h)),
                  pl.BlockSpec((CTX_LEN, NA_DIM), lambda h, i: (0, nh + h)),
                  pl.BlockSpec((CTX_LEN, NA_DIM), lambda h, i: (0, 2 * nh + h)),
                  pl.BlockSpec((1, NA_T, GRID_W, LANES), lambda h, i: (h, 0, 0, 0))],
        out_specs=pl.BlockSpec((tq, NA_DIM), lambda h, i: (i, h)),
        out_shape=jax.ShapeDtypeStruct((SEQ, D_MODEL), BF16),
        compiler_params=_cparams("parallel", "arbitrary"),
        name="na_attention",
    )(zl, zl, zl, zc, zc, tiles)


def _ctx_attn_kernel(q_ref, k_ref, v_ref, o_ref):
    scale = NA_DIM ** -0.5
    s = lax.dot_general(q_ref[...], k_ref[...], (((1,), (1,)), ((), ())), preferred_element_type=F32) * scale
    p = jnp.exp(s - jnp.max(s, axis=-1, keepdims=True))
    o = jnp.dot(p.astype(BF16), v_ref[...], preferred_element_type=F32)
    o_ref[...] = (o / jnp.sum(p, axis=-1, keepdims=True)).astype(o_ref.dtype)


def ctx_attention(zc):
    nh = NA_HEADS
    blk = lambda off: pl.BlockSpec((CTX_LEN, NA_DIM), lambda h: (0, off + h))
    return pl.pallas_call(
        _ctx_attn_kernel,
        grid=(nh,),
        in_specs=[blk(0), blk(nh), blk(2 * nh)],
        out_specs=blk(0),
        out_shape=jax.ShapeDtypeStruct((CTX_LEN, D_MODEL), BF16),
        compiler_params=_cparams("parallel"),
        name="ctx_attention",
    )(zc, zc, zc)


DN_SC = 256
HALO = 8
N_GB = 2 * DN_HEADS


def _softplus(x):
    return jnp.maximum(x, 0.0) + jnp.log1p(jnp.exp(-jnp.abs(x)))


def _short_conv(main_ref, prev_ref, next_ref, w_ref, col0, blk, nblk):
    ksize = w_ref.shape[0]
    tb, width = main_ref.shape
    prev = jnp.where(blk > 0, prev_ref[...], 0.0)
    nxt = jnp.where(blk < nblk - 1, next_ref[...], 0.0)
    xf = jnp.concatenate([prev, main_ref[...], nxt], axis=0)
    n = xf.shape[0]
    acc = None
    for j in range(ksize):
        sh = (ksize // 2 - j) % n
        xs = xf if sh == 0 else pltpu.roll(xf, sh, 0)
        term = xs[HALO:HALO + tb] * w_ref[j:j + 1, col0:col0 + width]
        acc = term if acc is None else acc + term
    return acc


def _chunk_masks(n):
    r = lax.broadcasted_iota(jnp.int32, (n, n), 0)
    c = lax.broadcasted_iota(jnp.int32, (n, n), 1)
    return r, c, (r // DN_CHUNK) == (c // DN_CHUNK)


def _dn_scalars_kernel(za_ref, zb_ref, zat_ref, alog_ref, dtb_ref, alogt_ref, dtbt_ref,
                       gc_ref, gt_ref, beta_ref, gct_ref):
    hi = lax.Precision.HIGHEST
    g = -jnp.exp(alog_ref[...]) * _softplus(za_ref[...] + dtb_ref[...])
    beta_ref[...] = _sigmoid(zb_ref[...])
    r, c, same = _chunk_masks(DN_SC)
    lo = (same & (c <= r)).astype(F32)
    up = (same & (c >= r)).astype(F32)
    fwd_col = lax.broadcasted_iota(jnp.int32, (DN_SC, N_GB), 1) < DN_HEADS
    gc_ref[...] = jnp.where(fwd_col, jnp.dot(lo, g, precision=hi), jnp.dot(up, g, precision=hi))
    gt_ref[...] = jnp.dot(same.astype(F32), g, precision=hi)
    g_t = -jnp.exp(alogt_ref[...]) * _softplus(zat_ref[...] + dtbt_ref[...])
    fwd_row = lax.broadcasted_iota(jnp.int32, (N_GB, DN_SC), 0) < DN_HEADS
    gct_ref[...] = jnp.where(fwd_row, jnp.dot(g_t, up, precision=hi), jnp.dot(g_t, lo, precision=hi))


def dn_scalars(za, zb, a_log, dt_bias):
    l = za.shape[0]
    col = pl.BlockSpec((DN_SC, N_GB), lambda i: (i, 0))
    row = pl.BlockSpec((N_GB, DN_SC), lambda i: (0, i))
    prow = pl.BlockSpec((1, N_GB), lambda i: (0, 0))
    pcol = pl.BlockSpec((N_GB, 1), lambda i: (0, 0))
    cshape = jax.ShapeDtypeStruct((l, N_GB), F32)
    return pl.pallas_call(
        _dn_scalars_kernel,
        grid=(l // DN_SC,),
        in_specs=[col, col, row, prow, prow, pcol, pcol],
        out_specs=[col, col, col, row],
        out_shape=[cshape, cshape, cshape, jax.ShapeDtypeStruct((N_GB, l), F32)],
        compiler_params=_cparams("parallel"),
        name="dn_scalars",
    )(za, zb, za.T, a_log.reshape(1, N_GB), dt_bias.reshape(1, N_GB), a_log.reshape(N_GB, 1),
      dt_bias.reshape(N_GB, 1))


def _gdn_kernel(zq_ref, zqp_ref, zqn_ref, zk_ref, zkp_ref, zkn_ref, zv_ref, zvp_ref, zvn_ref, cw_ref,
                gc_ref, gt_ref, beta_ref, gct_ref, s0_ref, o_ref, sfin_ref, state_ref, *, bwd, nblk):
    i = pl.program_id(0)
    blk = nblk - 1 - i if bwd else i

    @pl.when(i == 0)
    def _():
        state_ref[...] = s0_ref[...]

    def conv_silu(main_ref, prev_ref, next_ref, part):
        acc = _short_conv(main_ref, prev_ref, next_ref, cw_ref, part * DN_WIDTH, blk, nblk)
        return acc * _sigmoid(acc)

    q_all = conv_silu(zq_ref, zqp_ref, zqn_ref, 0)
    k_all = conv_silu(zk_ref, zkp_ref, zkn_ref, 1)
    v_all = conv_silu(zv_ref, zvp_ref, zvn_ref, 2)

    r, c, same64 = _chunk_masks(DN_SC)
    tri = same64 & ((c >= r) if bwd else (c <= r))
    offdiag = r != c
    same16 = (r // 16) == (c // 16)
    same32 = (r // 32) == (c // 32)
    contract_last = (((1,), (1,)), ((), ()))
    contract_first = (((0,), (0,)), ((), ()))
    nc = DN_SC // DN_CHUNK
    dot = functools.partial(jnp.dot, preferred_element_type=F32)

    heads = range(DN_HEADS)
    col0 = DN_HEADS if bwd else 0
    gc_c = [gc_ref[:, col0 + h:col0 + h + 1] for h in heads]
    gt_c = [gt_ref[:, col0 + h:col0 + h + 1] for h in heads]
    beta_c = [beta_ref[:, col0 + h:col0 + h + 1] for h in heads]
    q, k, v, k16, dec, qk16, m, p, e = ([None] * DN_HEADS for _ in range(9))
    for h in heads:
        sl = slice(h * DN_DIM, (h + 1) * DN_DIM)
        qh, kh = q_all[:, sl], k_all[:, sl]
        q[h] = qh * (lax.rsqrt(jnp.sum(qh * qh, axis=-1, keepdims=True) + NORM_EPS) * DN_DIM ** -0.5)
        k[h] = kh * lax.rsqrt(jnp.sum(kh * kh, axis=-1, keepdims=True) + NORM_EPS)
        v[h] = v_all[:, sl]
        k16[h] = k[h].astype(BF16)
        dec[h] = jnp.exp(jnp.where(tri, gc_c[h] - gct_ref[col0 + h:col0 + h + 1, :], NEG))
    for h in heads:
        kk = lax.dot_general(k16[h], k16[h], contract_last, preferred_element_type=F32)
        m[h] = jnp.where(offdiag, kk * beta_c[h] * dec[h], 0.0)
        qk16[h] = (lax.dot_general(q[h].astype(BF16), k16[h], contract_last, preferred_element_type=F32)
                   * dec[h]).astype(BF16)

    for h in heads:
        p[h] = jnp.where(same16, -m[h], 0.0)
        e[h] = p[h]
    for _ in range(3):
        for h in heads:
            p16 = p[h].astype(BF16)
            p[h] = dot(p16, p16)
        for h in heads:
            e[h] = e[h] + p[h] + dot(e[h].astype(BF16), p[h].astype(BF16))
    for inner, outer in ((same16, same32), (same32, None)):
        y = [None] * DN_HEADS
        for h in heads:
            cm = jnp.where(~inner if outer is None else (outer & ~inner), m[h], 0.0)
            y[h] = cm + dot(cm.astype(BF16), e[h].astype(BF16))
        for h in heads:
            e[h] = e[h] - y[h] - dot(e[h].astype(BF16), y[h].astype(BF16))

    u, w16, qg16, kd16, s = ([None] * DN_HEADS for _ in range(5))
    for h in heads:
        eg = jnp.exp(gc_c[h])
        rhs = jnp.concatenate([v[h] * beta_c[h], k[h] * (beta_c[h] * eg)], axis=1)
        uw = rhs + dot(e[h].astype(BF16), rhs.astype(BF16))
        u[h] = uw[:, :DN_DIM]
        w16[h] = uw[:, DN_DIM:].astype(BF16)
        qg16[h] = (q[h] * eg).astype(BF16)
        kd16[h] = (k[h] * jnp.exp(gt_c[h] - gc_c[h])).astype(BF16)
        s[h] = state_ref[h]

    outs = [[None] * nc for _ in heads]
    for ci in (range(nc - 1, -1, -1) if bwd else range(nc)):
        rows = slice(ci * DN_CHUNK, (ci + 1) * DN_CHUNK)
        ws = [dot(jnp.concatenate([w16[h][rows], qg16[h][rows]], axis=0), s[h].astype(BF16)) for h in heads]
        vn16 = [(u[h][rows] - ws[h][:DN_CHUNK]).astype(BF16) for h in heads]
        for h in heads:
            outs[h][ci] = ws[h][DN_CHUNK:] + dot(qk16[h][rows, ci * DN_CHUNK:(ci + 1) * DN_CHUNK], vn16[h])
        for h in heads:
            gl = jnp.exp(gt_c[h][ci * DN_CHUNK:ci * DN_CHUNK + 1, :])
            s[h] = s[h] * gl + lax.dot_general(kd16[h][rows], vn16[h], contract_first, preferred_element_type=F32)
    for h in heads:
        state_ref[h] = s[h]
        o_ref[:, h * DN_DIM:(h + 1) * DN_DIM] = jnp.concatenate(outs[h], axis=0)

    @pl.when(i == nblk - 1)
    def _():
        sfin_ref[...] = state_ref[...]


def gdn_scan(z, conv_w, gc, gt, beta, gct, s0, bwd):
    l = z.shape[0]
    nblk = l // DN_SC
    per = DN_SC // HALO

    def b(i):
        return nblk - 1 - i if bwd else i

    def part_specs(part):
        return [pl.BlockSpec((DN_SC, DN_WIDTH), lambda i: (b(i), part)),
                pl.BlockSpec((HALO, DN_WIDTH), lambda i: (jnp.maximum(b(i) * per - 1, 0), part)),
                pl.BlockSpec((HALO, DN_WIDTH), lambda i: (jnp.minimum((b(i) + 1) * per, l // HALO - 1), part))]

    col = pl.BlockSpec((DN_SC, N_GB), lambda i: (b(i), 0))
    st = pl.BlockSpec((DN_HEADS, DN_DIM, DN_DIM), lambda i: (0, 0, 0))
    return pl.pallas_call(
        functools.partial(_gdn_kernel, bwd=bwd, nblk=nblk),
        grid=(nblk,),
        in_specs=part_specs(0) + part_specs(1) + part_specs(2) + [
            pl.BlockSpec((DN_CONV, 3 * DN_WIDTH), lambda i: (0, 0)), col, col, col,
            pl.BlockSpec((N_GB, DN_SC), lambda i: (0, b(i))), st],
        out_specs=[pl.BlockSpec((DN_SC, DN_WIDTH), lambda i: (b(i), 0)), st],
        out_shape=[jax.ShapeDtypeStruct((l, DN_WIDTH), F32),
                   jax.ShapeDtypeStruct((DN_HEADS, DN_DIM, DN_DIM), F32)],
        scratch_shapes=[pltpu.VMEM((DN_HEADS, DN_DIM, DN_DIM), F32)],
        compiler_params=_cparams("arbitrary"),
        name="gdn_scan_bwd" if bwd else "gdn_scan_fwd",
    )(z, z, z, z, z, z, z, z, z, conv_w, gc, gt, beta, gct, s0)


def _dn_out_kernel(of_ref, ob_ref, gate_ref, nw_ref, y_ref):
    for h in range(DN_HEADS):
        sl = slice(h * DN_DIM, (h + 1) * DN_DIM)
        o = of_ref[:, sl] + ob_ref[:, sl]
        o = o * lax.rsqrt(jnp.mean(o * o, axis=-1, keepdims=True) + NORM_EPS) * nw_ref[...]
        gate = gate_ref[:, sl]
        y_ref[:, sl] = (o * (gate * _sigmoid(gate))).astype(y_ref.dtype)


def dn_output(o_f, o_b, z, onorm_w):
    l = z.shape[0]
    tm = min(l, 512)
    blk = pl.BlockSpec((tm, DN_WIDTH), lambda i: (i, 0))
    return pl.pallas_call(
        _dn_out_kernel,
        grid=(l // tm,),
        in_specs=[blk, blk, pl.BlockSpec((tm, DN_WIDTH), lambda i: (i, 3)),
                  pl.BlockSpec((1, DN_DIM), lambda i: (0, 0))],
        out_specs=blk,
        out_shape=jax.ShapeDtypeStruct((l, DN_WIDTH), BF16),
        compiler_params=_cparams("parallel"),
        name="dn_output",
    )(o_f, o_b, z, onorm_w.reshape(1, DN_DIM))


def gated_deltanet(zc, zl, conv_w, a_log, dt_bias, onorm_w):
    a0, b0 = EVEN_IN - 2 * N_GB, EVEN_IN - N_GB
    sc_c = dn_scalars(zc[:, a0:b0], zc[:, b0:b0 + N_GB], a_log, dt_bias)
    sc_l = dn_scalars(zl[:, a0:b0], zl[:, b0:b0 + N_GB], a_log, dt_bias)
    s0 = jnp.zeros((DN_HEADS, DN_DIM, DN_DIM), F32)
    outs_c, outs_l = [], []
    for bwd in (False, True):
        oc, s_ctx = gdn_scan(zc, conv_w, *sc_c, s0, bwd)
        ol, _ = gdn_scan(zl, conv_w, *sc_l, s_ctx, bwd)
        outs_c.append(oc)
        outs_l.append(ol)
    return dn_output(*outs_c, zc, onorm_w), dn_output(*outs_l, zl, onorm_w)


HY_SHORT = 3
HY_EMB = 1 + 2 * HY_BANDS
HY_EMB_PAD = 40
HY_TB = 256
FFT_B = 128
HY_CB = 16
HY_COL0 = 4


def _hy_pre_kernel(x0_ref, x0p_ref, x0n_ref, x1_ref, x1p_ref, x1n_ref, v_ref, vp_ref, vn_ref, w_ref, b_ref,
                   u_ref, x0c_ref, *, nblk, transpose_u):
    blk = pl.program_id(0)
    w = HY_WIDTH
    x0 = _short_conv(x0_ref, x0p_ref, x0n_ref, w_ref, 0, blk, nblk) + b_ref[:, :w]
    x1 = _short_conv(x1_ref, x1p_ref, x1n_ref, w_ref, w, blk, nblk) + b_ref[:, w:2 * w]
    v = _short_conv(v_ref, vp_ref, vn_ref, w_ref, 2 * w, blk, nblk) + b_ref[:, 2 * w:]
    u = v * x1
    x0c_ref[...] = x0
    u_ref[...] = u.T if transpose_u else u


def hyena_pre(z, short_w, short_b, transpose_u):
    l = z.shape[0]
    tb = min(HY_TB, l)
    nblk = l // tb
    per = tb // HALO
    w = HY_WIDTH

    def part_specs(part):
        cb = HY_COL0 + part
        return [pl.BlockSpec((tb, w), lambda i: (i, cb)),
                pl.BlockSpec((HALO, w), lambda i: (jnp.maximum(i * per - 1, 0), cb)),
                pl.BlockSpec((HALO, w), lambda i: (jnp.minimum((i + 1) * per, l // HALO - 1), cb))]

    u_shape, u_spec = ((w, l), pl.BlockSpec((w, tb), lambda i: (0, i))) if transpose_u else (
        (l, w), pl.BlockSpec((tb, w), lambda i: (i, 0)))
    return pl.pallas_call(
        functools.partial(_hy_pre_kernel, nblk=nblk, transpose_u=transpose_u),
        grid=(nblk,),
        in_specs=part_specs(0) + part_specs(1) + part_specs(2) + [
            pl.BlockSpec((HY_SHORT, HY_IN), lambda i: (0, 0)), pl.BlockSpec((1, HY_IN), lambda i: (0, 0))],
        out_specs=[u_spec, pl.BlockSpec((tb, w), lambda i: (i, 0))],
        out_shape=[jax.ShapeDtypeStruct(u_shape, F32), jax.ShapeDtypeStruct((l, w), F32)],
        compiler_params=_cparams("parallel"),
        name="hyena_pre",
    )(z, z, z, z, z, z, z, z, z, short_w, short_b.reshape(1, HY_IN))


def _hy_filter_kernel(w1t_ref, b1_ref, w2t_ref, b2_ref, w3t_ref, b3_ref, w4t_ref, fr_ref, band_ref, dl_ref,
                      f_ref, hb0_ref, *, l, fb):
    hi = lax.Precision.HIGHEST
    j = pl.program_id(0)
    second = j >= l // fb
    n = j * fb + lax.broadcasted_iota(jnp.int32, (1, fb), 1)
    pos = jnp.where(second, 2 * l - n, n).astype(F32)
    t = pos / max(l - 1, 1)
    wpos = 2.0 * math.pi * pos / l
    arg = band_ref[...] * wpos
    row = lax.broadcasted_iota(jnp.int32, (HY_EMB_PAD, fb), 0)
    feat = jnp.where(row == 0, t, jnp.where(row <= HY_BANDS, jnp.cos(arg),
                                            jnp.where(row <= 2 * HY_BANDS, -jnp.sin(arg), 0.0)))
    fr = fr_ref[...]
    h = jnp.sin(fr * (jnp.dot(w1t_ref[...], feat, precision=hi) + b1_ref[...]))
    h = jnp.sin(fr * (jnp.dot(w2t_ref[...], h, precision=hi) + b2_ref[...]))
    h = jnp.sin(fr * (jnp.dot(w3t_ref[...], h, precision=hi) + b3_ref[...]))
    window = jnp.exp(-t * dl_ref[...])
    half = pl.multiple_of(jnp.where(second, HY_WIDTH, 0), HY_WIDTH)
    f = jnp.dot(w4t_ref[pl.ds(half, HY_WIDTH), :], h, precision=hi) * window
    f_ref[...] = jnp.where(n == l, 0.0, f)

    @pl.when(j == 0)
    def _():
        hb0_ref[...] = (jnp.dot(w4t_ref[HY_WIDTH:, :], h[:, :LANES], precision=hi) * window[:, :LANES])


def hyena_filter(l, w1, b1, w2, b2, w3, b3, w4, freq):
    fb = min(1024, l)
    colv = lambda v: v.reshape(-1, 1)
    bands = np.zeros((HY_EMB_PAD, 1), np.float32)
    bands[1:1 + HY_BANDS, 0] = bands[1 + HY_BANDS:HY_EMB, 0] = np.linspace(1e-4, HY_BANDS - 1, HY_BANDS,
                                                                            dtype=np.float32)
    min_decay = math.log(HY_DECAY_TARGET) / HY_WEAK_DECAY_PCT
    max_decay = math.log(HY_DECAY_TARGET) / HY_STRONG_DECAY_PCT
    deltas = np.abs(np.linspace(min_decay, max_decay, HY_WIDTH, dtype=np.float32)).reshape(-1, 1)
    w1t = jnp.pad(w1.T, ((0, 0), (0, HY_EMB_PAD - HY_EMB)))
    full = lambda a: pl.BlockSpec(a.shape, lambda j: (0,) * a.ndim)
    args = (w1t, colv(b1), w2.T, colv(b2), w3.T, colv(b3), w4.T, colv(freq), jnp.asarray(bands), jnp.asarray(deltas))
    filt, hb0 = pl.pallas_call(
        functools.partial(_hy_filter_kernel, l=l, fb=fb),
        grid=(2 * l // fb,),
        in_specs=[full(a) for a in args],
        out_specs=[pl.BlockSpec((HY_WIDTH, fb), lambda j: (0, j)), pl.BlockSpec((HY_WIDTH, LANES), lambda j: (0, 0))],
        out_shape=[jax.ShapeDtypeStruct((HY_WIDTH, 2 * l), F32), jax.ShapeDtypeStruct((HY_WIDTH, LANES), F32)],
        compiler_params=_cparams("arbitrary"),
        name="hyena_filter",
    )(*args)
    return filt, hb0[:, :1]


def _dft_constants():
    b = FFT_B
    n = b * b
    idx = np.arange(b)
    ang = 2.0 * np.pi * np.outer(idx, idx) / b
    c, s = np.cos(ang), np.sin(ang)
    tw = 2.0 * np.pi * np.outer(idx, idx) / n
    fwd_b = np.concatenate([c, -s], axis=0)
    cs = np.concatenate([c, s], axis=1)
    inv_b = np.concatenate([c[:b // 2], -s[:b // 2]], axis=1) / n
    return (jnp.asarray(fwd_b, BF16), jnp.asarray(cs, BF16), jnp.asarray(np.cos(tw), F32),
            jnp.asarray(-np.sin(tw), F32), jnp.asarray(inv_b, BF16))


def _hy_conv_kernel(u_ref, f_ref, fwd_ref, cs_ref, tr_ref, ti_ref, inv_ref, y_ref, ur_s, ui_s, fr_s, fi_s):
    cb = u_ref.shape[0]
    b = FFT_B
    m = cb * b
    dot = functools.partial(jnp.dot, preferred_element_type=F32)
    fwd = fwd_ref[...]
    fwd_half = fwd[:, :b // 2]
    tr, ti = tr_ref[...], ti_ref[...]

    def first_stage(c, carry):
        for src, lhs, re_s, im_s in ((u_ref, fwd_half, ur_s, ui_s), (f_ref, fwd, fr_s, fi_s)):
            p = dot(lhs, src[c].astype(BF16))
            pr, pi = p[:b], p[b:]
            re_s[c] = (pr * tr - pi * ti).astype(BF16)
            im_s[c] = (pr * ti + pi * tr).astype(BF16)
        return carry

    lax.fori_loop(0, cb, first_stage, 0, unroll=2)

    cs = cs_ref[...]

    def times_cs(re, im):
        big = dot(jnp.concatenate([re, im], axis=0), cs)
        return big[:m, :b], big[:m, b:], big[m:, :b], big[m:, b:]

    def second_stage(re_s, im_s):
        rc, rs, ic, is_ = times_cs(re_s[...].reshape(m, b), im_s[...].reshape(m, b))
        return rc + is_, ic - rs

    xr, xi = second_stage(ur_s, ui_s)
    hr, hi = second_stage(fr_s, fi_s)
    zr = (xr * hr - xi * hi).astype(BF16)
    zi = (xr * hi + xi * hr).astype(BF16)
    rc, rs, ic, is_ = times_cs(zr, zi)
    gr = (rc - is_).reshape(cb, b, b)
    gi = (rs + ic).reshape(cb, b, b)
    ur_s[...] = (gr * tr + gi * ti).astype(BF16)
    ui_s[...] = (gi * tr - gr * ti).astype(BF16)
    inv = inv_ref[...]

    def last_stage(c, carry):
        y_ref[c] = dot(inv, jnp.concatenate([ur_s[c], ui_s[c]], axis=0))
        return carry

    lax.fori_loop(0, cb, last_stage, 0, unroll=2)


def hyena_long_conv(u_t, filt_t):
    c, l = u_t.shape
    b = FFT_B
    consts = _dft_constants()
    full = lambda a: pl.BlockSpec(a.shape, lambda i: (0,) * a.ndim)
    y = pl.pallas_call(
        _hy_conv_kernel,
        grid=(c // HY_CB,),
        in_specs=[pl.BlockSpec((HY_CB, b // 2, b), lambda i: (i, 0, 0)),
                  pl.BlockSpec((HY_CB, b, b), lambda i: (i, 0, 0))] + [full(a) for a in consts],
        out_specs=pl.BlockSpec((HY_CB, b // 2, b), lambda i: (i, 0, 0)),
        out_shape=jax.ShapeDtypeStruct((c, b // 2, b), F32),
        scratch_shapes=[pltpu.VMEM((HY_CB, b, b), BF16) for _ in range(4)],
        compiler_params=_cparams("parallel"),
        name="hyena_long_conv",
    )(u_t.reshape(c, b // 2, b), filt_t.reshape(c, b, b), *consts)
    return y.reshape(c, l)


def _hy_ctx_conv_kernel(u_ref, f_ref, fwdu_ref, fwdf_ref, inv_ref, y_ref):
    dot = functools.partial(jnp.dot, preferred_element_type=F32)
    n = f_ref.shape[0]
    x = dot(fwdu_ref[...], u_ref[...].astype(BF16))
    h = dot(fwdf_ref[...], f_ref[...].astype(BF16))
    xr, xi, hr, hi = x[:n], x[n:], h[:n], h[n:]
    z = jnp.concatenate([xr * hr - xi * hi, xr * hi + xi * hr], axis=0).astype(BF16)
    y_ref[...] = dot(inv_ref[...], z)


def hyena_ctx_conv(u, filt):
    l, c = u.shape
    n = 2 * l
    idx = np.arange(n)
    ang = 2.0 * np.pi * np.outer(idx, idx) / n
    cm, sm = np.cos(ang), np.sin(ang)
    fwdf = np.concatenate([cm, -sm], axis=0)
    inv = np.concatenate([cm[:l], -sm[:l]], axis=1) / n
    consts = (jnp.asarray(fwdf[:, :l], BF16), jnp.asarray(fwdf, BF16), jnp.asarray(inv, BF16))
    tc = 256
    full = lambda a: pl.BlockSpec(a.shape, lambda i: (0,) * a.ndim)
    return pl.pallas_call(
        _hy_ctx_conv_kernel,
        grid=(c // tc,),
        in_specs=[pl.BlockSpec((l, tc), lambda i: (0, i)), pl.BlockSpec((n, tc), lambda i: (0, i))]
        + [full(a) for a in consts],
        out_specs=pl.BlockSpec((l, tc), lambda i: (0, i)),
        out_shape=jax.ShapeDtypeStruct((l, c), F32),
        compiler_params=_cparams("parallel"),
        name="hyena_ctx_conv",
    )(u, filt, *consts)


def _hy_post_kernel(y_ref, u_ref, x0_ref, b_ref, o_ref, *, transposed):
    w = y_ref[...] + u_ref[...] * b_ref[...]
    if transposed:
        w = w.T
    o_ref[...] = (w * x0_ref[...]).astype(o_ref.dtype)


def hyena_post(y, u, x0, bias, transposed):
    l, w = x0.shape
    tb = min(HY_TB, l)
    tm = pl.BlockSpec((tb, w), lambda i: (i, 0))
    yu = pl.BlockSpec((w, tb), lambda i: (0, i)) if transposed else tm
    return pl.pallas_call(
        functools.partial(_hy_post_kernel, transposed=transposed),
        grid=(l // tb,),
        in_specs=[yu, yu, tm, pl.BlockSpec(bias.shape, lambda i: (0, 0))],
        out_specs=tm,
        out_shape=jax.ShapeDtypeStruct((l, w), BF16),
        compiler_params=_cparams("parallel"),
        name="hyena_post",
    )(y, u, x0, bias)


def hyena(z, short_w, short_b, filt, bias):
    l = z.shape[0]
    filt_t, hb0 = hyena_filter(l, *filt)
    if l == FFT_B * FFT_B // 2:
        u_t, x0 = hyena_pre(z, short_w, short_b, True)
        y_t = hyena_long_conv(u_t, filt_t)
        return hyena_post(y_t, u_t, x0, bias.reshape(-1, 1) + hb0, True)
    u, x0 = hyena_pre(z, short_w, short_b, False)
    y = hyena_ctx_conv(u, filt_t.T)
    return hyena_post(y, u, x0, (bias.reshape(-1, 1) + hb0).T, False)


def kernel(x, c, ctx, c_ctx, w_mod, b_mod, norm1_w, norm2_w, ffn_w_gate, ffn_w_up, ffn_w_down, even_w_in, even_w_out, dn_conv_w, dn_a_log, dn_dt_bias, dn_onorm_w, hy_short_w, hy_short_b, hy_f_w1, hy_f_b1, hy_f_w2, hy_f_b2, hy_f_w3, hy_f_b3, hy_f_w4, hy_f_freq, hy_bias, na_w_qkv, na_rpb, na_w_out, final_norm_w):
    d = D_MODEL
    xl, xc = x[0], ctx[0]
    mod = modulation(jnp.concatenate([c, c_ctx[None]], axis=0), w_mod, b_mod)

    def chunk(layer, row, idx):
        return mod[layer, row:row + 1, idx * d:(idx + 1) * d]

    for layer in range(DEPTH):
        need_ctx = layer < DEPTH - 1
        ml = [chunk(layer, 0, j) for j in range(N_MOD)]
        mc = [chunk(layer, 1, j) for j in range(N_MOD)]
        n1 = norm1_w[layer][None]
        n2 = norm2_w[layer][None]
        i = layer // 2
        if layer % 2 == 0:
            w_in = even_w_in[i]
            w_in = jnp.concatenate([w_in[:, :4 * DN_WIDTH], w_in[:, DN_IN:], w_in[:, 4 * DN_WIDTH:DN_IN]],
                                   axis=1).astype(BF16)
            zl = norm_mod_matmul(xl, n1, ml[1], ml[0], w_in, F32)
            zc = norm_mod_matmul(xc, n1, mc[1], mc[0], w_in, F32)
            dn_c, dn_l = gated_deltanet(zc, zl, dn_conv_w[i], dn_a_log[i], dn_dt_bias[i], dn_onorm_w[i])
            filt = (hy_f_w1[i], hy_f_b1[i], hy_f_w2[i], hy_f_b2[i], hy_f_w3[i], hy_f_b3[i], hy_f_w4[i],
                    hy_f_freq[i])
            yl = (dn_l, hyena(zl, hy_short_w[i], hy_short_b[i], filt, hy_bias[i]))
            w_out = even_w_out[i].astype(BF16)
            if need_ctx:
                yc = (dn_c, hyena(zc, hy_short_w[i], hy_short_b[i], filt, hy_bias[i]))
        else:
            zl = norm_mod_matmul(xl, n1, ml[1], ml[0], na_w_qkv, BF16, layer=i)
            zc = norm_mod_matmul(xc, n1, mc[1], mc[0], na_w_qkv, BF16, layer=i)
            ya = na_attention(zl, zc, na_bias_tiles(na_rpb[i]))
            yl = (ya, ya)
            w_out = na_w_out[i].astype(BF16)
            if need_ctx:
                ya = ctx_attention(zc)
                yc = (ya, ya)
        xl = proj_residual(*yl, w_out, xl, ml[2])
        xl = ffn_block(xl, n2, ml[4], ml[3], ml[5], ffn_w_gate, ffn_w_up, ffn_w_down, layer)
        if need_ctx:
            xc = proj_residual(*yc, w_out, xc, mc[2])
            xc = ffn_block(xc, n2, mc[4], mc[3], mc[5], ffn_w_gate, ffn_w_up, ffn_w_down, layer)
    return final_norm(xl, final_norm_w[None])[None]
```

```python
import functools
import math

import jax
import jax.numpy as jnp
import numpy as np
from jax import lax
from jax.experimental import pallas as pl
from jax.experimental.pallas import tpu as pltpu

D_MODEL = 2048
SEQ = 8192
DEPTH = 4
GRID_W = 64
ROWS = SEQ // GRID_W
CTX_LEN = 256
NORM_EPS = 1e-6
N_MOD = 6

DN_HEADS = 8
DN_DIM = 128
DN_WIDTH = DN_HEADS * DN_DIM
DN_CONV = 5
DN_CHUNK = 64
DN_IN = 4 * DN_WIDTH + 4 * DN_HEADS

HY_WIDTH = D_MODEL - DN_WIDTH
HY_BANDS = 16
HY_DECAY_TARGET = 1e-2
HY_STRONG_DECAY_PCT = 0.3
HY_WEAK_DECAY_PCT = 1.5
HY_IN = 3 * HY_WIDTH
EVEN_IN = DN_IN + HY_IN

NA_HEADS = D_MODEL // 128
NA_DIM = 128
WIN_R = 8
WIN_C = 16
FFN_HIDDEN = -(-8 * D_MODEL // (3 * 256)) * 256

LANES = 128
VMEM_LIMIT = 56 * 1024 * 1024
NEG = -1e30

F32 = jnp.float32
BF16 = jnp.bfloat16


def _cparams(*sem):
    return pltpu.CompilerParams(dimension_semantics=sem, vmem_limit_bytes=VMEM_LIMIT)


def _sigmoid(x):
    return 1.0 / (1.0 + jnp.exp(-x))


MOD_TN = 1024


def _mod_kernel(c_ref, w_ref, b_ref, o_ref):
    rows = []
    for r in range(2):
        c = c_ref[r]
        a = c * _sigmoid(c)
        cols = [jnp.sum(w_ref[0, :, j * LANES:(j + 1) * LANES] * a, axis=0, keepdims=True)
                for j in range(MOD_TN // LANES)]
        rows.append(jnp.concatenate(cols, axis=1))
    o_ref[0] = jnp.concatenate(rows, axis=0) + b_ref[0]


def modulation(c_pair, w_mod, b_mod):
    n = N_MOD * D_MODEL
    cb = jnp.broadcast_to(c_pair[:, :, None], (2, D_MODEL, LANES))
    return pl.pallas_call(
        _mod_kernel,
        grid=(DEPTH, n // MOD_TN),
        in_specs=[pl.BlockSpec((2, D_MODEL, LANES), lambda l, j: (0, 0, 0)),
                  pl.BlockSpec((1, D_MODEL, MOD_TN), lambda l, j: (l, 0, j)),
                  pl.BlockSpec((1, 1, MOD_TN), lambda l, j: (l, 0, j))],
        out_specs=pl.BlockSpec((1, 2, MOD_TN), lambda l, j: (l, 0, j)),
        out_shape=jax.ShapeDtypeStruct((DEPTH, 2, n), F32),
        compiler_params=_cparams("parallel", "parallel"),
        name="modulation",
    )(cb, w_mod, b_mod.reshape(DEPTH, 1, n))


def _norm_mod_kernel(x_ref, nw_ref, sc_ref, sh_ref, h_ref):
    x = x_ref[...]
    y = x * lax.rsqrt(jnp.mean(x * x, axis=-1, keepdims=True) + NORM_EPS)
    h_ref[...] = (y * nw_ref[...] * (1.0 + sc_ref[...]) + sh_ref[...]).astype(h_ref.dtype)


def norm_mod(x, nw, sc, sh):
    m, d = x.shape
    tm = min(m, 256)
    vec = pl.BlockSpec((1, d), lambda i: (0, 0))
    return pl.pallas_call(
        _norm_mod_kernel,
        grid=(m // tm,),
        in_specs=[pl.BlockSpec((tm, d), lambda i: (i, 0)), vec, vec, vec],
        out_specs=pl.BlockSpec((tm, d), lambda i: (i, 0)),
        out_shape=jax.ShapeDtypeStruct((m, d), BF16),
        compiler_params=_cparams("parallel"),
        name="norm_mod",
    )(x, nw, sc, sh)


PROJ_TN = 512


def _weight_spec(w, tn, layer):
    d = w.shape[-2]
    if layer is None:
        return pl.BlockSpec((d, tn), lambda j, i: (0, j))
    return pl.BlockSpec((None, d, tn), lambda j, i: (layer, 0, j))


def _proj_kernel(h_ref, w_ref, o_ref, *scratch, scaled_cols, scale):
    if scratch:
        wb_ref, = scratch

        @pl.when(pl.program_id(1) == 0)
        def _():
            wb_ref[...] = w_ref[...].astype(BF16)
    else:
        wb_ref = w_ref
    y = jnp.dot(h_ref[...], wb_ref[...], preferred_element_type=F32)
    if scaled_cols:
        y = y * jnp.where(pl.program_id(0) * o_ref.shape[1] < scaled_cols, scale, 1.0)
    o_ref[...] = y.astype(o_ref.dtype)


def projection(h, w, out_dtype, layer=None, scaled_cols=0, scale=1.0):
    m, d = h.shape
    n = w.shape[-1]
    tm = min(m, 1024)
    tn = PROJ_TN
    assert scaled_cols % tn == 0
    scratch = [] if w.dtype == BF16 else [pltpu.VMEM((d, tn), BF16)]
    return pl.pallas_call(
        functools.partial(_proj_kernel, scaled_cols=scaled_cols, scale=scale),
        grid=(pl.cdiv(n, tn), m // tm),
        in_specs=[pl.BlockSpec((tm, d), lambda j, i: (i, 0)), _weight_spec(w, tn, layer)],
        out_specs=pl.BlockSpec((tm, tn), lambda j, i: (i, j)),
        out_shape=jax.ShapeDtypeStruct((m, n), out_dtype),
        scratch_shapes=scratch,
        compiler_params=_cparams("parallel", "arbitrary"),
        name="projection",
    )(h, w)


FFN_TF = 512
FFN_TN = 256


def _ffn_up_kernel(h_ref, wg_ref, wu_ref, a_ref, wgb_ref, wub_ref):
    @pl.when(pl.program_id(1) == 0)
    def _():
        wgb_ref[...] = wg_ref[...].astype(BF16)
        wub_ref[...] = wu_ref[...].astype(BF16)

    h = h_ref[...]
    gate = jnp.dot(h, wgb_ref[...], preferred_element_type=F32)
    up = jnp.dot(h, wub_ref[...], preferred_element_type=F32)
    a_ref[...] = (gate * _sigmoid(gate) * up).astype(a_ref.dtype)


def _ffn_down_kernel(a_ref, wd_ref, x_ref, g_ref, o_ref):
    y = jnp.dot(a_ref[...], wd_ref[...].astype(BF16), preferred_element_type=F32)
    o_ref[...] = x_ref[...] + g_ref[...] * y


def ffn_block(x, h, g, wg, wu, wd, layer):
    m, d = x.shape
    hid = wg.shape[2]
    tm = min(m, 1024)
    a = pl.pallas_call(
        _ffn_up_kernel,
        grid=(hid // FFN_TF, m // tm),
        in_specs=[pl.BlockSpec((tm, d), lambda f, i: (i, 0)),
                  _weight_spec(wg, FFN_TF, layer), _weight_spec(wu, FFN_TF, layer)],
        out_specs=pl.BlockSpec((tm, FFN_TF), lambda f, i: (i, f)),
        out_shape=jax.ShapeDtypeStruct((m, hid), BF16),
        scratch_shapes=[pltpu.VMEM((d, FFN_TF), BF16), pltpu.VMEM((d, FFN_TF), BF16)],
        compiler_params=_cparams("parallel", "arbitrary"),
        name="ffn_up",
    )(h, wg, wu)
    return pl.pallas_call(
        _ffn_down_kernel,
        grid=(m // tm, d // FFN_TN),
        in_specs=[pl.BlockSpec((tm, hid), lambda i, j: (i, 0)),
                  pl.BlockSpec((None, hid, FFN_TN), lambda i, j: (layer, 0, j)),
                  pl.BlockSpec((tm, FFN_TN), lambda i, j: (i, j)),
                  pl.BlockSpec((1, FFN_TN), lambda i, j: (0, j))],
        out_specs=pl.BlockSpec((tm, FFN_TN), lambda i, j: (i, j)),
        out_shape=jax.ShapeDtypeStruct((m, d), F32),
        compiler_params=_cparams("parallel", "arbitrary"),
        name="ffn_down",
    )(a, wd, x, g)


def _proj_res_kernel(ya_ref, yb_ref, w_ref, x_ref, g_ref, nw_ref, sc_ref, sh_ref, o_ref, h_ref):
    ka = ya_ref.shape[1]
    y = (jnp.dot(ya_ref[...], w_ref[:ka, :], preferred_element_type=F32)
         + jnp.dot(yb_ref[...], w_ref[ka:, :], preferred_element_type=F32))
    x = x_ref[...] + g_ref[...] * y
    o_ref[...] = x
    xn = x * lax.rsqrt(jnp.mean(x * x, axis=-1, keepdims=True) + NORM_EPS)
    h_ref[...] = (xn * nw_ref[...] * (1.0 + sc_ref[...]) + sh_ref[...]).astype(h_ref.dtype)


def proj_residual(ya, yb, w_bf16, x, g, nw, sc, sh):
    m, d = x.shape
    ka = kb = d // 2
    second = 1 if yb is ya else 0
    tm = min(m, 512)
    vec = pl.BlockSpec((1, d), lambda i: (0, 0))
    row = pl.BlockSpec((tm, d), lambda i: (i, 0))
    return pl.pallas_call(
        _proj_res_kernel,
        grid=(m // tm,),
        in_specs=[pl.BlockSpec((tm, ka), lambda i: (i, 0)),
                  pl.BlockSpec((tm, kb), lambda i: (i, second)),
                  pl.BlockSpec((d, d), lambda i: (0, 0)),
                  row, vec, vec, vec, vec],
        out_specs=[row, row],
        out_shape=[jax.ShapeDtypeStruct((m, d), F32), jax.ShapeDtypeStruct((m, d), BF16)],
        compiler_params=_cparams("parallel"),
        name="proj_residual",
    )(ya, yb, w_bf16, x, g, nw, sc, sh)


def _final_norm_kernel(x_ref, w_ref, o_ref):
    x = x_ref[...]
    o_ref[...] = x * lax.rsqrt(jnp.mean(x * x, axis=-1, keepdims=True) + NORM_EPS) * w_ref[...]


def final_norm(x, w):
    m, d = x.shape
    tm = 512
    return pl.pallas_call(
        _final_norm_kernel,
        grid=(m // tm,),
        in_specs=[pl.BlockSpec((tm, d), lambda i: (i, 0)), pl.BlockSpec((1, d), lambda i: (0, 0))],
        out_specs=pl.BlockSpec((tm, d), lambda i: (i, 0)),
        out_shape=jax.ShapeDtypeStruct((m, d), F32),
        compiler_params=_cparams("parallel"),
        name="final_norm",
    )(x, w)


NA_R = 8
NA_KD = (NA_R + WIN_R) // 2
NA_T = 2 * WIN_R
NA_HP = 2


def _na_bias_kernel(rpb_ref, o_ref):
    h = pl.program_id(0)
    c = lax.broadcasted_iota(jnp.int32, (GRID_W, LANES), 0)
    lane = lax.broadcasted_iota(jnp.int32, (GRID_W, LANES), 1)
    left = lane < GRID_W
    kc = jnp.where(left, lane, lane - GRID_W)
    cs = jnp.clip(c - WIN_C // 2, 0, GRID_W - WIN_C)
    inwin = (kc >= cs) & (kc < cs + WIN_C)
    diff = kc - c + (WIN_C - 1)
    n_ro, n_co = 2 * WIN_R - 1, 2 * WIN_C - 1
    for t in range(NA_T):
        def body(d, acc, t=t):
            vl = rpb_ref[(h * n_ro + (t - 1)) * n_co + d] if t >= 1 else jnp.float32(NEG)
            vr = rpb_ref[(h * n_ro + t) * n_co + d] if t < n_ro else jnp.float32(NEG)
            return jnp.where(diff == d, jnp.where(left, vl, vr), acc)

        acc = lax.fori_loop(0, n_co, body, jnp.full((GRID_W, LANES), NEG, F32))
        o_ref[0, t] = jnp.where(inwin, acc, NEG)


def na_bias_tiles(rpb):
    return pl.pallas_call(
        _na_bias_kernel,
        grid=(NA_HEADS,),
        in_specs=[pl.BlockSpec(memory_space=pltpu.SMEM)],
        out_specs=pl.BlockSpec((1, NA_T, GRID_W, LANES), lambda h: (h, 0, 0, 0)),
        out_shape=jax.ShapeDtypeStruct((NA_HEADS, NA_T, GRID_W, LANES), F32),
        compiler_params=_cparams("parallel"),
        name="na_bias_tiles",
    )(rpb.reshape(-1))


def _na_kernel(q_ref, k_ref, v_ref, kc_ref, vc_ref, tt_ref, o_ref):
    r0 = pl.program_id(1) * NA_R
    ks = jnp.clip(r0 - WIN_R // 2, 0, ROWS - 2 * NA_KD)
    span = pl.ds(pl.multiple_of(ks * GRID_W, LANES), NA_KD * LANES)
    heads = range(NA_HP)
    cols = [slice(h * NA_DIM, (h + 1) * NA_DIM) for h in heads]
    contract_last = (((1,), (1,)), ((), ()))
    q = [q_ref[:, cols[h]] for h in heads]
    s_loc = [lax.dot_general(q[h], k_ref[span, cols[h]], contract_last, preferred_element_type=F32) for h in heads]
    s_ctx = [lax.dot_general(q[h], kc_ref[:, cols[h]], contract_last, preferred_element_type=F32) for h in heads]

    left = lax.broadcasted_iota(jnp.int32, (GRID_W, LANES), 1) < GRID_W
    tile_sel = []
    for qi in range(NA_R):
        r = r0 + qi
        rs = jnp.clip(r - WIN_R // 2, 0, ROWS - WIN_R)
        for dj in range(NA_KD):
            kl = ks + 2 * dj
            vl = ((kl >= rs) & (kl < rs + WIN_R)).astype(jnp.int32)
            vr = ((kl + 1 >= rs) & (kl + 1 < rs + WIN_R)).astype(jnp.int32)
            tile_sel.append((jnp.clip(kl - r + WIN_R, 0, NA_T - 1), jnp.where(left, vl, vr) > 0))
    for h in heads:
        rows = [jnp.concatenate([jnp.where(valid, tt_ref[h, t], NEG)
                                 for t, valid in tile_sel[qi * NA_KD:(qi + 1) * NA_KD]], axis=1)
                for qi in range(NA_R)]
        s_loc[h] = s_loc[h] + jnp.concatenate(rows, axis=0)

    m = [jnp.maximum(jnp.max(s_loc[h], axis=-1, keepdims=True), jnp.max(s_ctx[h], axis=-1, keepdims=True))
         for h in heads]
    p_loc = [jnp.exp(s_loc[h] - m[h]) for h in heads]
    p_ctx = [jnp.exp(s_ctx[h] - m[h]) for h in heads]
    denom = [jnp.sum(p_loc[h], axis=-1, keepdims=True) + jnp.sum(p_ctx[h], axis=-1, keepdims=True) for h in heads]
    o = [jnp.dot(p_loc[h].astype(BF16), v_ref[span, cols[h]], preferred_element_type=F32)
         + jnp.dot(p_ctx[h].astype(BF16), vc_ref[:, cols[h]], preferred_element_type=F32) for h in heads]
    for h in heads:
        o_ref[:, cols[h]] = (o[h] / denom[h]).astype(o_ref.dtype)


def na_attention(zl, zc, tiles):
    tq = NA_R * GRID_W
    w = NA_HP * NA_DIM
    ng = NA_HEADS // NA_HP
    return pl.pallas_call(
        _na_kernel,
        grid=(ng, ROWS // NA_R),
        in_specs=[pl.BlockSpec((tq, w), lambda g, i: (i, g)),
                  pl.BlockSpec((SEQ, w), lambda g, i: (0, ng + g)),
                  pl.BlockSpec((SEQ, w), lambda g, i: (0, 2 * ng + g)),
                  pl.BlockSpec((CTX_LEN, w), lambda g, i: (0, ng + g)),
                  pl.BlockSpec((CTX_LEN, w), lambda g, i: (0, 2 * ng + g)),
                  pl.BlockSpec((NA_HP, NA_T, GRID_W, LANES), lambda g, i: (g, 0, 0, 0))],
        out_specs=pl.BlockSpec((tq, w), lambda g, i: (i, g)),
        out_shape=jax.ShapeDtypeStruct((SEQ, D_MODEL), BF16),
        compiler_params=_cparams("parallel", "arbitrary"),
        name="na_attention",
    )(zl, zl, zl, zc, zc, tiles)


def _ctx_attn_kernel(q_ref, k_ref, v_ref, o_ref):
    s = lax.dot_general(q_ref[...], k_ref[...], (((1,), (1,)), ((), ())), preferred_element_type=F32)
    p = jnp.exp(s - jnp.max(s, axis=-1, keepdims=True))
    o = jnp.dot(p.astype(BF16), v_ref[...], preferred_element_type=F32)
    o_ref[...] = (o / jnp.sum(p, axis=-1, keepdims=True)).astype(o_ref.dtype)


def ctx_attention(zc):
    nh = NA_HEADS
    blk = lambda off: pl.BlockSpec((CTX_LEN, NA_DIM), lambda h: (0, off + h))
    return pl.pallas_call(
        _ctx_attn_kernel,
        grid=(nh,),
        in_specs=[blk(0), blk(nh), blk(2 * nh)],
        out_specs=blk(0),
        out_shape=jax.ShapeDtypeStruct((CTX_LEN, D_MODEL), BF16),
        compiler_params=_cparams("parallel"),
        name="ctx_attention",
    )(zc, zc, zc)


DN_SC = 256
HALO = 8
N_GB = 2 * DN_HEADS


def _softplus(x):
    return jnp.maximum(x, 0.0) + jnp.log1p(jnp.exp(-jnp.abs(x)))


def _short_conv(main_ref, prev_ref, next_ref, w_ref, col0, blk, nblk):
    ksize = w_ref.shape[0]
    tb, width = main_ref.shape
    prev = jnp.where(blk > 0, prev_ref[...], 0.0)
    nxt = jnp.where(blk < nblk - 1, next_ref[...], 0.0)
    xf = jnp.concatenate([prev, main_ref[...], nxt], axis=0)
    n = xf.shape[0]
    acc = None
    for j in range(ksize):
        sh = (ksize // 2 - j) % n
        xs = xf if sh == 0 else pltpu.roll(xf, sh, 0)
        term = xs[HALO:HALO + tb] * w_ref[j:j + 1, col0:col0 + width]
        acc = term if acc is None else acc + term
    return acc


def _chunk_masks(n):
    r = lax.broadcasted_iota(jnp.int32, (n, n), 0)
    c = lax.broadcasted_iota(jnp.int32, (n, n), 1)
    return r, c, (r // DN_CHUNK) == (c // DN_CHUNK)


def _dn_scalars_kernel(za_ref, zb_ref, zat_ref, alog_ref, dtb_ref, alogt_ref, dtbt_ref,
                       gc_ref, gt_ref, beta_ref, gct_ref):
    hi = lax.Precision.HIGHEST
    g = -jnp.exp(alog_ref[...]) * _softplus(za_ref[...] + dtb_ref[...])
    beta_ref[...] = _sigmoid(zb_ref[...])
    r, c, same = _chunk_masks(DN_SC)
    lo = (same & (c <= r)).astype(F32)
    up = (same & (c >= r)).astype(F32)
    fwd_col = lax.broadcasted_iota(jnp.int32, (DN_SC, N_GB), 1) < DN_HEADS
    gc_ref[...] = jnp.where(fwd_col, jnp.dot(lo, g, precision=hi), jnp.dot(up, g, precision=hi))
    gt_ref[...] = jnp.dot(same.astype(F32), g, precision=hi)
    g_t = -jnp.exp(alogt_ref[...]) * _softplus(zat_ref[...] + dtbt_ref[...])
    fwd_row = lax.broadcasted_iota(jnp.int32, (N_GB, DN_SC), 0) < DN_HEADS
    gct_ref[...] = jnp.where(fwd_row, jnp.dot(g_t, up, precision=hi), jnp.dot(g_t, lo, precision=hi))


def dn_scalars(za, zb, a_log, dt_bias):
    l = za.shape[0]
    col = pl.BlockSpec((DN_SC, N_GB), lambda i: (i, 0))
    row = pl.BlockSpec((N_GB, DN_SC), lambda i: (0, i))
    prow = pl.BlockSpec((1, N_GB), lambda i: (0, 0))
    pcol = pl.BlockSpec((N_GB, 1), lambda i: (0, 0))
    cshape = jax.ShapeDtypeStruct((l, N_GB), F32)
    return pl.pallas_call(
        _dn_scalars_kernel,
        grid=(l // DN_SC,),
        in_specs=[col, col, row, prow, prow, pcol, pcol],
        out_specs=[col, col, col, row],
        out_shape=[cshape, cshape, cshape, jax.ShapeDtypeStruct((N_GB, l), F32)],
        compiler_params=_cparams("parallel"),
        name="dn_scalars",
    )(za, zb, za.T, a_log.reshape(1, N_GB), dt_bias.reshape(1, N_GB), a_log.reshape(N_GB, 1),
      dt_bias.reshape(N_GB, 1))


def _gdn_kernel(zq_ref, zqp_ref, zqn_ref, zk_ref, zkp_ref, zkn_ref, zv_ref, zvp_ref, zvn_ref, cw_ref,
                gc_ref, gt_ref, beta_ref, gct_ref, s0_ref, o_ref, sfin_ref, state_ref, *, bwd, nblk):
    i = pl.program_id(0)
    blk = nblk - 1 - i if bwd else i

    @pl.when(i == 0)
    def _():
        state_ref[...] = s0_ref[...]

    def conv_silu(main_ref, prev_ref, next_ref, part):
        acc = _short_conv(main_ref, prev_ref, next_ref, cw_ref, part * DN_WIDTH, blk, nblk)
        return acc * _sigmoid(acc)

    q_all = conv_silu(zq_ref, zqp_ref, zqn_ref, 0)
    k_all = conv_silu(zk_ref, zkp_ref, zkn_ref, 1)
    v_all = conv_silu(zv_ref, zvp_ref, zvn_ref, 2)

    r, c, same64 = _chunk_masks(DN_SC)
    tri = same64 & ((c >= r) if bwd else (c <= r))
    offdiag = r != c
    same16 = (r // 16) == (c // 16)
    same32 = (r // 32) == (c // 32)
    contract_last = (((1,), (1,)), ((), ()))
    contract_first = (((0,), (0,)), ((), ()))
    nc = DN_SC // DN_CHUNK
    dot = functools.partial(jnp.dot, preferred_element_type=F32)

    heads = range(DN_HEADS)
    col0 = DN_HEADS if bwd else 0
    gc_c = [gc_ref[:, col0 + h:col0 + h + 1] for h in heads]
    gt_c = [gt_ref[:, col0 + h:col0 + h + 1] for h in heads]
    beta_c = [beta_ref[:, col0 + h:col0 + h + 1] for h in heads]
    q, k, v, k16, dec, qk16, m, p, e = ([None] * DN_HEADS for _ in range(9))
    for h in heads:
        sl = slice(h * DN_DIM, (h + 1) * DN_DIM)
        qh, kh = q_all[:, sl], k_all[:, sl]
        q[h] = qh * (lax.rsqrt(jnp.sum(qh * qh, axis=-1, keepdims=True) + NORM_EPS) * DN_DIM ** -0.5)
        k[h] = kh * lax.rsqrt(jnp.sum(kh * kh, axis=-1, keepdims=True) + NORM_EPS)
        v[h] = v_all[:, sl]
        k16[h] = k[h].astype(BF16)
        dec[h] = jnp.exp(jnp.where(tri, gc_c[h] - gct_ref[col0 + h:col0 + h + 1, :], NEG))
    for h in heads:
        kk = lax.dot_general(k16[h], k16[h], contract_last, preferred_element_type=F32)
        m[h] = jnp.where(offdiag, kk * beta_c[h] * dec[h], 0.0)
        qk16[h] = (lax.dot_general(q[h].astype(BF16), k16[h], contract_last, preferred_element_type=F32)
                   * dec[h]).astype(BF16)

    for h in heads:
        p[h] = jnp.where(same16, -m[h], 0.0)
        e[h] = p[h]
    for _ in range(3):
        for h in heads:
            p16 = p[h].astype(BF16)
            p[h] = dot(p16, p16)
        for h in heads:
            e[h] = e[h] + p[h] + dot(e[h].astype(BF16), p[h].astype(BF16))
    for inner, outer in ((same16, same32), (same32, None)):
        y = [None] * DN_HEADS
        for h in heads:
            cm = jnp.where(~inner if outer is None else (outer & ~inner), m[h], 0.0)
            y[h] = cm + dot(cm.astype(BF16), e[h].astype(BF16))
        for h in heads:
            e[h] = e[h] - y[h] - dot(e[h].astype(BF16), y[h].astype(BF16))

    u, w16, qg16, kd16, s = ([None] * DN_HEADS for _ in range(5))
    for h in heads:
        eg = jnp.exp(gc_c[h])
        rhs = jnp.concatenate([v[h] * beta_c[h], k[h] * (beta_c[h] * eg)], axis=1)
        uw = rhs + dot(e[h].astype(BF16), rhs.astype(BF16))
        u[h] = uw[:, :DN_DIM]
        w16[h] = uw[:, DN_DIM:].astype(BF16)
        qg16[h] = (q[h] * eg).astype(BF16)
        kd16[h] = (k[h] * jnp.exp(gt_c[h] - gc_c[h])).astype(BF16)
        s[h] = state_ref[h]

    outs = [[None] * nc for _ in heads]
    for ci in (range(nc - 1, -1, -1) if bwd else range(nc)):
        rows = slice(ci * DN_CHUNK, (ci + 1) * DN_CHUNK)
        ws = [dot(jnp.concatenate([w16[h][rows], qg16[h][rows]], axis=0), s[h].astype(BF16)) for h in heads]
        vn16 = [(u[h][rows] - ws[h][:DN_CHUNK]).astype(BF16) for h in heads]
        for h in heads:
            outs[h][ci] = ws[h][DN_CHUNK:] + dot(qk16[h][rows, ci * DN_CHUNK:(ci + 1) * DN_CHUNK], vn16[h])
        for h in heads:
            gl = jnp.exp(gt_c[h][ci * DN_CHUNK:ci * DN_CHUNK + 1, :])
            s[h] = s[h] * gl + lax.dot_general(kd16[h][rows], vn16[h], contract_first, preferred_element_type=F32)
    for h in heads:
        state_ref[h] = s[h]
        o_ref[:, h * DN_DIM:(h + 1) * DN_DIM] = jnp.concatenate(outs[h], axis=0)

    @pl.when(i == nblk - 1)
    def _():
        sfin_ref[...] = state_ref[...]


def gdn_scan(z, conv_w, gc, gt, beta, gct, s0, bwd):
    l = z.shape[0]
    nblk = l // DN_SC
    per = DN_SC // HALO

    def b(i):
        return nblk - 1 - i if bwd else i

    def part_specs(part):
        return [pl.BlockSpec((DN_SC, DN_WIDTH), lambda i: (b(i), part)),
                pl.BlockSpec((HALO, DN_WIDTH), lambda i: (jnp.maximum(b(i) * per - 1, 0), part)),
                pl.BlockSpec((HALO, DN_WIDTH), lambda i: (jnp.minimum((b(i) + 1) * per, l // HALO - 1), part))]

    col = pl.BlockSpec((DN_SC, N_GB), lambda i: (b(i), 0))
    st = pl.BlockSpec((DN_HEADS, DN_DIM, DN_DIM), lambda i: (0, 0, 0))
    return pl.pallas_call(
        functools.partial(_gdn_kernel, bwd=bwd, nblk=nblk),
        grid=(nblk,),
        in_specs=part_specs(0) + part_specs(1) + part_specs(2) + [
            pl.BlockSpec((DN_CONV, 3 * DN_WIDTH), lambda i: (0, 0)), col, col, col,
            pl.BlockSpec((N_GB, DN_SC), lambda i: (0, b(i))), st],
        out_specs=[pl.BlockSpec((DN_SC, DN_WIDTH), lambda i: (b(i), 0)), st],
        out_shape=[jax.ShapeDtypeStruct((l, DN_WIDTH), F32),
                   jax.ShapeDtypeStruct((DN_HEADS, DN_DIM, DN_DIM), F32)],
        scratch_shapes=[pltpu.VMEM((DN_HEADS, DN_DIM, DN_DIM), F32)],
        compiler_params=_cparams("arbitrary"),
        name="gdn_scan_bwd" if bwd else "gdn_scan_fwd",
    )(z, z, z, z, z, z, z, z, z, conv_w, gc, gt, beta, gct, s0)


def _dn_out_kernel(of_ref, ob_ref, gate_ref, nw_ref, y_ref):
    for h in range(DN_HEADS):
        sl = slice(h * DN_DIM, (h + 1) * DN_DIM)
        o = of_ref[:, sl] + ob_ref[:, sl]
        o = o * lax.rsqrt(jnp.mean(o * o, axis=-1, keepdims=True) + NORM_EPS) * nw_ref[...]
        gate = gate_ref[:, sl]
        y_ref[:, sl] = (o * (gate * _sigmoid(gate))).astype(y_ref.dtype)


def dn_output(o_f, o_b, z, onorm_w):
    l = z.shape[0]
    tm = min(l, 512)
    blk = pl.BlockSpec((tm, DN_WIDTH), lambda i: (i, 0))
    return pl.pallas_call(
        _dn_out_kernel,
        grid=(l // tm,),
        in_specs=[blk, blk, pl.BlockSpec((tm, DN_WIDTH), lambda i: (i, 3)),
                  pl.BlockSpec((1, DN_DIM), lambda i: (0, 0))],
        out_specs=blk,
        out_shape=jax.ShapeDtypeStruct((l, DN_WIDTH), BF16),
        compiler_params=_cparams("parallel"),
        name="dn_output",
    )(o_f, o_b, z, onorm_w.reshape(1, DN_DIM))


def gated_deltanet(zc, zl, conv_w, a_log, dt_bias, onorm_w):
    a0, b0 = EVEN_IN - 2 * N_GB, EVEN_IN - N_GB
    sc_c = dn_scalars(zc[:, a0:b0], zc[:, b0:b0 + N_GB], a_log, dt_bias)
    sc_l = dn_scalars(zl[:, a0:b0], zl[:, b0:b0 + N_GB], a_log, dt_bias)
    s0 = jnp.zeros((DN_HEADS, DN_DIM, DN_DIM), F32)
    outs_c, outs_l = [], []
    for bwd in (False, True):
        oc, s_ctx = gdn_scan(zc, conv_w, *sc_c, s0, bwd)
        ol, _ = gdn_scan(zl, conv_w, *sc_l, s_ctx, bwd)
        outs_c.append(oc)
        outs_l.append(ol)
    return dn_output(*outs_c, zc, onorm_w), dn_output(*outs_l, zl, onorm_w)


HY_SHORT = 3
HY_EMB = 1 + 2 * HY_BANDS
HY_EMB_PAD = 40
HY_TB = 256
FFT_B = 128
HY_CB = 16
HY_COL0 = 4


def _hy_pre_kernel(x0_ref, x0p_ref, x0n_ref, x1_ref, x1p_ref, x1n_ref, v_ref, vp_ref, vn_ref, w_ref, b_ref,
                   u_ref, x0c_ref, *, nblk, transpose_u):
    blk = pl.program_id(0)
    w = HY_WIDTH
    x0 = _short_conv(x0_ref, x0p_ref, x0n_ref, w_ref, 0, blk, nblk) + b_ref[:, :w]
    x1 = _short_conv(x1_ref, x1p_ref, x1n_ref, w_ref, w, blk, nblk) + b_ref[:, w:2 * w]
    v = _short_conv(v_ref, vp_ref, vn_ref, w_ref, 2 * w, blk, nblk) + b_ref[:, 2 * w:]
    u = v * x1
    x0c_ref[...] = x0
    u_ref[...] = u.T if transpose_u else u


def hyena_pre(z, short_w, short_b, transpose_u):
    l = z.shape[0]
    tb = min(HY_TB, l)
    nblk = l // tb
    per = tb // HALO
    w = HY_WIDTH

    def part_specs(part):
        cb = HY_COL0 + part
        return [pl.BlockSpec((tb, w), lambda i: (i, cb)),
                pl.BlockSpec((HALO, w), lambda i: (jnp.maximum(i * per - 1, 0), cb)),
                pl.BlockSpec((HALO, w), lambda i: (jnp.minimum((i + 1) * per, l // HALO - 1), cb))]

    u_shape, u_spec = ((w, l), pl.BlockSpec((w, tb), lambda i: (0, i))) if transpose_u else (
        (l, w), pl.BlockSpec((tb, w), lambda i: (i, 0)))
    return pl.pallas_call(
        functools.partial(_hy_pre_kernel, nblk=nblk, transpose_u=transpose_u),
        grid=(nblk,),
        in_specs=part_specs(0) + part_specs(1) + part_specs(2) + [
            pl.BlockSpec((HY_SHORT, HY_IN), lambda i: (0, 0)), pl.BlockSpec((1, HY_IN), lambda i: (0, 0))],
        out_specs=[u_spec, pl.BlockSpec((tb, w), lambda i: (i, 0))],
        out_shape=[jax.ShapeDtypeStruct(u_shape, F32), jax.ShapeDtypeStruct((l, w), F32)],
        compiler_params=_cparams("parallel"),
        name="hyena_pre",
    )(z, z, z, z, z, z, z, z, z, short_w, short_b.reshape(1, HY_IN))


def _hy_filter_kernel(w1t_ref, b1_ref, w2t_ref, b2_ref, w3t_ref, b3_ref, w4t_ref, fr_ref, band_ref, dl_ref,
                      f_ref, hb0_ref, *, l, fb):
    hi = lax.Precision.HIGHEST
    j = pl.program_id(0)
    second = j >= l // fb
    n = j * fb + lax.broadcasted_iota(jnp.int32, (1, fb), 1)
    pos = jnp.where(second, 2 * l - n, n).astype(F32)
    t = pos / max(l - 1, 1)
    wpos = 2.0 * math.pi * pos / l
    arg = band_ref[...] * wpos
    row = lax.broadcasted_iota(jnp.int32, (HY_EMB_PAD, fb), 0)
    feat = jnp.where(row == 0, t, jnp.where(row <= HY_BANDS, jnp.cos(arg),
                                            jnp.where(row <= 2 * HY_BANDS, -jnp.sin(arg), 0.0)))
    fr = fr_ref[...]
    h = jnp.sin(fr * (jnp.dot(w1t_ref[...], feat, precision=hi) + b1_ref[...]))
    h = jnp.sin(fr * (jnp.dot(w2t_ref[...], h, precision=hi) + b2_ref[...]))
    h = jnp.sin(fr * (jnp.dot(w3t_ref[...], h, precision=hi) + b3_ref[...]))
    window = jnp.exp(-t * dl_ref[...])
    half = pl.multiple_of(jnp.where(second, HY_WIDTH, 0), HY_WIDTH)
    f = jnp.dot(w4t_ref[pl.ds(half, HY_WIDTH), :], h, precision=hi) * window
    f_ref[...] = jnp.where(n == l, 0.0, f)

    @pl.when(j == 0)
    def _():
        hb0_ref[...] = (jnp.dot(w4t_ref[HY_WIDTH:, :], h[:, :LANES], precision=hi) * window[:, :LANES])


def hyena_filter(l, w1, b1, w2, b2, w3, b3, w4, freq):
    fb = min(1024, l)
    colv = lambda v: v.reshape(-1, 1)
    bands = np.zeros((HY_EMB_PAD, 1), np.float32)
    bands[1:1 + HY_BANDS, 0] = bands[1 + HY_BANDS:HY_EMB, 0] = np.linspace(1e-4, HY_BANDS - 1, HY_BANDS,
                                                                            dtype=np.float32)
    min_decay = math.log(HY_DECAY_TARGET) / HY_WEAK_DECAY_PCT
    max_decay = math.log(HY_DECAY_TARGET) / HY_STRONG_DECAY_PCT
    deltas = np.abs(np.linspace(min_decay, max_decay, HY_WIDTH, dtype=np.float32)).reshape(-1, 1)
    w1t = jnp.pad(w1.T, ((0, 0), (0, HY_EMB_PAD - HY_EMB)))
    full = lambda a: pl.BlockSpec(a.shape, lambda j: (0,) * a.ndim)
    args = (w1t, colv(b1), w2.T, colv(b2), w3.T, colv(b3), w4.T, colv(freq), jnp.asarray(bands), jnp.asarray(deltas))
    filt, hb0 = pl.pallas_call(
        functools.partial(_hy_filter_kernel, l=l, fb=fb),
        grid=(2 * l // fb,),
        in_specs=[full(a) for a in args],
        out_specs=[pl.BlockSpec((HY_WIDTH, fb), lambda j: (0, j)), pl.BlockSpec((HY_WIDTH, LANES), lambda j: (0, 0))],
        out_shape=[jax.ShapeDtypeStruct((HY_WIDTH, 2 * l), F32), jax.ShapeDtypeStruct((HY_WIDTH, LANES), F32)],
        compiler_params=_cparams("arbitrary"),
        name="hyena_filter",
    )(*args)
    return filt, hb0[:, :1]


def _dft_constants():
    b = FFT_B
    n = b * b
    idx = np.arange(b)
    ang = 2.0 * np.pi * np.outer(idx, idx) / b
    c, s = np.cos(ang), np.sin(ang)
    tw = 2.0 * np.pi * np.outer(idx, idx) / n
    fwd_b = np.concatenate([c, -s], axis=0)
    cs = np.concatenate([c, s], axis=1)
    inv_b = np.concatenate([c[:b // 2], -s[:b // 2]], axis=1) / n
    return (jnp.asarray(fwd_b, BF16), jnp.asarray(cs, BF16), jnp.asarray(np.cos(tw), F32),
            jnp.asarray(-np.sin(tw), F32), jnp.asarray(inv_b, BF16))


def _hy_conv_kernel(u_ref, f_ref, fwd_ref, cs_ref, tr_ref, ti_ref, inv_ref, y_ref, ur_s, ui_s, fr_s, fi_s):
    cb = u_ref.shape[0]
    b = FFT_B
    m = cb * b
    dot = functools.partial(jnp.dot, preferred_element_type=F32)
    fwd = fwd_ref[...]
    fwd_half = fwd[:, :b // 2]
    tr, ti = tr_ref[...], ti_ref[...]

    def first_stage(c, carry):
        for src, lhs, re_s, im_s in ((u_ref, fwd_half, ur_s, ui_s), (f_ref, fwd, fr_s, fi_s)):
            p = dot(lhs, src[c].astype(BF16))
            pr, pi = p[:b], p[b:]
            re_s[c] = (pr * tr - pi * ti).astype(BF16)
            im_s[c] = (pr * ti + pi * tr).astype(BF16)
        return carry

    lax.fori_loop(0, cb, first_stage, 0, unroll=2)

    cs = cs_ref[...]

    def times_cs(re, im):
        big = dot(jnp.concatenate([re, im], axis=0), cs)
        return big[:m, :b], big[:m, b:], big[m:, :b], big[m:, b:]

    def second_stage(re_s, im_s):
        rc, rs, ic, is_ = times_cs(re_s[...].reshape(m, b), im_s[...].reshape(m, b))
        return rc + is_, ic - rs

    xr, xi = second_stage(ur_s, ui_s)
    hr, hi = second_stage(fr_s, fi_s)
    zr = (xr * hr - xi * hi).astype(BF16)
    zi = (xr * hi + xi * hr).astype(BF16)
    rc, rs, ic, is_ = times_cs(zr, zi)
    gr = (rc - is_).reshape(cb, b, b)
    gi = (rs + ic).reshape(cb, b, b)
    ur_s[...] = (gr * tr + gi * ti).astype(BF16)
    ui_s[...] = (gi * tr - gr * ti).astype(BF16)
    inv = inv_ref[...]

    def last_stage(c, carry):
        y_ref[c] = dot(inv, jnp.concatenate([ur_s[c], ui_s[c]], axis=0))
        return carry

    lax.fori_loop(0, cb, last_stage, 0, unroll=2)


def hyena_long_conv(u_t, filt_t):
    c, l = u_t.shape
    b = FFT_B
    consts = _dft_constants()
    full = lambda a: pl.BlockSpec(a.shape, lambda i: (0,) * a.ndim)
    y = pl.pallas_call(
        _hy_conv_kernel,
        grid=(c // HY_CB,),
        in_specs=[pl.BlockSpec((HY_CB, b // 2, b), lambda i: (i, 0, 0)),
                  pl.BlockSpec((HY_CB, b, b), lambda i: (i, 0, 0))] + [full(a) for a in consts],
        out_specs=pl.BlockSpec((HY_CB, b // 2, b), lambda i: (i, 0, 0)),
        out_shape=jax.ShapeDtypeStruct((c, b // 2, b), F32),
        scratch_shapes=[pltpu.VMEM((HY_CB, b, b), BF16) for _ in range(4)],
        compiler_params=_cparams("parallel"),
        name="hyena_long_conv",
    )(u_t.reshape(c, b // 2, b), filt_t.reshape(c, b, b), *consts)
    return y.reshape(c, l)


def _hy_ctx_conv_kernel(u_ref, f_ref, fwdu_ref, fwdf_ref, inv_ref, y_ref):
    dot = functools.partial(jnp.dot, preferred_element_type=F32)
    n = f_ref.shape[0]
    x = dot(fwdu_ref[...], u_ref[...].astype(BF16))
    h = dot(fwdf_ref[...], f_ref[...].astype(BF16))
    xr, xi, hr, hi = x[:n], x[n:], h[:n], h[n:]
    z = jnp.concatenate([xr * hr - xi * hi, xr * hi + xi * hr], axis=0).astype(BF16)
    y_ref[...] = dot(inv_ref[...], z)


def hyena_ctx_conv(u, filt):
    l, c = u.shape
    n = 2 * l
    idx = np.arange(n)
    ang = 2.0 * np.pi * np.outer(idx, idx) / n
    cm, sm = np.cos(ang), np.sin(ang)
    fwdf = np.concatenate([cm, -sm], axis=0)
    inv = np.concatenate([cm[:l], -sm[:l]], axis=1) / n
    consts = (jnp.asarray(fwdf[:, :l], BF16), jnp.asarray(fwdf, BF16), jnp.asarray(inv, BF16))
    tc = 256
    full = lambda a: pl.BlockSpec(a.shape, lambda i: (0,) * a.ndim)
    return pl.pallas_call(
        _hy_ctx_conv_kernel,
        grid=(c // tc,),
        in_specs=[pl.BlockSpec((l, tc), lambda i: (0, i)), pl.BlockSpec((n, tc), lambda i: (0, i))]
        + [full(a) for a in consts],
        out_specs=pl.BlockSpec((l, tc), lambda i: (0, i)),
        out_shape=jax.ShapeDtypeStruct((l, c), F32),
        compiler_params=_cparams("parallel"),
        name="hyena_ctx_conv",
    )(u, filt, *consts)


def _hy_post_kernel(y_ref, u_ref, x0_ref, b_ref, o_ref, *, transposed):
    w = y_ref[...] + u_ref[...] * b_ref[...]
    if transposed:
        w = w.T
    o_ref[...] = (w * x0_ref[...]).astype(o_ref.dtype)


def hyena_post(y, u, x0, bias, transposed):
    l, w = x0.shape
    tb = min(HY_TB, l)
    tm = pl.BlockSpec((tb, w), lambda i: (i, 0))
    yu = pl.BlockSpec((w, tb), lambda i: (0, i)) if transposed else tm
    return pl.pallas_call(
        functools.partial(_hy_post_kernel, transposed=transposed),
        grid=(l // tb,),
        in_specs=[yu, yu, tm, pl.BlockSpec(bias.shape, lambda i: (0, 0))],
        out_specs=tm,
        out_shape=jax.ShapeDtypeStruct((l, w), BF16),
        compiler_params=_cparams("parallel"),
        name="hyena_post",
    )(y, u, x0, bias)


def hyena(z, short_w, short_b, filt, bias):
    l = z.shape[0]
    filt_t, hb0 = hyena_filter(l, *filt)
    if l == FFT_B * FFT_B // 2:
        u_t, x0 = hyena_pre(z, short_w, short_b, True)
        y_t = hyena_long_conv(u_t, filt_t)
        return hyena_post(y_t, u_t, x0, bias.reshape(-1, 1) + hb0, True)
    u, x0 = hyena_pre(z, short_w, short_b, False)
    y = hyena_ctx_conv(u, filt_t.T)
    return hyena_post(y, u, x0, (bias.reshape(-1, 1) + hb0).T, False)


def kernel(x, c, ctx, c_ctx, w_mod, b_mod, norm1_w, norm2_w, ffn_w_gate, ffn_w_up, ffn_w_down, even_w_in, even_w_out, dn_conv_w, dn_a_log, dn_dt_bias, dn_onorm_w, hy_short_w, hy_short_b, hy_f_w1, hy_f_b1, hy_f_w2, hy_f_b2, hy_f_w3, hy_f_b3, hy_f_w4, hy_f_freq, hy_bias, na_w_qkv, na_rpb, na_w_out, final_norm_w):
    d = D_MODEL
    xl, xc = x[0], ctx[0]
    mod = modulation(jnp.concatenate([c, c_ctx[None]], axis=0), w_mod, b_mod)

    def chunk(layer, row, idx):
        return mod[layer, row:row + 1, idx * d:(idx + 1) * d]

    wd16 = ffn_w_down.astype(BF16)
    for layer in range(DEPTH):
        need_ctx = layer < DEPTH - 1
        ml = [chunk(layer, 0, j) for j in range(N_MOD)]
        mc = [chunk(layer, 1, j) for j in range(N_MOD)]
        n1 = norm1_w[layer][None]
        n2 = norm2_w[layer][None]
        i = layer // 2
        hl = norm_mod(xl, n1, ml[1], ml[0])
        hc = norm_mod(xc, n1, mc[1], mc[0])
        if layer % 2 == 0:
            w_in = even_w_in[i]
            w_in = jnp.concatenate([w_in[:, :4 * DN_WIDTH], w_in[:, DN_IN:], w_in[:, 4 * DN_WIDTH:DN_IN]],
                                   axis=1).astype(BF16)
            zl = projection(hl, w_in, F32)
            zc = projection(hc, w_in, F32)
            dn_c, dn_l = gated_deltanet(zc, zl, dn_conv_w[i], dn_a_log[i], dn_dt_bias[i], dn_onorm_w[i])
            filt = (hy_f_w1[i], hy_f_b1[i], hy_f_w2[i], hy_f_b2[i], hy_f_w3[i], hy_f_b3[i], hy_f_w4[i],
                    hy_f_freq[i])
            yl = (dn_l, hyena(zl, hy_short_w[i], hy_short_b[i], filt, hy_bias[i]))
            w_out = even_w_out[i].astype(BF16)
            if need_ctx:
                yc = (dn_c, hyena(zc, hy_short_w[i], hy_short_b[i], filt, hy_bias[i]))
        else:
            zl = projection(hl, na_w_qkv, BF16, layer=i, scaled_cols=D_MODEL, scale=NA_DIM ** -0.5)
            zc = projection(hc, na_w_qkv, BF16, layer=i, scaled_cols=D_MODEL, scale=NA_DIM ** -0.5)
            ya = na_attention(zl, zc, na_bias_tiles(na_rpb[i]))
            yl = (ya, ya)
            w_out = na_w_out[i].astype(BF16)
            if need_ctx:
                ya = ctx_attention(zc)
                yc = (ya, ya)
        xl, hl = proj_residual(*yl, w_out, xl, ml[2], n2, ml[4], ml[3])
        xl = ffn_block(xl, hl, ml[5], ffn_w_gate, ffn_w_up, wd16, layer)
        if need_ctx:
            xc, hc = proj_residual(*yc, w_out, xc, mc[2], n2, mc[4], mc[3])
            xc = ffn_block(xc, hc, mc[5], ffn_w_gate, ffn_w_up, wd16, layer)
    return final_norm(xl, final_norm_w[None])[None]
```

```python
import functools
import math

import jax
import jax.numpy as jnp
import numpy as np
from jax import lax
from jax.experimental import pallas as pl
from jax.experimental.pallas import tpu as pltpu

D_MODEL = 2048
SEQ = 8192
DEPTH = 4
GRID_W = 64
ROWS = SEQ // GRID_W
CTX_LEN = 256
ALL_ROWS = SEQ + CTX_LEN
NORM_EPS = 1e-6
N_MOD = 6

DN_HEADS = 8
DN_DIM = 128
DN_WIDTH = DN_HEADS * DN_DIM
DN_CONV = 5
DN_CHUNK = 64
DN_IN = 4 * DN_WIDTH + 4 * DN_HEADS

HY_WIDTH = D_MODEL - DN_WIDTH
HY_BANDS = 16
HY_DECAY_TARGET = 1e-2
HY_STRONG_DECAY_PCT = 0.3
HY_WEAK_DECAY_PCT = 1.5
HY_IN = 3 * HY_WIDTH
EVEN_IN = DN_IN + HY_IN

NA_HEADS = D_MODEL // 128
NA_DIM = 128
WIN_R = 8
WIN_C = 16
FFN_HIDDEN = -(-8 * D_MODEL // (3 * 256)) * 256

LANES = 128
VMEM_LIMIT = 56 * 1024 * 1024
NEG = -1e30

F32 = jnp.float32
BF16 = jnp.bfloat16


def _cparams(*sem):
    return pltpu.CompilerParams(dimension_semantics=sem, vmem_limit_bytes=VMEM_LIMIT)


def _sigmoid(x):
    return 1.0 / (1.0 + jnp.exp(-x))


MOD_TN = 1024


def _mod_kernel(c_ref, w_ref, b_ref, o_ref):
    rows = []
    for r in range(2):
        c = c_ref[r]
        a = c * _sigmoid(c)
        cols = [jnp.sum(w_ref[0, :, j * LANES:(j + 1) * LANES] * a, axis=0, keepdims=True)
                for j in range(MOD_TN // LANES)]
        rows.append(jnp.concatenate(cols, axis=1))
    o_ref[0] = jnp.concatenate(rows, axis=0) + b_ref[0]


def modulation(c_pair, w_mod, b_mod):
    n = N_MOD * D_MODEL
    cb = jnp.broadcast_to(c_pair[:, :, None], (2, D_MODEL, LANES))
    return pl.pallas_call(
        _mod_kernel,
        grid=(DEPTH, n // MOD_TN),
        in_specs=[pl.BlockSpec((2, D_MODEL, LANES), lambda l, j: (0, 0, 0)),
                  pl.BlockSpec((1, D_MODEL, MOD_TN), lambda l, j: (l, 0, j)),
                  pl.BlockSpec((1, 1, MOD_TN), lambda l, j: (l, 0, j))],
        out_specs=pl.BlockSpec((1, 2, MOD_TN), lambda l, j: (l, 0, j)),
        out_shape=jax.ShapeDtypeStruct((DEPTH, 2, n), F32),
        compiler_params=_cparams("parallel", "parallel"),
        name="modulation",
    )(cb, w_mod, b_mod.reshape(DEPTH, 1, n))


ROW_TILE = 256


def _kind_spec(d):
    return pl.BlockSpec((None, 1, d), lambda i: (jnp.where(i >= SEQ // ROW_TILE, 1, 0), 0, 0))


def _norm_mod_kernel(x_ref, nw_ref, sc_ref, sh_ref, h_ref):
    x = x_ref[...]
    y = x * lax.rsqrt(jnp.mean(x * x, axis=-1, keepdims=True) + NORM_EPS)
    h_ref[...] = (y * nw_ref[...] * (1.0 + sc_ref[...]) + sh_ref[...]).astype(h_ref.dtype)


def norm_mod(x, nw, sc, sh):
    m, d = x.shape
    row = pl.BlockSpec((ROW_TILE, d), lambda i: (i, 0))
    return pl.pallas_call(
        _norm_mod_kernel,
        grid=(m // ROW_TILE,),
        in_specs=[row, pl.BlockSpec((1, d), lambda i: (0, 0)), _kind_spec(d), _kind_spec(d)],
        out_specs=row,
        out_shape=jax.ShapeDtypeStruct((m, d), BF16),
        compiler_params=_cparams("parallel"),
        name="norm_mod",
    )(x, nw, sc, sh)


PROJ_TN = 512
ROW_TILES = 8


def _weight_spec(w, tn, layer):
    d = w.shape[-2]
    if layer is None:
        return pl.BlockSpec((d, tn), lambda j, i: (0, j))
    return pl.BlockSpec((None, d, tn), lambda j, i: (layer, 0, j))


def _proj_kernel(h_ref, w_ref, o_ref, *scratch, scaled_cols, scale):
    if scratch:
        wb_ref, = scratch

        @pl.when(pl.program_id(1) == 0)
        def _():
            wb_ref[...] = w_ref[...].astype(BF16)
    else:
        wb_ref = w_ref
    y = jnp.dot(h_ref[...], wb_ref[...], preferred_element_type=F32)
    if scaled_cols:
        y = y * jnp.where(pl.program_id(0) * o_ref.shape[1] < scaled_cols, scale, 1.0)
    o_ref[...] = y.astype(o_ref.dtype)


def projection(h, w, rows, out_dtype, layer=None, scaled_cols=0, scale=1.0):
    m, d = h.shape
    n = w.shape[-1]
    tm = rows // ROW_TILES
    tn = PROJ_TN
    assert scaled_cols % tn == 0
    scratch = [] if w.dtype == BF16 else [pltpu.VMEM((d, tn), BF16)]
    return pl.pallas_call(
        functools.partial(_proj_kernel, scaled_cols=scaled_cols, scale=scale),
        grid=(pl.cdiv(n, tn), ROW_TILES),
        in_specs=[pl.BlockSpec((tm, d), lambda j, i: (i, 0)), _weight_spec(w, tn, layer)],
        out_specs=pl.BlockSpec((tm, tn), lambda j, i: (i, j)),
        out_shape=jax.ShapeDtypeStruct((m, n), out_dtype),
        scratch_shapes=scratch,
        compiler_params=_cparams("parallel", "arbitrary"),
        name="projection",
    )(h, w)


FFN_TF = 512
FFN_TN = 256


def _ffn_up_kernel(h_ref, wg_ref, wu_ref, a_ref, wgb_ref, wub_ref):
    @pl.when(pl.program_id(1) == 0)
    def _():
        wgb_ref[...] = wg_ref[...].astype(BF16)
        wub_ref[...] = wu_ref[...].astype(BF16)

    h = h_ref[...]
    gate = jnp.dot(h, wgb_ref[...], preferred_element_type=F32)
    up = jnp.dot(h, wub_ref[...], preferred_element_type=F32)
    a_ref[...] = (gate * _sigmoid(gate) * up).astype(a_ref.dtype)


def _ffn_down_kernel(a_ref, wd_ref, x_ref, g_ref, o_ref):
    tm = x_ref.shape[0]
    y = jnp.dot(a_ref[...], wd_ref[...].astype(BF16), preferred_element_type=F32)
    row = pl.program_id(0) * tm + lax.broadcasted_iota(jnp.int32, (tm, 1), 0)
    o_ref[...] = x_ref[...] + jnp.where(row < SEQ, g_ref[0], g_ref[1]) * y


def ffn_block(x, h, rows, g, wg, wu, wd, layer):
    m, d = x.shape
    hid = wg.shape[2]
    tm = rows // ROW_TILES
    a = pl.pallas_call(
        _ffn_up_kernel,
        grid=(hid // FFN_TF, ROW_TILES),
        in_specs=[pl.BlockSpec((tm, d), lambda f, i: (i, 0)),
                  _weight_spec(wg, FFN_TF, layer), _weight_spec(wu, FFN_TF, layer)],
        out_specs=pl.BlockSpec((tm, FFN_TF), lambda f, i: (i, f)),
        out_shape=jax.ShapeDtypeStruct((m, hid), BF16),
        scratch_shapes=[pltpu.VMEM((d, FFN_TF), BF16), pltpu.VMEM((d, FFN_TF), BF16)],
        compiler_params=_cparams("parallel", "arbitrary"),
        name="ffn_up",
    )(h, wg, wu)
    return pl.pallas_call(
        _ffn_down_kernel,
        grid=(ROW_TILES, d // FFN_TN),
        in_specs=[pl.BlockSpec((tm, hid), lambda i, j: (i, 0)),
                  pl.BlockSpec((None, hid, FFN_TN), lambda i, j: (layer, 0, j)),
                  pl.BlockSpec((tm, FFN_TN), lambda i, j: (i, j)),
                  pl.BlockSpec((2, 1, FFN_TN), lambda i, j: (0, 0, j))],
        out_specs=pl.BlockSpec((tm, FFN_TN), lambda i, j: (i, j)),
        out_shape=jax.ShapeDtypeStruct((m, d), F32),
        compiler_params=_cparams("parallel", "arbitrary"),
        name="ffn_down",
    )(a, wd, x, g)


def _proj_res_kernel(yal_ref, ybl_ref, yac_ref, ybc_ref, w_ref, x_ref, g_ref, nw_ref, sc_ref, sh_ref,
                     o_ref, h_ref, wb_ref):
    i = pl.program_id(0)

    @pl.when(i == 0)
    def _():
        wb_ref[...] = w_ref[...].astype(BF16)

    lat = i < SEQ // ROW_TILE
    ya = jnp.where(lat, yal_ref[...], yac_ref[...])
    yb = jnp.where(lat, ybl_ref[...], ybc_ref[...])
    ka = ya.shape[1]
    y = (jnp.dot(ya, wb_ref[:ka, :], preferred_element_type=F32)
         + jnp.dot(yb, wb_ref[ka:, :], preferred_element_type=F32))
    x = x_ref[...] + g_ref[...] * y
    o_ref[...] = x
    xn = x * lax.rsqrt(jnp.mean(x * x, axis=-1, keepdims=True) + NORM_EPS)
    h_ref[...] = (xn * nw_ref[...] * (1.0 + sc_ref[...]) + sh_ref[...]).astype(h_ref.dtype)


def proj_residual(y_lat, y_ctx, w, layer, x, rows, g, nw, sc, sh):
    m, d = x.shape
    half = d // 2
    n_lat = SEQ // ROW_TILE

    def y_specs(pair, ctx):
        second = 1 if pair[1] is pair[0] else 0
        rb = (lambda i: 0) if ctx else (lambda i: jnp.minimum(i, n_lat - 1))
        return [pl.BlockSpec((ROW_TILE, half), lambda i: (rb(i), 0)),
                pl.BlockSpec((ROW_TILE, half), lambda i: (rb(i), second))]

    row = pl.BlockSpec((ROW_TILE, d), lambda i: (i, 0))
    return pl.pallas_call(
        _proj_res_kernel,
        grid=(rows // ROW_TILE,),
        in_specs=y_specs(y_lat, False) + y_specs(y_ctx, True) + [
            pl.BlockSpec((None, d, d), lambda i: (layer, 0, 0), pipeline_mode=pl.Buffered(1)),
            row, _kind_spec(d), pl.BlockSpec((1, d), lambda i: (0, 0)), _kind_spec(d), _kind_spec(d)],
        out_specs=[row, row],
        out_shape=[jax.ShapeDtypeStruct((m, d), F32), jax.ShapeDtypeStruct((m, d), BF16)],
        scratch_shapes=[pltpu.VMEM((d, d), BF16)],
        compiler_params=_cparams("arbitrary"),
        name="proj_residual",
    )(y_lat[0], y_lat[1], y_ctx[0], y_ctx[1], w, x, g, nw, sc, sh)


def _final_norm_kernel(x_ref, w_ref, o_ref):
    x = x_ref[...]
    o_ref[...] = x * lax.rsqrt(jnp.mean(x * x, axis=-1, keepdims=True) + NORM_EPS) * w_ref[...]


def final_norm(x, rows, w):
    d = x.shape[1]
    tm = 512
    return pl.pallas_call(
        _final_norm_kernel,
        grid=(rows // tm,),
        in_specs=[pl.BlockSpec((tm, d), lambda i: (i, 0)), pl.BlockSpec((1, d), lambda i: (0, 0))],
        out_specs=pl.BlockSpec((tm, d), lambda i: (i, 0)),
        out_shape=jax.ShapeDtypeStruct((rows, d), F32),
        compiler_params=_cparams("parallel"),
        name="final_norm",
    )(x, w)


NA_R = 8
NA_KD = (NA_R + WIN_R) // 2
NA_T = 2 * WIN_R
NA_HP = 2


def _na_bias_kernel(rpb_ref, o_ref):
    h = pl.program_id(0)
    c = lax.broadcasted_iota(jnp.int32, (GRID_W, LANES), 0)
    lane = lax.broadcasted_iota(jnp.int32, (GRID_W, LANES), 1)
    left = lane < GRID_W
    kc = jnp.where(left, lane, lane - GRID_W)
    cs = jnp.clip(c - WIN_C // 2, 0, GRID_W - WIN_C)
    inwin = (kc >= cs) & (kc < cs + WIN_C)
    diff = kc - c + (WIN_C - 1)
    n_ro, n_co = 2 * WIN_R - 1, 2 * WIN_C - 1
    for t in range(NA_T):
        def body(d, acc, t=t):
            vl = rpb_ref[(h * n_ro + (t - 1)) * n_co + d] if t >= 1 else jnp.float32(NEG)
            vr = rpb_ref[(h * n_ro + t) * n_co + d] if t < n_ro else jnp.float32(NEG)
            return jnp.where(diff == d, jnp.where(left, vl, vr), acc)

        acc = lax.fori_loop(0, n_co, body, jnp.full((GRID_W, LANES), NEG, F32))
        o_ref[0, t] = jnp.where(inwin, acc, NEG)


def na_bias_tiles(rpb):
    return pl.pallas_call(
        _na_bias_kernel,
        grid=(NA_HEADS,),
        in_specs=[pl.BlockSpec(memory_space=pltpu.SMEM)],
        out_specs=pl.BlockSpec((1, NA_T, GRID_W, LANES), lambda h: (h, 0, 0, 0)),
        out_shape=jax.ShapeDtypeStruct((NA_HEADS, NA_T, GRID_W, LANES), F32),
        compiler_params=_cparams("parallel"),
        name="na_bias_tiles",
    )(rpb.reshape(-1))


def _na_kernel(q_ref, k_ref, v_ref, kc_ref, vc_ref, tt_ref, o_ref):
    r0 = pl.program_id(1) * NA_R
    ks = jnp.clip(r0 - WIN_R // 2, 0, ROWS - 2 * NA_KD)
    span = pl.ds(pl.multiple_of(ks * GRID_W, LANES), NA_KD * LANES)
    heads = range(NA_HP)
    cols = [slice(h * NA_DIM, (h + 1) * NA_DIM) for h in heads]
    contract_last = (((1,), (1,)), ((), ()))
    q = [q_ref[:, cols[h]] for h in heads]
    s_loc = [lax.dot_general(q[h], k_ref[span, cols[h]], contract_last, preferred_element_type=F32) for h in heads]
    s_ctx = [lax.dot_general(q[h], kc_ref[:, cols[h]], contract_last, preferred_element_type=F32) for h in heads]

    left = lax.broadcasted_iota(jnp.int32, (GRID_W, LANES), 1) < GRID_W
    tile_sel = []
    for qi in range(NA_R):
        r = r0 + qi
        rs = jnp.clip(r - WIN_R // 2, 0, ROWS - WIN_R)
        for dj in range(NA_KD):
            kl = ks + 2 * dj
            vl = ((kl >= rs) & (kl < rs + WIN_R)).astype(jnp.int32)
            vr = ((kl + 1 >= rs) & (kl + 1 < rs + WIN_R)).astype(jnp.int32)
            tile_sel.append((jnp.clip(kl - r + WIN_R, 0, NA_T - 1), jnp.where(left, vl, vr) > 0))
    for h in heads:
        rows = [jnp.concatenate([jnp.where(valid, tt_ref[h, t], NEG)
                                 for t, valid in tile_sel[qi * NA_KD:(qi + 1) * NA_KD]], axis=1)
                for qi in range(NA_R)]
        s_loc[h] = s_loc[h] + jnp.concatenate(rows, axis=0)

    m = [jnp.maximum(jnp.max(s_loc[h], axis=-1, keepdims=True), jnp.max(s_ctx[h], axis=-1, keepdims=True))
         for h in heads]
    p_loc = [jnp.exp(s_loc[h] - m[h]) for h in heads]
    p_ctx = [jnp.exp(s_ctx[h] - m[h]) for h in heads]
    denom = [jnp.sum(p_loc[h], axis=-1, keepdims=True) + jnp.sum(p_ctx[h], axis=-1, keepdims=True) for h in heads]
    o = [jnp.dot(p_loc[h].astype(BF16), v_ref[span, cols[h]], preferred_element_type=F32)
         + jnp.dot(p_ctx[h].astype(BF16), vc_ref[:, cols[h]], preferred_element_type=F32) for h in heads]
    for h in heads:
        o_ref[:, cols[h]] = (o[h] / denom[h]).astype(o_ref.dtype)


def na_attention(z, tiles):
    tq = NA_R * GRID_W
    w = NA_HP * NA_DIM
    ng = NA_HEADS // NA_HP
    return pl.pallas_call(
        _na_kernel,
        grid=(ng, ROWS // NA_R),
        in_specs=[pl.BlockSpec((tq, w), lambda g, i: (i, g)),
                  pl.BlockSpec((SEQ, w), lambda g, i: (0, ng + g)),
                  pl.BlockSpec((SEQ, w), lambda g, i: (0, 2 * ng + g)),
                  pl.BlockSpec((CTX_LEN, w), lambda g, i: (SEQ // CTX_LEN, ng + g)),
                  pl.BlockSpec((CTX_LEN, w), lambda g, i: (SEQ // CTX_LEN, 2 * ng + g)),
                  pl.BlockSpec((NA_HP, NA_T, GRID_W, LANES), lambda g, i: (g, 0, 0, 0))],
        out_specs=pl.BlockSpec((tq, w), lambda g, i: (i, g)),
        out_shape=jax.ShapeDtypeStruct((SEQ, D_MODEL), BF16),
        compiler_params=_cparams("parallel", "arbitrary"),
        name="na_attention",
    )(z, z, z, z, z, tiles)


def _ctx_attn_kernel(q_ref, k_ref, v_ref, o_ref):
    s = lax.dot_general(q_ref[...], k_ref[...], (((1,), (1,)), ((), ())), preferred_element_type=F32)
    p = jnp.exp(s - jnp.max(s, axis=-1, keepdims=True))
    o = jnp.dot(p.astype(BF16), v_ref[...], preferred_element_type=F32)
    o_ref[...] = (o / jnp.sum(p, axis=-1, keepdims=True)).astype(o_ref.dtype)


def ctx_attention(z):
    nh = NA_HEADS
    blk = lambda off: pl.BlockSpec((CTX_LEN, NA_DIM), lambda h: (SEQ // CTX_LEN, off + h))
    return pl.pallas_call(
        _ctx_attn_kernel,
        grid=(nh,),
        in_specs=[blk(0), blk(nh), blk(2 * nh)],
        out_specs=pl.BlockSpec((CTX_LEN, NA_DIM), lambda h: (0, h)),
        out_shape=jax.ShapeDtypeStruct((CTX_LEN, D_MODEL), BF16),
        compiler_params=_cparams("parallel"),
        name="ctx_attention",
    )(z, z, z)


DN_SC = 256
HALO = 8
N_GB = 2 * DN_HEADS


def _softplus(x):
    return jnp.maximum(x, 0.0) + jnp.log1p(jnp.exp(-jnp.abs(x)))


def _short_conv(main_ref, prev_ref, next_ref, w_ref, col0, blk, nblk):
    ksize = w_ref.shape[0]
    tb, width = main_ref.shape
    prev = jnp.where(blk > 0, prev_ref[...], 0.0)
    nxt = jnp.where(blk < nblk - 1, next_ref[...], 0.0)
    xf = jnp.concatenate([prev, main_ref[...], nxt], axis=0)
    n = xf.shape[0]
    acc = None
    for j in range(ksize):
        sh = (ksize // 2 - j) % n
        xs = xf if sh == 0 else pltpu.roll(xf, sh, 0)
        term = xs[HALO:HALO + tb] * w_ref[j:j + 1, col0:col0 + width]
        acc = term if acc is None else acc + term
    return acc


def _chunk_masks(n):
    r = lax.broadcasted_iota(jnp.int32, (n, n), 0)
    c = lax.broadcasted_iota(jnp.int32, (n, n), 1)
    return r, c, (r // DN_CHUNK) == (c // DN_CHUNK)


def _dn_scalars_kernel(za_ref, zb_ref, zat_ref, alog_ref, dtb_ref, alogt_ref, dtbt_ref,
                       gc_ref, gt_ref, beta_ref, gct_ref):
    hi = lax.Precision.HIGHEST
    g = -jnp.exp(alog_ref[...]) * _softplus(za_ref[...] + dtb_ref[...])
    beta_ref[...] = _sigmoid(zb_ref[...])
    r, c, same = _chunk_masks(DN_SC)
    lo = (same & (c <= r)).astype(F32)
    up = (same & (c >= r)).astype(F32)
    fwd_col = lax.broadcasted_iota(jnp.int32, (DN_SC, N_GB), 1) < DN_HEADS
    gc_ref[...] = jnp.where(fwd_col, jnp.dot(lo, g, precision=hi), jnp.dot(up, g, precision=hi))
    gt_ref[...] = jnp.dot(same.astype(F32), g, precision=hi)
    g_t = -jnp.exp(alogt_ref[...]) * _softplus(zat_ref[...] + dtbt_ref[...])
    fwd_row = lax.broadcasted_iota(jnp.int32, (N_GB, DN_SC), 0) < DN_HEADS
    gct_ref[...] = jnp.where(fwd_row, jnp.dot(g_t, up, precision=hi), jnp.dot(g_t, lo, precision=hi))


def dn_scalars(za, zb, a_log, dt_bias):
    l = za.shape[0]
    col = pl.BlockSpec((DN_SC, N_GB), lambda i: (i, 0))
    row = pl.BlockSpec((N_GB, DN_SC), lambda i: (0, i))
    prow = pl.BlockSpec((1, N_GB), lambda i: (0, 0))
    pcol = pl.BlockSpec((N_GB, 1), lambda i: (0, 0))
    cshape = jax.ShapeDtypeStruct((l, N_GB), F32)
    return pl.pallas_call(
        _dn_scalars_kernel,
        grid=(l // DN_SC,),
        in_specs=[col, col, row, prow, prow, pcol, pcol],
        out_specs=[col, col, col, row],
        out_shape=[cshape, cshape, cshape, jax.ShapeDtypeStruct((N_GB, l), F32)],
        compiler_params=_cparams("parallel"),
        name="dn_scalars",
    )(za, zb, za.T, a_log.reshape(1, N_GB), dt_bias.reshape(1, N_GB), a_log.reshape(N_GB, 1),
      dt_bias.reshape(N_GB, 1))


def _gdn_kernel(zq_ref, zqp_ref, zqn_ref, zk_ref, zkp_ref, zkn_ref, zv_ref, zvp_ref, zvn_ref, cw_ref,
                gc_ref, gt_ref, beta_ref, gct_ref, s0_ref, o_ref, sfin_ref, state_ref, *, bwd, nblk):
    i = pl.program_id(0)
    blk = nblk - 1 - i if bwd else i

    @pl.when(i == 0)
    def _():
        state_ref[...] = s0_ref[...]

    def conv_silu(main_ref, prev_ref, next_ref, part):
        acc = _short_conv(main_ref, prev_ref, next_ref, cw_ref, part * DN_WIDTH, blk, nblk)
        return acc * _sigmoid(acc)

    q_all = conv_silu(zq_ref, zqp_ref, zqn_ref, 0)
    k_all = conv_silu(zk_ref, zkp_ref, zkn_ref, 1)
    v_all = conv_silu(zv_ref, zvp_ref, zvn_ref, 2)

    r, c, same64 = _chunk_masks(DN_SC)
    tri = same64 & ((c >= r) if bwd else (c <= r))
    offdiag = r != c
    same16 = (r // 16) == (c // 16)
    same32 = (r // 32) == (c // 32)
    contract_last = (((1,), (1,)), ((), ()))
    contract_first = (((0,), (0,)), ((), ()))
    nc = DN_SC // DN_CHUNK
    dot = functools.partial(jnp.dot, preferred_element_type=F32)

    heads = range(DN_HEADS)
    col0 = DN_HEADS if bwd else 0
    gc_c = [gc_ref[:, col0 + h:col0 + h + 1] for h in heads]
    gt_c = [gt_ref[:, col0 + h:col0 + h + 1] for h in heads]
    beta_c = [beta_ref[:, col0 + h:col0 + h + 1] for h in heads]
    q, k, v, k16, dec, qk16, m, p, e = ([None] * DN_HEADS for _ in range(9))
    for h in heads:
        sl = slice(h * DN_DIM, (h + 1) * DN_DIM)
        qh, kh = q_all[:, sl], k_all[:, sl]
        q[h] = qh * (lax.rsqrt(jnp.sum(qh * qh, axis=-1, keepdims=True) + NORM_EPS) * DN_DIM ** -0.5)
        k[h] = kh * lax.rsqrt(jnp.sum(kh * kh, axis=-1, keepdims=True) + NORM_EPS)
        v[h] = v_all[:, sl]
        k16[h] = k[h].astype(BF16)
        dec[h] = jnp.exp(jnp.where(tri, gc_c[h] - gct_ref[col0 + h:col0 + h + 1, :], NEG))
    for h in heads:
        kk = lax.dot_general(k16[h], k16[h], contract_last, preferred_element_type=F32)
        m[h] = jnp.where(offdiag, kk * beta_c[h] * dec[h], 0.0)
        qk16[h] = (lax.dot_general(q[h].astype(BF16), k16[h], contract_last, preferred_element_type=F32)
                   * dec[h]).astype(BF16)

    for h in heads:
        p[h] = jnp.where(same16, -m[h], 0.0)
        e[h] = p[h]
    for _ in range(3):
        for h in heads:
            p16 = p[h].astype(BF16)
            p[h] = dot(p16, p16)
        for h in heads:
            e[h] = e[h] + p[h] + dot(e[h].astype(BF16), p[h].astype(BF16))
    for inner, outer in ((same16, same32), (same32, None)):
        y = [None] * DN_HEADS
        for h in heads:
            cm = jnp.where(~inner if outer is None else (outer & ~inner), m[h], 0.0)
            y[h] = cm + dot(cm.astype(BF16), e[h].astype(BF16))
        for h in heads:
            e[h] = e[h] - y[h] - dot(e[h].astype(BF16), y[h].astype(BF16))

    u, w16, qg16, kd16, s = ([None] * DN_HEADS for _ in range(5))
    for h in heads:
        eg = jnp.exp(gc_c[h])
        rhs = jnp.concatenate([v[h] * beta_c[h], k[h] * (beta_c[h] * eg)], axis=1)
        uw = rhs + dot(e[h].astype(BF16), rhs.astype(BF16))
        u[h] = uw[:, :DN_DIM]
        w16[h] = uw[:, DN_DIM:].astype(BF16)
        qg16[h] = (q[h] * eg).astype(BF16)
        kd16[h] = (k[h] * jnp.exp(gt_c[h] - gc_c[h])).astype(BF16)
        s[h] = state_ref[h]

    outs = [[None] * nc for _ in heads]
    for ci in (range(nc - 1, -1, -1) if bwd else range(nc)):
        rows = slice(ci * DN_CHUNK, (ci + 1) * DN_CHUNK)
        ws = [dot(jnp.concatenate([w16[h][rows], qg16[h][rows]], axis=0), s[h].astype(BF16)) for h in heads]
        vn16 = [(u[h][rows] - ws[h][:DN_CHUNK]).astype(BF16) for h in heads]
        for h in heads:
            outs[h][ci] = ws[h][DN_CHUNK:] + dot(qk16[h][rows, ci * DN_CHUNK:(ci + 1) * DN_CHUNK], vn16[h])
        for h in heads:
            gl = jnp.exp(gt_c[h][ci * DN_CHUNK:ci * DN_CHUNK + 1, :])
            s[h] = s[h] * gl + lax.dot_general(kd16[h][rows], vn16[h], contract_first, preferred_element_type=F32)
    for h in heads:
        state_ref[h] = s[h]
        o_ref[:, h * DN_DIM:(h + 1) * DN_DIM] = jnp.concatenate(outs[h], axis=0)

    @pl.when(i == nblk - 1)
    def _():
        sfin_ref[...] = state_ref[...]


def gdn_scan(z, conv_w, gc, gt, beta, gct, s0, bwd, row0, l):
    nblk = l // DN_SC
    per = DN_SC // HALO
    blk0 = row0 // DN_SC
    last_halo = z.shape[0] // HALO - 1

    def rel(i):
        return nblk - 1 - i if bwd else i

    def b(i):
        return blk0 + rel(i)

    def part_specs(part):
        return [pl.BlockSpec((DN_SC, DN_WIDTH), lambda i: (b(i), part)),
                pl.BlockSpec((HALO, DN_WIDTH), lambda i: (jnp.maximum(b(i) * per - 1, 0), part)),
                pl.BlockSpec((HALO, DN_WIDTH), lambda i: (jnp.minimum((b(i) + 1) * per, last_halo), part))]

    col = pl.BlockSpec((DN_SC, N_GB), lambda i: (b(i), 0))
    st = pl.BlockSpec((DN_HEADS, DN_DIM, DN_DIM), lambda i: (0, 0, 0))
    return pl.pallas_call(
        functools.partial(_gdn_kernel, bwd=bwd, nblk=nblk),
        grid=(nblk,),
        in_specs=part_specs(0) + part_specs(1) + part_specs(2) + [
            pl.BlockSpec((DN_CONV, 3 * DN_WIDTH), lambda i: (0, 0)), col, col, col,
            pl.BlockSpec((N_GB, DN_SC), lambda i: (0, b(i))), st],
        out_specs=[pl.BlockSpec((DN_SC, DN_WIDTH), lambda i: (rel(i), 0)), st],
        out_shape=[jax.ShapeDtypeStruct((l, DN_WIDTH), F32),
                   jax.ShapeDtypeStruct((DN_HEADS, DN_DIM, DN_DIM), F32)],
        scratch_shapes=[pltpu.VMEM((DN_HEADS, DN_DIM, DN_DIM), F32)],
        compiler_params=_cparams("arbitrary"),
        name="gdn_scan_bwd" if bwd else "gdn_scan_fwd",
    )(z, z, z, z, z, z, z, z, z, conv_w, gc, gt, beta, gct, s0)


def _dn_out_kernel(of_ref, ob_ref, gate_ref, nw_ref, y_ref):
    for h in range(DN_HEADS):
        sl = slice(h * DN_DIM, (h + 1) * DN_DIM)
        o = of_ref[:, sl] + ob_ref[:, sl]
        o = o * lax.rsqrt(jnp.mean(o * o, axis=-1, keepdims=True) + NORM_EPS) * nw_ref[...]
        gate = gate_ref[:, sl]
        y_ref[:, sl] = (o * (gate * _sigmoid(gate))).astype(y_ref.dtype)


def dn_output(o_f, o_b, z, row0, onorm_w):
    l = o_f.shape[0]
    tm = min(l, 512)
    blk0 = row0 // tm
    blk = pl.BlockSpec((tm, DN_WIDTH), lambda i: (i, 0))
    return pl.pallas_call(
        _dn_out_kernel,
        grid=(l // tm,),
        in_specs=[blk, blk, pl.BlockSpec((tm, DN_WIDTH), lambda i: (blk0 + i, 3)),
                  pl.BlockSpec((1, DN_DIM), lambda i: (0, 0))],
        out_specs=blk,
        out_shape=jax.ShapeDtypeStruct((l, DN_WIDTH), BF16),
        compiler_params=_cparams("parallel"),
        name="dn_output",
    )(o_f, o_b, z, onorm_w.reshape(1, DN_DIM))


def gated_deltanet(z, need_ctx, conv_w, a_log, dt_bias, onorm_w):
    a0, b0 = EVEN_IN - 2 * N_GB, EVEN_IN - N_GB
    scal = dn_scalars(z[:, a0:b0], z[:, b0:b0 + N_GB], a_log, dt_bias)
    s0 = jnp.zeros((DN_HEADS, DN_DIM, DN_DIM), F32)
    outs_c, outs_l = [], []
    for bwd in (False, True):
        oc, s_ctx = gdn_scan(z, conv_w, *scal, s0, bwd, SEQ, CTX_LEN)
        ol, _ = gdn_scan(z, conv_w, *scal, s_ctx, bwd, 0, SEQ)
        outs_c.append(oc)
        outs_l.append(ol)
    y_ctx = dn_output(*outs_c, z, SEQ, onorm_w) if need_ctx else None
    return dn_output(*outs_l, z, 0, onorm_w), y_ctx


HY_SHORT = 3
HY_EMB = 1 + 2 * HY_BANDS
HY_EMB_PAD = 40
HY_TB = 256
FFT_B = 128
HY_CB = 16
HY_COL0 = 4


def _hy_pre_kernel(x0_ref, x0p_ref, x0n_ref, x1_ref, x1p_ref, x1n_ref, v_ref, vp_ref, vn_ref, w_ref, b_ref,
                   u_ref, x0c_ref, *, nblk, transpose_u):
    blk = pl.program_id(0)
    w = HY_WIDTH
    x0 = _short_conv(x0_ref, x0p_ref, x0n_ref, w_ref, 0, blk, nblk) + b_ref[:, :w]
    x1 = _short_conv(x1_ref, x1p_ref, x1n_ref, w_ref, w, blk, nblk) + b_ref[:, w:2 * w]
    v = _short_conv(v_ref, vp_ref, vn_ref, w_ref, 2 * w, blk, nblk) + b_ref[:, 2 * w:]
    u = v * x1
    x0c_ref[...] = x0
    u_ref[...] = u.T if transpose_u else u


def hyena_pre(z, row0, l, short_w, short_b, transpose_u):
    tb = min(HY_TB, l)
    nblk = l // tb
    per = tb // HALO
    blk0 = row0 // tb
    last_halo = z.shape[0] // HALO - 1
    w = HY_WIDTH

    def part_specs(part):
        cb = HY_COL0 + part
        return [pl.BlockSpec((tb, w), lambda i: (blk0 + i, cb)),
                pl.BlockSpec((HALO, w), lambda i: (jnp.maximum((blk0 + i) * per - 1, 0), cb)),
                pl.BlockSpec((HALO, w), lambda i: (jnp.minimum((blk0 + i + 1) * per, last_halo), cb))]

    u_shape, u_spec = ((w, l), pl.BlockSpec((w, tb), lambda i: (0, i))) if transpose_u else (
        (l, w), pl.BlockSpec((tb, w), lambda i: (i, 0)))
    return pl.pallas_call(
        functools.partial(_hy_pre_kernel, nblk=nblk, transpose_u=transpose_u),
        grid=(nblk,),
        in_specs=part_specs(0) + part_specs(1) + part_specs(2) + [
            pl.BlockSpec((HY_SHORT, HY_IN), lambda i: (0, 0)), pl.BlockSpec((1, HY_IN), lambda i: (0, 0))],
        out_specs=[u_spec, pl.BlockSpec((tb, w), lambda i: (i, 0))],
        out_shape=[jax.ShapeDtypeStruct(u_shape, F32), jax.ShapeDtypeStruct((l, w), F32)],
        compiler_params=_cparams("parallel"),
        name="hyena_pre",
    )(z, z, z, z, z, z, z, z, z, short_w, short_b.reshape(1, HY_IN))


def _hy_filter_kernel(w1t_ref, b1_ref, w2t_ref, b2_ref, w3t_ref, b3_ref, w4t_ref, fr_ref, band_ref, dl_ref,
                      f_ref, hb0_ref, *, l, fb):
    hi = lax.Precision.HIGHEST
    j = pl.program_id(0)
    second = j >= l // fb
    n = j * fb + lax.broadcasted_iota(jnp.int32, (1, fb), 1)
    pos = jnp.where(second, 2 * l - n, n).astype(F32)
    t = pos / max(l - 1, 1)
    wpos = 2.0 * math.pi * pos / l
    arg = band_ref[...] * wpos
    row = lax.broadcasted_iota(jnp.int32, (HY_EMB_PAD, fb), 0)
    feat = jnp.where(row == 0, t, jnp.where(row <= HY_BANDS, jnp.cos(arg),
                                            jnp.where(row <= 2 * HY_BANDS, -jnp.sin(arg), 0.0)))
    fr = fr_ref[...]
    h = jnp.sin(fr * (jnp.dot(w1t_ref[...], feat, precision=hi) + b1_ref[...]))
    h = jnp.sin(fr * (jnp.dot(w2t_ref[...], h, precision=hi) + b2_ref[...]))
    h = jnp.sin(fr * (jnp.dot(w3t_ref[...], h, precision=hi) + b3_ref[...]))
    window = jnp.exp(-t * dl_ref[...])
    half = pl.multiple_of(jnp.where(second, HY_WIDTH, 0), HY_WIDTH)
    h16 = h.astype(BF16)
    f = jnp.dot(w4t_ref[pl.ds(half, HY_WIDTH), :].astype(BF16), h16, preferred_element_type=F32) * window
    f_ref[...] = jnp.where(n == l, 0.0, f)

    @pl.when(j == 0)
    def _():
        hb0_ref[...] = (jnp.dot(w4t_ref[HY_WIDTH:, :].astype(BF16), h16[:, :LANES], preferred_element_type=F32)
                        * window[:, :LANES])


def hyena_filter(l, w1, b1, w2, b2, w3, b3, w4, freq):
    fb = min(1024, l)
    colv = lambda v: v.reshape(-1, 1)
    bands = np.zeros((HY_EMB_PAD, 1), np.float32)
    bands[1:1 + HY_BANDS, 0] = bands[1 + HY_BANDS:HY_EMB, 0] = np.linspace(1e-4, HY_BANDS - 1, HY_BANDS,
                                                                            dtype=np.float32)
    min_decay = math.log(HY_DECAY_TARGET) / HY_WEAK_DECAY_PCT
    max_decay = math.log(HY_DECAY_TARGET) / HY_STRONG_DECAY_PCT
    deltas = np.abs(np.linspace(min_decay, max_decay, HY_WIDTH, dtype=np.float32)).reshape(-1, 1)
    w1t = jnp.pad(w1.T, ((0, 0), (0, HY_EMB_PAD - HY_EMB)))
    full = lambda a: pl.BlockSpec(a.shape, lambda j: (0,) * a.ndim)
    args = (w1t, colv(b1), w2.T, colv(b2), w3.T, colv(b3), w4.T, colv(freq), jnp.asarray(bands), jnp.asarray(deltas))
    filt, hb0 = pl.pallas_call(
        functools.partial(_hy_filter_kernel, l=l, fb=fb),
        grid=(2 * l // fb,),
        in_specs=[full(a) for a in args],
        out_specs=[pl.BlockSpec((HY_WIDTH, fb), lambda j: (0, j)), pl.BlockSpec((HY_WIDTH, LANES), lambda j: (0, 0))],
        out_shape=[jax.ShapeDtypeStruct((HY_WIDTH, 2 * l), F32), jax.ShapeDtypeStruct((HY_WIDTH, LANES), F32)],
        compiler_params=_cparams("arbitrary"),
        name="hyena_filter",
    )(*args)
    return filt, hb0[:, :1]


def _dft_constants():
    b = FFT_B
    n = b * b
    idx = np.arange(b)
    ang = 2.0 * np.pi * np.outer(idx, idx) / b
    c, s = np.cos(ang), np.sin(ang)
    tw = 2.0 * np.pi * np.outer(idx, idx) / n
    fwd_b = np.concatenate([c, -s], axis=0)
    cs = np.concatenate([c, s], axis=1)
    inv_b = np.concatenate([c[:b // 2], -s[:b // 2]], axis=1) / n
    return (jnp.asarray(fwd_b, BF16), jnp.asarray(cs, BF16), jnp.asarray(np.cos(tw), F32),
            jnp.asarray(-np.sin(tw), F32), jnp.asarray(inv_b, BF16))


def _hy_conv_kernel(u_ref, f_ref, fwd_ref, cs_ref, tr_ref, ti_ref, inv_ref, y_ref, ur_s, ui_s, fr_s, fi_s):
    cb = u_ref.shape[0]
    b = FFT_B
    m = cb * b
    dot = functools.partial(jnp.dot, preferred_element_type=F32)
    fwd = fwd_ref[...]
    fwd_half = fwd[:, :b // 2]
    tr, ti = tr_ref[...], ti_ref[...]

    def first_stage(c, carry):
        for src, lhs, re_s, im_s in ((u_ref, fwd_half, ur_s, ui_s), (f_ref, fwd, fr_s, fi_s)):
            p = dot(lhs, src[c].astype(BF16))
            pr, pi = p[:b], p[b:]
            re_s[c] = (pr * tr - pi * ti).astype(BF16)
            im_s[c] = (pr * ti + pi * tr).astype(BF16)
        return carry

    lax.fori_loop(0, cb, first_stage, 0, unroll=2)

    cs = cs_ref[...]

    def times_cs(re, im):
        big = dot(jnp.concatenate([re, im], axis=0), cs)
        return big[:m, :b], big[:m, b:], big[m:, :b], big[m:, b:]

    def second_stage(re_s, im_s):
        rc, rs, ic, is_ = times_cs(re_s[...].reshape(m, b), im_s[...].reshape(m, b))
        return rc + is_, ic - rs

    xr, xi = second_stage(ur_s, ui_s)
    hr, hi = second_stage(fr_s, fi_s)
    zr = (xr * hr - xi * hi).astype(BF16)
    zi = (xr * hi + xi * hr).astype(BF16)
    rc, rs, ic, is_ = times_cs(zr, zi)
    gr = (rc - is_).reshape(cb, b, b)
    gi = (rs + ic).reshape(cb, b, b)
    ur_s[...] = (gr * tr + gi * ti).astype(BF16)
    ui_s[...] = (gi * tr - gr * ti).astype(BF16)
    inv = inv_ref[...]

    def last_stage(c, carry):
        y_ref[c] = dot(inv, jnp.concatenate([ur_s[c], ui_s[c]], axis=0))
        return carry

    lax.fori_loop(0, cb, last_stage, 0, unroll=2)


def hyena_long_conv(u_t, filt_t):
    c, l = u_t.shape
    b = FFT_B
    consts = _dft_constants()
    full = lambda a: pl.BlockSpec(a.shape, lambda i: (0,) * a.ndim)
    y = pl.pallas_call(
        _hy_conv_kernel,
        grid=(c // HY_CB,),
        in_specs=[pl.BlockSpec((HY_CB, b // 2, b), lambda i: (i, 0, 0)),
                  pl.BlockSpec((HY_CB, b, b), lambda i: (i, 0, 0))] + [full(a) for a in consts],
        out_specs=pl.BlockSpec((HY_CB, b // 2, b), lambda i: (i, 0, 0)),
        out_shape=jax.ShapeDtypeStruct((c, b // 2, b), F32),
        scratch_shapes=[pltpu.VMEM((HY_CB, b, b), BF16) for _ in range(4)],
        compiler_params=_cparams("parallel"),
        name="hyena_long_conv",
    )(u_t.reshape(c, b // 2, b), filt_t.reshape(c, b, b), *consts)
    return y.reshape(c, l)


def _hy_ctx_conv_kernel(u_ref, f_ref, fwdu_ref, fwdf_ref, inv_ref, y_ref):
    dot = functools.partial(jnp.dot, preferred_element_type=F32)
    n = f_ref.shape[0]
    x = dot(fwdu_ref[...], u_ref[...].astype(BF16))
    h = dot(fwdf_ref[...], f_ref[...].astype(BF16))
    xr, xi, hr, hi = x[:n], x[n:], h[:n], h[n:]
    z = jnp.concatenate([xr * hr - xi * hi, xr * hi + xi * hr], axis=0).astype(BF16)
    y_ref[...] = dot(inv_ref[...], z)


def hyena_ctx_conv(u, filt):
    l, c = u.shape
    n = 2 * l
    idx = np.arange(n)
    ang = 2.0 * np.pi * np.outer(idx, idx) / n
    cm, sm = np.cos(ang), np.sin(ang)
    fwdf = np.concatenate([cm, -sm], axis=0)
    inv = np.concatenate([cm[:l], -sm[:l]], axis=1) / n
    consts = (jnp.asarray(fwdf[:, :l], BF16), jnp.asarray(fwdf, BF16), jnp.asarray(inv, BF16))
    tc = 256
    full = lambda a: pl.BlockSpec(a.shape, lambda i: (0,) * a.ndim)
    return pl.pallas_call(
        _hy_ctx_conv_kernel,
        grid=(c // tc,),
        in_specs=[pl.BlockSpec((l, tc), lambda i: (0, i)), pl.BlockSpec((n, tc), lambda i: (0, i))]
        + [full(a) for a in consts],
        out_specs=pl.BlockSpec((l, tc), lambda i: (0, i)),
        out_shape=jax.ShapeDtypeStruct((l, c), F32),
        compiler_params=_cparams("parallel"),
        name="hyena_ctx_conv",
    )(u, filt, *consts)


def _hy_post_kernel(y_ref, u_ref, x0_ref, b_ref, o_ref, *, transposed):
    w = y_ref[...] + u_ref[...] * b_ref[...]
    if transposed:
        w = w.T
    o_ref[...] = (w * x0_ref[...]).astype(o_ref.dtype)


def hyena_post(y, u, x0, bias, transposed):
    l, w = x0.shape
    tb = min(HY_TB, l)
    tm = pl.BlockSpec((tb, w), lambda i: (i, 0))
    yu = pl.BlockSpec((w, tb), lambda i: (0, i)) if transposed else tm
    return pl.pallas_call(
        functools.partial(_hy_post_kernel, transposed=transposed),
        grid=(l // tb,),
        in_specs=[yu, yu, tm, pl.BlockSpec(bias.shape, lambda i: (0, 0))],
        out_specs=tm,
        out_shape=jax.ShapeDtypeStruct((l, w), BF16),
        compiler_params=_cparams("parallel"),
        name="hyena_post",
    )(y, u, x0, bias)


def hyena(z, row0, l, short_w, short_b, filt, bias):
    filt_t, hb0 = hyena_filter(l, *filt)
    if l == FFT_B * FFT_B // 2:
        u_t, x0 = hyena_pre(z, row0, l, short_w, short_b, True)
        y_t = hyena_long_conv(u_t, filt_t)
        return hyena_post(y_t, u_t, x0, bias.reshape(-1, 1) + hb0, True)
    u, x0 = hyena_pre(z, row0, l, short_w, short_b, False)
    y = hyena_ctx_conv(u, filt_t.T)
    return hyena_post(y, u, x0, (bias.reshape(-1, 1) + hb0).T, False)


def kernel(x, c, ctx, c_ctx, w_mod, b_mod, norm1_w, norm2_w, ffn_w_gate, ffn_w_up, ffn_w_down, even_w_in, even_w_out, dn_conv_w, dn_a_log, dn_dt_bias, dn_onorm_w, hy_short_w, hy_short_b, hy_f_w1, hy_f_b1, hy_f_w2, hy_f_b2, hy_f_w3, hy_f_b3, hy_f_w4, hy_f_freq, hy_bias, na_w_qkv, na_rpb, na_w_out, final_norm_w):
    d = D_MODEL
    xa = jnp.concatenate([x[0], ctx[0]], axis=0)
    mod = modulation(jnp.concatenate([c, c_ctx[None]], axis=0), w_mod, b_mod)

    for layer in range(DEPTH):
        need_ctx = layer < DEPTH - 1
        rows = ALL_ROWS if need_ctx else SEQ
        sh1, sc1, g1, sh2, sc2, g2 = (mod[layer, :, None, j * d:(j + 1) * d] for j in range(N_MOD))
        i = layer // 2
        h = norm_mod(xa, norm1_w[layer][None], sc1, sh1)
        if layer % 2 == 0:
            w_in = even_w_in[i]
            w_in = jnp.concatenate([w_in[:, :4 * DN_WIDTH], w_in[:, DN_IN:], w_in[:, 4 * DN_WIDTH:DN_IN]],
                                   axis=1).astype(BF16)
            z = projection(h, w_in, ALL_ROWS, F32)
            dn_l, dn_c = gated_deltanet(z, need_ctx, dn_conv_w[i], dn_a_log[i], dn_dt_bias[i], dn_onorm_w[i])
            filt = (hy_f_w1[i], hy_f_b1[i], hy_f_w2[i], hy_f_b2[i], hy_f_w3[i], hy_f_b3[i], hy_f_w4[i],
                    hy_f_freq[i])
            y_lat = (dn_l, hyena(z, 0, SEQ, hy_short_w[i], hy_short_b[i], filt, hy_bias[i]))
            y_ctx = (dn_c, hyena(z, SEQ, CTX_LEN, hy_short_w[i], hy_short_b[i], filt, hy_bias[i])) if need_ctx else y_lat
            w_out = even_w_out
        else:
            z = projection(h, na_w_qkv, ALL_ROWS, BF16, layer=i, scaled_cols=D_MODEL, scale=NA_DIM ** -0.5)
            ya = na_attention(z, na_bias_tiles(na_rpb[i]))
            y_lat = (ya, ya)
            if need_ctx:
                yc = ctx_attention(z)
                y_ctx = (yc, yc)
            else:
                y_ctx = y_lat
            w_out = na_w_out
        xa, h = proj_residual(y_lat, y_ctx, w_out, i, xa, rows, g1, norm2_w[layer][None], sc2, sh2)
        xa = ffn_block(xa, h, rows, g2, ffn_w_gate, ffn_w_up, ffn_w_down, layer)
    return final_norm(xa, SEQ, final_norm_w[None])[None]
```

```python
import functools
import math

import jax
import jax.numpy as jnp
import numpy as np
from jax import lax
from jax.experimental import pallas as pl
from jax.experimental.pallas import tpu as pltpu

D_MODEL = 2048
SEQ = 8192
DEPTH = 4
GRID_W = 64
ROWS = SEQ // GRID_W
CTX_LEN = 256
ALL_ROWS = SEQ + CTX_LEN
NORM_EPS = 1e-6
N_MOD = 6

DN_HEADS = 8
DN_DIM = 128
DN_WIDTH = DN_HEADS * DN_DIM
DN_CONV = 5
DN_CHUNK = 64
DN_IN = 4 * DN_WIDTH + 4 * DN_HEADS

HY_WIDTH = D_MODEL - DN_WIDTH
HY_BANDS = 16
HY_DECAY_TARGET = 1e-2
HY_STRONG_DECAY_PCT = 0.3
HY_WEAK_DECAY_PCT = 1.5
HY_IN = 3 * HY_WIDTH
EVEN_IN = DN_IN + HY_IN

NA_HEADS = D_MODEL // 128
NA_DIM = 128
WIN_R = 8
WIN_C = 16
FFN_HIDDEN = -(-8 * D_MODEL // (3 * 256)) * 256

LANES = 128
VMEM_LIMIT = 56 * 1024 * 1024
NEG = -1e30

F32 = jnp.float32
BF16 = jnp.bfloat16


def _cparams(*sem):
    return pltpu.CompilerParams(dimension_semantics=sem, vmem_limit_bytes=VMEM_LIMIT)


def _sigmoid(x):
    return 1.0 / (1.0 + jnp.exp(-x))


MOD_TN = 1024


def _mod_kernel(c_ref, w_ref, b_ref, o_ref):
    rows = []
    for r in range(2):
        c = c_ref[r]
        a = c * _sigmoid(c)
        cols = [jnp.sum(w_ref[0, :, j * LANES:(j + 1) * LANES] * a, axis=0, keepdims=True)
                for j in range(MOD_TN // LANES)]
        rows.append(jnp.concatenate(cols, axis=1))
    o_ref[0] = jnp.concatenate(rows, axis=0) + b_ref[0]


def modulation(c_pair, w_mod, b_mod):
    n = N_MOD * D_MODEL
    cb = jnp.broadcast_to(c_pair[:, :, None], (2, D_MODEL, LANES))
    return pl.pallas_call(
        _mod_kernel,
        grid=(DEPTH, n // MOD_TN),
        in_specs=[pl.BlockSpec((2, D_MODEL, LANES), lambda l, j: (0, 0, 0)),
                  pl.BlockSpec((1, D_MODEL, MOD_TN), lambda l, j: (l, 0, j)),
                  pl.BlockSpec((1, 1, MOD_TN), lambda l, j: (l, 0, j))],
        out_specs=pl.BlockSpec((1, 2, MOD_TN), lambda l, j: (l, 0, j)),
        out_shape=jax.ShapeDtypeStruct((DEPTH, 2, n), F32),
        compiler_params=_cparams("parallel", "parallel"),
        name="modulation",
    )(cb, w_mod, b_mod.reshape(DEPTH, 1, n))


ROW_TILE = 256


def _kind_spec(d):
    return pl.BlockSpec((None, 1, d), lambda i: (jnp.where(i >= SEQ // ROW_TILE, 1, 0), 0, 0))


def _row_source_specs(src, width, col=0):
    n_lat = SEQ // ROW_TILE
    ctx_blk = src[2] // ROW_TILE
    return [pl.BlockSpec((ROW_TILE, width), lambda i: (jnp.minimum(i, n_lat - 1), col)),
            pl.BlockSpec((ROW_TILE, width), lambda i: (ctx_blk, col))]


def _pick_rows(lat_ref, ctx_ref):
    return jnp.where(pl.program_id(0) < SEQ // ROW_TILE, lat_ref[...], ctx_ref[...])


def _norm_mod_kernel(xl_ref, xc_ref, nw_ref, sc_ref, sh_ref, h_ref):
    x = _pick_rows(xl_ref, xc_ref)
    y = x * lax.rsqrt(jnp.mean(x * x, axis=-1, keepdims=True) + NORM_EPS)
    h_ref[...] = (y * nw_ref[...] * (1.0 + sc_ref[...]) + sh_ref[...]).astype(h_ref.dtype)


def norm_mod(x_src, nw, sc, sh):
    d = x_src[0].shape[1]
    return pl.pallas_call(
        _norm_mod_kernel,
        grid=(ALL_ROWS // ROW_TILE,),
        in_specs=_row_source_specs(x_src, d) + [pl.BlockSpec((1, d), lambda i: (0, 0)), _kind_spec(d), _kind_spec(d)],
        out_specs=pl.BlockSpec((ROW_TILE, d), lambda i: (i, 0)),
        out_shape=jax.ShapeDtypeStruct((ALL_ROWS, d), BF16),
        compiler_params=_cparams("parallel"),
        name="norm_mod",
    )(x_src[0], x_src[1], nw, sc, sh)


ROW_TILES = 8
PROJ_ROW_TILES = 16
EVEN_TN = 19 * LANES
EVEN_PAD = 3 * EVEN_TN
QKV_TN = 1024
assert EVEN_PAD >= EVEN_IN


def _weight_spec(w, tn, layer):
    d = w.shape[-2]
    if layer is None:
        return pl.BlockSpec((d, tn), lambda j, i: (0, j))
    return pl.BlockSpec((None, d, tn), lambda j, i: (layer, 0, j))


def _proj_kernel(h_ref, w_ref, o_ref, *scratch, scaled_cols, scale):
    if scratch:
        wb_ref, = scratch

        @pl.when(pl.program_id(1) == 0)
        def _():
            wb_ref[...] = w_ref[...].astype(BF16)
    else:
        wb_ref = w_ref
    y = jnp.dot(h_ref[...], wb_ref[...], preferred_element_type=F32)
    if scaled_cols:
        y = y * jnp.where(pl.program_id(0) * o_ref.shape[1] < scaled_cols, scale, 1.0)
    o_ref[...] = y.astype(o_ref.dtype)


def projection(h, w, tn, out_dtype, layer=None, scaled_cols=0, scale=1.0):
    m, d = h.shape
    n = w.shape[-1]
    tm = m // PROJ_ROW_TILES
    assert n % tn == 0 and scaled_cols % tn == 0
    scratch = [] if w.dtype == BF16 else [pltpu.VMEM((d, tn), BF16)]
    return pl.pallas_call(
        functools.partial(_proj_kernel, scaled_cols=scaled_cols, scale=scale),
        grid=(n // tn, PROJ_ROW_TILES),
        in_specs=[pl.BlockSpec((tm, d), lambda j, i: (i, 0)), _weight_spec(w, tn, layer)],
        out_specs=pl.BlockSpec((tm, tn), lambda j, i: (i, j)),
        out_shape=jax.ShapeDtypeStruct((m, n), out_dtype),
        scratch_shapes=scratch,
        compiler_params=_cparams("parallel", "arbitrary"),
        name="projection",
    )(h, w)


FFN_TF = 512
FFN_TN = 256


def _ffn_up_kernel(h_ref, wg_ref, wu_ref, a_ref, wgb_ref, wub_ref):
    @pl.when(pl.program_id(1) == 0)
    def _():
        wgb_ref[...] = wg_ref[...].astype(BF16)
        wub_ref[...] = wu_ref[...].astype(BF16)

    h = h_ref[...]
    gate = jnp.dot(h, wgb_ref[...], preferred_element_type=F32)
    up = jnp.dot(h, wub_ref[...], preferred_element_type=F32)
    a_ref[...] = (gate * _sigmoid(gate) * up).astype(a_ref.dtype)


def _ffn_down_kernel(a_ref, wd_ref, x_ref, g_ref, o_ref):
    tm = x_ref.shape[0]
    y = jnp.dot(a_ref[...], wd_ref[...].astype(BF16), preferred_element_type=F32)
    row = pl.program_id(0) * tm + lax.broadcasted_iota(jnp.int32, (tm, 1), 0)
    o_ref[...] = x_ref[...] + jnp.where(row < SEQ, g_ref[0], g_ref[1]) * y


def ffn_block(x, h, rows, g, wg, wu, wd, layer):
    m, d = x.shape
    hid = wg.shape[2]
    tm = rows // ROW_TILES
    a = pl.pallas_call(
        _ffn_up_kernel,
        grid=(hid // FFN_TF, ROW_TILES),
        in_specs=[pl.BlockSpec((tm, d), lambda f, i: (i, 0)),
                  _weight_spec(wg, FFN_TF, layer), _weight_spec(wu, FFN_TF, layer)],
        out_specs=pl.BlockSpec((tm, FFN_TF), lambda f, i: (i, f)),
        out_shape=jax.ShapeDtypeStruct((m, hid), BF16),
        scratch_shapes=[pltpu.VMEM((d, FFN_TF), BF16), pltpu.VMEM((d, FFN_TF), BF16)],
        compiler_params=_cparams("parallel", "arbitrary"),
        name="ffn_up",
    )(h, wg, wu)
    return pl.pallas_call(
        _ffn_down_kernel,
        grid=(ROW_TILES, d // FFN_TN),
        in_specs=[pl.BlockSpec((tm, hid), lambda i, j: (i, 0)),
                  pl.BlockSpec((None, hid, FFN_TN), lambda i, j: (layer, 0, j)),
                  pl.BlockSpec((tm, FFN_TN), lambda i, j: (i, j)),
                  pl.BlockSpec((2, 1, FFN_TN), lambda i, j: (0, 0, j))],
        out_specs=pl.BlockSpec((tm, FFN_TN), lambda i, j: (i, j)),
        out_shape=jax.ShapeDtypeStruct((m, d), F32),
        compiler_params=_cparams("parallel", "arbitrary"),
        name="ffn_down",
    )(a, wd, x, g)


def _proj_res_kernel(yal_ref, yac_ref, ybl_ref, ybc_ref, xl_ref, xc_ref, w_ref, g_ref, nw_ref, sc_ref, sh_ref,
                     o_ref, h_ref, wb_ref):
    @pl.when(pl.program_id(0) == 0)
    def _():
        wb_ref[...] = w_ref[...].astype(BF16)

    ya = _pick_rows(yal_ref, yac_ref)
    yb = _pick_rows(ybl_ref, ybc_ref)
    ka = ya.shape[1]
    y = (jnp.dot(ya, wb_ref[:ka, :], preferred_element_type=F32)
         + jnp.dot(yb, wb_ref[ka:, :], preferred_element_type=F32))
    x = _pick_rows(xl_ref, xc_ref) + g_ref[...] * y
    o_ref[...] = x
    xn = x * lax.rsqrt(jnp.mean(x * x, axis=-1, keepdims=True) + NORM_EPS)
    h_ref[...] = (xn * nw_ref[...] * (1.0 + sc_ref[...]) + sh_ref[...]).astype(h_ref.dtype)


def proj_residual(y_lat, y_ctx, w, layer, x_src, rows, g, nw, sc, sh):
    d = x_src[0].shape[1]
    half = d // 2
    second = [1 if pair[1] is pair[0] else 0 for pair in (y_lat, y_ctx)]
    assert second[0] == second[1]
    row = pl.BlockSpec((ROW_TILE, d), lambda i: (i, 0))
    return pl.pallas_call(
        _proj_res_kernel,
        grid=(rows // ROW_TILE,),
        in_specs=_row_source_specs((y_lat[0], y_ctx[0], 0), half) + _row_source_specs((y_lat[1], y_ctx[1], 0), half,
                                                                                      second[0])
        + _row_source_specs(x_src, d) + [
            pl.BlockSpec((None, d, d), lambda i: (layer, 0, 0), pipeline_mode=pl.Buffered(1)),
            _kind_spec(d), pl.BlockSpec((1, d), lambda i: (0, 0)), _kind_spec(d), _kind_spec(d)],
        out_specs=[row, row],
        out_shape=[jax.ShapeDtypeStruct((ALL_ROWS, d), F32), jax.ShapeDtypeStruct((ALL_ROWS, d), BF16)],
        scratch_shapes=[pltpu.VMEM((d, d), BF16)],
        compiler_params=_cparams("arbitrary"),
        name="proj_residual",
    )(y_lat[0], y_ctx[0], y_lat[1], y_ctx[1], x_src[0], x_src[1], w, g, nw, sc, sh)


def _final_norm_kernel(x_ref, w_ref, o_ref):
    x = x_ref[...]
    o_ref[...] = x * lax.rsqrt(jnp.mean(x * x, axis=-1, keepdims=True) + NORM_EPS) * w_ref[...]


def final_norm(x, rows, w):
    d = x.shape[1]
    tm = 512
    return pl.pallas_call(
        _final_norm_kernel,
        grid=(rows // tm,),
        in_specs=[pl.BlockSpec((tm, d), lambda i: (i, 0)), pl.BlockSpec((1, d), lambda i: (0, 0))],
        out_specs=pl.BlockSpec((tm, d), lambda i: (i, 0)),
        out_shape=jax.ShapeDtypeStruct((rows, d), F32),
        compiler_params=_cparams("parallel"),
        name="final_norm",
    )(x, w)


NA_R = 4
NA_KD = (NA_R + WIN_R) // 2
NA_T = 2 * WIN_R
NA_HP = 4


def _na_bias_kernel(rpb_ref, o_ref):
    h = pl.program_id(0)
    c = lax.broadcasted_iota(jnp.int32, (GRID_W, LANES), 0)
    lane = lax.broadcasted_iota(jnp.int32, (GRID_W, LANES), 1)
    left = lane < GRID_W
    kc = jnp.where(left, lane, lane - GRID_W)
    cs = jnp.clip(c - WIN_C // 2, 0, GRID_W - WIN_C)
    inwin = (kc >= cs) & (kc < cs + WIN_C)
    diff = kc - c + (WIN_C - 1)
    n_ro, n_co = 2 * WIN_R - 1, 2 * WIN_C - 1
    for t in range(NA_T):
        def body(d, acc, t=t):
            vl = rpb_ref[(h * n_ro + (t - 1)) * n_co + d] if t >= 1 else jnp.float32(NEG)
            vr = rpb_ref[(h * n_ro + t) * n_co + d] if t < n_ro else jnp.float32(NEG)
            return jnp.where(diff == d, jnp.where(left, vl, vr), acc)

        acc = lax.fori_loop(0, n_co, body, jnp.full((GRID_W, LANES), NEG, F32))
        o_ref[0, t] = jnp.where(inwin, acc, NEG)


def na_bias_tiles(rpb):
    return pl.pallas_call(
        _na_bias_kernel,
        grid=(NA_HEADS,),
        in_specs=[pl.BlockSpec(memory_space=pltpu.SMEM)],
        out_specs=pl.BlockSpec((1, NA_T, GRID_W, LANES), lambda h: (h, 0, 0, 0)),
        out_shape=jax.ShapeDtypeStruct((NA_HEADS, NA_T, GRID_W, LANES), F32),
        compiler_params=_cparams("parallel"),
        name="na_bias_tiles",
    )(rpb.reshape(-1))


def _na_kernel(q_ref, k_ref, v_ref, kc_ref, vc_ref, tt_ref, o_ref):
    r0 = pl.program_id(1) * NA_R
    ks = jnp.clip(r0 - WIN_R // 2, 0, ROWS - 2 * NA_KD)
    span = pl.ds(pl.multiple_of(ks * GRID_W, LANES), NA_KD * LANES)
    heads = range(NA_HP)
    cols = [slice(h * NA_DIM, (h + 1) * NA_DIM) for h in heads]
    contract_last = (((1,), (1,)), ((), ()))
    q = [q_ref[:, cols[h]] for h in heads]
    s_loc = [lax.dot_general(q[h], k_ref[span, cols[h]], contract_last, preferred_element_type=F32) for h in heads]
    s_ctx = [lax.dot_general(q[h], kc_ref[:, cols[h]], contract_last, preferred_element_type=F32) for h in heads]

    left = lax.broadcasted_iota(jnp.int32, (GRID_W, LANES), 1) < GRID_W
    tile_sel = []
    for qi in range(NA_R):
        r = r0 + qi
        rs = jnp.clip(r - WIN_R // 2, 0, ROWS - WIN_R)
        for dj in range(NA_KD):
            kl = ks + 2 * dj
            vl = ((kl >= rs) & (kl < rs + WIN_R)).astype(jnp.int32)
            vr = ((kl + 1 >= rs) & (kl + 1 < rs + WIN_R)).astype(jnp.int32)
            tile_sel.append((jnp.clip(kl - r + WIN_R, 0, NA_T - 1), jnp.where(left, vl, vr) > 0))
    for h in heads:
        rows = [jnp.concatenate([jnp.where(valid, tt_ref[h, t], NEG)
                                 for t, valid in tile_sel[qi * NA_KD:(qi + 1) * NA_KD]], axis=1)
                for qi in range(NA_R)]
        s_loc[h] = s_loc[h] + jnp.concatenate(rows, axis=0)

    m = [jnp.maximum(jnp.max(s_loc[h], axis=-1, keepdims=True), jnp.max(s_ctx[h], axis=-1, keepdims=True))
         for h in heads]
    p_loc = [jnp.exp(s_loc[h] - m[h]) for h in heads]
    p_ctx = [jnp.exp(s_ctx[h] - m[h]) for h in heads]
    denom = [jnp.sum(p_loc[h], axis=-1, keepdims=True) + jnp.sum(p_ctx[h], axis=-1, keepdims=True) for h in heads]
    o = [jnp.dot(p_loc[h].astype(BF16), v_ref[span, cols[h]], preferred_element_type=F32)
         + jnp.dot(p_ctx[h].astype(BF16), vc_ref[:, cols[h]], preferred_element_type=F32) for h in heads]
    for h in heads:
        o_ref[:, cols[h]] = (o[h] / denom[h]).astype(o_ref.dtype)


def na_attention(z, tiles):
    tq = NA_R * GRID_W
    w = NA_HP * NA_DIM
    ng = NA_HEADS // NA_HP
    return pl.pallas_call(
        _na_kernel,
        grid=(ng, ROWS // NA_R),
        in_specs=[pl.BlockSpec((tq, w), lambda g, i: (i, g)),
                  pl.BlockSpec((SEQ, w), lambda g, i: (0, ng + g)),
                  pl.BlockSpec((SEQ, w), lambda g, i: (0, 2 * ng + g)),
                  pl.BlockSpec((CTX_LEN, w), lambda g, i: (SEQ // CTX_LEN, ng + g)),
                  pl.BlockSpec((CTX_LEN, w), lambda g, i: (SEQ // CTX_LEN, 2 * ng + g)),
                  pl.BlockSpec((NA_HP, NA_T, GRID_W, LANES), lambda g, i: (g, 0, 0, 0))],
        out_specs=pl.BlockSpec((tq, w), lambda g, i: (i, g)),
        out_shape=jax.ShapeDtypeStruct((SEQ, D_MODEL), BF16),
        compiler_params=_cparams("parallel", "arbitrary"),
        name="na_attention",
    )(z, z, z, z, z, tiles)


def _ctx_attn_kernel(q_ref, k_ref, v_ref, o_ref):
    s = lax.dot_general(q_ref[...], k_ref[...], (((1,), (1,)), ((), ())), preferred_element_type=F32)
    p = jnp.exp(s - jnp.max(s, axis=-1, keepdims=True))
    o = jnp.dot(p.astype(BF16), v_ref[...], preferred_element_type=F32)
    o_ref[...] = (o / jnp.sum(p, axis=-1, keepdims=True)).astype(o_ref.dtype)


def ctx_attention(z):
    nh = NA_HEADS
    blk = lambda off: pl.BlockSpec((CTX_LEN, NA_DIM), lambda h: (SEQ // CTX_LEN, off + h))
    return pl.pallas_call(
        _ctx_attn_kernel,
        grid=(nh,),
        in_specs=[blk(0), blk(nh), blk(2 * nh)],
        out_specs=pl.BlockSpec((CTX_LEN, NA_DIM), lambda h: (0, h)),
        out_shape=jax.ShapeDtypeStruct((CTX_LEN, D_MODEL), BF16),
        compiler_params=_cparams("parallel"),
        name="ctx_attention",
    )(z, z, z)


DN_SC = 256
HALO = 8
N_GB = 2 * DN_HEADS


def _softplus(x):
    return jnp.maximum(x, 0.0) + jnp.log1p(jnp.exp(-jnp.abs(x)))


def _short_conv(main_ref, prev_ref, next_ref, w_ref, col0, blk, nblk):
    ksize = w_ref.shape[0]
    tb, width = main_ref.shape
    prev = jnp.where(blk > 0, prev_ref[...], 0.0)
    nxt = jnp.where(blk < nblk - 1, next_ref[...], 0.0)
    xf = jnp.concatenate([prev, main_ref[...], nxt], axis=0)
    n = xf.shape[0]
    acc = None
    for j in range(ksize):
        sh = (ksize // 2 - j) % n
        xs = xf if sh == 0 else pltpu.roll(xf, sh, 0)
        term = xs[HALO:HALO + tb] * w_ref[j:j + 1, col0:col0 + width]
        acc = term if acc is None else acc + term
    return acc


def _chunk_masks(n):
    r = lax.broadcasted_iota(jnp.int32, (n, n), 0)
    c = lax.broadcasted_iota(jnp.int32, (n, n), 1)
    return r, c, (r // DN_CHUNK) == (c // DN_CHUNK)


def _dn_scalars_kernel(za_ref, zb_ref, zat_ref, alog_ref, dtb_ref, alogt_ref, dtbt_ref,
                       gc_ref, gt_ref, beta_ref, gct_ref):
    hi = lax.Precision.HIGHEST
    g = -jnp.exp(alog_ref[...]) * _softplus(za_ref[...] + dtb_ref[...])
    beta_ref[...] = _sigmoid(zb_ref[...])
    r, c, same = _chunk_masks(DN_SC)
    lo = (same & (c <= r)).astype(F32)
    up = (same & (c >= r)).astype(F32)
    fwd_col = lax.broadcasted_iota(jnp.int32, (DN_SC, N_GB), 1) < DN_HEADS
    gc_ref[...] = jnp.where(fwd_col, jnp.dot(lo, g, precision=hi), jnp.dot(up, g, precision=hi))
    gt_ref[...] = jnp.dot(same.astype(F32), g, precision=hi)
    g_t = -jnp.exp(alogt_ref[...]) * _softplus(zat_ref[...] + dtbt_ref[...])
    fwd_row = lax.broadcasted_iota(jnp.int32, (N_GB, DN_SC), 0) < DN_HEADS
    gct_ref[...] = jnp.where(fwd_row, jnp.dot(g_t, up, precision=hi), jnp.dot(g_t, lo, precision=hi))


def dn_scalars(za, zb, a_log, dt_bias):
    l = za.shape[0]
    col = pl.BlockSpec((DN_SC, N_GB), lambda i: (i, 0))
    row = pl.BlockSpec((N_GB, DN_SC), lambda i: (0, i))
    prow = pl.BlockSpec((1, N_GB), lambda i: (0, 0))
    pcol = pl.BlockSpec((N_GB, 1), lambda i: (0, 0))
    cshape = jax.ShapeDtypeStruct((l, N_GB), F32)
    return pl.pallas_call(
        _dn_scalars_kernel,
        grid=(l // DN_SC,),
        in_specs=[col, col, row, prow, prow, pcol, pcol],
        out_specs=[col, col, col, row],
        out_shape=[cshape, cshape, cshape, jax.ShapeDtypeStruct((N_GB, l), F32)],
        compiler_params=_cparams("parallel"),
        name="dn_scalars",
    )(za, zb, za.T, a_log.reshape(1, N_GB), dt_bias.reshape(1, N_GB), a_log.reshape(N_GB, 1),
      dt_bias.reshape(N_GB, 1))


def _gdn_kernel(zq_ref, zqp_ref, zqn_ref, zk_ref, zkp_ref, zkn_ref, zv_ref, zvp_ref, zvn_ref, cw_ref,
                gc_ref, gt_ref, beta_ref, gct_ref, s0_ref, o_ref, sfin_ref, state_ref, *, bwd, nblk):
    i = pl.program_id(0)
    blk = nblk - 1 - i if bwd else i

    @pl.when(i == 0)
    def _():
        state_ref[...] = s0_ref[...]

    def conv_silu(main_ref, prev_ref, next_ref, part):
        acc = _short_conv(main_ref, prev_ref, next_ref, cw_ref, part * DN_WIDTH, blk, nblk)
        return acc * _sigmoid(acc)

    q_all = conv_silu(zq_ref, zqp_ref, zqn_ref, 0)
    k_all = conv_silu(zk_ref, zkp_ref, zkn_ref, 1)
    v_all = conv_silu(zv_ref, zvp_ref, zvn_ref, 2)

    r, c, same64 = _chunk_masks(DN_SC)
    tri = same64 & ((c >= r) if bwd else (c <= r))
    offdiag = r != c
    same16 = (r // 16) == (c // 16)
    same32 = (r // 32) == (c // 32)
    contract_last = (((1,), (1,)), ((), ()))
    contract_first = (((0,), (0,)), ((), ()))
    nc = DN_SC // DN_CHUNK
    dot = functools.partial(jnp.dot, preferred_element_type=F32)

    heads = range(DN_HEADS)
    col0 = DN_HEADS if bwd else 0
    gc_c = [gc_ref[:, col0 + h:col0 + h + 1] for h in heads]
    gt_c = [gt_ref[:, col0 + h:col0 + h + 1] for h in heads]
    beta_c = [beta_ref[:, col0 + h:col0 + h + 1] for h in heads]
    q, k, v, k16, dec, qk16, m, p, e = ([None] * DN_HEADS for _ in range(9))
    for h in heads:
        sl = slice(h * DN_DIM, (h + 1) * DN_DIM)
        qh, kh = q_all[:, sl], k_all[:, sl]
        q[h] = qh * (lax.rsqrt(jnp.sum(qh * qh, axis=-1, keepdims=True) + NORM_EPS) * DN_DIM ** -0.5)
        k[h] = kh * lax.rsqrt(jnp.sum(kh * kh, axis=-1, keepdims=True) + NORM_EPS)
        v[h] = v_all[:, sl]
        k16[h] = k[h].astype(BF16)
        dec[h] = jnp.exp(jnp.where(tri, gc_c[h] - gct_ref[col0 + h:col0 + h + 1, :], NEG))
    for h in heads:
        kk = lax.dot_general(k16[h], k16[h], contract_last, preferred_element_type=F32)
        m[h] = jnp.where(offdiag, kk * beta_c[h] * dec[h], 0.0)
        qk16[h] = (lax.dot_general(q[h].astype(BF16), k16[h], contract_last, preferred_element_type=F32)
                   * dec[h]).astype(BF16)

    for h in heads:
        p[h] = jnp.where(same16, -m[h], 0.0)
        e[h] = p[h]
    for _ in range(3):
        for h in heads:
            p16 = p[h].astype(BF16)
            p[h] = dot(p16, p16)
        for h in heads:
            e[h] = e[h] + p[h] + dot(e[h].astype(BF16), p[h].astype(BF16))
    for inner, outer in ((same16, same32), (same32, None)):
        y = [None] * DN_HEADS
        for h in heads:
            cm = jnp.where(~inner if outer is None else (outer & ~inner), m[h], 0.0)
            y[h] = cm + dot(cm.astype(BF16), e[h].astype(BF16))
        for h in heads:
            e[h] = e[h] - y[h] - dot(e[h].astype(BF16), y[h].astype(BF16))

    u, w16, qg16, kd16, s = ([None] * DN_HEADS for _ in range(5))
    for h in heads:
        eg = jnp.exp(gc_c[h])
        rhs = jnp.concatenate([v[h] * beta_c[h], k[h] * (beta_c[h] * eg)], axis=1)
        uw = rhs + dot(e[h].astype(BF16), rhs.astype(BF16))
        u[h] = uw[:, :DN_DIM]
        w16[h] = uw[:, DN_DIM:].astype(BF16)
        qg16[h] = (q[h] * eg).astype(BF16)
        kd16[h] = (k[h] * jnp.exp(gt_c[h] - gc_c[h])).astype(BF16)
        s[h] = state_ref[h]

    outs = [[None] * nc for _ in heads]
    for ci in (range(nc - 1, -1, -1) if bwd else range(nc)):
        rows = slice(ci * DN_CHUNK, (ci + 1) * DN_CHUNK)
        ws = [dot(jnp.concatenate([w16[h][rows], qg16[h][rows]], axis=0), s[h].astype(BF16)) for h in heads]
        vn16 = [(u[h][rows] - ws[h][:DN_CHUNK]).astype(BF16) for h in heads]
        for h in heads:
            outs[h][ci] = ws[h][DN_CHUNK:] + dot(qk16[h][rows, ci * DN_CHUNK:(ci + 1) * DN_CHUNK], vn16[h])
        for h in heads:
            gl = jnp.exp(gt_c[h][ci * DN_CHUNK:ci * DN_CHUNK + 1, :])
            s[h] = s[h] * gl + lax.dot_general(kd16[h][rows], vn16[h], contract_first, preferred_element_type=F32)
    for h in heads:
        state_ref[h] = s[h]
        o_ref[:, h * DN_DIM:(h + 1) * DN_DIM] = jnp.concatenate(outs[h], axis=0)

    @pl.when(i == nblk - 1)
    def _():
        sfin_ref[...] = state_ref[...]


def gdn_scan(z, conv_w, gc, gt, beta, gct, s0, bwd, row0, l):
    nblk = l // DN_SC
    per = DN_SC // HALO
    blk0 = row0 // DN_SC
    last_halo = z.shape[0] // HALO - 1

    def rel(i):
        return nblk - 1 - i if bwd else i

    def b(i):
        return blk0 + rel(i)

    def part_specs(part):
        return [pl.BlockSpec((DN_SC, DN_WIDTH), lambda i: (b(i), part)),
                pl.BlockSpec((HALO, DN_WIDTH), lambda i: (jnp.maximum(b(i) * per - 1, 0), part)),
                pl.BlockSpec((HALO, DN_WIDTH), lambda i: (jnp.minimum((b(i) + 1) * per, last_halo), part))]

    col = pl.BlockSpec((DN_SC, N_GB), lambda i: (b(i), 0))
    st = pl.BlockSpec((DN_HEADS, DN_DIM, DN_DIM), lambda i: (0, 0, 0))
    return pl.pallas_call(
        functools.partial(_gdn_kernel, bwd=bwd, nblk=nblk),
        grid=(nblk,),
        in_specs=part_specs(0) + part_specs(1) + part_specs(2) + [
            pl.BlockSpec((DN_CONV, 3 * DN_WIDTH), lambda i: (0, 0)), col, col, col,
            pl.BlockSpec((N_GB, DN_SC), lambda i: (0, b(i))), st],
        out_specs=[pl.BlockSpec((DN_SC, DN_WIDTH), lambda i: (rel(i), 0)), st],
        out_shape=[jax.ShapeDtypeStruct((l, DN_WIDTH), F32),
                   jax.ShapeDtypeStruct((DN_HEADS, DN_DIM, DN_DIM), F32)],
        scratch_shapes=[pltpu.VMEM((DN_HEADS, DN_DIM, DN_DIM), F32)],
        compiler_params=_cparams("arbitrary"),
        name="gdn_scan_bwd" if bwd else "gdn_scan_fwd",
    )(z, z, z, z, z, z, z, z, z, conv_w, gc, gt, beta, gct, s0)


def _dn_out_kernel(of_ref, ob_ref, gate_ref, nw_ref, y_ref):
    for h in range(DN_HEADS):
        sl = slice(h * DN_DIM, (h + 1) * DN_DIM)
        o = of_ref[:, sl] + ob_ref[:, sl]
        o = o * lax.rsqrt(jnp.mean(o * o, axis=-1, keepdims=True) + NORM_EPS) * nw_ref[...]
        gate = gate_ref[:, sl]
        y_ref[:, sl] = (o * (gate * _sigmoid(gate))).astype(y_ref.dtype)


def dn_output(o_f, o_b, z, row0, onorm_w):
    l = o_f.shape[0]
    tm = min(l, 512)
    blk0 = row0 // tm
    blk = pl.BlockSpec((tm, DN_WIDTH), lambda i: (i, 0))
    return pl.pallas_call(
        _dn_out_kernel,
        grid=(l // tm,),
        in_specs=[blk, blk, pl.BlockSpec((tm, DN_WIDTH), lambda i: (blk0 + i, 3)),
                  pl.BlockSpec((1, DN_DIM), lambda i: (0, 0))],
        out_specs=blk,
        out_shape=jax.ShapeDtypeStruct((l, DN_WIDTH), BF16),
        compiler_params=_cparams("parallel"),
        name="dn_output",
    )(o_f, o_b, z, onorm_w.reshape(1, DN_DIM))


def gated_deltanet(z, need_ctx, conv_w, a_log, dt_bias, onorm_w):
    a0, b0 = EVEN_IN - 2 * N_GB, EVEN_IN - N_GB
    scal = dn_scalars(z[:, a0:b0], z[:, b0:b0 + N_GB], a_log, dt_bias)
    s0 = jnp.zeros((DN_HEADS, DN_DIM, DN_DIM), F32)
    outs_c, outs_l = [], []
    for bwd in (False, True):
        oc, s_ctx = gdn_scan(z, conv_w, *scal, s0, bwd, SEQ, CTX_LEN)
        ol, _ = gdn_scan(z, conv_w, *scal, s_ctx, bwd, 0, SEQ)
        outs_c.append(oc)
        outs_l.append(ol)
    y_ctx = dn_output(*outs_c, z, SEQ, onorm_w) if need_ctx else None
    return dn_output(*outs_l, z, 0, onorm_w), y_ctx


HY_SHORT = 3
HY_EMB = 1 + 2 * HY_BANDS
HY_EMB_PAD = 40
HY_TB = 256
FFT_B = 128
HY_CB = 16
HY_COL0 = 4


def _hy_pre_kernel(x0_ref, x0p_ref, x0n_ref, x1_ref, x1p_ref, x1n_ref, v_ref, vp_ref, vn_ref, w_ref, b_ref,
                   u_ref, x0c_ref, *, nblk, transpose_u):
    blk = pl.program_id(0)
    w = HY_WIDTH
    x0 = _short_conv(x0_ref, x0p_ref, x0n_ref, w_ref, 0, blk, nblk) + b_ref[:, :w]
    x1 = _short_conv(x1_ref, x1p_ref, x1n_ref, w_ref, w, blk, nblk) + b_ref[:, w:2 * w]
    v = _short_conv(v_ref, vp_ref, vn_ref, w_ref, 2 * w, blk, nblk) + b_ref[:, 2 * w:]
    u = v * x1
    x0c_ref[...] = x0
    u_ref[...] = u.T if transpose_u else u


def hyena_pre(z, row0, l, short_w, short_b, transpose_u):
    tb = min(HY_TB, l)
    nblk = l // tb
    per = tb // HALO
    blk0 = row0 // tb
    last_halo = z.shape[0] // HALO - 1
    w = HY_WIDTH

    def part_specs(part):
        cb = HY_COL0 + part
        return [pl.BlockSpec((tb, w), lambda i: (blk0 + i, cb)),
                pl.BlockSpec((HALO, w), lambda i: (jnp.maximum((blk0 + i) * per - 1, 0), cb)),
                pl.BlockSpec((HALO, w), lambda i: (jnp.minimum((blk0 + i + 1) * per, last_halo), cb))]

    u_shape, u_spec = ((w, l), pl.BlockSpec((w, tb), lambda i: (0, i))) if transpose_u else (
        (l, w), pl.BlockSpec((tb, w), lambda i: (i, 0)))
    return pl.pallas_call(
        functools.partial(_hy_pre_kernel, nblk=nblk, transpose_u=transpose_u),
        grid=(nblk,),
        in_specs=part_specs(0) + part_specs(1) + part_specs(2) + [
            pl.BlockSpec((HY_SHORT, HY_IN), lambda i: (0, 0)), pl.BlockSpec((1, HY_IN), lambda i: (0, 0))],
        out_specs=[u_spec, pl.BlockSpec((tb, w), lambda i: (i, 0))],
        out_shape=[jax.ShapeDtypeStruct(u_shape, F32), jax.ShapeDtypeStruct((l, w), F32)],
        compiler_params=_cparams("parallel"),
        name="hyena_pre",
    )(z, z, z, z, z, z, z, z, z, short_w, short_b.reshape(1, HY_IN))


def _hy_filter_kernel(w1t_ref, b1_ref, w2t_ref, b2_ref, w3t_ref, b3_ref, w4t_ref, fr_ref, band_ref, dl_ref,
                      f_ref, hb0_ref, *, l, fb):
    hi = lax.Precision.HIGHEST
    j = pl.program_id(0)
    second = j >= l // fb
    n = j * fb + lax.broadcasted_iota(jnp.int32, (1, fb), 1)
    pos = jnp.where(second, 2 * l - n, n).astype(F32)
    t = pos / max(l - 1, 1)
    wpos = 2.0 * math.pi * pos / l
    arg = band_ref[...] * wpos
    row = lax.broadcasted_iota(jnp.int32, (HY_EMB_PAD, fb), 0)
    feat = jnp.where(row == 0, t, jnp.where(row <= HY_BANDS, jnp.cos(arg),
                                            jnp.where(row <= 2 * HY_BANDS, -jnp.sin(arg), 0.0)))
    fr = fr_ref[...]
    h = jnp.sin(fr * (jnp.dot(w1t_ref[...], feat, precision=hi) + b1_ref[...]))
    h = jnp.sin(fr * (jnp.dot(w2t_ref[...], h, precision=hi) + b2_ref[...]))
    h = jnp.sin(fr * (jnp.dot(w3t_ref[...], h, precision=hi) + b3_ref[...]))
    window = jnp.exp(-t * dl_ref[...])
    half = pl.multiple_of(jnp.where(second, HY_WIDTH, 0), HY_WIDTH)
    h16 = h.astype(BF16)
    f = jnp.dot(w4t_ref[pl.ds(half, HY_WIDTH), :].astype(BF16), h16, preferred_element_type=F32) * window
    f_ref[...] = jnp.where(n == l, 0.0, f)

    @pl.when(j == 0)
    def _():
        hb0_ref[...] = (jnp.dot(w4t_ref[HY_WIDTH:, :].astype(BF16), h16[:, :LANES], preferred_element_type=F32)
                        * window[:, :LANES])


def hyena_filter(l, w1, b1, w2, b2, w3, b3, w4, freq):
    fb = min(1024, l)
    colv = lambda v: v.reshape(-1, 1)
    bands = np.zeros((HY_EMB_PAD, 1), np.float32)
    bands[1:1 + HY_BANDS, 0] = bands[1 + HY_BANDS:HY_EMB, 0] = np.linspace(1e-4, HY_BANDS - 1, HY_BANDS,
                                                                            dtype=np.float32)
    min_decay = math.log(HY_DECAY_TARGET) / HY_WEAK_DECAY_PCT
    max_decay = math.log(HY_DECAY_TARGET) / HY_STRONG_DECAY_PCT
    deltas = np.abs(np.linspace(min_decay, max_decay, HY_WIDTH, dtype=np.float32)).reshape(-1, 1)
    w1t = jnp.pad(w1.T, ((0, 0), (0, HY_EMB_PAD - HY_EMB)))
    full = lambda a: pl.BlockSpec(a.shape, lambda j: (0,) * a.ndim)
    args = (w1t, colv(b1), w2.T, colv(b2), w3.T, colv(b3), w4.T, colv(freq), jnp.asarray(bands), jnp.asarray(deltas))
    filt, hb0 = pl.pallas_call(
        functools.partial(_hy_filter_kernel, l=l, fb=fb),
        grid=(2 * l // fb,),
        in_specs=[full(a) for a in args],
        out_specs=[pl.BlockSpec((HY_WIDTH, fb), lambda j: (0, j)), pl.BlockSpec((HY_WIDTH, LANES), lambda j: (0, 0))],
        out_shape=[jax.ShapeDtypeStruct((HY_WIDTH, 2 * l), F32), jax.ShapeDtypeStruct((HY_WIDTH, LANES), F32)],
        compiler_params=_cparams("arbitrary"),
        name="hyena_filter",
    )(*args)
    return filt, hb0[:, :1]


def _dft_constants():
    b = FFT_B
    n = b * b
    idx = np.arange(b)
    ang = 2.0 * np.pi * np.outer(idx, idx) / b
    c, s = np.cos(ang), np.sin(ang)
    tw = 2.0 * np.pi * np.outer(idx, idx) / n
    fwd_b = np.concatenate([c, -s], axis=0)
    cs = np.concatenate([c, s], axis=1)
    inv_b = np.concatenate([c[:b // 2], -s[:b // 2]], axis=1) / n
    return (jnp.asarray(fwd_b, BF16), jnp.asarray(cs, BF16), jnp.asarray(np.cos(tw), F32),
            jnp.asarray(-np.sin(tw), F32), jnp.asarray(inv_b, BF16))


def _hy_conv_kernel(u_ref, f_ref, fwd_ref, cs_ref, tr_ref, ti_ref, inv_ref, y_ref, ur_s, ui_s, fr_s, fi_s):
    cb = u_ref.shape[0]
    b = FFT_B
    m = cb * b
    dot = functools.partial(jnp.dot, preferred_element_type=F32)
    fwd = fwd_ref[...]
    fwd_half = fwd[:, :b // 2]
    tr, ti = tr_ref[...], ti_ref[...]

    def first_stage(c, carry):
        for src, lhs, re_s, im_s in ((u_ref, fwd_half, ur_s, ui_s), (f_ref, fwd, fr_s, fi_s)):
            p = dot(lhs, src[c].astype(BF16))
            pr, pi = p[:b], p[b:]
            re_s[c] = (pr * tr - pi * ti).astype(BF16)
            im_s[c] = (pr * ti + pi * tr).astype(BF16)
        return carry

    lax.fori_loop(0, cb, first_stage, 0, unroll=2)

    cs = cs_ref[...]

    def times_cs(re, im):
        big = dot(jnp.concatenate([re, im], axis=0), cs)
        return big[:m, :b], big[:m, b:], big[m:, :b], big[m:, b:]

    def second_stage(re_s, im_s):
        rc, rs, ic, is_ = times_cs(re_s[...].reshape(m, b), im_s[...].reshape(m, b))
        return rc + is_, ic - rs

    xr, xi = second_stage(ur_s, ui_s)
    hr, hi = second_stage(fr_s, fi_s)
    zr = (xr * hr - xi * hi).astype(BF16)
    zi = (xr * hi + xi * hr).astype(BF16)
    rc, rs, ic, is_ = times_cs(zr, zi)
    gr = (rc - is_).reshape(cb, b, b)
    gi = (rs + ic).reshape(cb, b, b)
    ur_s[...] = (gr * tr + gi * ti).astype(BF16)
    ui_s[...] = (gi * tr - gr * ti).astype(BF16)
    inv = inv_ref[...]

    def last_stage(c, carry):
        y_ref[c] = dot(inv, jnp.concatenate([ur_s[c], ui_s[c]], axis=0))
        return carry

    lax.fori_loop(0, cb, last_stage, 0, unroll=2)


def hyena_long_conv(u_t, filt_t):
    c, l = u_t.shape
    b = FFT_B
    consts = _dft_constants()
    full = lambda a: pl.BlockSpec(a.shape, lambda i: (0,) * a.ndim)
    y = pl.pallas_call(
        _hy_conv_kernel,
        grid=(c // HY_CB,),
        in_specs=[pl.BlockSpec((HY_CB, b // 2, b), lambda i: (i, 0, 0)),
                  pl.BlockSpec((HY_CB, b, b), lambda i: (i, 0, 0))] + [full(a) for a in consts],
        out_specs=pl.BlockSpec((HY_CB, b // 2, b), lambda i: (i, 0, 0)),
        out_shape=jax.ShapeDtypeStruct((c, b // 2, b), F32),
        scratch_shapes=[pltpu.VMEM((HY_CB, b, b), BF16) for _ in range(4)],
        compiler_params=_cparams("parallel"),
        name="hyena_long_conv",
    )(u_t.reshape(c, b // 2, b), filt_t.reshape(c, b, b), *consts)
    return y.reshape(c, l)


def _hy_ctx_conv_kernel(u_ref, f_ref, fwdu_ref, fwdf_ref, inv_ref, y_ref):
    dot = functools.partial(jnp.dot, preferred_element_type=F32)
    n = f_ref.shape[0]
    x = dot(fwdu_ref[...], u_ref[...].astype(BF16))
    h = dot(fwdf_ref[...], f_ref[...].astype(BF16))
    xr, xi, hr, hi = x[:n], x[n:], h[:n], h[n:]
    z = jnp.concatenate([xr * hr - xi * hi, xr * hi + xi * hr], axis=0).astype(BF16)
    y_ref[...] = dot(inv_ref[...], z)


def hyena_ctx_conv(u, filt):
    l, c = u.shape
    n = 2 * l
    idx = np.arange(n)
    ang = 2.0 * np.pi * np.outer(idx, idx) / n
    cm, sm = np.cos(ang), np.sin(ang)
    fwdf = np.concatenate([cm, -sm], axis=0)
    inv = np.concatenate([cm[:l], -sm[:l]], axis=1) / n
    consts = (jnp.asarray(fwdf[:, :l], BF16), jnp.asarray(fwdf, BF16), jnp.asarray(inv, BF16))
    tc = 256
    full = lambda a: pl.BlockSpec(a.shape, lambda i: (0,) * a.ndim)
    return pl.pallas_call(
        _hy_ctx_conv_kernel,
        grid=(c // tc,),
        in_specs=[pl.BlockSpec((l, tc), lambda i: (0, i)), pl.BlockSpec((n, tc), lambda i: (0, i))]
        + [full(a) for a in consts],
        out_specs=pl.BlockSpec((l, tc), lambda i: (0, i)),
        out_shape=jax.ShapeDtypeStruct((l, c), F32),
        compiler_params=_cparams("parallel"),
        name="hyena_ctx_conv",
    )(u, filt, *consts)


def _hy_post_kernel(y_ref, u_ref, x0_ref, b_ref, o_ref, *, transposed):
    w = y_ref[...] + u_ref[...] * b_ref[...]
    if transposed:
        w = w.T
    o_ref[...] = (w * x0_ref[...]).astype(o_ref.dtype)


def hyena_post(y, u, x0, bias, transposed):
    l, w = x0.shape
    tb = min(HY_TB, l)
    tm = pl.BlockSpec((tb, w), lambda i: (i, 0))
    yu = pl.BlockSpec((w, tb), lambda i: (0, i)) if transposed else tm
    return pl.pallas_call(
        functools.partial(_hy_post_kernel, transposed=transposed),
        grid=(l // tb,),
        in_specs=[yu, yu, tm, pl.BlockSpec(bias.shape, lambda i: (0, 0))],
        out_specs=tm,
        out_shape=jax.ShapeDtypeStruct((l, w), BF16),
        compiler_params=_cparams("parallel"),
        name="hyena_post",
    )(y, u, x0, bias)


def hyena(z, row0, l, short_w, short_b, filt, bias):
    filt_t, hb0 = hyena_filter(l, *filt)
    if l == FFT_B * FFT_B // 2:
        u_t, x0 = hyena_pre(z, row0, l, short_w, short_b, True)
        y_t = hyena_long_conv(u_t, filt_t)
        return hyena_post(y_t, u_t, x0, bias.reshape(-1, 1) + hb0, True)
    u, x0 = hyena_pre(z, row0, l, short_w, short_b, False)
    y = hyena_ctx_conv(u, filt_t.T)
    return hyena_post(y, u, x0, (bias.reshape(-1, 1) + hb0).T, False)


def kernel(x, c, ctx, c_ctx, w_mod, b_mod, norm1_w, norm2_w, ffn_w_gate, ffn_w_up, ffn_w_down, even_w_in, even_w_out, dn_conv_w, dn_a_log, dn_dt_bias, dn_onorm_w, hy_short_w, hy_short_b, hy_f_w1, hy_f_b1, hy_f_w2, hy_f_b2, hy_f_w3, hy_f_b3, hy_f_w4, hy_f_freq, hy_bias, na_w_qkv, na_rpb, na_w_out, final_norm_w):
    d = D_MODEL
    x_src = (x[0], ctx[0], 0)
    mod = modulation(jnp.concatenate([c, c_ctx[None]], axis=0), w_mod, b_mod)

    for layer in range(DEPTH):
        need_ctx = layer < DEPTH - 1
        rows = ALL_ROWS if need_ctx else SEQ
        sh1, sc1, g1, sh2, sc2, g2 = (mod[layer, :, None, j * d:(j + 1) * d] for j in range(N_MOD))
        i = layer // 2
        h = norm_mod(x_src, norm1_w[layer][None], sc1, sh1)
        if layer % 2 == 0:
            w_in = even_w_in[i]
            w_in = jnp.concatenate([w_in[:, :4 * DN_WIDTH], w_in[:, DN_IN:], w_in[:, 4 * DN_WIDTH:DN_IN],
                                    jnp.zeros((d, EVEN_PAD - EVEN_IN), F32)], axis=1).astype(BF16)
            z = projection(h, w_in, EVEN_TN, F32)
            dn_l, dn_c = gated_deltanet(z, need_ctx, dn_conv_w[i], dn_a_log[i], dn_dt_bias[i], dn_onorm_w[i])
            filt = (hy_f_w1[i], hy_f_b1[i], hy_f_w2[i], hy_f_b2[i], hy_f_w3[i], hy_f_b3[i], hy_f_w4[i],
                    hy_f_freq[i])
            y_lat = (dn_l, hyena(z, 0, SEQ, hy_short_w[i], hy_short_b[i], filt, hy_bias[i]))
            y_ctx = (dn_c, hyena(z, SEQ, CTX_LEN, hy_short_w[i], hy_short_b[i], filt, hy_bias[i])) if need_ctx else y_lat
            w_out = even_w_out
        else:
            z = projection(h, na_w_qkv, QKV_TN, BF16, layer=i, scaled_cols=D_MODEL, scale=NA_DIM ** -0.5)
            ya = na_attention(z, na_bias_tiles(na_rpb[i]))
            y_lat = (ya, ya)
            if need_ctx:
                yc = ctx_attention(z)
                y_ctx = (yc, yc)
            else:
                y_ctx = y_lat
            w_out = na_w_out
        xa, h = proj_residual(y_lat, y_ctx, w_out, i, x_src, rows, g1, norm2_w[layer][None], sc2, sh2)
        xa = ffn_block(xa, h, rows, g2, ffn_w_gate, ffn_w_up, ffn_w_down, layer)
        x_src = (xa, xa, SEQ)
    return final_norm(xa, SEQ, final_norm_w[None])[None]
```

```python
import functools
import math

import jax
import jax.numpy as jnp
import numpy as np
from jax import lax
from jax.experimental import pallas as pl
from jax.experimental.pallas import tpu as pltpu

D_MODEL = 2048
SEQ = 8192
DEPTH = 4
GRID_W = 64
ROWS = SEQ // GRID_W
CTX_LEN = 256
ALL_ROWS = SEQ + CTX_LEN
NORM_EPS = 1e-6
N_MOD = 6

DN_HEADS = 8
DN_DIM = 128
DN_WIDTH = DN_HEADS * DN_DIM
DN_CONV = 5
DN_CHUNK = 64
DN_IN = 4 * DN_WIDTH + 4 * DN_HEADS

HY_WIDTH = D_MODEL - DN_WIDTH
HY_BANDS = 16
HY_DECAY_TARGET = 1e-2
HY_STRONG_DECAY_PCT = 0.3
HY_WEAK_DECAY_PCT = 1.5
HY_IN = 3 * HY_WIDTH
EVEN_IN = DN_IN + HY_IN

NA_HEADS = D_MODEL // 128
NA_DIM = 128
WIN_R = 8
WIN_C = 16
FFN_HIDDEN = -(-8 * D_MODEL // (3 * 256)) * 256

LANES = 128
VMEM_LIMIT = 56 * 1024 * 1024
NEG = -1e30

F32 = jnp.float32
BF16 = jnp.bfloat16


def _cparams(*sem):
    return pltpu.CompilerParams(dimension_semantics=sem, vmem_limit_bytes=VMEM_LIMIT)


def _sigmoid(x):
    return 1.0 / (1.0 + jnp.exp(-x))


MOD_TN = 1024


def _mod_kernel(c_ref, w_ref, b_ref, o_ref):
    rows = []
    for r in range(2):
        c = c_ref[r]
        a = c * _sigmoid(c)
        cols = [jnp.sum(w_ref[0, :, j * LANES:(j + 1) * LANES] * a, axis=0, keepdims=True)
                for j in range(MOD_TN // LANES)]
        rows.append(jnp.concatenate(cols, axis=1))
    o_ref[0] = jnp.concatenate(rows, axis=0) + b_ref[0]


def modulation(c_pair, w_mod, b_mod):
    n = N_MOD * D_MODEL
    cb = jnp.broadcast_to(c_pair[:, :, None], (2, D_MODEL, LANES))
    return pl.pallas_call(
        _mod_kernel,
        grid=(DEPTH, n // MOD_TN),
        in_specs=[pl.BlockSpec((2, D_MODEL, LANES), lambda l, j: (0, 0, 0)),
                  pl.BlockSpec((1, D_MODEL, MOD_TN), lambda l, j: (l, 0, j)),
                  pl.BlockSpec((1, 1, MOD_TN), lambda l, j: (l, 0, j))],
        out_specs=pl.BlockSpec((1, 2, MOD_TN), lambda l, j: (l, 0, j)),
        out_shape=jax.ShapeDtypeStruct((DEPTH, 2, n), F32),
        compiler_params=_cparams("parallel", "parallel"),
        name="modulation",
    )(cb, w_mod, b_mod.reshape(DEPTH, 1, n))


ROW_TILE = 256


def _kind_spec(d):
    return pl.BlockSpec((None, 1, d), lambda i: (jnp.where(i >= SEQ // ROW_TILE, 1, 0), 0, 0))


def _row_source_specs(src, width, col=0):
    n_lat = SEQ // ROW_TILE
    ctx_blk = src[2] // ROW_TILE
    return [pl.BlockSpec((ROW_TILE, width), lambda i: (jnp.minimum(i, n_lat - 1), col)),
            pl.BlockSpec((ROW_TILE, width), lambda i: (ctx_blk, col))]


def _pick_rows(lat_ref, ctx_ref):
    return jnp.where(pl.program_id(0) < SEQ // ROW_TILE, lat_ref[...], ctx_ref[...])


def _norm_mod_kernel(xl_ref, xc_ref, nw_ref, sc_ref, sh_ref, h_ref):
    x = _pick_rows(xl_ref, xc_ref)
    y = x * lax.rsqrt(jnp.mean(x * x, axis=-1, keepdims=True) + NORM_EPS)
    h_ref[...] = (y * nw_ref[...] * (1.0 + sc_ref[...]) + sh_ref[...]).astype(h_ref.dtype)


def norm_mod(x_src, nw, sc, sh):
    d = x_src[0].shape[1]
    return pl.pallas_call(
        _norm_mod_kernel,
        grid=(ALL_ROWS // ROW_TILE,),
        in_specs=_row_source_specs(x_src, d) + [pl.BlockSpec((1, d), lambda i: (0, 0)), _kind_spec(d), _kind_spec(d)],
        out_specs=pl.BlockSpec((ROW_TILE, d), lambda i: (i, 0)),
        out_shape=jax.ShapeDtypeStruct((ALL_ROWS, d), BF16),
        compiler_params=_cparams("parallel"),
        name="norm_mod",
    )(x_src[0], x_src[1], nw, sc, sh)


ROW_TILES = 8
PROJ_ROW_TILES = 16
EVEN_TN = 19 * LANES
EVEN_PAD = 3 * EVEN_TN
QKV_TN = 1024
assert EVEN_PAD >= EVEN_IN


def _weight_spec(w, tn, layer):
    d = w.shape[-2]
    if layer is None:
        return pl.BlockSpec((d, tn), lambda j, i: (0, j))
    return pl.BlockSpec((None, d, tn), lambda j, i: (layer, 0, j))


def _proj_kernel(h_ref, w_ref, o_ref, *scratch, scaled_cols, scale):
    if scratch:
        wb_ref, = scratch

        @pl.when(pl.program_id(1) == 0)
        def _():
            wb_ref[...] = w_ref[...].astype(BF16)
    else:
        wb_ref = w_ref
    y = jnp.dot(h_ref[...], wb_ref[...], preferred_element_type=F32)
    if scaled_cols:
        y = y * jnp.where(pl.program_id(0) * o_ref.shape[1] < scaled_cols, scale, 1.0)
    o_ref[...] = y.astype(o_ref.dtype)


def projection(h, w, tn, out_dtype, layer=None, scaled_cols=0, scale=1.0):
    m, d = h.shape
    n = w.shape[-1]
    tm = m // PROJ_ROW_TILES
    assert n % tn == 0 and scaled_cols % tn == 0
    scratch = [] if w.dtype == BF16 else [pltpu.VMEM((d, tn), BF16)]
    return pl.pallas_call(
        functools.partial(_proj_kernel, scaled_cols=scaled_cols, scale=scale),
        grid=(n // tn, PROJ_ROW_TILES),
        in_specs=[pl.BlockSpec((tm, d), lambda j, i: (i, 0)), _weight_spec(w, tn, layer)],
        out_specs=pl.BlockSpec((tm, tn), lambda j, i: (i, j)),
        out_shape=jax.ShapeDtypeStruct((m, n), out_dtype),
        scratch_shapes=scratch,
        compiler_params=_cparams("parallel", "arbitrary"),
        name="projection",
    )(h, w)


FFN_TF = 512
FFN_TN = 256


def _ffn_up_kernel(h_ref, wg_ref, wu_ref, a_ref, wgb_ref, wub_ref):
    @pl.when(pl.program_id(1) == 0)
    def _():
        wgb_ref[...] = wg_ref[...].astype(BF16)
        wub_ref[...] = wu_ref[...].astype(BF16)

    h = h_ref[...]
    gate = jnp.dot(h, wgb_ref[...], preferred_element_type=F32)
    up = jnp.dot(h, wub_ref[...], preferred_element_type=F32)
    a_ref[...] = (gate * _sigmoid(gate) * up).astype(a_ref.dtype)


def _ffn_down_kernel(a_ref, wd_ref, x_ref, g_ref, o_ref):
    tm = x_ref.shape[0]
    y = jnp.dot(a_ref[...], wd_ref[...].astype(BF16), preferred_element_type=F32)
    row = pl.program_id(0) * tm + lax.broadcasted_iota(jnp.int32, (tm, 1), 0)
    o_ref[...] = x_ref[...] + jnp.where(row < SEQ, g_ref[0], g_ref[1]) * y


def ffn_block(x, h, rows, g, wg, wu, wd, layer):
    m, d = x.shape
    hid = wg.shape[2]
    tm = rows // ROW_TILES
    a = pl.pallas_call(
        _ffn_up_kernel,
        grid=(hid // FFN_TF, ROW_TILES),
        in_specs=[pl.BlockSpec((tm, d), lambda f, i: (i, 0)),
                  _weight_spec(wg, FFN_TF, layer), _weight_spec(wu, FFN_TF, layer)],
        out_specs=pl.BlockSpec((tm, FFN_TF), lambda f, i: (i, f)),
        out_shape=jax.ShapeDtypeStruct((m, hid), BF16),
        scratch_shapes=[pltpu.VMEM((d, FFN_TF), BF16), pltpu.VMEM((d, FFN_TF), BF16)],
        compiler_params=_cparams("parallel", "arbitrary"),
        name="ffn_up",
    )(h, wg, wu)
    return pl.pallas_call(
        _ffn_down_kernel,
        grid=(ROW_TILES, d // FFN_TN),
        in_specs=[pl.BlockSpec((tm, hid), lambda i, j: (i, 0)),
                  pl.BlockSpec((None, hid, FFN_TN), lambda i, j: (layer, 0, j)),
                  pl.BlockSpec((tm, FFN_TN), lambda i, j: (i, j)),
                  pl.BlockSpec((2, 1, FFN_TN), lambda i, j: (0, 0, j))],
        out_specs=pl.BlockSpec((tm, FFN_TN), lambda i, j: (i, j)),
        out_shape=jax.ShapeDtypeStruct((m, d), F32),
        compiler_params=_cparams("parallel", "arbitrary"),
        name="ffn_down",
    )(a, wd, x, g)


def _proj_res_kernel(yal_ref, yac_ref, ybl_ref, ybc_ref, xl_ref, xc_ref, w_ref, g_ref, nw_ref, sc_ref, sh_ref,
                     o_ref, h_ref, wb_ref):
    @pl.when(pl.program_id(0) == 0)
    def _():
        wb_ref[...] = w_ref[...].astype(BF16)

    ya = _pick_rows(yal_ref, yac_ref)
    yb = _pick_rows(ybl_ref, ybc_ref)
    ka = ya.shape[1]
    y = (jnp.dot(ya, wb_ref[:ka, :], preferred_element_type=F32)
         + jnp.dot(yb, wb_ref[ka:, :], preferred_element_type=F32))
    x = _pick_rows(xl_ref, xc_ref) + g_ref[...] * y
    o_ref[...] = x
    xn = x * lax.rsqrt(jnp.mean(x * x, axis=-1, keepdims=True) + NORM_EPS)
    h_ref[...] = (xn * nw_ref[...] * (1.0 + sc_ref[...]) + sh_ref[...]).astype(h_ref.dtype)


def proj_residual(y_lat, y_ctx, w, layer, x_src, rows, g, nw, sc, sh):
    d = x_src[0].shape[1]
    half = d // 2
    second = [1 if pair[1] is pair[0] else 0 for pair in (y_lat, y_ctx)]
    assert second[0] == second[1]
    row = pl.BlockSpec((ROW_TILE, d), lambda i: (i, 0))
    return pl.pallas_call(
        _proj_res_kernel,
        grid=(rows // ROW_TILE,),
        in_specs=_row_source_specs((y_lat[0], y_ctx[0], 0), half) + _row_source_specs((y_lat[1], y_ctx[1], 0), half,
                                                                                      second[0])
        + _row_source_specs(x_src, d) + [
            pl.BlockSpec((None, d, d), lambda i: (layer, 0, 0), pipeline_mode=pl.Buffered(1)),
            _kind_spec(d), pl.BlockSpec((1, d), lambda i: (0, 0)), _kind_spec(d), _kind_spec(d)],
        out_specs=[row, row],
        out_shape=[jax.ShapeDtypeStruct((ALL_ROWS, d), F32), jax.ShapeDtypeStruct((ALL_ROWS, d), BF16)],
        scratch_shapes=[pltpu.VMEM((d, d), BF16)],
        compiler_params=_cparams("arbitrary"),
        name="proj_residual",
    )(y_lat[0], y_ctx[0], y_lat[1], y_ctx[1], x_src[0], x_src[1], w, g, nw, sc, sh)


def _final_norm_kernel(x_ref, w_ref, o_ref):
    x = x_ref[...]
    o_ref[...] = x * lax.rsqrt(jnp.mean(x * x, axis=-1, keepdims=True) + NORM_EPS) * w_ref[...]


def final_norm(x, rows, w):
    d = x.shape[1]
    tm = 512
    return pl.pallas_call(
        _final_norm_kernel,
        grid=(rows // tm,),
        in_specs=[pl.BlockSpec((tm, d), lambda i: (i, 0)), pl.BlockSpec((1, d), lambda i: (0, 0))],
        out_specs=pl.BlockSpec((tm, d), lambda i: (i, 0)),
        out_shape=jax.ShapeDtypeStruct((rows, d), F32),
        compiler_params=_cparams("parallel"),
        name="final_norm",
    )(x, w)


NA_R = 4
NA_KD = (NA_R + WIN_R) // 2
NA_T = 2 * WIN_R
NA_HP = 4


def _na_bias_kernel(rpb_ref, o_ref):
    h = pl.program_id(0)
    c = lax.broadcasted_iota(jnp.int32, (GRID_W, LANES), 0)
    lane = lax.broadcasted_iota(jnp.int32, (GRID_W, LANES), 1)
    left = lane < GRID_W
    kc = jnp.where(left, lane, lane - GRID_W)
    cs = jnp.clip(c - WIN_C // 2, 0, GRID_W - WIN_C)
    inwin = (kc >= cs) & (kc < cs + WIN_C)
    diff = kc - c + (WIN_C - 1)
    n_ro, n_co = 2 * WIN_R - 1, 2 * WIN_C - 1
    for t in range(NA_T):
        def body(d, acc, t=t):
            vl = rpb_ref[(h * n_ro + (t - 1)) * n_co + d] if t >= 1 else jnp.float32(NEG)
            vr = rpb_ref[(h * n_ro + t) * n_co + d] if t < n_ro else jnp.float32(NEG)
            return jnp.where(diff == d, jnp.where(left, vl, vr), acc)

        acc = lax.fori_loop(0, n_co, body, jnp.full((GRID_W, LANES), NEG, F32))
        o_ref[0, t] = jnp.where(inwin, acc, NEG)


def na_bias_tiles(rpb):
    return pl.pallas_call(
        _na_bias_kernel,
        grid=(NA_HEADS,),
        in_specs=[pl.BlockSpec(memory_space=pltpu.SMEM)],
        out_specs=pl.BlockSpec((1, NA_T, GRID_W, LANES), lambda h: (h, 0, 0, 0)),
        out_shape=jax.ShapeDtypeStruct((NA_HEADS, NA_T, GRID_W, LANES), F32),
        compiler_params=_cparams("parallel"),
        name="na_bias_tiles",
    )(rpb.reshape(-1))


def _na_kernel(q_ref, k_ref, v_ref, kc_ref, vc_ref, tt_ref, o_ref):
    r0 = pl.program_id(1) * NA_R
    ks = jnp.clip(r0 - WIN_R // 2, 0, ROWS - 2 * NA_KD)
    span = pl.ds(pl.multiple_of(ks * GRID_W, LANES), NA_KD * LANES)
    heads = range(NA_HP)
    cols = [slice(h * NA_DIM, (h + 1) * NA_DIM) for h in heads]
    contract_last = (((1,), (1,)), ((), ()))
    q = [q_ref[:, cols[h]] for h in heads]
    s_loc = [lax.dot_general(q[h], k_ref[span, cols[h]], contract_last, preferred_element_type=F32) for h in heads]
    s_ctx = [lax.dot_general(q[h], kc_ref[:, cols[h]], contract_last, preferred_element_type=F32) for h in heads]

    left = lax.broadcasted_iota(jnp.int32, (GRID_W, LANES), 1) < GRID_W
    tile_sel = []
    for qi in range(NA_R):
        r = r0 + qi
        rs = jnp.clip(r - WIN_R // 2, 0, ROWS - WIN_R)
        for dj in range(NA_KD):
            kl = ks + 2 * dj
            vl = ((kl >= rs) & (kl < rs + WIN_R)).astype(jnp.int32)
            vr = ((kl + 1 >= rs) & (kl + 1 < rs + WIN_R)).astype(jnp.int32)
            tile_sel.append((jnp.clip(kl - r + WIN_R, 0, NA_T - 1), jnp.where(left, vl, vr) > 0))
    for h in heads:
        rows = [jnp.concatenate([jnp.where(valid, tt_ref[h, t], NEG)
                                 for t, valid in tile_sel[qi * NA_KD:(qi + 1) * NA_KD]], axis=1)
                for qi in range(NA_R)]
        s_loc[h] = s_loc[h] + jnp.concatenate(rows, axis=0)

    m = [jnp.maximum(jnp.max(s_loc[h], axis=-1, keepdims=True), jnp.max(s_ctx[h], axis=-1, keepdims=True))
         for h in heads]
    p_loc = [jnp.exp(s_loc[h] - m[h]) for h in heads]
    p_ctx = [jnp.exp(s_ctx[h] - m[h]) for h in heads]
    denom = [jnp.sum(p_loc[h], axis=-1, keepdims=True) + jnp.sum(p_ctx[h], axis=-1, keepdims=True) for h in heads]
    o = [jnp.dot(p_loc[h].astype(BF16), v_ref[span, cols[h]], preferred_element_type=F32)
         + jnp.dot(p_ctx[h].astype(BF16), vc_ref[:, cols[h]], preferred_element_type=F32) for h in heads]
    for h in heads:
        o_ref[:, cols[h]] = (o[h] / denom[h]).astype(o_ref.dtype)


def na_attention(z, tiles):
    tq = NA_R * GRID_W
    w = NA_HP * NA_DIM
    ng = NA_HEADS // NA_HP
    return pl.pallas_call(
        _na_kernel,
        grid=(ng, ROWS // NA_R),
        in_specs=[pl.BlockSpec((tq, w), lambda g, i: (i, g)),
                  pl.BlockSpec((SEQ, w), lambda g, i: (0, ng + g)),
                  pl.BlockSpec((SEQ, w), lambda g, i: (0, 2 * ng + g)),
                  pl.BlockSpec((CTX_LEN, w), lambda g, i: (SEQ // CTX_LEN, ng + g)),
                  pl.BlockSpec((CTX_LEN, w), lambda g, i: (SEQ // CTX_LEN, 2 * ng + g)),
                  pl.BlockSpec((NA_HP, NA_T, GRID_W, LANES), lambda g, i: (g, 0, 0, 0))],
        out_specs=pl.BlockSpec((tq, w), lambda g, i: (i, g)),
        out_shape=jax.ShapeDtypeStruct((SEQ, D_MODEL), BF16),
        compiler_params=_cparams("parallel", "arbitrary"),
        name="na_attention",
    )(z, z, z, z, z, tiles)


def _ctx_attn_kernel(q_ref, k_ref, v_ref, o_ref):
    s = lax.dot_general(q_ref[...], k_ref[...], (((1,), (1,)), ((), ())), preferred_element_type=F32)
    p = jnp.exp(s - jnp.max(s, axis=-1, keepdims=True))
    o = jnp.dot(p.astype(BF16), v_ref[...], preferred_element_type=F32)
    o_ref[...] = (o / jnp.sum(p, axis=-1, keepdims=True)).astype(o_ref.dtype)


def ctx_attention(z):
    nh = NA_HEADS
    blk = lambda off: pl.BlockSpec((CTX_LEN, NA_DIM), lambda h: (SEQ // CTX_LEN, off + h))
    return pl.pallas_call(
        _ctx_attn_kernel,
        grid=(nh,),
        in_specs=[blk(0), blk(nh), blk(2 * nh)],
        out_specs=pl.BlockSpec((CTX_LEN, NA_DIM), lambda h: (0, h)),
        out_shape=jax.ShapeDtypeStruct((CTX_LEN, D_MODEL), BF16),
        compiler_params=_cparams("parallel"),
        name="ctx_attention",
    )(z, z, z)


DN_SC = 256
HALO = 8
N_GB = 2 * DN_HEADS


def _softplus(x):
    return jnp.maximum(x, 0.0) + jnp.log1p(jnp.exp(-jnp.abs(x)))


def _short_conv(main_ref, prev_ref, next_ref, w_ref, col0, first, last):
    ksize = w_ref.shape[0]
    tb, width = main_ref.shape
    prev = jnp.where(first, 0.0, prev_ref[...])
    nxt = jnp.where(last, 0.0, next_ref[...])
    xf = jnp.concatenate([prev, main_ref[...], nxt], axis=0)
    n = xf.shape[0]
    acc = None
    for j in range(ksize):
        sh = (ksize // 2 - j) % n
        xs = xf if sh == 0 else pltpu.roll(xf, sh, 0)
        term = xs[HALO:HALO + tb] * w_ref[j:j + 1, col0:col0 + width]
        acc = term if acc is None else acc + term
    return acc


def _chunk_masks(n):
    r = lax.broadcasted_iota(jnp.int32, (n, n), 0)
    c = lax.broadcasted_iota(jnp.int32, (n, n), 1)
    return r, c, (r // DN_CHUNK) == (c // DN_CHUNK)


def _dn_scalars_kernel(za_ref, zb_ref, zat_ref, alog_ref, dtb_ref, alogt_ref, dtbt_ref,
                       gc_ref, gt_ref, beta_ref, gct_ref):
    hi = lax.Precision.HIGHEST
    g = -jnp.exp(alog_ref[...]) * _softplus(za_ref[...] + dtb_ref[...])
    beta_ref[...] = _sigmoid(zb_ref[...])
    r, c, same = _chunk_masks(DN_SC)
    lo = (same & (c <= r)).astype(F32)
    up = (same & (c >= r)).astype(F32)
    fwd_col = lax.broadcasted_iota(jnp.int32, (DN_SC, N_GB), 1) < DN_HEADS
    gc_ref[...] = jnp.where(fwd_col, jnp.dot(lo, g, precision=hi), jnp.dot(up, g, precision=hi))
    gt_ref[...] = jnp.dot(same.astype(F32), g, precision=hi)
    g_t = -jnp.exp(alogt_ref[...]) * _softplus(zat_ref[...] + dtbt_ref[...])
    fwd_row = lax.broadcasted_iota(jnp.int32, (N_GB, DN_SC), 0) < DN_HEADS
    gct_ref[...] = jnp.where(fwd_row, jnp.dot(g_t, up, precision=hi), jnp.dot(g_t, lo, precision=hi))


def dn_scalars(za, zb, a_log, dt_bias):
    l = za.shape[0]
    col = pl.BlockSpec((DN_SC, N_GB), lambda i: (i, 0))
    row = pl.BlockSpec((N_GB, DN_SC), lambda i: (0, i))
    prow = pl.BlockSpec((1, N_GB), lambda i: (0, 0))
    pcol = pl.BlockSpec((N_GB, 1), lambda i: (0, 0))
    cshape = jax.ShapeDtypeStruct((l, N_GB), F32)
    return pl.pallas_call(
        _dn_scalars_kernel,
        grid=(l // DN_SC,),
        in_specs=[col, col, row, prow, prow, pcol, pcol],
        out_specs=[col, col, col, row],
        out_shape=[cshape, cshape, cshape, jax.ShapeDtypeStruct((N_GB, l), F32)],
        compiler_params=_cparams("parallel"),
        name="dn_scalars",
    )(za, zb, za.T, a_log.reshape(1, N_GB), dt_bias.reshape(1, N_GB), a_log.reshape(N_GB, 1),
      dt_bias.reshape(N_GB, 1))


def _dn_prep_kernel(zq_ref, zqp_ref, zqn_ref, zk_ref, zkp_ref, zkn_ref, zv_ref, zvp_ref, zvn_ref, cw_ref, o_ref):
    i = pl.program_id(0)
    n_lat = SEQ // DN_SC
    first = (i == 0) | (i == n_lat)
    last = (i == n_lat - 1) | (i == pl.num_programs(0) - 1)
    parts = ((zq_ref, zqp_ref, zqn_ref), (zk_ref, zkp_ref, zkn_ref), (zv_ref, zvp_ref, zvn_ref))
    for part, refs in enumerate(parts):
        acc = _short_conv(*refs, cw_ref, part * DN_WIDTH, first, last)
        x = acc * _sigmoid(acc)
        for h in range(DN_HEADS):
            xh = x[:, h * DN_DIM:(h + 1) * DN_DIM]
            if part < 2:
                xh = xh * (lax.rsqrt(jnp.sum(xh * xh, axis=-1, keepdims=True) + NORM_EPS)
                           * (DN_DIM ** -0.5 if part == 0 else 1.0))
            o_ref[:, part * DN_WIDTH + h * DN_DIM:part * DN_WIDTH + (h + 1) * DN_DIM] = xh


def dn_prep(z, conv_w):
    m = z.shape[0]
    per = DN_SC // HALO
    last_halo = m // HALO - 1

    def part_specs(part):
        return [pl.BlockSpec((DN_SC, DN_WIDTH), lambda i: (i, part)),
                pl.BlockSpec((HALO, DN_WIDTH), lambda i: (jnp.maximum(i * per - 1, 0), part)),
                pl.BlockSpec((HALO, DN_WIDTH), lambda i: (jnp.minimum((i + 1) * per, last_halo), part))]

    return pl.pallas_call(
        _dn_prep_kernel,
        grid=(m // DN_SC,),
        in_specs=part_specs(0) + part_specs(1) + part_specs(2) + [
            pl.BlockSpec((DN_CONV, 3 * DN_WIDTH), lambda i: (0, 0))],
        out_specs=pl.BlockSpec((DN_SC, 3 * DN_WIDTH), lambda i: (i, 0)),
        out_shape=jax.ShapeDtypeStruct((m, 3 * DN_WIDTH), F32),
        compiler_params=_cparams("parallel"),
        name="dn_prep",
    )(z, z, z, z, z, z, z, z, z, conv_w)


def _gdn_kernel(q_ref, k_ref, v_ref, gc_ref, gt_ref, beta_ref, gct_ref, s0_ref, o_ref, sfin_ref, state_ref,
                *, bwd, nblk):
    i = pl.program_id(0)

    @pl.when(i == 0)
    def _():
        state_ref[...] = s0_ref[...]

    r, c, same64 = _chunk_masks(DN_SC)
    tri = same64 & ((c >= r) if bwd else (c <= r))
    offdiag = r != c
    same16 = (r // 16) == (c // 16)
    same32 = (r // 32) == (c // 32)
    contract_last = (((1,), (1,)), ((), ()))
    contract_first = (((0,), (0,)), ((), ()))
    nc = DN_SC // DN_CHUNK
    dot = functools.partial(jnp.dot, preferred_element_type=F32)

    heads = range(DN_HEADS)
    col0 = DN_HEADS if bwd else 0
    gc_c = [gc_ref[:, col0 + h:col0 + h + 1] for h in heads]
    gt_c = [gt_ref[:, col0 + h:col0 + h + 1] for h in heads]
    beta_c = [beta_ref[:, col0 + h:col0 + h + 1] for h in heads]
    q, k, v, k16, dec, qk16, m, p, e = ([None] * DN_HEADS for _ in range(9))
    for h in heads:
        sl = slice(h * DN_DIM, (h + 1) * DN_DIM)
        q[h], k[h], v[h] = q_ref[:, sl], k_ref[:, sl], v_ref[:, sl]
        k16[h] = k[h].astype(BF16)
        dec[h] = jnp.exp(jnp.where(tri, gc_c[h] - gct_ref[col0 + h:col0 + h + 1, :], NEG))
    for h in heads:
        kk = lax.dot_general(k16[h], k16[h], contract_last, preferred_element_type=F32)
        m[h] = jnp.where(offdiag, kk * beta_c[h] * dec[h], 0.0)
        qk16[h] = (lax.dot_general(q[h].astype(BF16), k16[h], contract_last, preferred_element_type=F32)
                   * dec[h]).astype(BF16)

    for h in heads:
        p[h] = jnp.where(same16, -m[h], 0.0)
        e[h] = p[h]
    p16 = [p[h].astype(BF16) for h in heads]
    for _ in range(3):
        for h in heads:
            p[h] = dot(p16[h], p16[h])
            p16[h] = p[h].astype(BF16)
        for h in heads:
            e[h] = e[h] + p[h] + dot(e[h].astype(BF16), p16[h])
    for inner, outer in ((same16, same32), (same32, None)):
        y = [None] * DN_HEADS
        for h in heads:
            cm = jnp.where(~inner if outer is None else (outer & ~inner), m[h], 0.0)
            y[h] = cm + dot(cm.astype(BF16), e[h].astype(BF16))
        for h in heads:
            e[h] = e[h] - y[h] - dot(e[h].astype(BF16), y[h].astype(BF16))

    u, w16, qg16, kd16, s = ([None] * DN_HEADS for _ in range(5))
    for h in heads:
        eg = jnp.exp(gc_c[h])
        rhs = jnp.concatenate([v[h] * beta_c[h], k[h] * (beta_c[h] * eg)], axis=1)
        uw = rhs + dot(e[h].astype(BF16), rhs.astype(BF16))
        u[h] = uw[:, :DN_DIM]
        w16[h] = uw[:, DN_DIM:].astype(BF16)
        qg16[h] = (q[h] * eg).astype(BF16)
        kd16[h] = (k[h] * jnp.exp(gt_c[h] - gc_c[h])).astype(BF16)
        s[h] = state_ref[h]

    outs = [[None] * nc for _ in heads]
    for ci in (range(nc - 1, -1, -1) if bwd else range(nc)):
        rows = slice(ci * DN_CHUNK, (ci + 1) * DN_CHUNK)
        ws = [dot(jnp.concatenate([w16[h][rows], qg16[h][rows]], axis=0), s[h].astype(BF16)) for h in heads]
        vn16 = [(u[h][rows] - ws[h][:DN_CHUNK]).astype(BF16) for h in heads]
        for h in heads:
            outs[h][ci] = ws[h][DN_CHUNK:] + dot(qk16[h][rows, ci * DN_CHUNK:(ci + 1) * DN_CHUNK], vn16[h])
        for h in heads:
            gl = jnp.exp(gt_c[h][ci * DN_CHUNK:ci * DN_CHUNK + 1, :])
            s[h] = s[h] * gl + lax.dot_general(kd16[h][rows], vn16[h], contract_first, preferred_element_type=F32)
    for h in heads:
        state_ref[h] = s[h]
        o_ref[:, h * DN_DIM:(h + 1) * DN_DIM] = jnp.concatenate(outs[h], axis=0)

    @pl.when(i == nblk - 1)
    def _():
        sfin_ref[...] = state_ref[...]


def gdn_scan(qkv, gc, gt, beta, gct, s0, bwd, row0, l):
    nblk = l // DN_SC
    blk0 = row0 // DN_SC

    def rel(i):
        return nblk - 1 - i if bwd else i

    def b(i):
        return blk0 + rel(i)

    def part_spec(part):
        return pl.BlockSpec((DN_SC, DN_WIDTH), lambda i: (b(i), part))

    col = pl.BlockSpec((DN_SC, N_GB), lambda i: (b(i), 0))
    st = pl.BlockSpec((DN_HEADS, DN_DIM, DN_DIM), lambda i: (0, 0, 0))
    return pl.pallas_call(
        functools.partial(_gdn_kernel, bwd=bwd, nblk=nblk),
        grid=(nblk,),
        in_specs=[part_spec(0), part_spec(1), part_spec(2), col, col, col,
                  pl.BlockSpec((N_GB, DN_SC), lambda i: (0, b(i))), st],
        out_specs=[pl.BlockSpec((DN_SC, DN_WIDTH), lambda i: (rel(i), 0)), st],
        out_shape=[jax.ShapeDtypeStruct((l, DN_WIDTH), F32),
                   jax.ShapeDtypeStruct((DN_HEADS, DN_DIM, DN_DIM), F32)],
        scratch_shapes=[pltpu.VMEM((DN_HEADS, DN_DIM, DN_DIM), F32)],
        compiler_params=_cparams("arbitrary"),
        name="gdn_scan_bwd" if bwd else "gdn_scan_fwd",
    )(qkv, qkv, qkv, gc, gt, beta, gct, s0)


def _dn_out_kernel(of_ref, ob_ref, gate_ref, nw_ref, y_ref):
    for h in range(DN_HEADS):
        sl = slice(h * DN_DIM, (h + 1) * DN_DIM)
        o = of_ref[:, sl] + ob_ref[:, sl]
        o = o * lax.rsqrt(jnp.mean(o * o, axis=-1, keepdims=True) + NORM_EPS) * nw_ref[...]
        gate = gate_ref[:, sl]
        y_ref[:, sl] = (o * (gate * _sigmoid(gate))).astype(y_ref.dtype)


def dn_output(o_f, o_b, z, row0, onorm_w):
    l = o_f.shape[0]
    tm = min(l, 512)
    blk0 = row0 // tm
    blk = pl.BlockSpec((tm, DN_WIDTH), lambda i: (i, 0))
    return pl.pallas_call(
        _dn_out_kernel,
        grid=(l // tm,),
        in_specs=[blk, blk, pl.BlockSpec((tm, DN_WIDTH), lambda i: (blk0 + i, 3)),
                  pl.BlockSpec((1, DN_DIM), lambda i: (0, 0))],
        out_specs=blk,
        out_shape=jax.ShapeDtypeStruct((l, DN_WIDTH), BF16),
        compiler_params=_cparams("parallel"),
        name="dn_output",
    )(o_f, o_b, z, onorm_w.reshape(1, DN_DIM))


def gated_deltanet(z, need_ctx, conv_w, a_log, dt_bias, onorm_w):
    a0, b0 = EVEN_IN - 2 * N_GB, EVEN_IN - N_GB
    scal = dn_scalars(z[:, a0:b0], z[:, b0:b0 + N_GB], a_log, dt_bias)
    qkv = dn_prep(z, conv_w)
    s0 = jnp.zeros((DN_HEADS, DN_DIM, DN_DIM), F32)
    outs_c, outs_l = [], []
    for bwd in (False, True):
        oc, s_ctx = gdn_scan(qkv, *scal, s0, bwd, SEQ, CTX_LEN)
        ol, _ = gdn_scan(qkv, *scal, s_ctx, bwd, 0, SEQ)
        outs_c.append(oc)
        outs_l.append(ol)
    y_ctx = dn_output(*outs_c, z, SEQ, onorm_w) if need_ctx else None
    return dn_output(*outs_l, z, 0, onorm_w), y_ctx


HY_SHORT = 3
HY_EMB = 1 + 2 * HY_BANDS
HY_EMB_PAD = 40
HY_TB = 256
FFT_B = 128
HY_CB = 16
HY_COL0 = 4


def _hy_pre_kernel(x0_ref, x0p_ref, x0n_ref, x1_ref, x1p_ref, x1n_ref, v_ref, vp_ref, vn_ref, w_ref, b_ref,
                   u_ref, x0c_ref, *, nblk, transpose_u):
    first = pl.program_id(0) == 0
    last = pl.program_id(0) == nblk - 1
    w = HY_WIDTH
    x0 = _short_conv(x0_ref, x0p_ref, x0n_ref, w_ref, 0, first, last) + b_ref[:, :w]
    x1 = _short_conv(x1_ref, x1p_ref, x1n_ref, w_ref, w, first, last) + b_ref[:, w:2 * w]
    v = _short_conv(v_ref, vp_ref, vn_ref, w_ref, 2 * w, first, last) + b_ref[:, 2 * w:]
    u = v * x1
    x0c_ref[...] = x0
    u_ref[...] = u.T if transpose_u else u


def hyena_pre(z, row0, l, short_w, short_b, transpose_u):
    tb = min(HY_TB, l)
    nblk = l // tb
    per = tb // HALO
    blk0 = row0 // tb
    last_halo = z.shape[0] // HALO - 1
    w = HY_WIDTH

    def part_specs(part):
        cb = HY_COL0 + part
        return [pl.BlockSpec((tb, w), lambda i: (blk0 + i, cb)),
                pl.BlockSpec((HALO, w), lambda i: (jnp.maximum((blk0 + i) * per - 1, 0), cb)),
                pl.BlockSpec((HALO, w), lambda i: (jnp.minimum((blk0 + i + 1) * per, last_halo), cb))]

    u_shape, u_spec = ((w, l), pl.BlockSpec((w, tb), lambda i: (0, i))) if transpose_u else (
        (l, w), pl.BlockSpec((tb, w), lambda i: (i, 0)))
    return pl.pallas_call(
        functools.partial(_hy_pre_kernel, nblk=nblk, transpose_u=transpose_u),
        grid=(nblk,),
        in_specs=part_specs(0) + part_specs(1) + part_specs(2) + [
            pl.BlockSpec((HY_SHORT, HY_IN), lambda i: (0, 0)), pl.BlockSpec((1, HY_IN), lambda i: (0, 0))],
        out_specs=[u_spec, pl.BlockSpec((tb, w), lambda i: (i, 0))],
        out_shape=[jax.ShapeDtypeStruct(u_shape, F32), jax.ShapeDtypeStruct((l, w), F32)],
        compiler_params=_cparams("parallel"),
        name="hyena_pre",
    )(z, z, z, z, z, z, z, z, z, short_w, short_b.reshape(1, HY_IN))


def _hy_filter_kernel(w1t_ref, b1_ref, w2t_ref, b2_ref, w3t_ref, b3_ref, w4t_ref, fr_ref, band_ref, dl_ref,
                      f_ref, hb0_ref, *, l, fb):
    hi = lax.Precision.HIGHEST
    j = pl.program_id(0)
    second = j >= l // fb
    n = j * fb + lax.broadcasted_iota(jnp.int32, (1, fb), 1)
    pos = jnp.where(second, 2 * l - n, n).astype(F32)
    t = pos / max(l - 1, 1)
    wpos = 2.0 * math.pi * pos / l
    arg = band_ref[...] * wpos
    row = lax.broadcasted_iota(jnp.int32, (HY_EMB_PAD, fb), 0)
    feat = jnp.where(row == 0, t, jnp.where(row <= HY_BANDS, jnp.cos(arg),
                                            jnp.where(row <= 2 * HY_BANDS, -jnp.sin(arg), 0.0)))
    fr = fr_ref[...]
    h = jnp.sin(fr * (jnp.dot(w1t_ref[...], feat, precision=hi) + b1_ref[...]))
    h = jnp.sin(fr * (jnp.dot(w2t_ref[...], h, precision=hi) + b2_ref[...]))
    h = jnp.sin(fr * (jnp.dot(w3t_ref[...], h, precision=hi) + b3_ref[...]))
    window = jnp.exp(-t * dl_ref[...])
    half = pl.multiple_of(jnp.where(second, HY_WIDTH, 0), HY_WIDTH)
    h16 = h.astype(BF16)
    f = jnp.dot(w4t_ref[pl.ds(half, HY_WIDTH), :].astype(BF16), h16, preferred_element_type=F32) * window
    f_ref[...] = jnp.where(n == l, 0.0, f)

    @pl.when(j == 0)
    def _():
        hb0_ref[...] = (jnp.dot(w4t_ref[HY_WIDTH:, :].astype(BF16), h16[:, :LANES], preferred_element_type=F32)
                        * window[:, :LANES])


def hyena_filter(l, w1, b1, w2, b2, w3, b3, w4, freq):
    fb = min(1024, l)
    colv = lambda v: v.reshape(-1, 1)
    bands = np.zeros((HY_EMB_PAD, 1), np.float32)
    bands[1:1 + HY_BANDS, 0] = bands[1 + HY_BANDS:HY_EMB, 0] = np.linspace(1e-4, HY_BANDS - 1, HY_BANDS,
                                                                            dtype=np.float32)
    min_decay = math.log(HY_DECAY_TARGET) / HY_WEAK_DECAY_PCT
    max_decay = math.log(HY_DECAY_TARGET) / HY_STRONG_DECAY_PCT
    deltas = np.abs(np.linspace(min_decay, max_decay, HY_WIDTH, dtype=np.float32)).reshape(-1, 1)
    w1t = jnp.pad(w1.T, ((0, 0), (0, HY_EMB_PAD - HY_EMB)))
    full = lambda a: pl.BlockSpec(a.shape, lambda j: (0,) * a.ndim)
    args = (w1t, colv(b1), w2.T, colv(b2), w3.T, colv(b3), w4.T, colv(freq), jnp.asarray(bands), jnp.asarray(deltas))
    filt, hb0 = pl.pallas_call(
        functools.partial(_hy_filter_kernel, l=l, fb=fb),
        grid=(2 * l // fb,),
        in_specs=[full(a) for a in args],
        out_specs=[pl.BlockSpec((HY_WIDTH, fb), lambda j: (0, j)), pl.BlockSpec((HY_WIDTH, LANES), lambda j: (0, 0))],
        out_shape=[jax.ShapeDtypeStruct((HY_WIDTH, 2 * l), F32), jax.ShapeDtypeStruct((HY_WIDTH, LANES), F32)],
        compiler_params=_cparams("arbitrary"),
        name="hyena_filter",
    )(*args)
    return filt, hb0[:, :1]


def _dft_constants():
    b = FFT_B
    n = b * b
    idx = np.arange(b)
    ang = 2.0 * np.pi * np.outer(idx, idx) / b
    c, s = np.cos(ang), np.sin(ang)
    tw = 2.0 * np.pi * np.outer(idx, idx) / n
    fwd_b = np.concatenate([c, -s], axis=0)
    cs = np.concatenate([c, s], axis=1)
    inv_b = np.concatenate([c[:b // 2], -s[:b // 2]], axis=1) / n
    return (jnp.asarray(fwd_b, BF16), jnp.asarray(cs, BF16), jnp.asarray(np.cos(tw), F32),
            jnp.asarray(-np.sin(tw), F32), jnp.asarray(inv_b, BF16))


def _hy_conv_kernel(u_ref, f_ref, fwd_ref, cs_ref, tr_ref, ti_ref, inv_ref, y_ref, ur_s, ui_s, fr_s, fi_s):
    cb = u_ref.shape[0]
    b = FFT_B
    m = cb * b
    dot = functools.partial(jnp.dot, preferred_element_type=F32)
    fwd = fwd_ref[...]
    fwd_half = fwd[:, :b // 2]
    tr, ti = tr_ref[...], ti_ref[...]

    def first_stage(c, carry):
        for src, lhs, re_s, im_s in ((u_ref, fwd_half, ur_s, ui_s), (f_ref, fwd, fr_s, fi_s)):
            p = dot(lhs, src[c].astype(BF16))
            pr, pi = p[:b], p[b:]
            re_s[c] = (pr * tr - pi * ti).astype(BF16)
            im_s[c] = (pr * ti + pi * tr).astype(BF16)
        return carry

    lax.fori_loop(0, cb, first_stage, 0, unroll=True)

    cs = cs_ref[...]

    def times_cs(re, im):
        big = dot(jnp.concatenate([re, im], axis=0), cs)
        return big[:m, :b], big[:m, b:], big[m:, :b], big[m:, b:]

    def second_stage(re_s, im_s):
        rc, rs, ic, is_ = times_cs(re_s[...].reshape(m, b), im_s[...].reshape(m, b))
        return rc + is_, ic - rs

    xr, xi = second_stage(ur_s, ui_s)
    hr, hi = second_stage(fr_s, fi_s)
    zr = (xr * hr - xi * hi).astype(BF16)
    zi = (xr * hi + xi * hr).astype(BF16)
    rc, rs, ic, is_ = times_cs(zr, zi)
    gr = (rc - is_).reshape(cb, b, b)
    gi = (rs + ic).reshape(cb, b, b)
    ur_s[...] = (gr * tr + gi * ti).astype(BF16)
    ui_s[...] = (gi * tr - gr * ti).astype(BF16)
    inv = inv_ref[...]

    def last_stage(c, carry):
        y_ref[c] = dot(inv, jnp.concatenate([ur_s[c], ui_s[c]], axis=0))
        return carry

    lax.fori_loop(0, cb, last_stage, 0, unroll=True)


def hyena_long_conv(u_t, filt_t):
    c, l = u_t.shape
    b = FFT_B
    consts = _dft_constants()
    full = lambda a: pl.BlockSpec(a.shape, lambda i: (0,) * a.ndim)
    y = pl.pallas_call(
        _hy_conv_kernel,
        grid=(c // HY_CB,),
        in_specs=[pl.BlockSpec((HY_CB, b // 2, b), lambda i: (i, 0, 0)),
                  pl.BlockSpec((HY_CB, b, b), lambda i: (i, 0, 0))] + [full(a) for a in consts],
        out_specs=pl.BlockSpec((HY_CB, b // 2, b), lambda i: (i, 0, 0)),
        out_shape=jax.ShapeDtypeStruct((c, b // 2, b), F32),
        scratch_shapes=[pltpu.VMEM((HY_CB, b, b), BF16) for _ in range(4)],
        compiler_params=_cparams("parallel"),
        name="hyena_long_conv",
    )(u_t.reshape(c, b // 2, b), filt_t.reshape(c, b, b), *consts)
    return y.reshape(c, l)


def _hy_ctx_conv_kernel(u_ref, f_ref, fwdu_ref, fwdf_ref, inv_ref, y_ref):
    dot = functools.partial(jnp.dot, preferred_element_type=F32)
    n = f_ref.shape[0]
    x = dot(fwdu_ref[...], u_ref[...].astype(BF16))
    h = dot(fwdf_ref[...], f_ref[...].astype(BF16))
    xr, xi, hr, hi = x[:n], x[n:], h[:n], h[n:]
    z = jnp.concatenate([xr * hr - xi * hi, xr * hi + xi * hr], axis=0).astype(BF16)
    y_ref[...] = dot(inv_ref[...], z)


def hyena_ctx_conv(u, filt):
    l, c = u.shape
    n = 2 * l
    idx = np.arange(n)
    ang = 2.0 * np.pi * np.outer(idx, idx) / n
    cm, sm = np.cos(ang), np.sin(ang)
    fwdf = np.concatenate([cm, -sm], axis=0)
    inv = np.concatenate([cm[:l], -sm[:l]], axis=1) / n
    consts = (jnp.asarray(fwdf[:, :l], BF16), jnp.asarray(fwdf, BF16), jnp.asarray(inv, BF16))
    tc = 256
    full = lambda a: pl.BlockSpec(a.shape, lambda i: (0,) * a.ndim)
    return pl.pallas_call(
        _hy_ctx_conv_kernel,
        grid=(c // tc,),
        in_specs=[pl.BlockSpec((l, tc), lambda i: (0, i)), pl.BlockSpec((n, tc), lambda i: (0, i))]
        + [full(a) for a in consts],
        out_specs=pl.BlockSpec((l, tc), lambda i: (0, i)),
        out_shape=jax.ShapeDtypeStruct((l, c), F32),
        compiler_params=_cparams("parallel"),
        name="hyena_ctx_conv",
    )(u, filt, *consts)


def _hy_post_kernel(y_ref, u_ref, x0_ref, b_ref, o_ref, *, transposed):
    w = y_ref[...] + u_ref[...] * b_ref[...]
    if transposed:
        w = w.T
    o_ref[...] = (w * x0_ref[...]).astype(o_ref.dtype)


def hyena_post(y, u, x0, bias, transposed):
    l, w = x0.shape
    tb = min(HY_TB, l)
    tm = pl.BlockSpec((tb, w), lambda i: (i, 0))
    yu = pl.BlockSpec((w, tb), lambda i: (0, i)) if transposed else tm
    return pl.pallas_call(
        functools.partial(_hy_post_kernel, transposed=transposed),
        grid=(l // tb,),
        in_specs=[yu, yu, tm, pl.BlockSpec(bias.shape, lambda i: (0, 0))],
        out_specs=tm,
        out_shape=jax.ShapeDtypeStruct((l, w), BF16),
        compiler_params=_cparams("parallel"),
        name="hyena_post",
    )(y, u, x0, bias)


def hyena(z, row0, l, short_w, short_b, filt, bias):
    filt_t, hb0 = hyena_filter(l, *filt)
    if l == FFT_B * FFT_B // 2:
        u_t, x0 = hyena_pre(z, row0, l, short_w, short_b, True)
        y_t = hyena_long_conv(u_t, filt_t)
        return hyena_post(y_t, u_t, x0, bias.reshape(-1, 1) + hb0, True)
    u, x0 = hyena_pre(z, row0, l, short_w, short_b, False)
    y = hyena_ctx_conv(u, filt_t.T)
    return hyena_post(y, u, x0, (bias.reshape(-1, 1) + hb0).T, False)


def kernel(x, c, ctx, c_ctx, w_mod, b_mod, norm1_w, norm2_w, ffn_w_gate, ffn_w_up, ffn_w_down, even_w_in, even_w_out, dn_conv_w, dn_a_log, dn_dt_bias, dn_onorm_w, hy_short_w, hy_short_b, hy_f_w1, hy_f_b1, hy_f_w2, hy_f_b2, hy_f_w3, hy_f_b3, hy_f_w4, hy_f_freq, hy_bias, na_w_qkv, na_rpb, na_w_out, final_norm_w):
    d = D_MODEL
    x_src = (x[0], ctx[0], 0)
    mod = modulation(jnp.concatenate([c, c_ctx[None]], axis=0), w_mod, b_mod)

    for layer in range(DEPTH):
        need_ctx = layer < DEPTH - 1
        rows = ALL_ROWS if need_ctx else SEQ
        sh1, sc1, g1, sh2, sc2, g2 = (mod[layer, :, None, j * d:(j + 1) * d] for j in range(N_MOD))
        i = layer // 2
        h = norm_mod(x_src, norm1_w[layer][None], sc1, sh1)
        if layer % 2 == 0:
            w_in = even_w_in[i]
            w_in = jnp.concatenate([w_in[:, :4 * DN_WIDTH], w_in[:, DN_IN:], w_in[:, 4 * DN_WIDTH:DN_IN],
                                    jnp.zeros((d, EVEN_PAD - EVEN_IN), F32)], axis=1).astype(BF16)
            z = projection(h, w_in, EVEN_TN, F32)
            dn_l, dn_c = gated_deltanet(z, need_ctx, dn_conv_w[i], dn_a_log[i], dn_dt_bias[i], dn_onorm_w[i])
            filt = (hy_f_w1[i], hy_f_b1[i], hy_f_w2[i], hy_f_b2[i], hy_f_w3[i], hy_f_b3[i], hy_f_w4[i],
                    hy_f_freq[i])
            y_lat = (dn_l, hyena(z, 0, SEQ, hy_short_w[i], hy_short_b[i], filt, hy_bias[i]))
            y_ctx = (dn_c, hyena(z, SEQ, CTX_LEN, hy_short_w[i], hy_short_b[i], filt, hy_bias[i])) if need_ctx else y_lat
            w_out = even_w_out
        else:
            z = projection(h, na_w_qkv, QKV_TN, BF16, layer=i, scaled_cols=D_MODEL, scale=NA_DIM ** -0.5)
            ya = na_attention(z, na_bias_tiles(na_rpb[i]))
            y_lat = (ya, ya)
            if need_ctx:
                yc = ctx_attention(z)
                y_ctx = (yc, yc)
            else:
                y_ctx = y_lat
            w_out = na_w_out
        xa, h = proj_residual(y_lat, y_ctx, w_out, i, x_src, rows, g1, norm2_w[layer][None], sc2, sh2)
        xa = ffn_block(xa, h, rows, g2, ffn_w_gate, ffn_w_up, ffn_w_down, layer)
        x_src = (xa, xa, SEQ)
    return final_norm(xa, SEQ, final_norm_w[None])[None]
```

```python
import functools
import math

import jax
import jax.numpy as jnp
import numpy as np
from jax import lax
from jax.experimental import pallas as pl
from jax.experimental.pallas import tpu as pltpu

D_MODEL = 2048
SEQ = 8192
DEPTH = 4
GRID_W = 64
ROWS = SEQ // GRID_W
CTX_LEN = 256
ALL_ROWS = SEQ + CTX_LEN
NORM_EPS = 1e-6
N_MOD = 6

DN_HEADS = 8
DN_DIM = 128
DN_WIDTH = DN_HEADS * DN_DIM
DN_CONV = 5
DN_CHUNK = 64
DN_IN = 4 * DN_WIDTH + 4 * DN_HEADS

HY_WIDTH = D_MODEL - DN_WIDTH
HY_BANDS = 16
HY_DECAY_TARGET = 1e-2
HY_STRONG_DECAY_PCT = 0.3
HY_WEAK_DECAY_PCT = 1.5
HY_IN = 3 * HY_WIDTH
EVEN_IN = DN_IN + HY_IN

NA_HEADS = D_MODEL // 128
NA_DIM = 128
WIN_R = 8
WIN_C = 16
FFN_HIDDEN = -(-8 * D_MODEL // (3 * 256)) * 256

LANES = 128
VMEM_LIMIT = 56 * 1024 * 1024
NEG = -1e30

F32 = jnp.float32
BF16 = jnp.bfloat16


def _cparams(*sem):
    return pltpu.CompilerParams(dimension_semantics=sem, vmem_limit_bytes=VMEM_LIMIT)


def _sigmoid(x):
    return 1.0 / (1.0 + jnp.exp(-x))


MOD_TK = 128


def _mod_kernel(c_ref, w_ref, b_ref, o_ref):
    k = pl.program_id(1)
    n = o_ref.shape[2]

    @pl.when(k == 0)
    def _():
        o_ref[0] = jnp.broadcast_to(b_ref[0], (2, n))

    rows = []
    for r in range(2):
        c = c_ref[r, pl.ds(pl.multiple_of(k * MOD_TK, MOD_TK), MOD_TK), :]
        a = c * _sigmoid(c)
        cols = [jnp.sum(w_ref[0, :, j * LANES:(j + 1) * LANES] * a, axis=0, keepdims=True)
                for j in range(n // LANES)]
        rows.append(jnp.concatenate(cols, axis=1))
    o_ref[0] += jnp.concatenate(rows, axis=0)


def modulation(c_pair, w_mod, b_mod):
    n = N_MOD * D_MODEL
    cb = jnp.broadcast_to(c_pair[:, :, None], (2, D_MODEL, LANES))
    return pl.pallas_call(
        _mod_kernel,
        grid=(DEPTH, D_MODEL // MOD_TK),
        in_specs=[pl.BlockSpec((2, D_MODEL, LANES), lambda l, k: (0, 0, 0)),
                  pl.BlockSpec((1, MOD_TK, n), lambda l, k: (l, k, 0)),
                  pl.BlockSpec((1, 1, n), lambda l, k: (l, 0, 0))],
        out_specs=pl.BlockSpec((1, 2, n), lambda l, k: (l, 0, 0)),
        out_shape=jax.ShapeDtypeStruct((DEPTH, 2, n), F32),
        compiler_params=_cparams("parallel", "arbitrary"),
        name="modulation",
    )(cb, w_mod, b_mod.reshape(DEPTH, 1, n))


ROW_TILE = 256


def _kind_spec(d):
    return pl.BlockSpec((None, 1, d), lambda i: (jnp.where(i >= SEQ // ROW_TILE, 1, 0), 0, 0))


def _row_source_specs(src, width, col=0):
    n_lat = SEQ // ROW_TILE
    ctx_blk = src[2] // ROW_TILE
    return [pl.BlockSpec((ROW_TILE, width), lambda i: (jnp.minimum(i, n_lat - 1), col)),
            pl.BlockSpec((ROW_TILE, width), lambda i: (ctx_blk, col))]


def _pick_rows(lat_ref, ctx_ref):
    return jnp.where(pl.program_id(0) < SEQ // ROW_TILE, lat_ref[...], ctx_ref[...])


def _norm_mod_kernel(xl_ref, xc_ref, nw_ref, sc_ref, sh_ref, h_ref):
    x = _pick_rows(xl_ref, xc_ref)
    y = x * lax.rsqrt(jnp.mean(x * x, axis=-1, keepdims=True) + NORM_EPS)
    h_ref[...] = (y * nw_ref[...] * (1.0 + sc_ref[...]) + sh_ref[...]).astype(h_ref.dtype)


def norm_mod(x_src, nw, sc, sh):
    d = x_src[0].shape[1]
    return pl.pallas_call(
        _norm_mod_kernel,
        grid=(ALL_ROWS // ROW_TILE,),
        in_specs=_row_source_specs(x_src, d) + [pl.BlockSpec((1, d), lambda i: (0, 0)), _kind_spec(d), _kind_spec(d)],
        out_specs=pl.BlockSpec((ROW_TILE, d), lambda i: (i, 0)),
        out_shape=jax.ShapeDtypeStruct((ALL_ROWS, d), BF16),
        compiler_params=_cparams("parallel"),
        name="norm_mod",
    )(x_src[0], x_src[1], nw, sc, sh)


ROW_TILES = 8
PROJ_ROW_TILES = 16
EVEN_TN = 19 * LANES
EVEN_PAD = 3 * EVEN_TN
QKV_TN = 1024
assert EVEN_PAD >= EVEN_IN


def _weight_spec(w, tn, layer):
    d = w.shape[-2]
    if layer is None:
        return pl.BlockSpec((d, tn), lambda j, i: (0, j))
    return pl.BlockSpec((None, d, tn), lambda j, i: (layer, 0, j))


def _proj_kernel(h_ref, w_ref, o_ref, *scratch, scaled_cols, scale):
    if scratch:
        wb_ref, = scratch

        @pl.when(pl.program_id(1) == 0)
        def _():
            wb_ref[...] = w_ref[...].astype(BF16)
    else:
        wb_ref = w_ref
    y = jnp.dot(h_ref[...], wb_ref[...], preferred_element_type=F32)
    if scaled_cols:
        y = y * jnp.where(pl.program_id(0) * o_ref.shape[1] < scaled_cols, scale, 1.0)
    o_ref[...] = y.astype(o_ref.dtype)


def projection(h, w, tn, out_dtype, layer=None, scaled_cols=0, scale=1.0):
    m, d = h.shape
    n = w.shape[-1]
    tm = m // PROJ_ROW_TILES
    assert n % tn == 0 and scaled_cols % tn == 0
    scratch = [] if w.dtype == BF16 else [pltpu.VMEM((d, tn), BF16)]
    return pl.pallas_call(
        functools.partial(_proj_kernel, scaled_cols=scaled_cols, scale=scale),
        grid=(n // tn, PROJ_ROW_TILES),
        in_specs=[pl.BlockSpec((tm, d), lambda j, i: (i, 0)), _weight_spec(w, tn, layer)],
        out_specs=pl.BlockSpec((tm, tn), lambda j, i: (i, j)),
        out_shape=jax.ShapeDtypeStruct((m, n), out_dtype),
        scratch_shapes=scratch,
        compiler_params=_cparams("parallel", "arbitrary"),
        name="projection",
    )(h, w)


FFN_TF = 512
FFN_TN = 256


def _ffn_up_kernel(h_ref, wg_ref, wu_ref, a_ref, wgb_ref, wub_ref):
    @pl.when(pl.program_id(1) == 0)
    def _():
        wgb_ref[...] = wg_ref[...].astype(BF16)
        wub_ref[...] = wu_ref[...].astype(BF16)

    h = h_ref[...]
    gate = jnp.dot(h, wgb_ref[...], preferred_element_type=F32)
    up = jnp.dot(h, wub_ref[...], preferred_element_type=F32)
    a_ref[...] = (gate * _sigmoid(gate) * up).astype(a_ref.dtype)


def _ffn_down_kernel(a_ref, wd_ref, x_ref, g_ref, o_ref):
    tm = x_ref.shape[0]
    y = jnp.dot(a_ref[...], wd_ref[...].astype(BF16), preferred_element_type=F32)
    row = pl.program_id(0) * tm + lax.broadcasted_iota(jnp.int32, (tm, 1), 0)
    o_ref[...] = x_ref[...] + jnp.where(row < SEQ, g_ref[0], g_ref[1]) * y


def ffn_block(x, h, rows, g, wg, wu, wd, layer):
    m, d = x.shape
    hid = wg.shape[2]
    tm = rows // ROW_TILES
    a = pl.pallas_call(
        _ffn_up_kernel,
        grid=(hid // FFN_TF, ROW_TILES),
        in_specs=[pl.BlockSpec((tm, d), lambda f, i: (i, 0)),
                  _weight_spec(wg, FFN_TF, layer), _weight_spec(wu, FFN_TF, layer)],
        out_specs=pl.BlockSpec((tm, FFN_TF), lambda f, i: (i, f)),
        out_shape=jax.ShapeDtypeStruct((m, hid), BF16),
        scratch_shapes=[pltpu.VMEM((d, FFN_TF), BF16), pltpu.VMEM((d, FFN_TF), BF16)],
        compiler_params=_cparams("parallel", "arbitrary"),
        name="ffn_up",
    )(h, wg, wu)
    return pl.pallas_call(
        _ffn_down_kernel,
        grid=(ROW_TILES, d // FFN_TN),
        in_specs=[pl.BlockSpec((tm, hid), lambda i, j: (i, 0)),
                  pl.BlockSpec((None, hid, FFN_TN), lambda i, j: (layer, 0, j)),
                  pl.BlockSpec((tm, FFN_TN), lambda i, j: (i, j)),
                  pl.BlockSpec((2, 1, FFN_TN), lambda i, j: (0, 0, j))],
        out_specs=pl.BlockSpec((tm, FFN_TN), lambda i, j: (i, j)),
        out_shape=jax.ShapeDtypeStruct((m, d), F32),
        compiler_params=_cparams("parallel", "arbitrary"),
        name="ffn_down",
    )(a, wd, x, g)


def _proj_res_kernel(yal_ref, yac_ref, ybl_ref, ybc_ref, xl_ref, xc_ref, w_ref, g_ref, nw_ref, sc_ref, sh_ref,
                     o_ref, h_ref, wb_ref):
    @pl.when(pl.program_id(0) == 0)
    def _():
        wb_ref[...] = w_ref[...].astype(BF16)

    ya = _pick_rows(yal_ref, yac_ref)
    yb = _pick_rows(ybl_ref, ybc_ref)
    ka = ya.shape[1]
    y = (jnp.dot(ya, wb_ref[:ka, :], preferred_element_type=F32)
         + jnp.dot(yb, wb_ref[ka:, :], preferred_element_type=F32))
    x = _pick_rows(xl_ref, xc_ref) + g_ref[...] * y
    o_ref[...] = x
    xn = x * lax.rsqrt(jnp.mean(x * x, axis=-1, keepdims=True) + NORM_EPS)
    h_ref[...] = (xn * nw_ref[...] * (1.0 + sc_ref[...]) + sh_ref[...]).astype(h_ref.dtype)


def proj_residual(y_lat, y_ctx, w, layer, x_src, rows, g, nw, sc, sh):
    d = x_src[0].shape[1]
    half = d // 2
    second = [1 if pair[1] is pair[0] else 0 for pair in (y_lat, y_ctx)]
    assert second[0] == second[1]
    row = pl.BlockSpec((ROW_TILE, d), lambda i: (i, 0))
    return pl.pallas_call(
        _proj_res_kernel,
        grid=(rows // ROW_TILE,),
        in_specs=_row_source_specs((y_lat[0], y_ctx[0], 0), half) + _row_source_specs((y_lat[1], y_ctx[1], 0), half,
                                                                                      second[0])
        + _row_source_specs(x_src, d) + [
            pl.BlockSpec((None, d, d), lambda i: (layer, 0, 0), pipeline_mode=pl.Buffered(1)),
            _kind_spec(d), pl.BlockSpec((1, d), lambda i: (0, 0)), _kind_spec(d), _kind_spec(d)],
        out_specs=[row, row],
        out_shape=[jax.ShapeDtypeStruct((ALL_ROWS, d), F32), jax.ShapeDtypeStruct((ALL_ROWS, d), BF16)],
        scratch_shapes=[pltpu.VMEM((d, d), BF16)],
        compiler_params=_cparams("arbitrary"),
        name="proj_residual",
    )(y_lat[0], y_ctx[0], y_lat[1], y_ctx[1], x_src[0], x_src[1], w, g, nw, sc, sh)


def _final_norm_kernel(x_ref, w_ref, o_ref):
    x = x_ref[...]
    o_ref[...] = x * lax.rsqrt(jnp.mean(x * x, axis=-1, keepdims=True) + NORM_EPS) * w_ref[...]


def final_norm(x, rows, w):
    d = x.shape[1]
    tm = 512
    return pl.pallas_call(
        _final_norm_kernel,
        grid=(rows // tm,),
        in_specs=[pl.BlockSpec((tm, d), lambda i: (i, 0)), pl.BlockSpec((1, d), lambda i: (0, 0))],
        out_specs=pl.BlockSpec((tm, d), lambda i: (i, 0)),
        out_shape=jax.ShapeDtypeStruct((rows, d), F32),
        compiler_params=_cparams("parallel"),
        name="final_norm",
    )(x, w)


NA_R = 4
NA_KD = (NA_R + WIN_R) // 2
NA_T = 2 * WIN_R
NA_HP = 4


def _na_bias_kernel(rpb_ref, o_ref):
    h = pl.program_id(0)
    c = lax.broadcasted_iota(jnp.int32, (GRID_W, LANES), 0)
    lane = lax.broadcasted_iota(jnp.int32, (GRID_W, LANES), 1)
    left = lane < GRID_W
    kc = jnp.where(left, lane, lane - GRID_W)
    cs = jnp.clip(c - WIN_C // 2, 0, GRID_W - WIN_C)
    inwin = (kc >= cs) & (kc < cs + WIN_C)
    diff = kc - c + (WIN_C - 1)
    n_ro, n_co = 2 * WIN_R - 1, 2 * WIN_C - 1
    for t in range(NA_T):
        def body(d, acc, t=t):
            vl = rpb_ref[(h * n_ro + (t - 1)) * n_co + d] if t >= 1 else jnp.float32(NEG)
            vr = rpb_ref[(h * n_ro + t) * n_co + d] if t < n_ro else jnp.float32(NEG)
            return jnp.where(diff == d, jnp.where(left, vl, vr), acc)

        acc = lax.fori_loop(0, n_co, body, jnp.full((GRID_W, LANES), NEG, F32))
        o_ref[0, t] = jnp.where(inwin, acc, NEG)


def na_bias_tiles(rpb):
    return pl.pallas_call(
        _na_bias_kernel,
        grid=(NA_HEADS,),
        in_specs=[pl.BlockSpec(memory_space=pltpu.SMEM)],
        out_specs=pl.BlockSpec((1, NA_T, GRID_W, LANES), lambda h: (h, 0, 0, 0)),
        out_shape=jax.ShapeDtypeStruct((NA_HEADS, NA_T, GRID_W, LANES), F32),
        compiler_params=_cparams("parallel"),
        name="na_bias_tiles",
    )(rpb.reshape(-1))


def _na_kernel(q_ref, k_ref, v_ref, kc_ref, vc_ref, tt_ref, o_ref):
    r0 = pl.program_id(1) * NA_R
    ks = jnp.clip(r0 - WIN_R // 2, 0, ROWS - 2 * NA_KD)
    span = pl.ds(pl.multiple_of(ks * GRID_W, LANES), NA_KD * LANES)
    heads = range(NA_HP)
    cols = [slice(h * NA_DIM, (h + 1) * NA_DIM) for h in heads]
    contract_last = (((1,), (1,)), ((), ()))
    q = [q_ref[:, cols[h]] for h in heads]
    s_loc = [lax.dot_general(q[h], k_ref[span, cols[h]], contract_last, preferred_element_type=F32) for h in heads]
    s_ctx = [lax.dot_general(q[h], kc_ref[:, cols[h]], contract_last, preferred_element_type=F32) for h in heads]

    left = lax.broadcasted_iota(jnp.int32, (GRID_W, LANES), 1) < GRID_W
    tile_sel = []
    for qi in range(NA_R):
        r = r0 + qi
        rs = jnp.clip(r - WIN_R // 2, 0, ROWS - WIN_R)
        for dj in range(NA_KD):
            kl = ks + 2 * dj
            vl = ((kl >= rs) & (kl < rs + WIN_R)).astype(jnp.int32)
            vr = ((kl + 1 >= rs) & (kl + 1 < rs + WIN_R)).astype(jnp.int32)
            tile_sel.append((jnp.clip(kl - r + WIN_R, 0, NA_T - 1), jnp.where(left, vl, vr) > 0))
    for h in heads:
        rows = [jnp.concatenate([jnp.where(valid, tt_ref[h, t], NEG)
                                 for t, valid in tile_sel[qi * NA_KD:(qi + 1) * NA_KD]], axis=1)
                for qi in range(NA_R)]
        s_loc[h] = s_loc[h] + jnp.concatenate(rows, axis=0)

    m = [jnp.maximum(jnp.max(s_loc[h], axis=-1, keepdims=True), jnp.max(s_ctx[h], axis=-1, keepdims=True))
         for h in heads]
    p_loc = [jnp.exp(s_loc[h] - m[h]) for h in heads]
    p_ctx = [jnp.exp(s_ctx[h] - m[h]) for h in heads]
    denom = [jnp.sum(p_loc[h], axis=-1, keepdims=True) + jnp.sum(p_ctx[h], axis=-1, keepdims=True) for h in heads]
    o = [jnp.dot(p_loc[h].astype(BF16), v_ref[span, cols[h]], preferred_element_type=F32)
         + jnp.dot(p_ctx[h].astype(BF16), vc_ref[:, cols[h]], preferred_element_type=F32) for h in heads]
    for h in heads:
        o_ref[:, cols[h]] = (o[h] / denom[h]).astype(o_ref.dtype)


def na_attention(z, tiles):
    tq = NA_R * GRID_W
    w = NA_HP * NA_DIM
    ng = NA_HEADS // NA_HP
    return pl.pallas_call(
        _na_kernel,
        grid=(ng, ROWS // NA_R),
        in_specs=[pl.BlockSpec((tq, w), lambda g, i: (i, g)),
                  pl.BlockSpec((SEQ, w), lambda g, i: (0, ng + g)),
                  pl.BlockSpec((SEQ, w), lambda g, i: (0, 2 * ng + g)),
                  pl.BlockSpec((CTX_LEN, w), lambda g, i: (SEQ // CTX_LEN, ng + g)),
                  pl.BlockSpec((CTX_LEN, w), lambda g, i: (SEQ // CTX_LEN, 2 * ng + g)),
                  pl.BlockSpec((NA_HP, NA_T, GRID_W, LANES), lambda g, i: (g, 0, 0, 0))],
        out_specs=pl.BlockSpec((tq, w), lambda g, i: (i, g)),
        out_shape=jax.ShapeDtypeStruct((SEQ, D_MODEL), BF16),
        compiler_params=_cparams("parallel", "arbitrary"),
        name="na_attention",
    )(z, z, z, z, z, tiles)


def _ctx_attn_kernel(q_ref, k_ref, v_ref, o_ref):
    s = lax.dot_general(q_ref[...], k_ref[...], (((1,), (1,)), ((), ())), preferred_element_type=F32)
    p = jnp.exp(s - jnp.max(s, axis=-1, keepdims=True))
    o = jnp.dot(p.astype(BF16), v_ref[...], preferred_element_type=F32)
    o_ref[...] = (o / jnp.sum(p, axis=-1, keepdims=True)).astype(o_ref.dtype)


def ctx_attention(z):
    nh = NA_HEADS
    blk = lambda off: pl.BlockSpec((CTX_LEN, NA_DIM), lambda h: (SEQ // CTX_LEN, off + h))
    return pl.pallas_call(
        _ctx_attn_kernel,
        grid=(nh,),
        in_specs=[blk(0), blk(nh), blk(2 * nh)],
        out_specs=pl.BlockSpec((CTX_LEN, NA_DIM), lambda h: (0, h)),
        out_shape=jax.ShapeDtypeStruct((CTX_LEN, D_MODEL), BF16),
        compiler_params=_cparams("parallel"),
        name="ctx_attention",
    )(z, z, z)


DN_SC = 256
HALO = 16
N_GB = 2 * DN_HEADS


def _softplus(x):
    return jnp.maximum(x, 0.0) + jnp.log1p(jnp.exp(-jnp.abs(x)))


def _short_conv(main_ref, prev_ref, next_ref, w_ref, col0, first, last):
    ksize = w_ref.shape[0]
    tb, width = main_ref.shape
    prev = jnp.where(first, 0.0, prev_ref[...].astype(F32))
    nxt = jnp.where(last, 0.0, next_ref[...].astype(F32))
    xf = jnp.concatenate([prev, main_ref[...].astype(F32), nxt], axis=0)
    n = xf.shape[0]
    acc = None
    for j in range(ksize):
        sh = (ksize // 2 - j) % n
        xs = xf if sh == 0 else pltpu.roll(xf, sh, 0)
        term = xs[HALO:HALO + tb] * w_ref[j:j + 1, col0:col0 + width]
        acc = term if acc is None else acc + term
    return acc


def _chunk_masks(n):
    r = lax.broadcasted_iota(jnp.int32, (n, n), 0)
    c = lax.broadcasted_iota(jnp.int32, (n, n), 1)
    return r, c, (r // DN_CHUNK) == (c // DN_CHUNK)


def _dn_scalars_kernel(za_ref, zb_ref, zat_ref, alog_ref, dtb_ref, alogt_ref, dtbt_ref,
                       gc_ref, gt_ref, beta_ref, gct_ref):
    hi = lax.Precision.HIGHEST
    g = -jnp.exp(alog_ref[...]) * _softplus(za_ref[...] + dtb_ref[...])
    beta_ref[...] = _sigmoid(zb_ref[...])
    r, c, same = _chunk_masks(DN_SC)
    lo = (same & (c <= r)).astype(F32)
    up = (same & (c >= r)).astype(F32)
    fwd_col = lax.broadcasted_iota(jnp.int32, (DN_SC, N_GB), 1) < DN_HEADS
    gc_ref[...] = jnp.where(fwd_col, jnp.dot(lo, g, precision=hi), jnp.dot(up, g, precision=hi))
    gt_ref[...] = jnp.dot(same.astype(F32), g, precision=hi)
    g_t = -jnp.exp(alogt_ref[...]) * _softplus(zat_ref[...] + dtbt_ref[...])
    fwd_row = lax.broadcasted_iota(jnp.int32, (N_GB, DN_SC), 0) < DN_HEADS
    gct_ref[...] = jnp.where(fwd_row, jnp.dot(g_t, up, precision=hi), jnp.dot(g_t, lo, precision=hi))


def dn_scalars(za, zb, a_log, dt_bias):
    l = za.shape[0]
    col = pl.BlockSpec((DN_SC, N_GB), lambda i: (i, 0))
    row = pl.BlockSpec((N_GB, DN_SC), lambda i: (0, i))
    prow = pl.BlockSpec((1, N_GB), lambda i: (0, 0))
    pcol = pl.BlockSpec((N_GB, 1), lambda i: (0, 0))
    cshape = jax.ShapeDtypeStruct((l, N_GB), F32)
    return pl.pallas_call(
        _dn_scalars_kernel,
        grid=(l // DN_SC,),
        in_specs=[col, col, row, prow, prow, pcol, pcol],
        out_specs=[col, col, col, row],
        out_shape=[cshape, cshape, cshape, jax.ShapeDtypeStruct((N_GB, l), F32)],
        compiler_params=_cparams("parallel"),
        name="dn_scalars",
    )(za, zb, za.T, a_log.reshape(1, N_GB), dt_bias.reshape(1, N_GB), a_log.reshape(N_GB, 1),
      dt_bias.reshape(N_GB, 1))


def _dn_prep_kernel(zq_ref, zqp_ref, zqn_ref, zk_ref, zkp_ref, zkn_ref, zv_ref, zvp_ref, zvn_ref, cw_ref, o_ref):
    i = pl.program_id(0)
    n_lat = SEQ // DN_SC
    first = (i == 0) | (i == n_lat)
    last = (i == n_lat - 1) | (i == pl.num_programs(0) - 1)
    parts = ((zq_ref, zqp_ref, zqn_ref), (zk_ref, zkp_ref, zkn_ref), (zv_ref, zvp_ref, zvn_ref))
    for part, refs in enumerate(parts):
        acc = _short_conv(*refs, cw_ref, part * DN_WIDTH, first, last)
        x = acc * _sigmoid(acc)
        for h in range(DN_HEADS):
            xh = x[:, h * DN_DIM:(h + 1) * DN_DIM]
            if part < 2:
                xh = xh * (lax.rsqrt(jnp.sum(xh * xh, axis=-1, keepdims=True) + NORM_EPS)
                           * (DN_DIM ** -0.5 if part == 0 else 1.0))
            o_ref[:, part * DN_WIDTH + h * DN_DIM:part * DN_WIDTH + (h + 1) * DN_DIM] = xh


def dn_prep(z, conv_w):
    m = z.shape[0]
    per = DN_SC // HALO
    last_halo = m // HALO - 1

    def part_specs(part):
        return [pl.BlockSpec((DN_SC, DN_WIDTH), lambda i: (i, part)),
                pl.BlockSpec((HALO, DN_WIDTH), lambda i: (jnp.maximum(i * per - 1, 0), part)),
                pl.BlockSpec((HALO, DN_WIDTH), lambda i: (jnp.minimum((i + 1) * per, last_halo), part))]

    return pl.pallas_call(
        _dn_prep_kernel,
        grid=(m // DN_SC,),
        in_specs=part_specs(0) + part_specs(1) + part_specs(2) + [
            pl.BlockSpec((DN_CONV, 3 * DN_WIDTH), lambda i: (0, 0))],
        out_specs=pl.BlockSpec((DN_SC, 3 * DN_WIDTH), lambda i: (i, 0)),
        out_shape=jax.ShapeDtypeStruct((m, 3 * DN_WIDTH), F32),
        compiler_params=_cparams("parallel"),
        name="dn_prep",
    )(z, z, z, z, z, z, z, z, z, conv_w)


def _gdn_kernel(q_ref, k_ref, v_ref, gc_ref, gt_ref, beta_ref, gct_ref, s0_ref, o_ref, sfin_ref, state_ref,
                *, bwd, nblk):
    i = pl.program_id(0)

    @pl.when(i == 0)
    def _():
        state_ref[...] = s0_ref[...]

    r, c, same64 = _chunk_masks(DN_SC)
    tri = same64 & ((c >= r) if bwd else (c <= r))
    offdiag = r != c
    same16 = (r // 16) == (c // 16)
    same32 = (r // 32) == (c // 32)
    contract_last = (((1,), (1,)), ((), ()))
    contract_first = (((0,), (0,)), ((), ()))
    nc = DN_SC // DN_CHUNK
    dot = functools.partial(jnp.dot, preferred_element_type=F32)

    heads = range(DN_HEADS)
    col0 = DN_HEADS if bwd else 0
    gc_c = [gc_ref[:, col0 + h:col0 + h + 1] for h in heads]
    gt_c = [gt_ref[:, col0 + h:col0 + h + 1] for h in heads]
    beta_c = [beta_ref[:, col0 + h:col0 + h + 1] for h in heads]
    q, k, v, k16, dec, qk16, m, p, e = ([None] * DN_HEADS for _ in range(9))
    for h in heads:
        sl = slice(h * DN_DIM, (h + 1) * DN_DIM)
        q[h], k[h], v[h] = q_ref[:, sl], k_ref[:, sl], v_ref[:, sl]
        k16[h] = k[h].astype(BF16)
        dec[h] = jnp.exp(jnp.where(tri, gc_c[h] - gct_ref[col0 + h:col0 + h + 1, :], NEG))
    for h in heads:
        kk = lax.dot_general(k16[h], k16[h], contract_last, preferred_element_type=F32)
        m[h] = jnp.where(offdiag, kk * beta_c[h] * dec[h], 0.0)
        qk16[h] = (lax.dot_general(q[h].astype(BF16), k16[h], contract_last, preferred_element_type=F32)
                   * dec[h]).astype(BF16)

    for h in heads:
        p[h] = jnp.where(same16, -m[h], 0.0)
        e[h] = p[h]
    p16 = [p[h].astype(BF16) for h in heads]
    for _ in range(3):
        for h in heads:
            p[h] = dot(p16[h], p16[h])
            p16[h] = p[h].astype(BF16)
        for h in heads:
            e[h] = e[h] + p[h] + dot(e[h].astype(BF16), p16[h])
    for inner, outer in ((same16, same32), (same32, None)):
        y = [None] * DN_HEADS
        for h in heads:
            cm = jnp.where(~inner if outer is None else (outer & ~inner), m[h], 0.0)
            y[h] = cm + dot(cm.astype(BF16), e[h].astype(BF16))
        for h in heads:
            e[h] = e[h] - y[h] - dot(e[h].astype(BF16), y[h].astype(BF16))

    u, w16, qg16, kd16, s = ([None] * DN_HEADS for _ in range(5))
    for h in heads:
        eg = jnp.exp(gc_c[h])
        rhs = jnp.concatenate([v[h] * beta_c[h], k[h] * (beta_c[h] * eg)], axis=1)
        uw = rhs + dot(e[h].astype(BF16), rhs.astype(BF16))
        u[h] = uw[:, :DN_DIM]
        w16[h] = uw[:, DN_DIM:].astype(BF16)
        qg16[h] = (q[h] * eg).astype(BF16)
        kd16[h] = (k[h] * jnp.exp(gt_c[h] - gc_c[h])).astype(BF16)
        s[h] = state_ref[h]

    outs = [[None] * nc for _ in heads]
    for ci in (range(nc - 1, -1, -1) if bwd else range(nc)):
        rows = slice(ci * DN_CHUNK, (ci + 1) * DN_CHUNK)
        ws = [dot(jnp.concatenate([w16[h][rows], qg16[h][rows]], axis=0), s[h].astype(BF16)) for h in heads]
        vn16 = [(u[h][rows] - ws[h][:DN_CHUNK]).astype(BF16) for h in heads]
        for h in heads:
            outs[h][ci] = ws[h][DN_CHUNK:] + dot(qk16[h][rows, ci * DN_CHUNK:(ci + 1) * DN_CHUNK], vn16[h])
        for h in heads:
            gl = jnp.exp(gt_c[h][ci * DN_CHUNK:ci * DN_CHUNK + 1, :])
            s[h] = s[h] * gl + lax.dot_general(kd16[h][rows], vn16[h], contract_first, preferred_element_type=F32)
    for h in heads:
        state_ref[h] = s[h]
        o_ref[:, h * DN_DIM:(h + 1) * DN_DIM] = jnp.concatenate(outs[h], axis=0)

    @pl.when(i == nblk - 1)
    def _():
        sfin_ref[...] = state_ref[...]


def gdn_scan(qkv, gc, gt, beta, gct, s0, bwd, row0, l):
    nblk = l // DN_SC
    blk0 = row0 // DN_SC

    def rel(i):
        return nblk - 1 - i if bwd else i

    def b(i):
        return blk0 + rel(i)

    def part_spec(part):
        return pl.BlockSpec((DN_SC, DN_WIDTH), lambda i: (b(i), part))

    col = pl.BlockSpec((DN_SC, N_GB), lambda i: (b(i), 0))
    st = pl.BlockSpec((DN_HEADS, DN_DIM, DN_DIM), lambda i: (0, 0, 0))
    return pl.pallas_call(
        functools.partial(_gdn_kernel, bwd=bwd, nblk=nblk),
        grid=(nblk,),
        in_specs=[part_spec(0), part_spec(1), part_spec(2), col, col, col,
                  pl.BlockSpec((N_GB, DN_SC), lambda i: (0, b(i))), st],
        out_specs=[pl.BlockSpec((DN_SC, DN_WIDTH), lambda i: (rel(i), 0)), st],
        out_shape=[jax.ShapeDtypeStruct((l, DN_WIDTH), F32),
                   jax.ShapeDtypeStruct((DN_HEADS, DN_DIM, DN_DIM), F32)],
        scratch_shapes=[pltpu.VMEM((DN_HEADS, DN_DIM, DN_DIM), F32)],
        compiler_params=_cparams("arbitrary"),
        name="gdn_scan_bwd" if bwd else "gdn_scan_fwd",
    )(qkv, qkv, qkv, gc, gt, beta, gct, s0)


def _dn_out_kernel(of_ref, ob_ref, gate_ref, nw_ref, y_ref):
    for h in range(DN_HEADS):
        sl = slice(h * DN_DIM, (h + 1) * DN_DIM)
        o = of_ref[:, sl] + ob_ref[:, sl]
        o = o * lax.rsqrt(jnp.mean(o * o, axis=-1, keepdims=True) + NORM_EPS) * nw_ref[...]
        gate = gate_ref[:, sl].astype(F32)
        y_ref[:, sl] = (o * (gate * _sigmoid(gate))).astype(y_ref.dtype)


def dn_output(o_f, o_b, z, row0, onorm_w):
    l = o_f.shape[0]
    tm = min(l, 512)
    blk0 = row0 // tm
    blk = pl.BlockSpec((tm, DN_WIDTH), lambda i: (i, 0))
    return pl.pallas_call(
        _dn_out_kernel,
        grid=(l // tm,),
        in_specs=[blk, blk, pl.BlockSpec((tm, DN_WIDTH), lambda i: (blk0 + i, 3)),
                  pl.BlockSpec((1, DN_DIM), lambda i: (0, 0))],
        out_specs=blk,
        out_shape=jax.ShapeDtypeStruct((l, DN_WIDTH), BF16),
        compiler_params=_cparams("parallel"),
        name="dn_output",
    )(o_f, o_b, z, onorm_w.reshape(1, DN_DIM))


def gated_deltanet(z, need_ctx, conv_w, a_log, dt_bias, onorm_w):
    a0, b0 = EVEN_IN - 2 * N_GB, EVEN_IN - N_GB
    scal = dn_scalars(z[:, a0:b0].astype(F32), z[:, b0:b0 + N_GB].astype(F32), a_log, dt_bias)
    qkv = dn_prep(z, conv_w)
    s0 = jnp.zeros((DN_HEADS, DN_DIM, DN_DIM), F32)
    outs_c, outs_l = [], []
    for bwd in (False, True):
        oc, s_ctx = gdn_scan(qkv, *scal, s0, bwd, SEQ, CTX_LEN)
        ol, _ = gdn_scan(qkv, *scal, s_ctx, bwd, 0, SEQ)
        outs_c.append(oc)
        outs_l.append(ol)
    y_ctx = dn_output(*outs_c, z, SEQ, onorm_w) if need_ctx else None
    return dn_output(*outs_l, z, 0, onorm_w), y_ctx


HY_SHORT = 3
HY_EMB = 1 + 2 * HY_BANDS
HY_EMB_PAD = 40
HY_TB = 256
FFT_B = 128
HY_CB = 32
HY_COL0 = 4


def _hy_pre_kernel(x0_ref, x0p_ref, x0n_ref, x1_ref, x1p_ref, x1n_ref, v_ref, vp_ref, vn_ref, w_ref, b_ref,
                   u_ref, x0c_ref, *, nblk, transpose_u):
    first = pl.program_id(0) == 0
    last = pl.program_id(0) == nblk - 1
    w = HY_WIDTH
    x0 = _short_conv(x0_ref, x0p_ref, x0n_ref, w_ref, 0, first, last) + b_ref[:, :w]
    x1 = _short_conv(x1_ref, x1p_ref, x1n_ref, w_ref, w, first, last) + b_ref[:, w:2 * w]
    v = _short_conv(v_ref, vp_ref, vn_ref, w_ref, 2 * w, first, last) + b_ref[:, 2 * w:]
    u = v * x1
    x0c_ref[...] = x0
    u_ref[...] = u.T if transpose_u else u


def hyena_pre(z, row0, l, short_w, short_b, transpose_u):
    tb = min(HY_TB, l)
    nblk = l // tb
    per = tb // HALO
    blk0 = row0 // tb
    last_halo = z.shape[0] // HALO - 1
    w = HY_WIDTH

    def part_specs(part):
        cb = HY_COL0 + part
        return [pl.BlockSpec((tb, w), lambda i: (blk0 + i, cb)),
                pl.BlockSpec((HALO, w), lambda i: (jnp.maximum((blk0 + i) * per - 1, 0), cb)),
                pl.BlockSpec((HALO, w), lambda i: (jnp.minimum((blk0 + i + 1) * per, last_halo), cb))]

    u_shape, u_spec = ((w, l), pl.BlockSpec((w, tb), lambda i: (0, i))) if transpose_u else (
        (l, w), pl.BlockSpec((tb, w), lambda i: (i, 0)))
    return pl.pallas_call(
        functools.partial(_hy_pre_kernel, nblk=nblk, transpose_u=transpose_u),
        grid=(nblk,),
        in_specs=part_specs(0) + part_specs(1) + part_specs(2) + [
            pl.BlockSpec((HY_SHORT, HY_IN), lambda i: (0, 0)), pl.BlockSpec((1, HY_IN), lambda i: (0, 0))],
        out_specs=[u_spec, pl.BlockSpec((tb, w), lambda i: (i, 0))],
        out_shape=[jax.ShapeDtypeStruct(u_shape, F32), jax.ShapeDtypeStruct((l, w), F32)],
        compiler_params=_cparams("parallel"),
        name="hyena_pre",
    )(z, z, z, z, z, z, z, z, z, short_w, short_b.reshape(1, HY_IN))


def _hy_filter_kernel(w1t_ref, b1_ref, w2t_ref, b2_ref, w3t_ref, b3_ref, w4t_ref, fr_ref, band_ref, dl_ref,
                      f_ref, hb0_ref, *, l, fb):
    hi = lax.Precision.HIGHEST
    j = pl.program_id(0)
    second = j >= l // fb
    n = j * fb + lax.broadcasted_iota(jnp.int32, (1, fb), 1)
    pos = jnp.where(second, 2 * l - n, n).astype(F32)
    t = pos / max(l - 1, 1)
    wpos = 2.0 * math.pi * pos / l
    arg = band_ref[...] * wpos
    row = lax.broadcasted_iota(jnp.int32, (HY_EMB_PAD, fb), 0)
    feat = jnp.where(row == 0, t, jnp.where(row <= HY_BANDS, jnp.cos(arg),
                                            jnp.where(row <= 2 * HY_BANDS, -jnp.sin(arg), 0.0)))
    fr = fr_ref[...]
    h = jnp.sin(fr * (jnp.dot(w1t_ref[...], feat, precision=hi) + b1_ref[...]))
    h = jnp.sin(fr * (jnp.dot(w2t_ref[...], h, precision=hi) + b2_ref[...]))
    h = jnp.sin(fr * (jnp.dot(w3t_ref[...], h, precision=hi) + b3_ref[...]))
    window = jnp.exp(-t * dl_ref[...])
    half = pl.multiple_of(jnp.where(second, HY_WIDTH, 0), HY_WIDTH)
    h16 = h.astype(BF16)
    f = jnp.dot(w4t_ref[pl.ds(half, HY_WIDTH), :].astype(BF16), h16, preferred_element_type=F32) * window
    f_ref[...] = jnp.where(n == l, 0.0, f)

    @pl.when(j == 0)
    def _():
        hb0_ref[...] = (jnp.dot(w4t_ref[HY_WIDTH:, :].astype(BF16), h16[:, :LANES], preferred_element_type=F32)
                        * window[:, :LANES])


def hyena_filter(l, w1, b1, w2, b2, w3, b3, w4, freq):
    fb = min(1024, l)
    colv = lambda v: v.reshape(-1, 1)
    bands = np.zeros((HY_EMB_PAD, 1), np.float32)
    bands[1:1 + HY_BANDS, 0] = bands[1 + HY_BANDS:HY_EMB, 0] = np.linspace(1e-4, HY_BANDS - 1, HY_BANDS,
                                                                            dtype=np.float32)
    min_decay = math.log(HY_DECAY_TARGET) / HY_WEAK_DECAY_PCT
    max_decay = math.log(HY_DECAY_TARGET) / HY_STRONG_DECAY_PCT
    deltas = np.abs(np.linspace(min_decay, max_decay, HY_WIDTH, dtype=np.float32)).reshape(-1, 1)
    w1t = jnp.pad(w1.T, ((0, 0), (0, HY_EMB_PAD - HY_EMB)))
    full = lambda a: pl.BlockSpec(a.shape, lambda j: (0,) * a.ndim)
    args = (w1t, colv(b1), w2.T, colv(b2), w3.T, colv(b3), w4.T, colv(freq), jnp.asarray(bands), jnp.asarray(deltas))
    filt, hb0 = pl.pallas_call(
        functools.partial(_hy_filter_kernel, l=l, fb=fb),
        grid=(2 * l // fb,),
        in_specs=[full(a) for a in args],
        out_specs=[pl.BlockSpec((HY_WIDTH, fb), lambda j: (0, j)), pl.BlockSpec((HY_WIDTH, LANES), lambda j: (0, 0))],
        out_shape=[jax.ShapeDtypeStruct((HY_WIDTH, 2 * l), F32), jax.ShapeDtypeStruct((HY_WIDTH, LANES), F32)],
        compiler_params=_cparams("arbitrary"),
        name="hyena_filter",
    )(*args)
    return filt, hb0[:, :1]


def _dft_constants():
    b = FFT_B
    n = b * b
    idx = np.arange(b)
    ang = 2.0 * np.pi * np.outer(idx, idx) / b
    c, s = np.cos(ang), np.sin(ang)
    tw = 2.0 * np.pi * np.outer(idx, idx) / n
    fwd_b = np.concatenate([c, -s], axis=0)
    cs = np.concatenate([c, s], axis=1)
    inv_b = np.concatenate([c[:b // 2], -s[:b // 2]], axis=1) / n
    return (jnp.asarray(fwd_b, BF16), jnp.asarray(cs, BF16), jnp.asarray(np.cos(tw), F32),
            jnp.asarray(-np.sin(tw), F32), jnp.asarray(inv_b, BF16))


def _hy_conv_kernel(u_ref, f_ref, fwd_ref, cs_ref, tr_ref, ti_ref, inv_ref, y_ref, ur_s, ui_s, fr_s, fi_s):
    cb = u_ref.shape[0]
    b = FFT_B
    m = cb * b
    dot = functools.partial(jnp.dot, preferred_element_type=F32)
    fwd = fwd_ref[...]
    fwd_half = fwd[:, :b // 2]
    tr, ti = tr_ref[...], ti_ref[...]

    def first_stage(c, carry):
        for src, lhs, re_s, im_s in ((u_ref, fwd_half, ur_s, ui_s), (f_ref, fwd, fr_s, fi_s)):
            p = dot(lhs, src[c].astype(BF16))
            pr, pi = p[:b], p[b:]
            re_s[c] = (pr * tr - pi * ti).astype(BF16)
            im_s[c] = (pr * ti + pi * tr).astype(BF16)
        return carry

    lax.fori_loop(0, cb, first_stage, 0, unroll=True)

    cs = cs_ref[...]

    def times_cs(re, im):
        big = dot(jnp.concatenate([re, im], axis=0), cs)
        return big[:m, :b], big[:m, b:], big[m:, :b], big[m:, b:]

    def second_stage(re_s, im_s):
        rc, rs, ic, is_ = times_cs(re_s[...].reshape(m, b), im_s[...].reshape(m, b))
        return rc + is_, ic - rs

    xr, xi = second_stage(ur_s, ui_s)
    hr, hi = second_stage(fr_s, fi_s)
    zr = (xr * hr - xi * hi).astype(BF16)
    zi = (xr * hi + xi * hr).astype(BF16)
    rc, rs, ic, is_ = times_cs(zr, zi)
    gr = (rc - is_).reshape(cb, b, b)
    gi = (rs + ic).reshape(cb, b, b)
    ur_s[...] = (gr * tr + gi * ti).astype(BF16)
    ui_s[...] = (gi * tr - gr * ti).astype(BF16)
    inv = inv_ref[...]

    def last_stage(c, carry):
        y_ref[c] = dot(inv, jnp.concatenate([ur_s[c], ui_s[c]], axis=0))
        return carry

    lax.fori_loop(0, cb, last_stage, 0, unroll=True)


def hyena_long_conv(u_t, filt_t):
    c, l = u_t.shape
    b = FFT_B
    consts = _dft_constants()
    full = lambda a: pl.BlockSpec(a.shape, lambda i: (0,) * a.ndim)
    y = pl.pallas_call(
        _hy_conv_kernel,
        grid=(c // HY_CB,),
        in_specs=[pl.BlockSpec((HY_CB, b // 2, b), lambda i: (i, 0, 0)),
                  pl.BlockSpec((HY_CB, b, b), lambda i: (i, 0, 0))] + [full(a) for a in consts],
        out_specs=pl.BlockSpec((HY_CB, b // 2, b), lambda i: (i, 0, 0)),
        out_shape=jax.ShapeDtypeStruct((c, b // 2, b), F32),
        scratch_shapes=[pltpu.VMEM((HY_CB, b, b), BF16) for _ in range(4)],
        compiler_params=_cparams("parallel"),
        name="hyena_long_conv",
    )(u_t.reshape(c, b // 2, b), filt_t.reshape(c, b, b), *consts)
    return y.reshape(c, l)


def _hy_ctx_conv_kernel(u_ref, f_ref, fwdu_ref, fwdf_ref, inv_ref, y_ref):
    dot = functools.partial(jnp.dot, preferred_element_type=F32)
    n = f_ref.shape[0]
    x = dot(fwdu_ref[...], u_ref[...].astype(BF16))
    h = dot(fwdf_ref[...], f_ref[...].astype(BF16))
    xr, xi, hr, hi = x[:n], x[n:], h[:n], h[n:]
    z = jnp.concatenate([xr * hr - xi * hi, xr * hi + xi * hr], axis=0).astype(BF16)
    y_ref[...] = dot(inv_ref[...], z)


def hyena_ctx_conv(u, filt):
    l, c = u.shape
    n = 2 * l
    idx = np.arange(n)
    ang = 2.0 * np.pi * np.outer(idx, idx) / n
    cm, sm = np.cos(ang), np.sin(ang)
    fwdf = np.concatenate([cm, -sm], axis=0)
    inv = np.concatenate([cm[:l], -sm[:l]], axis=1) / n
    consts = (jnp.asarray(fwdf[:, :l], BF16), jnp.asarray(fwdf, BF16), jnp.asarray(inv, BF16))
    tc = 256
    full = lambda a: pl.BlockSpec(a.shape, lambda i: (0,) * a.ndim)
    return pl.pallas_call(
        _hy_ctx_conv_kernel,
        grid=(c // tc,),
        in_specs=[pl.BlockSpec((l, tc), lambda i: (0, i)), pl.BlockSpec((n, tc), lambda i: (0, i))]
        + [full(a) for a in consts],
        out_specs=pl.BlockSpec((l, tc), lambda i: (0, i)),
        out_shape=jax.ShapeDtypeStruct((l, c), F32),
        compiler_params=_cparams("parallel"),
        name="hyena_ctx_conv",
    )(u, filt, *consts)


def _hy_post_kernel(y_ref, u_ref, x0_ref, b_ref, o_ref, *, transposed):
    w = y_ref[...] + u_ref[...] * b_ref[...]
    if transposed:
        w = w.T
    o_ref[...] = (w * x0_ref[...]).astype(o_ref.dtype)


def hyena_post(y, u, x0, bias, transposed):
    l, w = x0.shape
    tb = min(HY_TB, l)
    tm = pl.BlockSpec((tb, w), lambda i: (i, 0))
    yu = pl.BlockSpec((w, tb), lambda i: (0, i)) if transposed else tm
    return pl.pallas_call(
        functools.partial(_hy_post_kernel, transposed=transposed),
        grid=(l // tb,),
        in_specs=[yu, yu, tm, pl.BlockSpec(bias.shape, lambda i: (0, 0))],
        out_specs=tm,
        out_shape=jax.ShapeDtypeStruct((l, w), BF16),
        compiler_params=_cparams("parallel"),
        name="hyena_post",
    )(y, u, x0, bias)


def hyena(z, row0, l, short_w, short_b, filt, bias):
    filt_t, hb0 = hyena_filter(l, *filt)
    if l == FFT_B * FFT_B // 2:
        u_t, x0 = hyena_pre(z, row0, l, short_w, short_b, True)
        y_t = hyena_long_conv(u_t, filt_t)
        return hyena_post(y_t, u_t, x0, bias.reshape(-1, 1) + hb0, True)
    u, x0 = hyena_pre(z, row0, l, short_w, short_b, False)
    y = hyena_ctx_conv(u, filt_t.T)
    return hyena_post(y, u, x0, (bias.reshape(-1, 1) + hb0).T, False)


def kernel(x, c, ctx, c_ctx, w_mod, b_mod, norm1_w, norm2_w, ffn_w_gate, ffn_w_up, ffn_w_down, even_w_in, even_w_out, dn_conv_w, dn_a_log, dn_dt_bias, dn_onorm_w, hy_short_w, hy_short_b, hy_f_w1, hy_f_b1, hy_f_w2, hy_f_b2, hy_f_w3, hy_f_b3, hy_f_w4, hy_f_freq, hy_bias, na_w_qkv, na_rpb, na_w_out, final_norm_w):
    d = D_MODEL
    x_src = (x[0], ctx[0], 0)
    mod = modulation(jnp.concatenate([c, c_ctx[None]], axis=0), w_mod, b_mod)

    for layer in range(DEPTH):
        need_ctx = layer < DEPTH - 1
        rows = ALL_ROWS if need_ctx else SEQ
        sh1, sc1, g1, sh2, sc2, g2 = (mod[layer, :, None, j * d:(j + 1) * d] for j in range(N_MOD))
        i = layer // 2
        h = norm_mod(x_src, norm1_w[layer][None], sc1, sh1)
        if layer % 2 == 0:
            w_in = even_w_in[i]
            w_in = jnp.concatenate([w_in[:, :4 * DN_WIDTH], w_in[:, DN_IN:], w_in[:, 4 * DN_WIDTH:DN_IN],
                                    jnp.zeros((d, EVEN_PAD - EVEN_IN), F32)], axis=1).astype(BF16)
            z = projection(h, w_in, EVEN_TN, BF16)
            dn_l, dn_c = gated_deltanet(z, need_ctx, dn_conv_w[i], dn_a_log[i], dn_dt_bias[i], dn_onorm_w[i])
            filt = (hy_f_w1[i], hy_f_b1[i], hy_f_w2[i], hy_f_b2[i], hy_f_w3[i], hy_f_b3[i], hy_f_w4[i],
                    hy_f_freq[i])
            y_lat = (dn_l, hyena(z, 0, SEQ, hy_short_w[i], hy_short_b[i], filt, hy_bias[i]))
            y_ctx = (dn_c, hyena(z, SEQ, CTX_LEN, hy_short_w[i], hy_short_b[i], filt, hy_bias[i])) if need_ctx else y_lat
            w_out = even_w_out
        else:
            z = projection(h, na_w_qkv, QKV_TN, BF16, layer=i, scaled_cols=D_MODEL, scale=NA_DIM ** -0.5)
            ya = na_attention(z, na_bias_tiles(na_rpb[i]))
            y_lat = (ya, ya)
            if need_ctx:
                yc = ctx_attention(z)
                y_ctx = (yc, yc)
            else:
                y_ctx = y_lat
            w_out = na_w_out
        xa, h = proj_residual(y_lat, y_ctx, w_out, i, x_src, rows, g1, norm2_w[layer][None], sc2, sh2)
        xa = ffn_block(xa, h, rows, g2, ffn_w_gate, ffn_w_up, ffn_w_down, layer)
        x_src = (xa, xa, SEQ)
    return final_norm(xa, SEQ, final_norm_w[None])[None]
```

```python
import functools
import math

import jax
import jax.numpy as jnp
import numpy as np
from jax import lax
from jax.experimental import pallas as pl
from jax.experimental.pallas import tpu as pltpu

D_MODEL = 2048
SEQ = 8192
DEPTH = 4
GRID_W = 64
ROWS = SEQ // GRID_W
CTX_LEN = 256
ALL_ROWS = SEQ + CTX_LEN
NORM_EPS = 1e-6
N_MOD = 6

DN_HEADS = 8
DN_DIM = 128
DN_WIDTH = DN_HEADS * DN_DIM
DN_CONV = 5
DN_CHUNK = 64
DN_IN = 4 * DN_WIDTH + 4 * DN_HEADS

HY_WIDTH = D_MODEL - DN_WIDTH
HY_BANDS = 16
HY_DECAY_TARGET = 1e-2
HY_STRONG_DECAY_PCT = 0.3
HY_WEAK_DECAY_PCT = 1.5
HY_IN = 3 * HY_WIDTH
EVEN_IN = DN_IN + HY_IN

NA_HEADS = D_MODEL // 128
NA_DIM = 128
WIN_R = 8
WIN_C = 16
FFN_HIDDEN = -(-8 * D_MODEL // (3 * 256)) * 256

LANES = 128
VMEM_LIMIT = 56 * 1024 * 1024
NEG = -1e30

F32 = jnp.float32
BF16 = jnp.bfloat16


def _cparams(*sem):
    return pltpu.CompilerParams(dimension_semantics=sem, vmem_limit_bytes=VMEM_LIMIT)


def _sigmoid(x):
    return 1.0 / (1.0 + jnp.exp(-x))


MOD_TK = 128


def _mod_kernel(c_ref, w_ref, b_ref, o_ref):
    k = pl.program_id(1)
    n = o_ref.shape[2]

    @pl.when(k == 0)
    def _():
        o_ref[0] = jnp.broadcast_to(b_ref[0], (2, n))

    rows = []
    for r in range(2):
        c = c_ref[r, pl.ds(pl.multiple_of(k * MOD_TK, MOD_TK), MOD_TK), :]
        a = c * _sigmoid(c)
        cols = [jnp.sum(w_ref[0, :, j * LANES:(j + 1) * LANES] * a, axis=0, keepdims=True)
                for j in range(n // LANES)]
        rows.append(jnp.concatenate(cols, axis=1))
    o_ref[0] += jnp.concatenate(rows, axis=0)


def modulation(c_pair, w_mod, b_mod):
    n = N_MOD * D_MODEL
    cb = jnp.broadcast_to(c_pair[:, :, None], (2, D_MODEL, LANES))
    return pl.pallas_call(
        _mod_kernel,
        grid=(DEPTH, D_MODEL // MOD_TK),
        in_specs=[pl.BlockSpec((2, D_MODEL, LANES), lambda l, k: (0, 0, 0)),
                  pl.BlockSpec((1, MOD_TK, n), lambda l, k: (l, k, 0)),
                  pl.BlockSpec((1, 1, n), lambda l, k: (l, 0, 0))],
        out_specs=pl.BlockSpec((1, 2, n), lambda l, k: (l, 0, 0)),
        out_shape=jax.ShapeDtypeStruct((DEPTH, 2, n), F32),
        compiler_params=_cparams("parallel", "arbitrary"),
        name="modulation",
    )(cb, w_mod, b_mod.reshape(DEPTH, 1, n))


ROW_TILE = 256


def _kind_spec(d):
    return pl.BlockSpec((None, 1, d), lambda i: (jnp.where(i >= SEQ // ROW_TILE, 1, 0), 0, 0))


def _row_source_specs(src, width, col=0):
    n_lat = SEQ // ROW_TILE
    ctx_blk = src[2] // ROW_TILE
    return [pl.BlockSpec((ROW_TILE, width), lambda i: (jnp.minimum(i, n_lat - 1), col)),
            pl.BlockSpec((ROW_TILE, width), lambda i: (ctx_blk, col))]


def _pick_rows(lat_ref, ctx_ref):
    return jnp.where(pl.program_id(0) < SEQ // ROW_TILE, lat_ref[...], ctx_ref[...])


def _norm_mod_kernel(xl_ref, xc_ref, nw_ref, sc_ref, sh_ref, h_ref):
    x = _pick_rows(xl_ref, xc_ref)
    y = x * lax.rsqrt(jnp.mean(x * x, axis=-1, keepdims=True) + NORM_EPS)
    h_ref[...] = (y * nw_ref[...] * (1.0 + sc_ref[...]) + sh_ref[...]).astype(h_ref.dtype)


def norm_mod(x_src, nw, sc, sh):
    d = x_src[0].shape[1]
    return pl.pallas_call(
        _norm_mod_kernel,
        grid=(ALL_ROWS // ROW_TILE,),
        in_specs=_row_source_specs(x_src, d) + [pl.BlockSpec((1, d), lambda i: (0, 0)), _kind_spec(d), _kind_spec(d)],
        out_specs=pl.BlockSpec((ROW_TILE, d), lambda i: (i, 0)),
        out_shape=jax.ShapeDtypeStruct((ALL_ROWS, d), BF16),
        compiler_params=_cparams("parallel"),
        name="norm_mod",
    )(x_src[0], x_src[1], nw, sc, sh)


ROW_TILES = 8
FFN_UP_ROW_TILES = 4
PROJ_ROW_TILES = 16
EVEN_TN = 19 * LANES
EVEN_PAD = 3 * EVEN_TN
QKV_TN = 1024
assert EVEN_PAD >= EVEN_IN


def _weight_spec(w, tn, layer):
    d = w.shape[-2]
    if layer is None:
        return pl.BlockSpec((d, tn), lambda j, i: (0, j))
    return pl.BlockSpec((None, d, tn), lambda j, i: (layer, 0, j))


def _proj_kernel(h_ref, w_ref, o_ref, *scratch, scaled_cols, scale):
    if scratch:
        wb_ref, = scratch

        @pl.when(pl.program_id(1) == 0)
        def _():
            wb_ref[...] = w_ref[...].astype(BF16)
    else:
        wb_ref = w_ref
    y = jnp.dot(h_ref[...], wb_ref[...], preferred_element_type=F32)
    if scaled_cols:
        y = y * jnp.where(pl.program_id(0) * o_ref.shape[1] < scaled_cols, scale, 1.0)
    o_ref[...] = y.astype(o_ref.dtype)


def projection(h, w, tn, out_dtype, layer=None, scaled_cols=0, scale=1.0):
    m, d = h.shape
    n = w.shape[-1]
    tm = m // PROJ_ROW_TILES
    assert n % tn == 0 and scaled_cols % tn == 0
    scratch = [] if w.dtype == BF16 else [pltpu.VMEM((d, tn), BF16)]
    return pl.pallas_call(
        functools.partial(_proj_kernel, scaled_cols=scaled_cols, scale=scale),
        grid=(n // tn, PROJ_ROW_TILES),
        in_specs=[pl.BlockSpec((tm, d), lambda j, i: (i, 0)), _weight_spec(w, tn, layer)],
        out_specs=pl.BlockSpec((tm, tn), lambda j, i: (i, j)),
        out_shape=jax.ShapeDtypeStruct((m, n), out_dtype),
        scratch_shapes=scratch,
        compiler_params=_cparams("parallel", "arbitrary"),
        name="projection",
    )(h, w)


FFN_TF = 512
FFN_TN = 256


def _ffn_up_kernel(h_ref, wg_ref, wu_ref, a_ref, wgb_ref, wub_ref):
    @pl.when(pl.program_id(1) == 0)
    def _():
        wgb_ref[...] = wg_ref[...].astype(BF16)
        wub_ref[...] = wu_ref[...].astype(BF16)

    h = h_ref[...]
    gate = jnp.dot(h, wgb_ref[...], preferred_element_type=F32)
    up = jnp.dot(h, wub_ref[...], preferred_element_type=F32)
    a_ref[...] = (gate * _sigmoid(gate) * up).astype(a_ref.dtype)


def _ffn_down_kernel(a_ref, wd_ref, x_ref, g_ref, o_ref):
    tm = x_ref.shape[0]
    y = jnp.dot(a_ref[...], wd_ref[...].astype(BF16), preferred_element_type=F32)
    row = pl.program_id(0) * tm + lax.broadcasted_iota(jnp.int32, (tm, 1), 0)
    o_ref[...] = x_ref[...] + jnp.where(row < SEQ, g_ref[0], g_ref[1]) * y


def ffn_block(x, h, rows, g, wg, wu, wd, layer):
    m, d = x.shape
    hid = wg.shape[2]
    tm = rows // ROW_TILES
    tu = rows // FFN_UP_ROW_TILES
    a = pl.pallas_call(
        _ffn_up_kernel,
        grid=(hid // FFN_TF, FFN_UP_ROW_TILES),
        in_specs=[pl.BlockSpec((tu, d), lambda f, i: (i, 0)),
                  _weight_spec(wg, FFN_TF, layer), _weight_spec(wu, FFN_TF, layer)],
        out_specs=pl.BlockSpec((tu, FFN_TF), lambda f, i: (i, f)),
        out_shape=jax.ShapeDtypeStruct((m, hid), BF16),
        scratch_shapes=[pltpu.VMEM((d, FFN_TF), BF16), pltpu.VMEM((d, FFN_TF), BF16)],
        compiler_params=_cparams("parallel", "arbitrary"),
        name="ffn_up",
    )(h, wg, wu)
    return pl.pallas_call(
        _ffn_down_kernel,
        grid=(ROW_TILES, d // FFN_TN),
        in_specs=[pl.BlockSpec((tm, hid), lambda i, j: (i, 0)),
                  pl.BlockSpec((None, hid, FFN_TN), lambda i, j: (layer, 0, j)),
                  pl.BlockSpec((tm, FFN_TN), lambda i, j: (i, j)),
                  pl.BlockSpec((2, 1, FFN_TN), lambda i, j: (0, 0, j))],
        out_specs=pl.BlockSpec((tm, FFN_TN), lambda i, j: (i, j)),
        out_shape=jax.ShapeDtypeStruct((m, d), F32),
        compiler_params=_cparams("parallel", "arbitrary"),
        name="ffn_down",
    )(a, wd, x, g)


def _proj_res_kernel(yal_ref, yac_ref, ybl_ref, ybc_ref, xl_ref, xc_ref, w_ref, g_ref, nw_ref, sc_ref, sh_ref,
                     o_ref, h_ref, wb_ref):
    @pl.when(pl.program_id(0) == 0)
    def _():
        wb_ref[...] = w_ref[...].astype(BF16)

    ya = _pick_rows(yal_ref, yac_ref)
    yb = _pick_rows(ybl_ref, ybc_ref)
    ka = ya.shape[1]
    y = (jnp.dot(ya, wb_ref[:ka, :], preferred_element_type=F32)
         + jnp.dot(yb, wb_ref[ka:, :], preferred_element_type=F32))
    x = _pick_rows(xl_ref, xc_ref) + g_ref[...] * y
    o_ref[...] = x
    xn = x * lax.rsqrt(jnp.mean(x * x, axis=-1, keepdims=True) + NORM_EPS)
    h_ref[...] = (xn * nw_ref[...] * (1.0 + sc_ref[...]) + sh_ref[...]).astype(h_ref.dtype)


def proj_residual(y_lat, y_ctx, w, layer, x_src, rows, g, nw, sc, sh):
    d = x_src[0].shape[1]
    half = d // 2
    second = [1 if pair[1] is pair[0] else 0 for pair in (y_lat, y_ctx)]
    assert second[0] == second[1]
    row = pl.BlockSpec((ROW_TILE, d), lambda i: (i, 0))
    return pl.pallas_call(
        _proj_res_kernel,
        grid=(rows // ROW_TILE,),
        in_specs=_row_source_specs((y_lat[0], y_ctx[0], 0), half) + _row_source_specs((y_lat[1], y_ctx[1], 0), half,
                                                                                      second[0])
        + _row_source_specs(x_src, d) + [
            pl.BlockSpec((None, d, d), lambda i: (layer, 0, 0), pipeline_mode=pl.Buffered(1)),
            _kind_spec(d), pl.BlockSpec((1, d), lambda i: (0, 0)), _kind_spec(d), _kind_spec(d)],
        out_specs=[row, row],
        out_shape=[jax.ShapeDtypeStruct((ALL_ROWS, d), F32), jax.ShapeDtypeStruct((ALL_ROWS, d), BF16)],
        scratch_shapes=[pltpu.VMEM((d, d), BF16)],
        compiler_params=_cparams("arbitrary"),
        name="proj_residual",
    )(y_lat[0], y_ctx[0], y_lat[1], y_ctx[1], x_src[0], x_src[1], w, g, nw, sc, sh)


def _final_norm_kernel(x_ref, w_ref, o_ref):
    x = x_ref[...]
    o_ref[...] = x * lax.rsqrt(jnp.mean(x * x, axis=-1, keepdims=True) + NORM_EPS) * w_ref[...]


def final_norm(x, rows, w):
    d = x.shape[1]
    tm = 512
    return pl.pallas_call(
        _final_norm_kernel,
        grid=(rows // tm,),
        in_specs=[pl.BlockSpec((tm, d), lambda i: (i, 0)), pl.BlockSpec((1, d), lambda i: (0, 0))],
        out_specs=pl.BlockSpec((tm, d), lambda i: (i, 0)),
        out_shape=jax.ShapeDtypeStruct((rows, d), F32),
        compiler_params=_cparams("parallel"),
        name="final_norm",
    )(x, w)


NA_R = 4
NA_KD = (NA_R + WIN_R) // 2
NA_T = 2 * WIN_R
NA_HP = 4


def _na_bias_kernel(rpb_ref, o_ref):
    c = lax.broadcasted_iota(jnp.int32, (GRID_W, LANES), 0)
    lane = lax.broadcasted_iota(jnp.int32, (GRID_W, LANES), 1)
    left = lane < GRID_W
    kc = jnp.where(left, lane, lane - GRID_W)
    cs = jnp.clip(c - WIN_C // 2, 0, GRID_W - WIN_C)
    inwin = (kc >= cs) & (kc < cs + WIN_C)
    n_ro = 2 * WIN_R - 1

    def toeplitz(ro):
        if ro < 0 or ro >= n_ro:
            return jnp.full((GRID_W, LANES), NEG, F32)
        row = jnp.broadcast_to(rpb_ref[0, ro:ro + 1, :], (GRID_W, LANES))
        return pltpu.roll(pltpu.roll(row, LANES - (WIN_C - 1), 1), 0, 1, stride=1, stride_axis=0)

    for t in range(NA_T):
        tile = jnp.where(left, toeplitz(t - 1), pltpu.roll(toeplitz(t), GRID_W, 1))
        o_ref[0, t] = jnp.where(inwin, tile, NEG)


def na_bias_tiles(rpb):
    padded = jnp.pad(rpb, ((0, 0), (0, 1), (0, LANES - rpb.shape[2])))
    return pl.pallas_call(
        _na_bias_kernel,
        grid=(NA_HEADS,),
        in_specs=[pl.BlockSpec((1, NA_T, LANES), lambda h: (h, 0, 0))],
        out_specs=pl.BlockSpec((1, NA_T, GRID_W, LANES), lambda h: (h, 0, 0, 0)),
        out_shape=jax.ShapeDtypeStruct((NA_HEADS, NA_T, GRID_W, LANES), F32),
        compiler_params=_cparams("parallel"),
        name="na_bias_tiles",
    )(padded)


def _na_kernel(q_ref, k_ref, v_ref, kc_ref, vc_ref, tt_ref, o_ref):
    r0 = pl.program_id(1) * NA_R
    ks = jnp.clip(r0 - WIN_R // 2, 0, ROWS - 2 * NA_KD)
    span = pl.ds(pl.multiple_of(ks * GRID_W, LANES), NA_KD * LANES)
    heads = range(NA_HP)
    cols = [slice(h * NA_DIM, (h + 1) * NA_DIM) for h in heads]
    contract_last = (((1,), (1,)), ((), ()))
    q = [q_ref[:, cols[h]] for h in heads]
    s_loc = [lax.dot_general(q[h], k_ref[span, cols[h]], contract_last, preferred_element_type=F32) for h in heads]
    s_ctx = [lax.dot_general(q[h], kc_ref[:, cols[h]], contract_last, preferred_element_type=F32) for h in heads]

    left = lax.broadcasted_iota(jnp.int32, (GRID_W, LANES), 1) < GRID_W
    tile_sel = []
    for qi in range(NA_R):
        r = r0 + qi
        rs = jnp.clip(r - WIN_R // 2, 0, ROWS - WIN_R)
        for dj in range(NA_KD):
            kl = ks + 2 * dj
            vl = ((kl >= rs) & (kl < rs + WIN_R)).astype(jnp.int32)
            vr = ((kl + 1 >= rs) & (kl + 1 < rs + WIN_R)).astype(jnp.int32)
            tile_sel.append((jnp.clip(kl - r + WIN_R, 0, NA_T - 1), jnp.where(left, vl, vr) > 0))
    for h in heads:
        rows = [jnp.concatenate([jnp.where(valid, tt_ref[h, t], NEG)
                                 for t, valid in tile_sel[qi * NA_KD:(qi + 1) * NA_KD]], axis=1)
                for qi in range(NA_R)]
        s_loc[h] = s_loc[h] + jnp.concatenate(rows, axis=0)

    m = [jnp.maximum(jnp.max(s_loc[h], axis=-1, keepdims=True), jnp.max(s_ctx[h], axis=-1, keepdims=True))
         for h in heads]
    p_loc = [jnp.exp(s_loc[h] - m[h]) for h in heads]
    p_ctx = [jnp.exp(s_ctx[h] - m[h]) for h in heads]
    denom = [jnp.sum(p_loc[h], axis=-1, keepdims=True) + jnp.sum(p_ctx[h], axis=-1, keepdims=True) for h in heads]
    o = [jnp.dot(p_loc[h].astype(BF16), v_ref[span, cols[h]], preferred_element_type=F32)
         + jnp.dot(p_ctx[h].astype(BF16), vc_ref[:, cols[h]], preferred_element_type=F32) for h in heads]
    for h in heads:
        o_ref[:, cols[h]] = (o[h] / denom[h]).astype(o_ref.dtype)


def na_attention(z, tiles):
    tq = NA_R * GRID_W
    w = NA_HP * NA_DIM
    ng = NA_HEADS // NA_HP
    return pl.pallas_call(
        _na_kernel,
        grid=(ng, ROWS // NA_R),
        in_specs=[pl.BlockSpec((tq, w), lambda g, i: (i, g)),
                  pl.BlockSpec((SEQ, w), lambda g, i: (0, ng + g)),
                  pl.BlockSpec((SEQ, w), lambda g, i: (0, 2 * ng + g)),
                  pl.BlockSpec((CTX_LEN, w), lambda g, i: (SEQ // CTX_LEN, ng + g)),
                  pl.BlockSpec((CTX_LEN, w), lambda g, i: (SEQ // CTX_LEN, 2 * ng + g)),
                  pl.BlockSpec((NA_HP, NA_T, GRID_W, LANES), lambda g, i: (g, 0, 0, 0))],
        out_specs=pl.BlockSpec((tq, w), lambda g, i: (i, g)),
        out_shape=jax.ShapeDtypeStruct((SEQ, D_MODEL), BF16),
        compiler_params=_cparams("parallel", "arbitrary"),
        name="na_attention",
    )(z, z, z, z, z, tiles)


def _ctx_attn_kernel(q_ref, k_ref, v_ref, o_ref):
    s = lax.dot_general(q_ref[...], k_ref[...], (((1,), (1,)), ((), ())), preferred_element_type=F32)
    p = jnp.exp(s - jnp.max(s, axis=-1, keepdims=True))
    o = jnp.dot(p.astype(BF16), v_ref[...], preferred_element_type=F32)
    o_ref[...] = (o / jnp.sum(p, axis=-1, keepdims=True)).astype(o_ref.dtype)


def ctx_attention(z):
    nh = NA_HEADS
    blk = lambda off: pl.BlockSpec((CTX_LEN, NA_DIM), lambda h: (SEQ // CTX_LEN, off + h))
    return pl.pallas_call(
        _ctx_attn_kernel,
        grid=(nh,),
        in_specs=[blk(0), blk(nh), blk(2 * nh)],
        out_specs=pl.BlockSpec((CTX_LEN, NA_DIM), lambda h: (0, h)),
        out_shape=jax.ShapeDtypeStruct((CTX_LEN, D_MODEL), BF16),
        compiler_params=_cparams("parallel"),
        name="ctx_attention",
    )(z, z, z)


DN_SC = 256
HALO = 16
N_GB = 2 * DN_HEADS


def _softplus(x):
    return jnp.maximum(x, 0.0) + jnp.log1p(jnp.exp(-jnp.abs(x)))


def _short_conv(main_ref, prev_ref, next_ref, w_ref, col0, first, last):
    ksize = w_ref.shape[0]
    tb, width = main_ref.shape
    prev = jnp.where(first, 0.0, prev_ref[...].astype(F32))
    nxt = jnp.where(last, 0.0, next_ref[...].astype(F32))
    xf = jnp.concatenate([prev, main_ref[...].astype(F32), nxt], axis=0)
    n = xf.shape[0]
    acc = None
    for j in range(ksize):
        sh = (ksize // 2 - j) % n
        xs = xf if sh == 0 else pltpu.roll(xf, sh, 0)
        term = xs[HALO:HALO + tb] * w_ref[j:j + 1, col0:col0 + width]
        acc = term if acc is None else acc + term
    return acc


def _chunk_masks(n):
    r = lax.broadcasted_iota(jnp.int32, (n, n), 0)
    c = lax.broadcasted_iota(jnp.int32, (n, n), 1)
    return r, c, (r // DN_CHUNK) == (c // DN_CHUNK)


def _dn_scalars_kernel(za_ref, zb_ref, zat_ref, alog_ref, dtb_ref, alogt_ref, dtbt_ref,
                       gc_ref, gt_ref, beta_ref, gct_ref):
    hi = lax.Precision.HIGHEST
    g = -jnp.exp(alog_ref[...]) * _softplus(za_ref[...] + dtb_ref[...])
    beta_ref[...] = _sigmoid(zb_ref[...])
    r, c, same = _chunk_masks(DN_SC)
    lo = (same & (c <= r)).astype(F32)
    up = (same & (c >= r)).astype(F32)
    fwd_col = lax.broadcasted_iota(jnp.int32, (DN_SC, N_GB), 1) < DN_HEADS
    gc_ref[...] = jnp.where(fwd_col, jnp.dot(lo, g, precision=hi), jnp.dot(up, g, precision=hi))
    gt_ref[...] = jnp.dot(same.astype(F32), g, precision=hi)
    g_t = -jnp.exp(alogt_ref[...]) * _softplus(zat_ref[...] + dtbt_ref[...])
    fwd_row = lax.broadcasted_iota(jnp.int32, (N_GB, DN_SC), 0) < DN_HEADS
    gct_ref[...] = jnp.where(fwd_row, jnp.dot(g_t, up, precision=hi), jnp.dot(g_t, lo, precision=hi))


def dn_scalars(za, zb, a_log, dt_bias):
    l = za.shape[0]
    col = pl.BlockSpec((DN_SC, N_GB), lambda i: (i, 0))
    row = pl.BlockSpec((N_GB, DN_SC), lambda i: (0, i))
    prow = pl.BlockSpec((1, N_GB), lambda i: (0, 0))
    pcol = pl.BlockSpec((N_GB, 1), lambda i: (0, 0))
    cshape = jax.ShapeDtypeStruct((l, N_GB), F32)
    return pl.pallas_call(
        _dn_scalars_kernel,
        grid=(l // DN_SC,),
        in_specs=[col, col, row, prow, prow, pcol, pcol],
        out_specs=[col, col, col, row],
        out_shape=[cshape, cshape, cshape, jax.ShapeDtypeStruct((N_GB, l), F32)],
        compiler_params=_cparams("parallel"),
        name="dn_scalars",
    )(za, zb, za.T, a_log.reshape(1, N_GB), dt_bias.reshape(1, N_GB), a_log.reshape(N_GB, 1),
      dt_bias.reshape(N_GB, 1))


def _dn_prep_kernel(zq_ref, zqp_ref, zqn_ref, zk_ref, zkp_ref, zkn_ref, zv_ref, zvp_ref, zvn_ref, cw_ref, o_ref):
    i = pl.program_id(0)
    n_lat = SEQ // DN_SC
    first = (i == 0) | (i == n_lat)
    last = (i == n_lat - 1) | (i == pl.num_programs(0) - 1)
    parts = ((zq_ref, zqp_ref, zqn_ref), (zk_ref, zkp_ref, zkn_ref), (zv_ref, zvp_ref, zvn_ref))
    for part, refs in enumerate(parts):
        acc = _short_conv(*refs, cw_ref, part * DN_WIDTH, first, last)
        x = acc * _sigmoid(acc)
        for h in range(DN_HEADS):
            xh = x[:, h * DN_DIM:(h + 1) * DN_DIM]
            if part < 2:
                xh = xh * (lax.rsqrt(jnp.sum(xh * xh, axis=-1, keepdims=True) + NORM_EPS)
                           * (DN_DIM ** -0.5 if part == 0 else 1.0))
            o_ref[:, part * DN_WIDTH + h * DN_DIM:part * DN_WIDTH + (h + 1) * DN_DIM] = xh


def dn_prep(z, conv_w):
    m = z.shape[0]
    per = DN_SC // HALO
    last_halo = m // HALO - 1

    def part_specs(part):
        return [pl.BlockSpec((DN_SC, DN_WIDTH), lambda i: (i, part)),
                pl.BlockSpec((HALO, DN_WIDTH), lambda i: (jnp.maximum(i * per - 1, 0), part)),
                pl.BlockSpec((HALO, DN_WIDTH), lambda i: (jnp.minimum((i + 1) * per, last_halo), part))]

    return pl.pallas_call(
        _dn_prep_kernel,
        grid=(m // DN_SC,),
        in_specs=part_specs(0) + part_specs(1) + part_specs(2) + [
            pl.BlockSpec((DN_CONV, 3 * DN_WIDTH), lambda i: (0, 0))],
        out_specs=pl.BlockSpec((DN_SC, 3 * DN_WIDTH), lambda i: (i, 0)),
        out_shape=jax.ShapeDtypeStruct((m, 3 * DN_WIDTH), F32),
        compiler_params=_cparams("parallel"),
        name="dn_prep",
    )(z, z, z, z, z, z, z, z, z, conv_w)


def _gdn_kernel(q_ref, k_ref, v_ref, gc_ref, gt_ref, beta_ref, gct_ref, s0_ref, o_ref, sfin_ref, state_ref,
                *, bwd, nblk):
    i = pl.program_id(0)

    @pl.when(i == 0)
    def _():
        state_ref[...] = s0_ref[...]

    r, c, same64 = _chunk_masks(DN_SC)
    tri = same64 & ((c >= r) if bwd else (c <= r))
    offdiag = r != c
    same16 = (r // 16) == (c // 16)
    same32 = (r // 32) == (c // 32)
    contract_last = (((1,), (1,)), ((), ()))
    contract_first = (((0,), (0,)), ((), ()))
    nc = DN_SC // DN_CHUNK
    dot = functools.partial(jnp.dot, preferred_element_type=F32)

    heads = range(DN_HEADS)
    col0 = DN_HEADS if bwd else 0
    gc_c = [gc_ref[:, col0 + h:col0 + h + 1] for h in heads]
    gt_c = [gt_ref[:, col0 + h:col0 + h + 1] for h in heads]
    beta_c = [beta_ref[:, col0 + h:col0 + h + 1] for h in heads]
    q, k, v, k16, dec, qk16, m, p, e = ([None] * DN_HEADS for _ in range(9))
    for h in heads:
        sl = slice(h * DN_DIM, (h + 1) * DN_DIM)
        q[h], k[h], v[h] = q_ref[:, sl], k_ref[:, sl], v_ref[:, sl]
        k16[h] = k[h].astype(BF16)
        dec[h] = jnp.exp(jnp.where(tri, gc_c[h] - gct_ref[col0 + h:col0 + h + 1, :], NEG))
    for h in heads:
        kk = lax.dot_general(k16[h], k16[h], contract_last, preferred_element_type=F32)
        m[h] = jnp.where(offdiag, kk * beta_c[h] * dec[h], 0.0)
        qk16[h] = (lax.dot_general(q[h].astype(BF16), k16[h], contract_last, preferred_element_type=F32)
                   * dec[h]).astype(BF16)

    for h in heads:
        p[h] = jnp.where(same16, -m[h], 0.0)
        e[h] = p[h]
    p16 = [p[h].astype(BF16) for h in heads]
    for _ in range(3):
        for h in heads:
            p[h] = dot(p16[h], p16[h])
            p16[h] = p[h].astype(BF16)
        for h in heads:
            e[h] = e[h] + p[h] + dot(e[h].astype(BF16), p16[h])
    for inner, outer in ((same16, same32), (same32, None)):
        y = [None] * DN_HEADS
        for h in heads:
            cm = jnp.where(~inner if outer is None else (outer & ~inner), m[h], 0.0)
            y[h] = cm + dot(cm.astype(BF16), e[h].astype(BF16))
        for h in heads:
            e[h] = e[h] - y[h] - dot(e[h].astype(BF16), y[h].astype(BF16))

    u, w16, qg16, kd16, s = ([None] * DN_HEADS for _ in range(5))
    for h in heads:
        eg = jnp.exp(gc_c[h])
        rhs = jnp.concatenate([v[h] * beta_c[h], k[h] * (beta_c[h] * eg)], axis=1)
        uw = rhs + dot(e[h].astype(BF16), rhs.astype(BF16))
        u[h] = uw[:, :DN_DIM]
        w16[h] = uw[:, DN_DIM:].astype(BF16)
        qg16[h] = (q[h] * eg).astype(BF16)
        kd16[h] = (k[h] * jnp.exp(gt_c[h] - gc_c[h])).astype(BF16)
        s[h] = state_ref[h]

    outs = [[None] * nc for _ in heads]
    for ci in (range(nc - 1, -1, -1) if bwd else range(nc)):
        rows = slice(ci * DN_CHUNK, (ci + 1) * DN_CHUNK)
        ws = [dot(jnp.concatenate([w16[h][rows], qg16[h][rows]], axis=0), s[h].astype(BF16)) for h in heads]
        vn16 = [(u[h][rows] - ws[h][:DN_CHUNK]).astype(BF16) for h in heads]
        for h in heads:
            outs[h][ci] = ws[h][DN_CHUNK:] + dot(qk16[h][rows, ci * DN_CHUNK:(ci + 1) * DN_CHUNK], vn16[h])
        for h in heads:
            gl = jnp.exp(gt_c[h][ci * DN_CHUNK:ci * DN_CHUNK + 1, :])
            s[h] = s[h] * gl + lax.dot_general(kd16[h][rows], vn16[h], contract_first, preferred_element_type=F32)
    for h in heads:
        state_ref[h] = s[h]
        o_ref[:, h * DN_DIM:(h + 1) * DN_DIM] = jnp.concatenate(outs[h], axis=0)

    @pl.when(i == nblk - 1)
    def _():
        sfin_ref[...] = state_ref[...]


def gdn_scan(qkv, gc, gt, beta, gct, s0, bwd, row0, l):
    nblk = l // DN_SC
    blk0 = row0 // DN_SC

    def rel(i):
        return nblk - 1 - i if bwd else i

    def b(i):
        return blk0 + rel(i)

    def part_spec(part):
        return pl.BlockSpec((DN_SC, DN_WIDTH), lambda i: (b(i), part))

    col = pl.BlockSpec((DN_SC, N_GB), lambda i: (b(i), 0))
    st = pl.BlockSpec((DN_HEADS, DN_DIM, DN_DIM), lambda i: (0, 0, 0))
    return pl.pallas_call(
        functools.partial(_gdn_kernel, bwd=bwd, nblk=nblk),
        grid=(nblk,),
        in_specs=[part_spec(0), part_spec(1), part_spec(2), col, col, col,
                  pl.BlockSpec((N_GB, DN_SC), lambda i: (0, b(i))), st],
        out_specs=[pl.BlockSpec((DN_SC, DN_WIDTH), lambda i: (rel(i), 0)), st],
        out_shape=[jax.ShapeDtypeStruct((l, DN_WIDTH), F32),
                   jax.ShapeDtypeStruct((DN_HEADS, DN_DIM, DN_DIM), F32)],
        scratch_shapes=[pltpu.VMEM((DN_HEADS, DN_DIM, DN_DIM), F32)],
        compiler_params=_cparams("arbitrary"),
        name="gdn_scan_bwd" if bwd else "gdn_scan_fwd",
    )(qkv, qkv, qkv, gc, gt, beta, gct, s0)


def _dn_out_kernel(of_ref, ob_ref, gate_ref, nw_ref, y_ref):
    for h in range(DN_HEADS):
        sl = slice(h * DN_DIM, (h + 1) * DN_DIM)
        o = of_ref[:, sl] + ob_ref[:, sl]
        o = o * lax.rsqrt(jnp.mean(o * o, axis=-1, keepdims=True) + NORM_EPS) * nw_ref[...]
        gate = gate_ref[:, sl].astype(F32)
        y_ref[:, sl] = (o * (gate * _sigmoid(gate))).astype(y_ref.dtype)


def dn_output(o_f, o_b, z, row0, onorm_w):
    l = o_f.shape[0]
    tm = min(l, 512)
    blk0 = row0 // tm
    blk = pl.BlockSpec((tm, DN_WIDTH), lambda i: (i, 0))
    return pl.pallas_call(
        _dn_out_kernel,
        grid=(l // tm,),
        in_specs=[blk, blk, pl.BlockSpec((tm, DN_WIDTH), lambda i: (blk0 + i, 3)),
                  pl.BlockSpec((1, DN_DIM), lambda i: (0, 0))],
        out_specs=blk,
        out_shape=jax.ShapeDtypeStruct((l, DN_WIDTH), BF16),
        compiler_params=_cparams("parallel"),
        name="dn_output",
    )(o_f, o_b, z, onorm_w.reshape(1, DN_DIM))


def gated_deltanet(z, need_ctx, conv_w, a_log, dt_bias, onorm_w):
    a0, b0 = EVEN_IN - 2 * N_GB, EVEN_IN - N_GB
    scal = dn_scalars(z[:, a0:b0].astype(F32), z[:, b0:b0 + N_GB].astype(F32), a_log, dt_bias)
    qkv = dn_prep(z, conv_w)
    s0 = jnp.zeros((DN_HEADS, DN_DIM, DN_DIM), F32)
    outs_c, outs_l = [], []
    for bwd in (False, True):
        oc, s_ctx = gdn_scan(qkv, *scal, s0, bwd, SEQ, CTX_LEN)
        ol, _ = gdn_scan(qkv, *scal, s_ctx, bwd, 0, SEQ)
        outs_c.append(oc)
        outs_l.append(ol)
    y_ctx = dn_output(*outs_c, z, SEQ, onorm_w) if need_ctx else None
    return dn_output(*outs_l, z, 0, onorm_w), y_ctx


HY_SHORT = 3
HY_EMB = 1 + 2 * HY_BANDS
HY_EMB_PAD = 40
HY_TB = 256
FFT_B = 128
HY_CB = 32
HY_COL0 = 4


def _hy_pre_kernel(x0_ref, x0p_ref, x0n_ref, x1_ref, x1p_ref, x1n_ref, v_ref, vp_ref, vn_ref, w_ref, b_ref,
                   u_ref, x0c_ref, *, nblk, transpose_u):
    first = pl.program_id(0) == 0
    last = pl.program_id(0) == nblk - 1
    w = HY_WIDTH
    x0 = _short_conv(x0_ref, x0p_ref, x0n_ref, w_ref, 0, first, last) + b_ref[:, :w]
    x1 = _short_conv(x1_ref, x1p_ref, x1n_ref, w_ref, w, first, last) + b_ref[:, w:2 * w]
    v = _short_conv(v_ref, vp_ref, vn_ref, w_ref, 2 * w, first, last) + b_ref[:, 2 * w:]
    u = v * x1
    x0c_ref[...] = x0
    u_ref[...] = u.T if transpose_u else u


def hyena_pre(z, row0, l, short_w, short_b, transpose_u):
    tb = min(HY_TB, l)
    nblk = l // tb
    per = tb // HALO
    blk0 = row0 // tb
    last_halo = z.shape[0] // HALO - 1
    w = HY_WIDTH

    def part_specs(part):
        cb = HY_COL0 + part
        return [pl.BlockSpec((tb, w), lambda i: (blk0 + i, cb)),
                pl.BlockSpec((HALO, w), lambda i: (jnp.maximum((blk0 + i) * per - 1, 0), cb)),
                pl.BlockSpec((HALO, w), lambda i: (jnp.minimum((blk0 + i + 1) * per, last_halo), cb))]

    u_shape, u_spec = ((w, l), pl.BlockSpec((w, tb), lambda i: (0, i))) if transpose_u else (
        (l, w), pl.BlockSpec((tb, w), lambda i: (i, 0)))
    return pl.pallas_call(
        functools.partial(_hy_pre_kernel, nblk=nblk, transpose_u=transpose_u),
        grid=(nblk,),
        in_specs=part_specs(0) + part_specs(1) + part_specs(2) + [
            pl.BlockSpec((HY_SHORT, HY_IN), lambda i: (0, 0)), pl.BlockSpec((1, HY_IN), lambda i: (0, 0))],
        out_specs=[u_spec, pl.BlockSpec((tb, w), lambda i: (i, 0))],
        out_shape=[jax.ShapeDtypeStruct(u_shape, F32), jax.ShapeDtypeStruct((l, w), F32)],
        compiler_params=_cparams("parallel"),
        name="hyena_pre",
    )(z, z, z, z, z, z, z, z, z, short_w, short_b.reshape(1, HY_IN))


def _hy_filter_kernel(w1t_ref, b1_ref, w2t_ref, b2_ref, w3t_ref, b3_ref, w4t_ref, fr_ref, band_ref, dl_ref,
                      f_ref, hb0_ref, *, l, fb):
    hi = lax.Precision.HIGHEST
    j = pl.program_id(0)
    second = j >= l // fb
    n = j * fb + lax.broadcasted_iota(jnp.int32, (1, fb), 1)
    pos = jnp.where(second, 2 * l - n, n).astype(F32)
    t = pos / max(l - 1, 1)
    wpos = 2.0 * math.pi * pos / l
    arg = band_ref[...] * wpos
    row = lax.broadcasted_iota(jnp.int32, (HY_EMB_PAD, fb), 0)
    feat = jnp.where(row == 0, t, jnp.where(row <= HY_BANDS, jnp.cos(arg),
                                            jnp.where(row <= 2 * HY_BANDS, -jnp.sin(arg), 0.0)))
    fr = fr_ref[...]
    h = jnp.sin(fr * (jnp.dot(w1t_ref[...], feat, precision=hi) + b1_ref[...]))
    h = jnp.sin(fr * (jnp.dot(w2t_ref[...], h, precision=hi) + b2_ref[...]))
    h = jnp.sin(fr * (jnp.dot(w3t_ref[...], h, precision=hi) + b3_ref[...]))
    window = jnp.exp(-t * dl_ref[...])
    half = pl.multiple_of(jnp.where(second, HY_WIDTH, 0), HY_WIDTH)
    h16 = h.astype(BF16)
    f = jnp.dot(w4t_ref[pl.ds(half, HY_WIDTH), :].astype(BF16), h16, preferred_element_type=F32) * window
    f_ref[...] = jnp.where(n == l, 0.0, f)

    @pl.when(j == 0)
    def _():
        hb0_ref[...] = (jnp.dot(w4t_ref[HY_WIDTH:, :].astype(BF16), h16[:, :LANES], preferred_element_type=F32)
                        * window[:, :LANES])


def hyena_filter(l, w1, b1, w2, b2, w3, b3, w4, freq):
    fb = min(1024, l)
    colv = lambda v: v.reshape(-1, 1)
    bands = np.zeros((HY_EMB_PAD, 1), np.float32)
    bands[1:1 + HY_BANDS, 0] = bands[1 + HY_BANDS:HY_EMB, 0] = np.linspace(1e-4, HY_BANDS - 1, HY_BANDS,
                                                                            dtype=np.float32)
    min_decay = math.log(HY_DECAY_TARGET) / HY_WEAK_DECAY_PCT
    max_decay = math.log(HY_DECAY_TARGET) / HY_STRONG_DECAY_PCT
    deltas = np.abs(np.linspace(min_decay, max_decay, HY_WIDTH, dtype=np.float32)).reshape(-1, 1)
    w1t = jnp.pad(w1.T, ((0, 0), (0, HY_EMB_PAD - HY_EMB)))
    full = lambda a: pl.BlockSpec(a.shape, lambda j: (0,) * a.ndim)
    args = (w1t, colv(b1), w2.T, colv(b2), w3.T, colv(b3), w4.T, colv(freq), jnp.asarray(bands), jnp.asarray(deltas))
    filt, hb0 = pl.pallas_call(
        functools.partial(_hy_filter_kernel, l=l, fb=fb),
        grid=(2 * l // fb,),
        in_specs=[full(a) for a in args],
        out_specs=[pl.BlockSpec((HY_WIDTH, fb), lambda j: (0, j)), pl.BlockSpec((HY_WIDTH, LANES), lambda j: (0, 0))],
        out_shape=[jax.ShapeDtypeStruct((HY_WIDTH, 2 * l), F32), jax.ShapeDtypeStruct((HY_WIDTH, LANES), F32)],
        compiler_params=_cparams("arbitrary"),
        name="hyena_filter",
    )(*args)
    return filt, hb0[:, :1]


def _dft_constants():
    b = FFT_B
    n = b * b
    idx = np.arange(b)
    ang = 2.0 * np.pi * np.outer(idx, idx) / b
    c, s = np.cos(ang), np.sin(ang)
    tw = 2.0 * np.pi * np.outer(idx, idx) / n
    fwd_b = np.concatenate([c, -s], axis=0)
    cs = np.concatenate([c, s], axis=1)
    inv_b = np.concatenate([c[:b // 2], -s[:b // 2]], axis=1) / n
    return (jnp.asarray(fwd_b, BF16), jnp.asarray(cs, BF16), jnp.asarray(np.cos(tw), F32),
            jnp.asarray(-np.sin(tw), F32), jnp.asarray(inv_b, BF16))


def _hy_conv_kernel(u_ref, f_ref, fwd_ref, cs_ref, tr_ref, ti_ref, inv_ref, y_ref, ur_s, ui_s, fr_s, fi_s):
    cb = u_ref.shape[0]
    b = FFT_B
    m = cb * b
    dot = functools.partial(jnp.dot, preferred_element_type=F32)
    fwd = fwd_ref[...]
    fwd_half = fwd[:, :b // 2]
    tr, ti = tr_ref[...], ti_ref[...]

    def first_stage(c, carry):
        for src, lhs, re_s, im_s in ((u_ref, fwd_half, ur_s, ui_s), (f_ref, fwd, fr_s, fi_s)):
            p = dot(lhs, src[c].astype(BF16))
            pr, pi = p[:b], p[b:]
            re_s[c] = (pr * tr - pi * ti).astype(BF16)
            im_s[c] = (pr * ti + pi * tr).astype(BF16)
        return carry

    lax.fori_loop(0, cb, first_stage, 0, unroll=True)

    cs = cs_ref[...]

    def times_cs(re, im):
        big = dot(jnp.concatenate([re, im], axis=0), cs)
        return big[:m, :b], big[:m, b:], big[m:, :b], big[m:, b:]

    def second_stage(re_s, im_s):
        rc, rs, ic, is_ = times_cs(re_s[...].reshape(m, b), im_s[...].reshape(m, b))
        return rc + is_, ic - rs

    xr, xi = second_stage(ur_s, ui_s)
    hr, hi = second_stage(fr_s, fi_s)
    zr = (xr * hr - xi * hi).astype(BF16)
    zi = (xr * hi + xi * hr).astype(BF16)
    rc, rs, ic, is_ = times_cs(zr, zi)
    gr = (rc - is_).reshape(cb, b, b)
    gi = (rs + ic).reshape(cb, b, b)
    ur_s[...] = (gr * tr + gi * ti).astype(BF16)
    ui_s[...] = (gi * tr - gr * ti).astype(BF16)
    inv = inv_ref[...]

    def last_stage(c, carry):
        y_ref[c] = dot(inv, jnp.concatenate([ur_s[c], ui_s[c]], axis=0))
        return carry

    lax.fori_loop(0, cb, last_stage, 0, unroll=True)


def hyena_long_conv(u_t, filt_t):
    c, l = u_t.shape
    b = FFT_B
    consts = _dft_constants()
    full = lambda a: pl.BlockSpec(a.shape, lambda i: (0,) * a.ndim)
    y = pl.pallas_call(
        _hy_conv_kernel,
        grid=(c // HY_CB,),
        in_specs=[pl.BlockSpec((HY_CB, b // 2, b), lambda i: (i, 0, 0)),
                  pl.BlockSpec((HY_CB, b, b), lambda i: (i, 0, 0))] + [full(a) for a in consts],
        out_specs=pl.BlockSpec((HY_CB, b // 2, b), lambda i: (i, 0, 0)),
        out_shape=jax.ShapeDtypeStruct((c, b // 2, b), F32),
        scratch_shapes=[pltpu.VMEM((HY_CB, b, b), BF16) for _ in range(4)],
        compiler_params=_cparams("parallel"),
        name="hyena_long_conv",
    )(u_t.reshape(c, b // 2, b), filt_t.reshape(c, b, b), *consts)
    return y.reshape(c, l)


def _hy_ctx_conv_kernel(u_ref, f_ref, fwdu_ref, fwdf_ref, inv_ref, y_ref):
    dot = functools.partial(jnp.dot, preferred_element_type=F32)
    n = f_ref.shape[0]
    x = dot(fwdu_ref[...], u_ref[...].astype(BF16))
    h = dot(fwdf_ref[...], f_ref[...].astype(BF16))
    xr, xi, hr, hi = x[:n], x[n:], h[:n], h[n:]
    z = jnp.concatenate([xr * hr - xi * hi, xr * hi + xi * hr], axis=0).astype(BF16)
    y_ref[...] = dot(inv_ref[...], z)


def hyena_ctx_conv(u, filt):
    l, c = u.shape
    n = 2 * l
    idx = np.arange(n)
    ang = 2.0 * np.pi * np.outer(idx, idx) / n
    cm, sm = np.cos(ang), np.sin(ang)
    fwdf = np.concatenate([cm, -sm], axis=0)
    inv = np.concatenate([cm[:l], -sm[:l]], axis=1) / n
    consts = (jnp.asarray(fwdf[:, :l], BF16), jnp.asarray(fwdf, BF16), jnp.asarray(inv, BF16))
    tc = 256
    full = lambda a: pl.BlockSpec(a.shape, lambda i: (0,) * a.ndim)
    return pl.pallas_call(
        _hy_ctx_conv_kernel,
        grid=(c // tc,),
        in_specs=[pl.BlockSpec((l, tc), lambda i: (0, i)), pl.BlockSpec((n, tc), lambda i: (0, i))]
        + [full(a) for a in consts],
        out_specs=pl.BlockSpec((l, tc), lambda i: (0, i)),
        out_shape=jax.ShapeDtypeStruct((l, c), F32),
        compiler_params=_cparams("parallel"),
        name="hyena_ctx_conv",
    )(u, filt, *consts)


def _hy_post_kernel(y_ref, u_ref, x0_ref, b_ref, o_ref, *, transposed):
    w = y_ref[...] + u_ref[...] * b_ref[...]
    if transposed:
        w = w.T
    o_ref[...] = (w * x0_ref[...]).astype(o_ref.dtype)


def hyena_post(y, u, x0, bias, transposed):
    l, w = x0.shape
    tb = min(HY_TB, l)
    tm = pl.BlockSpec((tb, w), lambda i: (i, 0))
    yu = pl.BlockSpec((w, tb), lambda i: (0, i)) if transposed else tm
    return pl.pallas_call(
        functools.partial(_hy_post_kernel, transposed=transposed),
        grid=(l // tb,),
        in_specs=[yu, yu, tm, pl.BlockSpec(bias.shape, lambda i: (0, 0))],
        out_specs=tm,
        out_shape=jax.ShapeDtypeStruct((l, w), BF16),
        compiler_params=_cparams("parallel"),
        name="hyena_post",
    )(y, u, x0, bias)


def hyena(z, row0, l, short_w, short_b, filt, bias):
    filt_t, hb0 = hyena_filter(l, *filt)
    if l == FFT_B * FFT_B // 2:
        u_t, x0 = hyena_pre(z, row0, l, short_w, short_b, True)
        y_t = hyena_long_conv(u_t, filt_t)
        return hyena_post(y_t, u_t, x0, bias.reshape(-1, 1) + hb0, True)
    u, x0 = hyena_pre(z, row0, l, short_w, short_b, False)
    y = hyena_ctx_conv(u, filt_t.T)
    return hyena_post(y, u, x0, (bias.reshape(-1, 1) + hb0).T, False)


def kernel(x, c, ctx, c_ctx, w_mod, b_mod, norm1_w, norm2_w, ffn_w_gate, ffn_w_up, ffn_w_down, even_w_in, even_w_out, dn_conv_w, dn_a_log, dn_dt_bias, dn_onorm_w, hy_short_w, hy_short_b, hy_f_w1, hy_f_b1, hy_f_w2, hy_f_b2, hy_f_w3, hy_f_b3, hy_f_w4, hy_f_freq, hy_bias, na_w_qkv, na_rpb, na_w_out, final_norm_w):
    d = D_MODEL
    x_src = (x[0], ctx[0], 0)
    mod = modulation(jnp.concatenate([c, c_ctx[None]], axis=0), w_mod, b_mod)

    for layer in range(DEPTH):
        need_ctx = layer < DEPTH - 1
        rows = ALL_ROWS if need_ctx else SEQ
        sh1, sc1, g1, sh2, sc2, g2 = (mod[layer, :, None, j * d:(j + 1) * d] for j in range(N_MOD))
        i = layer // 2
        h = norm_mod(x_src, norm1_w[layer][None], sc1, sh1)
        if layer % 2 == 0:
            w_in = even_w_in[i]
            w_in = jnp.concatenate([w_in[:, :4 * DN_WIDTH], w_in[:, DN_IN:], w_in[:, 4 * DN_WIDTH:DN_IN],
                                    jnp.zeros((d, EVEN_PAD - EVEN_IN), F32)], axis=1).astype(BF16)
            z = projection(h, w_in, EVEN_TN, BF16)
            dn_l, dn_c = gated_deltanet(z, need_ctx, dn_conv_w[i], dn_a_log[i], dn_dt_bias[i], dn_onorm_w[i])
            filt = (hy_f_w1[i], hy_f_b1[i], hy_f_w2[i], hy_f_b2[i], hy_f_w3[i], hy_f_b3[i], hy_f_w4[i],
                    hy_f_freq[i])
            y_lat = (dn_l, hyena(z, 0, SEQ, hy_short_w[i], hy_short_b[i], filt, hy_bias[i]))
            y_ctx = (dn_c, hyena(z, SEQ, CTX_LEN, hy_short_w[i], hy_short_b[i], filt, hy_bias[i])) if need_ctx else y_lat
            w_out = even_w_out
        else:
            z = projection(h, na_w_qkv, QKV_TN, BF16, layer=i, scaled_cols=D_MODEL, scale=NA_DIM ** -0.5)
            ya = na_attention(z, na_bias_tiles(na_rpb[i]))
            y_lat = (ya, ya)
            if need_ctx:
                yc = ctx_attention(z)
                y_ctx = (yc, yc)
            else:
                y_ctx = y_lat
            w_out = na_w_out
        xa, h = proj_residual(y_lat, y_ctx, w_out, i, x_src, rows, g1, norm2_w[layer][None], sc2, sh2)
        xa = ffn_block(xa, h, rows, g2, ffn_w_gate, ffn_w_up, ffn_w_down, layer)
        x_src = (xa, xa, SEQ)
    return final_norm(xa, SEQ, final_norm_w[None])[None]
```

```python
import functools
import math

import jax
import jax.numpy as jnp
import numpy as np
from jax import lax
from jax.experimental import pallas as pl
from jax.experimental.pallas import tpu as pltpu

D_MODEL = 2048
SEQ = 8192
DEPTH = 4
GRID_W = 64
ROWS = SEQ // GRID_W
CTX_LEN = 256
ALL_ROWS = SEQ + CTX_LEN
NORM_EPS = 1e-6
N_MOD = 6

DN_HEADS = 8
DN_DIM = 128
DN_WIDTH = DN_HEADS * DN_DIM
DN_CONV = 5
DN_CHUNK = 64
DN_IN = 4 * DN_WIDTH + 4 * DN_HEADS

HY_WIDTH = D_MODEL - DN_WIDTH
HY_BANDS = 16
HY_DECAY_TARGET = 1e-2
HY_STRONG_DECAY_PCT = 0.3
HY_WEAK_DECAY_PCT = 1.5
HY_IN = 3 * HY_WIDTH
EVEN_IN = DN_IN + HY_IN

NA_HEADS = D_MODEL // 128
NA_DIM = 128
WIN_R = 8
WIN_C = 16
FFN_HIDDEN = -(-8 * D_MODEL // (3 * 256)) * 256

LANES = 128
VMEM_LIMIT = 56 * 1024 * 1024
NEG = -1e30

F32 = jnp.float32
BF16 = jnp.bfloat16


def _cparams(*sem):
    return pltpu.CompilerParams(dimension_semantics=sem, vmem_limit_bytes=VMEM_LIMIT)


def _sigmoid(x):
    return 1.0 / (1.0 + jnp.exp(-x))


MOD_TK = 256


def _mod_kernel(c_ref, w_ref, b_ref, o_ref):
    k = pl.program_id(1)
    n = o_ref.shape[2]

    @pl.when(k == 0)
    def _():
        o_ref[0] = jnp.broadcast_to(b_ref[0], (2, n))

    rows = []
    for r in range(2):
        c = c_ref[r, pl.ds(pl.multiple_of(k * MOD_TK, MOD_TK), MOD_TK), :]
        a = c * _sigmoid(c)
        cols = [jnp.sum(w_ref[0, :, j * LANES:(j + 1) * LANES] * a, axis=0, keepdims=True)
                for j in range(n // LANES)]
        rows.append(jnp.concatenate(cols, axis=1))
    o_ref[0] += jnp.concatenate(rows, axis=0)


def modulation(c_pair, w_mod, b_mod):
    n = N_MOD * D_MODEL
    cb = jnp.broadcast_to(c_pair[:, :, None], (2, D_MODEL, LANES))
    return pl.pallas_call(
        _mod_kernel,
        grid=(DEPTH, D_MODEL // MOD_TK),
        in_specs=[pl.BlockSpec((2, D_MODEL, LANES), lambda l, k: (0, 0, 0)),
                  pl.BlockSpec((1, MOD_TK, n), lambda l, k: (l, k, 0)),
                  pl.BlockSpec((1, 1, n), lambda l, k: (l, 0, 0))],
        out_specs=pl.BlockSpec((1, 2, n), lambda l, k: (l, 0, 0)),
        out_shape=jax.ShapeDtypeStruct((DEPTH, 2, n), F32),
        compiler_params=_cparams("parallel", "arbitrary"),
        name="modulation",
    )(cb, w_mod, b_mod.reshape(DEPTH, 1, n))


ROW_TILE = 256


def _kind_spec(d):
    return pl.BlockSpec((None, 1, d), lambda i: (jnp.where(i >= SEQ // ROW_TILE, 1, 0), 0, 0))


def _row_source_specs(src, width, col=0):
    n_lat = SEQ // ROW_TILE
    ctx_blk = src[2] // ROW_TILE
    return [pl.BlockSpec((ROW_TILE, width), lambda i: (jnp.minimum(i, n_lat - 1), col)),
            pl.BlockSpec((ROW_TILE, width), lambda i: (ctx_blk, col))]


def _pick_rows(lat_ref, ctx_ref):
    return jnp.where(pl.program_id(0) < SEQ // ROW_TILE, lat_ref[...], ctx_ref[...])


def _norm_mod_kernel(xl_ref, xc_ref, nw_ref, sc_ref, sh_ref, h_ref):
    x = _pick_rows(xl_ref, xc_ref)
    y = x * lax.rsqrt(jnp.mean(x * x, axis=-1, keepdims=True) + NORM_EPS)
    h_ref[...] = (y * nw_ref[...] * (1.0 + sc_ref[...]) + sh_ref[...]).astype(h_ref.dtype)


def norm_mod(x_src, nw, sc, sh):
    d = x_src[0].shape[1]
    return pl.pallas_call(
        _norm_mod_kernel,
        grid=(ALL_ROWS // ROW_TILE,),
        in_specs=_row_source_specs(x_src, d) + [pl.BlockSpec((1, d), lambda i: (0, 0)), _kind_spec(d), _kind_spec(d)],
        out_specs=pl.BlockSpec((ROW_TILE, d), lambda i: (i, 0)),
        out_shape=jax.ShapeDtypeStruct((ALL_ROWS, d), BF16),
        compiler_params=_cparams("parallel"),
        name="norm_mod",
    )(x_src[0], x_src[1], nw, sc, sh)


ROW_TILES = 8
FFN_UP_ROW_TILES = 4
PROJ_ROW_TILES = 8
EVEN_TN = 19 * LANES
EVEN_PAD = 3 * EVEN_TN
QKV_TN = 1024
assert EVEN_PAD >= EVEN_IN


def _weight_spec(w, tn, layer):
    d = w.shape[-2]
    if layer is None:
        return pl.BlockSpec((d, tn), lambda j, i: (0, j))
    return pl.BlockSpec((None, d, tn), lambda j, i: (layer, 0, j))


def _proj_kernel(h_ref, w_ref, o_ref, *scratch, scaled_cols, scale):
    if scratch:
        wb_ref, = scratch

        @pl.when(pl.program_id(1) == 0)
        def _():
            wb_ref[...] = w_ref[...].astype(BF16)
    else:
        wb_ref = w_ref
    y = jnp.dot(h_ref[...], wb_ref[...], preferred_element_type=F32)
    if scaled_cols:
        y = y * jnp.where(pl.program_id(0) * o_ref.shape[1] < scaled_cols, scale, 1.0)
    o_ref[...] = y.astype(o_ref.dtype)


def projection(h, w, tn, out_dtype, layer=None, scaled_cols=0, scale=1.0):
    m, d = h.shape
    n = w.shape[-1]
    tm = m // PROJ_ROW_TILES
    assert n % tn == 0 and scaled_cols % tn == 0
    scratch = [] if w.dtype == BF16 else [pltpu.VMEM((d, tn), BF16)]
    return pl.pallas_call(
        functools.partial(_proj_kernel, scaled_cols=scaled_cols, scale=scale),
        grid=(n // tn, PROJ_ROW_TILES),
        in_specs=[pl.BlockSpec((tm, d), lambda j, i: (i, 0)), _weight_spec(w, tn, layer)],
        out_specs=pl.BlockSpec((tm, tn), lambda j, i: (i, j)),
        out_shape=jax.ShapeDtypeStruct((m, n), out_dtype),
        scratch_shapes=scratch,
        compiler_params=_cparams("parallel", "arbitrary"),
        name="projection",
    )(h, w)


FFN_TF = 512
FFN_TN = 256


def _ffn_up_kernel(h_ref, wg_ref, wu_ref, a_ref, wgb_ref, wub_ref):
    @pl.when(pl.program_id(1) == 0)
    def _():
        wgb_ref[...] = wg_ref[...].astype(BF16)
        wub_ref[...] = wu_ref[...].astype(BF16)

    h = h_ref[...]
    gate = jnp.dot(h, wgb_ref[...], preferred_element_type=F32)
    up = jnp.dot(h, wub_ref[...], preferred_element_type=F32)
    a_ref[...] = (gate * _sigmoid(gate) * up).astype(a_ref.dtype)


def _ffn_down_kernel(a_ref, wd_ref, x_ref, g_ref, o_ref):
    tm = x_ref.shape[0]
    y = jnp.dot(a_ref[...], wd_ref[...].astype(BF16), preferred_element_type=F32)
    row = pl.program_id(0) * tm + lax.broadcasted_iota(jnp.int32, (tm, 1), 0)
    o_ref[...] = x_ref[...] + jnp.where(row < SEQ, g_ref[0], g_ref[1]) * y


def ffn_block(x, h, rows, g, wg, wu, wd, layer):
    m, d = x.shape
    hid = wg.shape[2]
    tm = rows // ROW_TILES
    tu = rows // FFN_UP_ROW_TILES
    a = pl.pallas_call(
        _ffn_up_kernel,
        grid=(hid // FFN_TF, FFN_UP_ROW_TILES),
        in_specs=[pl.BlockSpec((tu, d), lambda f, i: (i, 0)),
                  _weight_spec(wg, FFN_TF, layer), _weight_spec(wu, FFN_TF, layer)],
        out_specs=pl.BlockSpec((tu, FFN_TF), lambda f, i: (i, f)),
        out_shape=jax.ShapeDtypeStruct((m, hid), BF16),
        scratch_shapes=[pltpu.VMEM((d, FFN_TF), BF16), pltpu.VMEM((d, FFN_TF), BF16)],
        compiler_params=_cparams("parallel", "arbitrary"),
        name="ffn_up",
    )(h, wg, wu)
    return pl.pallas_call(
        _ffn_down_kernel,
        grid=(ROW_TILES, d // FFN_TN),
        in_specs=[pl.BlockSpec((tm, hid), lambda i, j: (i, 0)),
                  pl.BlockSpec((None, hid, FFN_TN), lambda i, j: (layer, 0, j)),
                  pl.BlockSpec((tm, FFN_TN), lambda i, j: (i, j)),
                  pl.BlockSpec((2, 1, FFN_TN), lambda i, j: (0, 0, j))],
        out_specs=pl.BlockSpec((tm, FFN_TN), lambda i, j: (i, j)),
        out_shape=jax.ShapeDtypeStruct((m, d), F32),
        compiler_params=_cparams("parallel", "arbitrary"),
        name="ffn_down",
    )(a, wd, x, g)


def _proj_res_kernel(yal_ref, yac_ref, ybl_ref, ybc_ref, xl_ref, xc_ref, w_ref, g_ref, nw_ref, sc_ref, sh_ref,
                     o_ref, h_ref, wb_ref):
    @pl.when(pl.program_id(0) == 0)
    def _():
        wb_ref[...] = w_ref[...].astype(BF16)

    ya = _pick_rows(yal_ref, yac_ref)
    yb = _pick_rows(ybl_ref, ybc_ref)
    ka = ya.shape[1]
    y = (jnp.dot(ya, wb_ref[:ka, :], preferred_element_type=F32)
         + jnp.dot(yb, wb_ref[ka:, :], preferred_element_type=F32))
    x = _pick_rows(xl_ref, xc_ref) + g_ref[...] * y
    o_ref[...] = x
    xn = x * lax.rsqrt(jnp.mean(x * x, axis=-1, keepdims=True) + NORM_EPS)
    h_ref[...] = (xn * nw_ref[...] * (1.0 + sc_ref[...]) + sh_ref[...]).astype(h_ref.dtype)


def proj_residual(y_lat, y_ctx, w, layer, x_src, rows, g, nw, sc, sh):
    d = x_src[0].shape[1]
    half = d // 2
    second = [1 if pair[1] is pair[0] else 0 for pair in (y_lat, y_ctx)]
    assert second[0] == second[1]
    row = pl.BlockSpec((ROW_TILE, d), lambda i: (i, 0))
    return pl.pallas_call(
        _proj_res_kernel,
        grid=(rows // ROW_TILE,),
        in_specs=_row_source_specs((y_lat[0], y_ctx[0], 0), half) + _row_source_specs((y_lat[1], y_ctx[1], 0), half,
                                                                                      second[0])
        + _row_source_specs(x_src, d) + [
            pl.BlockSpec((None, d, d), lambda i: (layer, 0, 0), pipeline_mode=pl.Buffered(1)),
            _kind_spec(d), pl.BlockSpec((1, d), lambda i: (0, 0)), _kind_spec(d), _kind_spec(d)],
        out_specs=[row, row],
        out_shape=[jax.ShapeDtypeStruct((ALL_ROWS, d), F32), jax.ShapeDtypeStruct((ALL_ROWS, d), BF16)],
        scratch_shapes=[pltpu.VMEM((d, d), BF16)],
        compiler_params=_cparams("arbitrary"),
        name="proj_residual",
    )(y_lat[0], y_ctx[0], y_lat[1], y_ctx[1], x_src[0], x_src[1], w, g, nw, sc, sh)


def _final_norm_kernel(x_ref, w_ref, o_ref):
    x = x_ref[...]
    o_ref[...] = x * lax.rsqrt(jnp.mean(x * x, axis=-1, keepdims=True) + NORM_EPS) * w_ref[...]


def final_norm(x, rows, w):
    d = x.shape[1]
    tm = 512
    return pl.pallas_call(
        _final_norm_kernel,
        grid=(rows // tm,),
        in_specs=[pl.BlockSpec((tm, d), lambda i: (i, 0)), pl.BlockSpec((1, d), lambda i: (0, 0))],
        out_specs=pl.BlockSpec((tm, d), lambda i: (i, 0)),
        out_shape=jax.ShapeDtypeStruct((rows, d), F32),
        compiler_params=_cparams("parallel"),
        name="final_norm",
    )(x, w)


NA_R = 4
NA_KD = (NA_R + WIN_R) // 2
NA_T = 2 * WIN_R
NA_HP = 4


def _na_bias_kernel(rpb_ref, o_ref):
    c = lax.broadcasted_iota(jnp.int32, (GRID_W, LANES), 0)
    lane = lax.broadcasted_iota(jnp.int32, (GRID_W, LANES), 1)
    left = lane < GRID_W
    kc = jnp.where(left, lane, lane - GRID_W)
    cs = jnp.clip(c - WIN_C // 2, 0, GRID_W - WIN_C)
    inwin = (kc >= cs) & (kc < cs + WIN_C)
    n_ro = 2 * WIN_R - 1

    def toeplitz(ro):
        if ro < 0 or ro >= n_ro:
            return jnp.full((GRID_W, LANES), NEG, F32)
        row = jnp.broadcast_to(rpb_ref[0, ro:ro + 1, :], (GRID_W, LANES))
        return pltpu.roll(pltpu.roll(row, LANES - (WIN_C - 1), 1), 0, 1, stride=1, stride_axis=0)

    for t in range(NA_T):
        tile = jnp.where(left, toeplitz(t - 1), pltpu.roll(toeplitz(t), GRID_W, 1))
        o_ref[0, t] = jnp.where(inwin, tile, NEG)


def na_bias_tiles(rpb):
    padded = jnp.pad(rpb, ((0, 0), (0, 1), (0, LANES - rpb.shape[2])))
    return pl.pallas_call(
        _na_bias_kernel,
        grid=(NA_HEADS,),
        in_specs=[pl.BlockSpec((1, NA_T, LANES), lambda h: (h, 0, 0))],
        out_specs=pl.BlockSpec((1, NA_T, GRID_W, LANES), lambda h: (h, 0, 0, 0)),
        out_shape=jax.ShapeDtypeStruct((NA_HEADS, NA_T, GRID_W, LANES), F32),
        compiler_params=_cparams("parallel"),
        name="na_bias_tiles",
    )(padded)


def _na_kernel(q_ref, k_ref, v_ref, kc_ref, vc_ref, tt_ref, o_ref):
    r0 = pl.program_id(1) * NA_R
    ks = jnp.clip(r0 - WIN_R // 2, 0, ROWS - 2 * NA_KD)
    span = pl.ds(pl.multiple_of(ks * GRID_W, LANES), NA_KD * LANES)
    heads = range(NA_HP)
    cols = [slice(h * NA_DIM, (h + 1) * NA_DIM) for h in heads]
    contract_last = (((1,), (1,)), ((), ()))
    q = [q_ref[:, cols[h]] for h in heads]
    s_loc = [lax.dot_general(q[h], k_ref[span, cols[h]], contract_last, preferred_element_type=F32) for h in heads]
    s_ctx = [lax.dot_general(q[h], kc_ref[:, cols[h]], contract_last, preferred_element_type=F32) for h in heads]

    left = lax.broadcasted_iota(jnp.int32, (GRID_W, LANES), 1) < GRID_W
    tile_sel = []
    for qi in range(NA_R):
        r = r0 + qi
        rs = jnp.clip(r - WIN_R // 2, 0, ROWS - WIN_R)
        for dj in range(NA_KD):
            kl = ks + 2 * dj
            vl = ((kl >= rs) & (kl < rs + WIN_R)).astype(jnp.int32)
            vr = ((kl + 1 >= rs) & (kl + 1 < rs + WIN_R)).astype(jnp.int32)
            tile_sel.append((jnp.clip(kl - r + WIN_R, 0, NA_T - 1), jnp.where(left, vl, vr) > 0))
    for h in heads:
        rows = [jnp.concatenate([jnp.where(valid, tt_ref[h, t], NEG)
                                 for t, valid in tile_sel[qi * NA_KD:(qi + 1) * NA_KD]], axis=1)
                for qi in range(NA_R)]
        s_loc[h] = s_loc[h] + jnp.concatenate(rows, axis=0)

    m = [jnp.maximum(jnp.max(s_loc[h], axis=-1, keepdims=True), jnp.max(s_ctx[h], axis=-1, keepdims=True))
         for h in heads]
    p_loc = [jnp.exp(s_loc[h] - m[h]) for h in heads]
    p_ctx = [jnp.exp(s_ctx[h] - m[h]) for h in heads]
    denom = [jnp.sum(p_loc[h], axis=-1, keepdims=True) + jnp.sum(p_ctx[h], axis=-1, keepdims=True) for h in heads]
    o = [jnp.dot(p_loc[h].astype(BF16), v_ref[span, cols[h]], preferred_element_type=F32)
         + jnp.dot(p_ctx[h].astype(BF16), vc_ref[:, cols[h]], preferred_element_type=F32) for h in heads]
    for h in heads:
        o_ref[:, cols[h]] = (o[h] / denom[h]).astype(o_ref.dtype)


def na_attention(z, tiles):
    tq = NA_R * GRID_W
    w = NA_HP * NA_DIM
    ng = NA_HEADS // NA_HP
    return pl.pallas_call(
        _na_kernel,
        grid=(ng, ROWS // NA_R),
        in_specs=[pl.BlockSpec((tq, w), lambda g, i: (i, g)),
                  pl.BlockSpec((SEQ, w), lambda g, i: (0, ng + g)),
                  pl.BlockSpec((SEQ, w), lambda g, i: (0, 2 * ng + g)),
                  pl.BlockSpec((CTX_LEN, w), lambda g, i: (SEQ // CTX_LEN, ng + g)),
                  pl.BlockSpec((CTX_LEN, w), lambda g, i: (SEQ // CTX_LEN, 2 * ng + g)),
                  pl.BlockSpec((NA_HP, NA_T, GRID_W, LANES), lambda g, i: (g, 0, 0, 0))],
        out_specs=pl.BlockSpec((tq, w), lambda g, i: (i, g)),
        out_shape=jax.ShapeDtypeStruct((SEQ, D_MODEL), BF16),
        compiler_params=_cparams("parallel", "arbitrary"),
        name="na_attention",
    )(z, z, z, z, z, tiles)


def _ctx_attn_kernel(q_ref, k_ref, v_ref, o_ref):
    s = lax.dot_general(q_ref[...], k_ref[...], (((1,), (1,)), ((), ())), preferred_element_type=F32)
    p = jnp.exp(s - jnp.max(s, axis=-1, keepdims=True))
    o = jnp.dot(p.astype(BF16), v_ref[...], preferred_element_type=F32)
    o_ref[...] = (o / jnp.sum(p, axis=-1, keepdims=True)).astype(o_ref.dtype)


def ctx_attention(z):
    nh = NA_HEADS
    blk = lambda off: pl.BlockSpec((CTX_LEN, NA_DIM), lambda h: (SEQ // CTX_LEN, off + h))
    return pl.pallas_call(
        _ctx_attn_kernel,
        grid=(nh,),
        in_specs=[blk(0), blk(nh), blk(2 * nh)],
        out_specs=pl.BlockSpec((CTX_LEN, NA_DIM), lambda h: (0, h)),
        out_shape=jax.ShapeDtypeStruct((CTX_LEN, D_MODEL), BF16),
        compiler_params=_cparams("parallel"),
        name="ctx_attention",
    )(z, z, z)


DN_SC = 256
HALO = 16
N_GB = 2 * DN_HEADS


def _softplus(x):
    return jnp.maximum(x, 0.0) + jnp.log1p(jnp.exp(-jnp.abs(x)))


def _short_conv(main_ref, prev_ref, next_ref, w_ref, col0, first, last):
    ksize = w_ref.shape[0]
    tb, width = main_ref.shape
    prev = jnp.where(first, 0.0, prev_ref[...].astype(F32))
    nxt = jnp.where(last, 0.0, next_ref[...].astype(F32))
    xf = jnp.concatenate([prev, main_ref[...].astype(F32), nxt], axis=0)
    n = xf.shape[0]
    acc = None
    for j in range(ksize):
        sh = (ksize // 2 - j) % n
        xs = xf if sh == 0 else pltpu.roll(xf, sh, 0)
        term = xs[HALO:HALO + tb] * w_ref[j:j + 1, col0:col0 + width]
        acc = term if acc is None else acc + term
    return acc


def _chunk_masks(n):
    r = lax.broadcasted_iota(jnp.int32, (n, n), 0)
    c = lax.broadcasted_iota(jnp.int32, (n, n), 1)
    return r, c, (r // DN_CHUNK) == (c // DN_CHUNK)


def _dn_scalars_kernel(za_ref, zb_ref, zat_ref, alog_ref, dtb_ref, alogt_ref, dtbt_ref,
                       gc_ref, gt_ref, beta_ref, gct_ref):
    hi = lax.Precision.HIGHEST
    g = -jnp.exp(alog_ref[...]) * _softplus(za_ref[...] + dtb_ref[...])
    beta_ref[...] = _sigmoid(zb_ref[...])
    r, c, same = _chunk_masks(DN_SC)
    lo = (same & (c <= r)).astype(F32)
    up = (same & (c >= r)).astype(F32)
    fwd_col = lax.broadcasted_iota(jnp.int32, (DN_SC, N_GB), 1) < DN_HEADS
    gc_ref[...] = jnp.where(fwd_col, jnp.dot(lo, g, precision=hi), jnp.dot(up, g, precision=hi))
    gt_ref[...] = jnp.dot(same.astype(F32), g, precision=hi)
    g_t = -jnp.exp(alogt_ref[...]) * _softplus(zat_ref[...] + dtbt_ref[...])
    fwd_row = lax.broadcasted_iota(jnp.int32, (N_GB, DN_SC), 0) < DN_HEADS
    gct_ref[...] = jnp.where(fwd_row, jnp.dot(g_t, up, precision=hi), jnp.dot(g_t, lo, precision=hi))


def dn_scalars(za, zb, a_log, dt_bias):
    l = za.shape[0]
    col = pl.BlockSpec((DN_SC, N_GB), lambda i: (i, 0))
    row = pl.BlockSpec((N_GB, DN_SC), lambda i: (0, i))
    prow = pl.BlockSpec((1, N_GB), lambda i: (0, 0))
    pcol = pl.BlockSpec((N_GB, 1), lambda i: (0, 0))
    cshape = jax.ShapeDtypeStruct((l, N_GB), F32)
    return pl.pallas_call(
        _dn_scalars_kernel,
        grid=(l // DN_SC,),
        in_specs=[col, col, row, prow, prow, pcol, pcol],
        out_specs=[col, col, col, row],
        out_shape=[cshape, cshape, cshape, jax.ShapeDtypeStruct((N_GB, l), F32)],
        compiler_params=_cparams("parallel"),
        name="dn_scalars",
    )(za, zb, za.T, a_log.reshape(1, N_GB), dt_bias.reshape(1, N_GB), a_log.reshape(N_GB, 1),
      dt_bias.reshape(N_GB, 1))


def _dn_prep_kernel(zq_ref, zqp_ref, zqn_ref, zk_ref, zkp_ref, zkn_ref, zv_ref, zvp_ref, zvn_ref, cw_ref, o_ref):
    i = pl.program_id(0)
    n_lat = SEQ // DN_SC
    first = (i == 0) | (i == n_lat)
    last = (i == n_lat - 1) | (i == pl.num_programs(0) - 1)
    parts = ((zq_ref, zqp_ref, zqn_ref), (zk_ref, zkp_ref, zkn_ref), (zv_ref, zvp_ref, zvn_ref))
    for part, refs in enumerate(parts):
        acc = _short_conv(*refs, cw_ref, part * DN_WIDTH, first, last)
        x = acc * _sigmoid(acc)
        for h in range(DN_HEADS):
            xh = x[:, h * DN_DIM:(h + 1) * DN_DIM]
            if part < 2:
                xh = xh * (lax.rsqrt(jnp.sum(xh * xh, axis=-1, keepdims=True) + NORM_EPS)
                           * (DN_DIM ** -0.5 if part == 0 else 1.0))
            o_ref[:, part * DN_WIDTH + h * DN_DIM:part * DN_WIDTH + (h + 1) * DN_DIM] = xh


def dn_prep(z, conv_w):
    m = z.shape[0]
    per = DN_SC // HALO
    last_halo = m // HALO - 1

    def part_specs(part):
        return [pl.BlockSpec((DN_SC, DN_WIDTH), lambda i: (i, part)),
                pl.BlockSpec((HALO, DN_WIDTH), lambda i: (jnp.maximum(i * per - 1, 0), part)),
                pl.BlockSpec((HALO, DN_WIDTH), lambda i: (jnp.minimum((i + 1) * per, last_halo), part))]

    return pl.pallas_call(
        _dn_prep_kernel,
        grid=(m // DN_SC,),
        in_specs=part_specs(0) + part_specs(1) + part_specs(2) + [
            pl.BlockSpec((DN_CONV, 3 * DN_WIDTH), lambda i: (0, 0))],
        out_specs=pl.BlockSpec((DN_SC, 3 * DN_WIDTH), lambda i: (i, 0)),
        out_shape=jax.ShapeDtypeStruct((m, 3 * DN_WIDTH), F32),
        compiler_params=_cparams("parallel"),
        name="dn_prep",
    )(z, z, z, z, z, z, z, z, z, conv_w)


def _gdn_kernel(q_ref, k_ref, v_ref, gc_ref, gt_ref, beta_ref, gct_ref, s0_ref, *rest, bwd, nblk, fuse):
    if fuse:
        other_ref, gate_ref, nw_ref, o_ref, sfin_ref, state_ref = rest
    else:
        o_ref, sfin_ref, state_ref = rest
    i = pl.program_id(0)

    @pl.when(i == 0)
    def _():
        state_ref[...] = s0_ref[...]

    r, c, same64 = _chunk_masks(DN_SC)
    tri = same64 & ((c >= r) if bwd else (c <= r))
    offdiag = r != c
    same16 = (r // 16) == (c // 16)
    same32 = (r // 32) == (c // 32)
    contract_last = (((1,), (1,)), ((), ()))
    contract_first = (((0,), (0,)), ((), ()))
    nc = DN_SC // DN_CHUNK
    dot = functools.partial(jnp.dot, preferred_element_type=F32)

    heads = range(DN_HEADS)
    col0 = DN_HEADS if bwd else 0
    gc_c = [gc_ref[:, col0 + h:col0 + h + 1] for h in heads]
    gt_c = [gt_ref[:, col0 + h:col0 + h + 1] for h in heads]
    beta_c = [beta_ref[:, col0 + h:col0 + h + 1] for h in heads]
    q, k, v, k16, dec, qk16, m, p, e = ([None] * DN_HEADS for _ in range(9))
    for h in heads:
        sl = slice(h * DN_DIM, (h + 1) * DN_DIM)
        q[h], k[h], v[h] = q_ref[:, sl], k_ref[:, sl], v_ref[:, sl]
        k16[h] = k[h].astype(BF16)
        dec[h] = jnp.exp(jnp.where(tri, gc_c[h] - gct_ref[col0 + h:col0 + h + 1, :], NEG))
    for h in heads:
        kk = lax.dot_general(k16[h], k16[h], contract_last, preferred_element_type=F32)
        m[h] = jnp.where(offdiag, kk * beta_c[h] * dec[h], 0.0)
        qk16[h] = (lax.dot_general(q[h].astype(BF16), k16[h], contract_last, preferred_element_type=F32)
                   * dec[h]).astype(BF16)

    for h in heads:
        p[h] = jnp.where(same16, -m[h], 0.0)
        e[h] = p[h]
    p16 = [p[h].astype(BF16) for h in heads]
    for _ in range(3):
        for h in heads:
            p[h] = dot(p16[h], p16[h])
            p16[h] = p[h].astype(BF16)
        for h in heads:
            e[h] = e[h] + p[h] + dot(e[h].astype(BF16), p16[h])
    for inner, outer in ((same16, same32), (same32, None)):
        y = [None] * DN_HEADS
        for h in heads:
            cm = jnp.where(~inner if outer is None else (outer & ~inner), m[h], 0.0)
            y[h] = cm + dot(cm.astype(BF16), e[h].astype(BF16))
        for h in heads:
            e[h] = e[h] - y[h] - dot(e[h].astype(BF16), y[h].astype(BF16))

    u, w16, qg16, kd16, s = ([None] * DN_HEADS for _ in range(5))
    for h in heads:
        eg = jnp.exp(gc_c[h])
        rhs = jnp.concatenate([v[h] * beta_c[h], k[h] * (beta_c[h] * eg)], axis=1)
        uw = rhs + dot(e[h].astype(BF16), rhs.astype(BF16))
        u[h] = uw[:, :DN_DIM]
        w16[h] = uw[:, DN_DIM:].astype(BF16)
        qg16[h] = (q[h] * eg).astype(BF16)
        kd16[h] = (k[h] * jnp.exp(gt_c[h] - gc_c[h])).astype(BF16)
        s[h] = state_ref[h]

    outs = [[None] * nc for _ in heads]
    for ci in (range(nc - 1, -1, -1) if bwd else range(nc)):
        rows = slice(ci * DN_CHUNK, (ci + 1) * DN_CHUNK)
        ws = [dot(jnp.concatenate([w16[h][rows], qg16[h][rows]], axis=0), s[h].astype(BF16)) for h in heads]
        vn16 = [(u[h][rows] - ws[h][:DN_CHUNK]).astype(BF16) for h in heads]
        for h in heads:
            outs[h][ci] = ws[h][DN_CHUNK:] + dot(qk16[h][rows, ci * DN_CHUNK:(ci + 1) * DN_CHUNK], vn16[h])
        for h in heads:
            gl = jnp.exp(gt_c[h][ci * DN_CHUNK:ci * DN_CHUNK + 1, :])
            s[h] = s[h] * gl + lax.dot_general(kd16[h][rows], vn16[h], contract_first, preferred_element_type=F32)
    for h in heads:
        state_ref[h] = s[h]
        sl = slice(h * DN_DIM, (h + 1) * DN_DIM)
        o = jnp.concatenate(outs[h], axis=0)
        if fuse:
            o = o + other_ref[:, sl]
            o = o * lax.rsqrt(jnp.mean(o * o, axis=-1, keepdims=True) + NORM_EPS) * nw_ref[...]
            gate = gate_ref[:, sl].astype(F32)
            o = o * (gate * _sigmoid(gate))
        o_ref[:, sl] = o.astype(o_ref.dtype)

    @pl.when(i == nblk - 1)
    def _():
        sfin_ref[...] = state_ref[...]


def gdn_scan(qkv, gc, gt, beta, gct, s0, bwd, row0, l, fuse=None):
    nblk = l // DN_SC
    blk0 = row0 // DN_SC

    def rel(i):
        return nblk - 1 - i if bwd else i

    def b(i):
        return blk0 + rel(i)

    def part_spec(part):
        return pl.BlockSpec((DN_SC, DN_WIDTH), lambda i: (b(i), part))

    col = pl.BlockSpec((DN_SC, N_GB), lambda i: (b(i), 0))
    st = pl.BlockSpec((DN_HEADS, DN_DIM, DN_DIM), lambda i: (0, 0, 0))
    out_blk = pl.BlockSpec((DN_SC, DN_WIDTH), lambda i: (rel(i), 0))
    extra_specs, extra_args = [], []
    if fuse is not None:
        o_other, z, onorm_w = fuse
        extra_specs = [out_blk, part_spec(3), pl.BlockSpec((1, DN_DIM), lambda i: (0, 0))]
        extra_args = [o_other, z, onorm_w.reshape(1, DN_DIM)]
    return pl.pallas_call(
        functools.partial(_gdn_kernel, bwd=bwd, nblk=nblk, fuse=fuse is not None),
        grid=(nblk,),
        in_specs=[part_spec(0), part_spec(1), part_spec(2), col, col, col,
                  pl.BlockSpec((N_GB, DN_SC), lambda i: (0, b(i))), st] + extra_specs,
        out_specs=[out_blk, st],
        out_shape=[jax.ShapeDtypeStruct((l, DN_WIDTH), F32 if fuse is None else BF16),
                   jax.ShapeDtypeStruct((DN_HEADS, DN_DIM, DN_DIM), F32)],
        scratch_shapes=[pltpu.VMEM((DN_HEADS, DN_DIM, DN_DIM), F32)],
        compiler_params=_cparams("arbitrary"),
        name="gdn_scan_bwd" if bwd else "gdn_scan_fwd",
    )(qkv, qkv, qkv, gc, gt, beta, gct, s0, *extra_args)


def gated_deltanet(z, conv_w, a_log, dt_bias, onorm_w):
    a0, b0 = EVEN_IN - 2 * N_GB, EVEN_IN - N_GB
    scal = dn_scalars(z[:, a0:b0].astype(F32), z[:, b0:b0 + N_GB].astype(F32), a_log, dt_bias)
    qkv = dn_prep(z, conv_w)
    s0 = jnp.zeros((DN_HEADS, DN_DIM, DN_DIM), F32)
    oc_f, s_ctx = gdn_scan(qkv, *scal, s0, False, SEQ, CTX_LEN)
    ol_f, _ = gdn_scan(qkv, *scal, s_ctx, False, 0, SEQ)
    y_ctx, s_ctx = gdn_scan(qkv, *scal, s0, True, SEQ, CTX_LEN, fuse=(oc_f, z, onorm_w))
    y_lat, _ = gdn_scan(qkv, *scal, s_ctx, True, 0, SEQ, fuse=(ol_f, z, onorm_w))
    return y_lat, y_ctx


HY_SHORT = 3
HY_EMB = 1 + 2 * HY_BANDS
HY_EMB_PAD = 40
HY_TB = 256
FFT_B = 128
HY_CB = 32
HY_COL0 = 4


def _hy_pre_kernel(x0_ref, x0p_ref, x0n_ref, x1_ref, x1p_ref, x1n_ref, v_ref, vp_ref, vn_ref, w_ref, b_ref,
                   u_ref, x0c_ref, *, nblk, transpose_u):
    first = pl.program_id(0) == 0
    last = pl.program_id(0) == nblk - 1
    w = HY_WIDTH
    x0 = _short_conv(x0_ref, x0p_ref, x0n_ref, w_ref, 0, first, last) + b_ref[:, :w]
    x1 = _short_conv(x1_ref, x1p_ref, x1n_ref, w_ref, w, first, last) + b_ref[:, w:2 * w]
    v = _short_conv(v_ref, vp_ref, vn_ref, w_ref, 2 * w, first, last) + b_ref[:, 2 * w:]
    u = v * x1
    x0c_ref[...] = x0
    u_ref[...] = u.T if transpose_u else u


def hyena_pre(z, row0, l, short_w, short_b, transpose_u):
    tb = min(HY_TB, l)
    nblk = l // tb
    per = tb // HALO
    blk0 = row0 // tb
    last_halo = z.shape[0] // HALO - 1
    w = HY_WIDTH

    def part_specs(part):
        cb = HY_COL0 + part
        return [pl.BlockSpec((tb, w), lambda i: (blk0 + i, cb)),
                pl.BlockSpec((HALO, w), lambda i: (jnp.maximum((blk0 + i) * per - 1, 0), cb)),
                pl.BlockSpec((HALO, w), lambda i: (jnp.minimum((blk0 + i + 1) * per, last_halo), cb))]

    u_shape, u_spec = ((w, l), pl.BlockSpec((w, tb), lambda i: (0, i))) if transpose_u else (
        (l, w), pl.BlockSpec((tb, w), lambda i: (i, 0)))
    return pl.pallas_call(
        functools.partial(_hy_pre_kernel, nblk=nblk, transpose_u=transpose_u),
        grid=(nblk,),
        in_specs=part_specs(0) + part_specs(1) + part_specs(2) + [
            pl.BlockSpec((HY_SHORT, HY_IN), lambda i: (0, 0)), pl.BlockSpec((1, HY_IN), lambda i: (0, 0))],
        out_specs=[u_spec, pl.BlockSpec((tb, w), lambda i: (i, 0))],
        out_shape=[jax.ShapeDtypeStruct(u_shape, F32), jax.ShapeDtypeStruct((l, w), F32)],
        compiler_params=_cparams("parallel"),
        name="hyena_pre",
    )(z, z, z, z, z, z, z, z, z, short_w, short_b.reshape(1, HY_IN))


def _hy_filter_kernel(w1t_ref, b1_ref, w2t_ref, b2_ref, w3t_ref, b3_ref, w4t_ref, fr_ref, band_ref, dl_ref,
                      f_ref, hb0_ref, *, l, fb):
    hi = lax.Precision.HIGHEST
    j = pl.program_id(0)
    second = j >= l // fb
    n = j * fb + lax.broadcasted_iota(jnp.int32, (1, fb), 1)
    pos = jnp.where(second, 2 * l - n, n).astype(F32)
    t = pos / max(l - 1, 1)
    wpos = 2.0 * math.pi * pos / l
    arg = band_ref[...] * wpos
    row = lax.broadcasted_iota(jnp.int32, (HY_EMB_PAD, fb), 0)
    feat = jnp.where(row == 0, t, jnp.where(row <= HY_BANDS, jnp.cos(arg),
                                            jnp.where(row <= 2 * HY_BANDS, -jnp.sin(arg), 0.0)))
    fr = fr_ref[...]
    h = jnp.sin(fr * (jnp.dot(w1t_ref[...], feat, precision=hi) + b1_ref[...]))
    h = jnp.sin(fr * (jnp.dot(w2t_ref[...], h, precision=hi) + b2_ref[...]))
    h = jnp.sin(fr * (jnp.dot(w3t_ref[...], h, precision=hi) + b3_ref[...]))
    window = jnp.exp(-t * dl_ref[...])
    half = pl.multiple_of(jnp.where(second, HY_WIDTH, 0), HY_WIDTH)
    h16 = h.astype(BF16)
    f = jnp.dot(w4t_ref[pl.ds(half, HY_WIDTH), :].astype(BF16), h16, preferred_element_type=F32) * window
    f_ref[...] = jnp.where(n == l, 0.0, f)

    @pl.when(j == 0)
    def _():
        hb0_ref[...] = (jnp.dot(w4t_ref[HY_WIDTH:, :].astype(BF16), h16[:, :LANES], preferred_element_type=F32)
                        * window[:, :LANES])


def hyena_filter(l, w1, b1, w2, b2, w3, b3, w4, freq):
    fb = min(1024, l)
    colv = lambda v: v.reshape(-1, 1)
    bands = np.zeros((HY_EMB_PAD, 1), np.float32)
    bands[1:1 + HY_BANDS, 0] = bands[1 + HY_BANDS:HY_EMB, 0] = np.linspace(1e-4, HY_BANDS - 1, HY_BANDS,
                                                                            dtype=np.float32)
    min_decay = math.log(HY_DECAY_TARGET) / HY_WEAK_DECAY_PCT
    max_decay = math.log(HY_DECAY_TARGET) / HY_STRONG_DECAY_PCT
    deltas = np.abs(np.linspace(min_decay, max_decay, HY_WIDTH, dtype=np.float32)).reshape(-1, 1)
    w1t = jnp.pad(w1.T, ((0, 0), (0, HY_EMB_PAD - HY_EMB)))
    full = lambda a: pl.BlockSpec(a.shape, lambda j: (0,) * a.ndim)
    args = (w1t, colv(b1), w2.T, colv(b2), w3.T, colv(b3), w4.T, colv(freq), jnp.asarray(bands), jnp.asarray(deltas))
    filt, hb0 = pl.pallas_call(
        functools.partial(_hy_filter_kernel, l=l, fb=fb),
        grid=(2 * l // fb,),
        in_specs=[full(a) for a in args],
        out_specs=[pl.BlockSpec((HY_WIDTH, fb), lambda j: (0, j)), pl.BlockSpec((HY_WIDTH, LANES), lambda j: (0, 0))],
        out_shape=[jax.ShapeDtypeStruct((HY_WIDTH, 2 * l), F32), jax.ShapeDtypeStruct((HY_WIDTH, LANES), F32)],
        compiler_params=_cparams("arbitrary"),
        name="hyena_filter",
    )(*args)
    return filt, hb0[:, :1]


def _dft_constants():
    b = FFT_B
    n = b * b
    idx = np.arange(b)
    ang = 2.0 * np.pi * np.outer(idx, idx) / b
    c, s = np.cos(ang), np.sin(ang)
    tw = 2.0 * np.pi * np.outer(idx, idx) / n
    fwd_b = np.concatenate([c, -s], axis=0)
    cs = np.concatenate([c, s], axis=1)
    inv_b = np.concatenate([c[:b // 2], -s[:b // 2]], axis=1) / n
    return (jnp.asarray(fwd_b, BF16), jnp.asarray(cs, BF16), jnp.asarray(np.cos(tw), F32),
            jnp.asarray(-np.sin(tw), F32), jnp.asarray(inv_b, BF16))


def _hy_conv_kernel(u_ref, f_ref, fwd_ref, cs_ref, tr_ref, ti_ref, inv_ref, y_ref, ur_s, ui_s, fr_s, fi_s):
    cb = u_ref.shape[0]
    b = FFT_B
    m = cb * b
    dot = functools.partial(jnp.dot, preferred_element_type=F32)
    fwd = fwd_ref[...]
    fwd_half = fwd[:, :b // 2]
    tr, ti = tr_ref[...], ti_ref[...]

    def first_stage(c, carry):
        for src, lhs, re_s, im_s in ((u_ref, fwd_half, ur_s, ui_s), (f_ref, fwd, fr_s, fi_s)):
            p = dot(lhs, src[c].astype(BF16))
            pr, pi = p[:b], p[b:]
            re_s[c] = (pr * tr - pi * ti).astype(BF16)
            im_s[c] = (pr * ti + pi * tr).astype(BF16)
        return carry

    lax.fori_loop(0, cb, first_stage, 0, unroll=True)

    cs = cs_ref[...]

    def times_cs(re, im):
        big = dot(jnp.concatenate([re, im], axis=0), cs)
        return big[:m, :b], big[:m, b:], big[m:, :b], big[m:, b:]

    def second_stage(re_s, im_s):
        rc, rs, ic, is_ = times_cs(re_s[...].reshape(m, b), im_s[...].reshape(m, b))
        return rc + is_, ic - rs

    xr, xi = second_stage(ur_s, ui_s)
    hr, hi = second_stage(fr_s, fi_s)
    zr = (xr * hr - xi * hi).astype(BF16)
    zi = (xr * hi + xi * hr).astype(BF16)
    rc, rs, ic, is_ = times_cs(zr, zi)
    gr = (rc - is_).reshape(cb, b, b)
    gi = (rs + ic).reshape(cb, b, b)
    ur_s[...] = (gr * tr + gi * ti).astype(BF16)
    ui_s[...] = (gi * tr - gr * ti).astype(BF16)
    inv = inv_ref[...]

    def last_stage(c, carry):
        y_ref[c] = dot(inv, jnp.concatenate([ur_s[c], ui_s[c]], axis=0))
        return carry

    lax.fori_loop(0, cb, last_stage, 0, unroll=True)


def hyena_long_conv(u_t, filt_t):
    c, l = u_t.shape
    b = FFT_B
    consts = _dft_constants()
    full = lambda a: pl.BlockSpec(a.shape, lambda i: (0,) * a.ndim)
    y = pl.pallas_call(
        _hy_conv_kernel,
        grid=(c // HY_CB,),
        in_specs=[pl.BlockSpec((HY_CB, b // 2, b), lambda i: (i, 0, 0)),
                  pl.BlockSpec((HY_CB, b, b), lambda i: (i, 0, 0))] + [full(a) for a in consts],
        out_specs=pl.BlockSpec((HY_CB, b // 2, b), lambda i: (i, 0, 0)),
        out_shape=jax.ShapeDtypeStruct((c, b // 2, b), F32),
        scratch_shapes=[pltpu.VMEM((HY_CB, b, b), BF16) for _ in range(4)],
        compiler_params=_cparams("parallel"),
        name="hyena_long_conv",
    )(u_t.reshape(c, b // 2, b), filt_t.reshape(c, b, b), *consts)
    return y.reshape(c, l)


def _hy_ctx_conv_kernel(u_ref, f_ref, fwdu_ref, fwdf_ref, inv_ref, y_ref):
    dot = functools.partial(jnp.dot, preferred_element_type=F32)
    n = f_ref.shape[0]
    x = dot(fwdu_ref[...], u_ref[...].astype(BF16))
    h = dot(fwdf_ref[...], f_ref[...].astype(BF16))
    xr, xi, hr, hi = x[:n], x[n:], h[:n], h[n:]
    z = jnp.concatenate([xr * hr - xi * hi, xr * hi + xi * hr], axis=0).astype(BF16)
    y_ref[...] = dot(inv_ref[...], z)


def hyena_ctx_conv(u, filt):
    l, c = u.shape
    n = 2 * l
    idx = np.arange(n)
    ang = 2.0 * np.pi * np.outer(idx, idx) / n
    cm, sm = np.cos(ang), np.sin(ang)
    fwdf = np.concatenate([cm, -sm], axis=0)
    inv = np.concatenate([cm[:l], -sm[:l]], axis=1) / n
    consts = (jnp.asarray(fwdf[:, :l], BF16), jnp.asarray(fwdf, BF16), jnp.asarray(inv, BF16))
    tc = 256
    full = lambda a: pl.BlockSpec(a.shape, lambda i: (0,) * a.ndim)
    return pl.pallas_call(
        _hy_ctx_conv_kernel,
        grid=(c // tc,),
        in_specs=[pl.BlockSpec((l, tc), lambda i: (0, i)), pl.BlockSpec((n, tc), lambda i: (0, i))]
        + [full(a) for a in consts],
        out_specs=pl.BlockSpec((l, tc), lambda i: (0, i)),
        out_shape=jax.ShapeDtypeStruct((l, c), F32),
        compiler_params=_cparams("parallel"),
        name="hyena_ctx_conv",
    )(u, filt, *consts)


def _hy_post_kernel(y_ref, u_ref, x0_ref, b_ref, o_ref, *, transposed):
    w = y_ref[...] + u_ref[...] * b_ref[...]
    if transposed:
        w = w.T
    o_ref[...] = (w * x0_ref[...]).astype(o_ref.dtype)


def hyena_post(y, u, x0, bias, transposed):
    l, w = x0.shape
    tb = min(HY_TB, l)
    tm = pl.BlockSpec((tb, w), lambda i: (i, 0))
    yu = pl.BlockSpec((w, tb), lambda i: (0, i)) if transposed else tm
    return pl.pallas_call(
        functools.partial(_hy_post_kernel, transposed=transposed),
        grid=(l // tb,),
        in_specs=[yu, yu, tm, pl.BlockSpec(bias.shape, lambda i: (0, 0))],
        out_specs=tm,
        out_shape=jax.ShapeDtypeStruct((l, w), BF16),
        compiler_params=_cparams("parallel"),
        name="hyena_post",
    )(y, u, x0, bias)


def hyena(z, row0, l, short_w, short_b, filt, bias):
    filt_t, hb0 = hyena_filter(l, *filt)
    if l == FFT_B * FFT_B // 2:
        u_t, x0 = hyena_pre(z, row0, l, short_w, short_b, True)
        y_t = hyena_long_conv(u_t, filt_t)
        return hyena_post(y_t, u_t, x0, bias.reshape(-1, 1) + hb0, True)
    u, x0 = hyena_pre(z, row0, l, short_w, short_b, False)
    y = hyena_ctx_conv(u, filt_t.T)
    return hyena_post(y, u, x0, (bias.reshape(-1, 1) + hb0).T, False)


def kernel(x, c, ctx, c_ctx, w_mod, b_mod, norm1_w, norm2_w, ffn_w_gate, ffn_w_up, ffn_w_down, even_w_in, even_w_out, dn_conv_w, dn_a_log, dn_dt_bias, dn_onorm_w, hy_short_w, hy_short_b, hy_f_w1, hy_f_b1, hy_f_w2, hy_f_b2, hy_f_w3, hy_f_b3, hy_f_w4, hy_f_freq, hy_bias, na_w_qkv, na_rpb, na_w_out, final_norm_w):
    d = D_MODEL
    x_src = (x[0], ctx[0], 0)
    mod = modulation(jnp.concatenate([c, c_ctx[None]], axis=0), w_mod, b_mod)

    for layer in range(DEPTH):
        need_ctx = layer < DEPTH - 1
        rows = ALL_ROWS if need_ctx else SEQ
        sh1, sc1, g1, sh2, sc2, g2 = (mod[layer, :, None, j * d:(j + 1) * d] for j in range(N_MOD))
        i = layer // 2
        h = norm_mod(x_src, norm1_w[layer][None], sc1, sh1)
        if layer % 2 == 0:
            w_in = even_w_in[i]
            w_in = jnp.concatenate([w_in[:, :4 * DN_WIDTH], w_in[:, DN_IN:], w_in[:, 4 * DN_WIDTH:DN_IN],
                                    jnp.zeros((d, EVEN_PAD - EVEN_IN), F32)], axis=1).astype(BF16)
            z = projection(h, w_in, EVEN_TN, BF16)
            dn_l, dn_c = gated_deltanet(z, dn_conv_w[i], dn_a_log[i], dn_dt_bias[i], dn_onorm_w[i])
            filt = (hy_f_w1[i], hy_f_b1[i], hy_f_w2[i], hy_f_b2[i], hy_f_w3[i], hy_f_b3[i], hy_f_w4[i],
                    hy_f_freq[i])
            y_lat = (dn_l, hyena(z, 0, SEQ, hy_short_w[i], hy_short_b[i], filt, hy_bias[i]))
            y_ctx = (dn_c, hyena(z, SEQ, CTX_LEN, hy_short_w[i], hy_short_b[i], filt, hy_bias[i])) if need_ctx else y_lat
            w_out = even_w_out
        else:
            z = projection(h, na_w_qkv, QKV_TN, BF16, layer=i, scaled_cols=D_MODEL, scale=NA_DIM ** -0.5)
            ya = na_attention(z, na_bias_tiles(na_rpb[i]))
            y_lat = (ya, ya)
            if need_ctx:
                yc = ctx_attention(z)
                y_ctx = (yc, yc)
            else:
                y_ctx = y_lat
            w_out = na_w_out
        xa, h = proj_residual(y_lat, y_ctx, w_out, i, x_src, rows, g1, norm2_w[layer][None], sc2, sh2)
        xa = ffn_block(xa, h, rows, g2, ffn_w_gate, ffn_w_up, ffn_w_down, layer)
        x_src = (xa, xa, SEQ)
    return final_norm(xa, SEQ, final_norm_w[None])[None]
```

```python
import functools
import math

import jax
import jax.numpy as jnp
import numpy as np
from jax import lax
from jax.experimental import pallas as pl
from jax.experimental.pallas import tpu as pltpu

D_MODEL = 2048
SEQ = 8192
DEPTH = 4
GRID_W = 64
ROWS = SEQ // GRID_W
CTX_LEN = 256
ALL_ROWS = SEQ + CTX_LEN
NORM_EPS = 1e-6
N_MOD = 6

DN_HEADS = 8
DN_DIM = 128
DN_WIDTH = DN_HEADS * DN_DIM
DN_CONV = 5
DN_CHUNK = 64
DN_IN = 4 * DN_WIDTH + 4 * DN_HEADS

HY_WIDTH = D_MODEL - DN_WIDTH
HY_BANDS = 16
HY_DECAY_TARGET = 1e-2
HY_STRONG_DECAY_PCT = 0.3
HY_WEAK_DECAY_PCT = 1.5
HY_IN = 3 * HY_WIDTH
EVEN_IN = DN_IN + HY_IN

NA_HEADS = D_MODEL // 128
NA_DIM = 128
WIN_R = 8
WIN_C = 16
FFN_HIDDEN = -(-8 * D_MODEL // (3 * 256)) * 256

LANES = 128
VMEM_LIMIT = 56 * 1024 * 1024
NEG = -1e30

F32 = jnp.float32
BF16 = jnp.bfloat16


def _cparams(*sem):
    return pltpu.CompilerParams(dimension_semantics=sem, vmem_limit_bytes=VMEM_LIMIT)


def _sigmoid(x):
    return 1.0 / (1.0 + jnp.exp(-x))


MOD_TK = 256


def _mod_kernel(c_ref, w_ref, b_ref, o_ref):
    k = pl.program_id(1)
    n = o_ref.shape[2]

    @pl.when(k == 0)
    def _():
        o_ref[0] = jnp.broadcast_to(b_ref[0], (2, n))

    rows = []
    for r in range(2):
        c = c_ref[r, pl.ds(pl.multiple_of(k * MOD_TK, MOD_TK), MOD_TK), :]
        a = c * _sigmoid(c)
        cols = [jnp.sum(w_ref[0, :, j * LANES:(j + 1) * LANES] * a, axis=0, keepdims=True)
                for j in range(n // LANES)]
        rows.append(jnp.concatenate(cols, axis=1))
    o_ref[0] += jnp.concatenate(rows, axis=0)


def modulation(c_pair, w_mod, b_mod):
    n = N_MOD * D_MODEL
    cb = jnp.broadcast_to(c_pair[:, :, None], (2, D_MODEL, LANES))
    return pl.pallas_call(
        _mod_kernel,
        grid=(DEPTH, D_MODEL // MOD_TK),
        in_specs=[pl.BlockSpec((2, D_MODEL, LANES), lambda l, k: (0, 0, 0)),
                  pl.BlockSpec((1, MOD_TK, n), lambda l, k: (l, k, 0)),
                  pl.BlockSpec((1, 1, n), lambda l, k: (l, 0, 0))],
        out_specs=pl.BlockSpec((1, 2, n), lambda l, k: (l, 0, 0)),
        out_shape=jax.ShapeDtypeStruct((DEPTH, 2, n), F32),
        compiler_params=_cparams("parallel", "arbitrary"),
        name="modulation",
    )(cb, w_mod, b_mod.reshape(DEPTH, 1, n))


ROW_TILE = 256


def _kind_spec(d):
    return pl.BlockSpec((None, 1, d), lambda i: (jnp.where(i >= SEQ // ROW_TILE, 1, 0), 0, 0))


def _row_source_specs(src, width, col=0):
    n_lat = SEQ // ROW_TILE
    ctx_blk = src[2] // ROW_TILE
    return [pl.BlockSpec((ROW_TILE, width), lambda i: (jnp.minimum(i, n_lat - 1), col)),
            pl.BlockSpec((ROW_TILE, width), lambda i: (ctx_blk, col))]


def _pick_rows(lat_ref, ctx_ref):
    return jnp.where(pl.program_id(0) < SEQ // ROW_TILE, lat_ref[...], ctx_ref[...])


def _norm_mod_kernel(xl_ref, xc_ref, nw_ref, sc_ref, sh_ref, h_ref):
    x = _pick_rows(xl_ref, xc_ref)
    y = x * lax.rsqrt(jnp.mean(x * x, axis=-1, keepdims=True) + NORM_EPS)
    h_ref[...] = (y * nw_ref[...] * (1.0 + sc_ref[...]) + sh_ref[...]).astype(h_ref.dtype)


def norm_mod(x_src, nw, sc, sh):
    d = x_src[0].shape[1]
    return pl.pallas_call(
        _norm_mod_kernel,
        grid=(ALL_ROWS // ROW_TILE,),
        in_specs=_row_source_specs(x_src, d) + [pl.BlockSpec((1, d), lambda i: (0, 0)), _kind_spec(d), _kind_spec(d)],
        out_specs=pl.BlockSpec((ROW_TILE, d), lambda i: (i, 0)),
        out_shape=jax.ShapeDtypeStruct((ALL_ROWS, d), BF16),
        compiler_params=_cparams("parallel"),
        name="norm_mod",
    )(x_src[0], x_src[1], nw, sc, sh)


ROW_TILES = 8
FFN_UP_ROW_TILES = 4
PROJ_ROW_TILES = 8
EVEN_TN = 19 * LANES
EVEN_PAD = 3 * EVEN_TN
QKV_TN = 1024
assert EVEN_PAD >= EVEN_IN


def _weight_spec(w, tn, layer):
    d = w.shape[-2]
    if layer is None:
        return pl.BlockSpec((d, tn), lambda j, i: (0, j))
    return pl.BlockSpec((None, d, tn), lambda j, i: (layer, 0, j))


def _proj_kernel(h_ref, w_ref, o_ref, *scratch, scaled_cols, scale):
    if scratch:
        wb_ref, = scratch

        @pl.when(pl.program_id(1) == 0)
        def _():
            wb_ref[...] = w_ref[...].astype(BF16)
    else:
        wb_ref = w_ref
    y = jnp.dot(h_ref[...], wb_ref[...], preferred_element_type=F32)
    if scaled_cols:
        y = y * jnp.where(pl.program_id(0) * o_ref.shape[1] < scaled_cols, scale, 1.0)
    o_ref[...] = y.astype(o_ref.dtype)


def projection(h, w, tn, out_dtype, layer=None, scaled_cols=0, scale=1.0):
    m, d = h.shape
    n = w.shape[-1]
    tm = m // PROJ_ROW_TILES
    assert n % tn == 0 and scaled_cols % tn == 0
    scratch = [] if w.dtype == BF16 else [pltpu.VMEM((d, tn), BF16)]
    return pl.pallas_call(
        functools.partial(_proj_kernel, scaled_cols=scaled_cols, scale=scale),
        grid=(n // tn, PROJ_ROW_TILES),
        in_specs=[pl.BlockSpec((tm, d), lambda j, i: (i, 0)), _weight_spec(w, tn, layer)],
        out_specs=pl.BlockSpec((tm, tn), lambda j, i: (i, j)),
        out_shape=jax.ShapeDtypeStruct((m, n), out_dtype),
        scratch_shapes=scratch,
        compiler_params=_cparams("parallel", "arbitrary"),
        name="projection",
    )(h, w)


FFN_TF = 512
FFN_TN = 256


def _ffn_up_kernel(h_ref, wg_ref, wu_ref, a_ref, wgb_ref, wub_ref):
    @pl.when(pl.program_id(1) == 0)
    def _():
        wgb_ref[...] = wg_ref[...].astype(BF16)
        wub_ref[...] = wu_ref[...].astype(BF16)

    h = h_ref[...]
    gate = jnp.dot(h, wgb_ref[...], preferred_element_type=F32)
    up = jnp.dot(h, wub_ref[...], preferred_element_type=F32)
    a_ref[...] = (gate * _sigmoid(gate) * up).astype(a_ref.dtype)


def _ffn_down_kernel(a_ref, wd_ref, x_ref, g_ref, o_ref):
    tm = x_ref.shape[0]
    y = jnp.dot(a_ref[...], wd_ref[...].astype(BF16), preferred_element_type=F32)
    row = pl.program_id(0) * tm + lax.broadcasted_iota(jnp.int32, (tm, 1), 0)
    o_ref[...] = x_ref[...] + jnp.where(row < SEQ, g_ref[0], g_ref[1]) * y


def ffn_block(x, h, rows, g, wg, wu, wd, layer):
    m, d = x.shape
    hid = wg.shape[2]
    tm = rows // ROW_TILES
    tu = rows // FFN_UP_ROW_TILES
    a = pl.pallas_call(
        _ffn_up_kernel,
        grid=(hid // FFN_TF, FFN_UP_ROW_TILES),
        in_specs=[pl.BlockSpec((tu, d), lambda f, i: (i, 0)),
                  _weight_spec(wg, FFN_TF, layer), _weight_spec(wu, FFN_TF, layer)],
        out_specs=pl.BlockSpec((tu, FFN_TF), lambda f, i: (i, f)),
        out_shape=jax.ShapeDtypeStruct((m, hid), BF16),
        scratch_shapes=[pltpu.VMEM((d, FFN_TF), BF16), pltpu.VMEM((d, FFN_TF), BF16)],
        compiler_params=_cparams("parallel", "arbitrary"),
        name="ffn_up",
    )(h, wg, wu)
    return pl.pallas_call(
        _ffn_down_kernel,
        grid=(ROW_TILES, d // FFN_TN),
        in_specs=[pl.BlockSpec((tm, hid), lambda i, j: (i, 0)),
                  pl.BlockSpec((None, hid, FFN_TN), lambda i, j: (layer, 0, j)),
                  pl.BlockSpec((tm, FFN_TN), lambda i, j: (i, j)),
                  pl.BlockSpec((2, 1, FFN_TN), lambda i, j: (0, 0, j))],
        out_specs=pl.BlockSpec((tm, FFN_TN), lambda i, j: (i, j)),
        out_shape=jax.ShapeDtypeStruct((m, d), F32),
        compiler_params=_cparams("parallel", "arbitrary"),
        name="ffn_down",
    )(a, wd, x, g)


def _proj_res_kernel(yal_ref, yac_ref, ybl_ref, ybc_ref, xl_ref, xc_ref, w_ref, g_ref, nw_ref, sc_ref, sh_ref,
                     o_ref, h_ref, wb_ref):
    @pl.when(pl.program_id(0) == 0)
    def _():
        wb_ref[...] = w_ref[...].astype(BF16)

    ya = _pick_rows(yal_ref, yac_ref)
    yb = _pick_rows(ybl_ref, ybc_ref)
    ka = ya.shape[1]
    y = (jnp.dot(ya, wb_ref[:ka, :], preferred_element_type=F32)
         + jnp.dot(yb, wb_ref[ka:, :], preferred_element_type=F32))
    x = _pick_rows(xl_ref, xc_ref) + g_ref[...] * y
    o_ref[...] = x
    xn = x * lax.rsqrt(jnp.mean(x * x, axis=-1, keepdims=True) + NORM_EPS)
    h_ref[...] = (xn * nw_ref[...] * (1.0 + sc_ref[...]) + sh_ref[...]).astype(h_ref.dtype)


def proj_residual(y_lat, y_ctx, w, layer, x_src, rows, g, nw, sc, sh):
    d = x_src[0].shape[1]
    half = d // 2
    second = [1 if pair[1] is pair[0] else 0 for pair in (y_lat, y_ctx)]
    assert second[0] == second[1]
    row = pl.BlockSpec((ROW_TILE, d), lambda i: (i, 0))
    return pl.pallas_call(
        _proj_res_kernel,
        grid=(rows // ROW_TILE,),
        in_specs=_row_source_specs((y_lat[0], y_ctx[0], 0), half) + _row_source_specs((y_lat[1], y_ctx[1], 0), half,
                                                                                      second[0])
        + _row_source_specs(x_src, d) + [
            pl.BlockSpec((None, d, d), lambda i: (layer, 0, 0), pipeline_mode=pl.Buffered(1)),
            _kind_spec(d), pl.BlockSpec((1, d), lambda i: (0, 0)), _kind_spec(d), _kind_spec(d)],
        out_specs=[row, row],
        out_shape=[jax.ShapeDtypeStruct((ALL_ROWS, d), F32), jax.ShapeDtypeStruct((ALL_ROWS, d), BF16)],
        scratch_shapes=[pltpu.VMEM((d, d), BF16)],
        compiler_params=_cparams("arbitrary"),
        name="proj_residual",
    )(y_lat[0], y_ctx[0], y_lat[1], y_ctx[1], x_src[0], x_src[1], w, g, nw, sc, sh)


def _final_norm_kernel(x_ref, w_ref, o_ref):
    x = x_ref[...]
    o_ref[...] = x * lax.rsqrt(jnp.mean(x * x, axis=-1, keepdims=True) + NORM_EPS) * w_ref[...]


def final_norm(x, rows, w):
    d = x.shape[1]
    tm = 512
    return pl.pallas_call(
        _final_norm_kernel,
        grid=(rows // tm,),
        in_specs=[pl.BlockSpec((tm, d), lambda i: (i, 0)), pl.BlockSpec((1, d), lambda i: (0, 0))],
        out_specs=pl.BlockSpec((tm, d), lambda i: (i, 0)),
        out_shape=jax.ShapeDtypeStruct((rows, d), F32),
        compiler_params=_cparams("parallel"),
        name="final_norm",
    )(x, w)


NA_R = 4
NA_KD = (NA_R + WIN_R) // 2
NA_T = 2 * WIN_R
NA_HP = 4


def _na_bias_kernel(rpb_ref, o_ref):
    c = lax.broadcasted_iota(jnp.int32, (GRID_W, LANES), 0)
    lane = lax.broadcasted_iota(jnp.int32, (GRID_W, LANES), 1)
    left = lane < GRID_W
    kc = jnp.where(left, lane, lane - GRID_W)
    cs = jnp.clip(c - WIN_C // 2, 0, GRID_W - WIN_C)
    inwin = (kc >= cs) & (kc < cs + WIN_C)
    n_ro = 2 * WIN_R - 1

    def toeplitz(ro):
        if ro < 0 or ro >= n_ro:
            return jnp.full((GRID_W, LANES), NEG, F32)
        row = jnp.broadcast_to(rpb_ref[0, ro:ro + 1, :], (GRID_W, LANES))
        return pltpu.roll(pltpu.roll(row, LANES - (WIN_C - 1), 1), 0, 1, stride=1, stride_axis=0)

    for t in range(NA_T):
        tile = jnp.where(left, toeplitz(t - 1), pltpu.roll(toeplitz(t), GRID_W, 1))
        o_ref[0, t] = jnp.where(inwin, tile, NEG)


def na_bias_tiles(rpb):
    padded = jnp.pad(rpb, ((0, 0), (0, 1), (0, LANES - rpb.shape[2])))
    return pl.pallas_call(
        _na_bias_kernel,
        grid=(NA_HEADS,),
        in_specs=[pl.BlockSpec((1, NA_T, LANES), lambda h: (h, 0, 0))],
        out_specs=pl.BlockSpec((1, NA_T, GRID_W, LANES), lambda h: (h, 0, 0, 0)),
        out_shape=jax.ShapeDtypeStruct((NA_HEADS, NA_T, GRID_W, LANES), F32),
        compiler_params=_cparams("parallel"),
        name="na_bias_tiles",
    )(padded)


def _na_kernel(q_ref, k_ref, v_ref, kc_ref, vc_ref, tt_ref, o_ref):
    r0 = pl.program_id(1) * NA_R
    ks = jnp.clip(r0 - WIN_R // 2, 0, ROWS - 2 * NA_KD)
    span = pl.ds(pl.multiple_of(ks * GRID_W, LANES), NA_KD * LANES)
    heads = range(NA_HP)
    cols = [slice(h * NA_DIM, (h + 1) * NA_DIM) for h in heads]
    contract_last = (((1,), (1,)), ((), ()))
    q = [q_ref[:, cols[h]] for h in heads]
    s_loc = [lax.dot_general(q[h], k_ref[span, cols[h]], contract_last, preferred_element_type=F32) for h in heads]
    s_ctx = [lax.dot_general(q[h], kc_ref[:, cols[h]], contract_last, preferred_element_type=F32) for h in heads]

    left = lax.broadcasted_iota(jnp.int32, (GRID_W, LANES), 1) < GRID_W
    tile_sel = []
    for qi in range(NA_R):
        r = r0 + qi
        rs = jnp.clip(r - WIN_R // 2, 0, ROWS - WIN_R)
        for dj in range(NA_KD):
            kl = ks + 2 * dj
            vl = ((kl >= rs) & (kl < rs + WIN_R)).astype(jnp.int32)
            vr = ((kl + 1 >= rs) & (kl + 1 < rs + WIN_R)).astype(jnp.int32)
            tile_sel.append((jnp.clip(kl - r + WIN_R, 0, NA_T - 1), jnp.where(left, vl, vr) > 0))
    for h in heads:
        rows = [jnp.concatenate([jnp.where(valid, tt_ref[h, t], NEG)
                                 for t, valid in tile_sel[qi * NA_KD:(qi + 1) * NA_KD]], axis=1)
                for qi in range(NA_R)]
        s_loc[h] = s_loc[h] + jnp.concatenate(rows, axis=0)

    m = [jnp.maximum(jnp.max(s_loc[h], axis=-1, keepdims=True), jnp.max(s_ctx[h], axis=-1, keepdims=True))
         for h in heads]
    p_loc = [jnp.exp(s_loc[h] - m[h]) for h in heads]
    p_ctx = [jnp.exp(s_ctx[h] - m[h]) for h in heads]
    denom = [jnp.sum(p_loc[h], axis=-1, keepdims=True) + jnp.sum(p_ctx[h], axis=-1, keepdims=True) for h in heads]
    o = [jnp.dot(p_loc[h].astype(BF16), v_ref[span, cols[h]], preferred_element_type=F32)
         + jnp.dot(p_ctx[h].astype(BF16), vc_ref[:, cols[h]], preferred_element_type=F32) for h in heads]
    for h in heads:
        o_ref[:, cols[h]] = (o[h] / denom[h]).astype(o_ref.dtype)


def na_attention(z, tiles):
    tq = NA_R * GRID_W
    w = NA_HP * NA_DIM
    ng = NA_HEADS // NA_HP
    return pl.pallas_call(
        _na_kernel,
        grid=(ng, ROWS // NA_R),
        in_specs=[pl.BlockSpec((tq, w), lambda g, i: (i, g)),
                  pl.BlockSpec((SEQ, w), lambda g, i: (0, ng + g)),
                  pl.BlockSpec((SEQ, w), lambda g, i: (0, 2 * ng + g)),
                  pl.BlockSpec((CTX_LEN, w), lambda g, i: (SEQ // CTX_LEN, ng + g)),
                  pl.BlockSpec((CTX_LEN, w), lambda g, i: (SEQ // CTX_LEN, 2 * ng + g)),
                  pl.BlockSpec((NA_HP, NA_T, GRID_W, LANES), lambda g, i: (g, 0, 0, 0))],
        out_specs=pl.BlockSpec((tq, w), lambda g, i: (i, g)),
        out_shape=jax.ShapeDtypeStruct((SEQ, D_MODEL), BF16),
        compiler_params=_cparams("parallel", "arbitrary"),
        name="na_attention",
    )(z, z, z, z, z, tiles)


def _ctx_attn_kernel(q_ref, k_ref, v_ref, o_ref):
    s = lax.dot_general(q_ref[...], k_ref[...], (((1,), (1,)), ((), ())), preferred_element_type=F32)
    p = jnp.exp(s - jnp.max(s, axis=-1, keepdims=True))
    o = jnp.dot(p.astype(BF16), v_ref[...], preferred_element_type=F32)
    o_ref[...] = (o / jnp.sum(p, axis=-1, keepdims=True)).astype(o_ref.dtype)


def ctx_attention(z):
    nh = NA_HEADS
    blk = lambda off: pl.BlockSpec((CTX_LEN, NA_DIM), lambda h: (SEQ // CTX_LEN, off + h))
    return pl.pallas_call(
        _ctx_attn_kernel,
        grid=(nh,),
        in_specs=[blk(0), blk(nh), blk(2 * nh)],
        out_specs=pl.BlockSpec((CTX_LEN, NA_DIM), lambda h: (0, h)),
        out_shape=jax.ShapeDtypeStruct((CTX_LEN, D_MODEL), BF16),
        compiler_params=_cparams("parallel"),
        name="ctx_attention",
    )(z, z, z)


DN_SC = 256
HALO = 16
N_GB = 2 * DN_HEADS


def _softplus(x):
    return jnp.maximum(x, 0.0) + jnp.log1p(jnp.exp(-jnp.abs(x)))


def _short_conv(main_ref, prev_ref, next_ref, w_ref, col0, first, last):
    ksize = w_ref.shape[0]
    tb, width = main_ref.shape
    prev = jnp.where(first, 0.0, prev_ref[...].astype(F32))
    nxt = jnp.where(last, 0.0, next_ref[...].astype(F32))
    xf = jnp.concatenate([prev, main_ref[...].astype(F32), nxt], axis=0)
    n = xf.shape[0]
    acc = None
    for j in range(ksize):
        sh = (ksize // 2 - j) % n
        xs = xf if sh == 0 else pltpu.roll(xf, sh, 0)
        term = xs[HALO:HALO + tb] * w_ref[j:j + 1, col0:col0 + width]
        acc = term if acc is None else acc + term
    return acc


def _chunk_masks(n):
    r = lax.broadcasted_iota(jnp.int32, (n, n), 0)
    c = lax.broadcasted_iota(jnp.int32, (n, n), 1)
    return r, c, (r // DN_CHUNK) == (c // DN_CHUNK)


def _dn_scalars_kernel(za_ref, zb_ref, zat_ref, alog_ref, dtb_ref, alogt_ref, dtbt_ref,
                       gc_ref, gt_ref, beta_ref, gct_ref):
    hi = lax.Precision.HIGHEST
    g = -jnp.exp(alog_ref[...]) * _softplus(za_ref[...] + dtb_ref[...])
    beta_ref[...] = _sigmoid(zb_ref[...])
    r, c, same = _chunk_masks(DN_SC)
    lo = (same & (c <= r)).astype(F32)
    up = (same & (c >= r)).astype(F32)
    fwd_col = lax.broadcasted_iota(jnp.int32, (DN_SC, N_GB), 1) < DN_HEADS
    gc_ref[...] = jnp.where(fwd_col, jnp.dot(lo, g, precision=hi), jnp.dot(up, g, precision=hi))
    gt_ref[...] = jnp.dot(same.astype(F32), g, precision=hi)
    g_t = -jnp.exp(alogt_ref[...]) * _softplus(zat_ref[...] + dtbt_ref[...])
    fwd_row = lax.broadcasted_iota(jnp.int32, (N_GB, DN_SC), 0) < DN_HEADS
    gct_ref[...] = jnp.where(fwd_row, jnp.dot(g_t, up, precision=hi), jnp.dot(g_t, lo, precision=hi))


def dn_scalars(za, zb, a_log, dt_bias):
    l = za.shape[0]
    col = pl.BlockSpec((DN_SC, N_GB), lambda i: (i, 0))
    row = pl.BlockSpec((N_GB, DN_SC), lambda i: (0, i))
    prow = pl.BlockSpec((1, N_GB), lambda i: (0, 0))
    pcol = pl.BlockSpec((N_GB, 1), lambda i: (0, 0))
    cshape = jax.ShapeDtypeStruct((l, N_GB), F32)
    return pl.pallas_call(
        _dn_scalars_kernel,
        grid=(l // DN_SC,),
        in_specs=[col, col, row, prow, prow, pcol, pcol],
        out_specs=[col, col, col, row],
        out_shape=[cshape, cshape, cshape, jax.ShapeDtypeStruct((N_GB, l), F32)],
        compiler_params=_cparams("parallel"),
        name="dn_scalars",
    )(za, zb, za.T, a_log.reshape(1, N_GB), dt_bias.reshape(1, N_GB), a_log.reshape(N_GB, 1),
      dt_bias.reshape(N_GB, 1))


def _dn_prep_kernel(zq_ref, zqp_ref, zqn_ref, zk_ref, zkp_ref, zkn_ref, zv_ref, zvp_ref, zvn_ref, cw_ref, o_ref):
    i = pl.program_id(0)
    n_lat = SEQ // DN_SC
    first = (i == 0) | (i == n_lat)
    last = (i == n_lat - 1) | (i == pl.num_programs(0) - 1)
    parts = ((zq_ref, zqp_ref, zqn_ref), (zk_ref, zkp_ref, zkn_ref), (zv_ref, zvp_ref, zvn_ref))
    for part, refs in enumerate(parts):
        acc = _short_conv(*refs, cw_ref, part * DN_WIDTH, first, last)
        x = acc * _sigmoid(acc)
        for h in range(DN_HEADS):
            xh = x[:, h * DN_DIM:(h + 1) * DN_DIM]
            if part < 2:
                xh = xh * (lax.rsqrt(jnp.sum(xh * xh, axis=-1, keepdims=True) + NORM_EPS)
                           * (DN_DIM ** -0.5 if part == 0 else 1.0))
            o_ref[:, part * DN_WIDTH + h * DN_DIM:part * DN_WIDTH + (h + 1) * DN_DIM] = xh


def dn_prep(z, conv_w):
    m = z.shape[0]
    per = DN_SC // HALO
    last_halo = m // HALO - 1

    def part_specs(part):
        return [pl.BlockSpec((DN_SC, DN_WIDTH), lambda i: (i, part)),
                pl.BlockSpec((HALO, DN_WIDTH), lambda i: (jnp.maximum(i * per - 1, 0), part)),
                pl.BlockSpec((HALO, DN_WIDTH), lambda i: (jnp.minimum((i + 1) * per, last_halo), part))]

    return pl.pallas_call(
        _dn_prep_kernel,
        grid=(m // DN_SC,),
        in_specs=part_specs(0) + part_specs(1) + part_specs(2) + [
            pl.BlockSpec((DN_CONV, 3 * DN_WIDTH), lambda i: (0, 0))],
        out_specs=pl.BlockSpec((DN_SC, 3 * DN_WIDTH), lambda i: (i, 0)),
        out_shape=jax.ShapeDtypeStruct((m, 3 * DN_WIDTH), F32),
        compiler_params=_cparams("parallel"),
        name="dn_prep",
    )(z, z, z, z, z, z, z, z, z, conv_w)


def _gdn_kernel(q_ref, k_ref, v_ref, gc_ref, gt_ref, beta_ref, gct_ref, s0_ref, *rest, bwd, nblk, fuse):
    if fuse:
        other_ref, gate_ref, nw_ref, o_ref, sfin_ref, state_ref = rest
    else:
        o_ref, sfin_ref, state_ref = rest
    i = pl.program_id(0)

    @pl.when(i == 0)
    def _():
        state_ref[...] = s0_ref[...]

    r, c, same64 = _chunk_masks(DN_SC)
    tri = same64 & ((c >= r) if bwd else (c <= r))
    offdiag = r != c
    same16 = (r // 16) == (c // 16)
    same32 = (r // 32) == (c // 32)
    contract_last = (((1,), (1,)), ((), ()))
    contract_first = (((0,), (0,)), ((), ()))
    nc = DN_SC // DN_CHUNK
    dot = functools.partial(jnp.dot, preferred_element_type=F32)

    heads = range(DN_HEADS)
    col0 = DN_HEADS if bwd else 0
    gc_c = [gc_ref[:, col0 + h:col0 + h + 1] for h in heads]
    gt_c = [gt_ref[:, col0 + h:col0 + h + 1] for h in heads]
    beta_c = [beta_ref[:, col0 + h:col0 + h + 1] for h in heads]
    q, k, v, k16, dec, qk16, m, p, e = ([None] * DN_HEADS for _ in range(9))
    for h in heads:
        sl = slice(h * DN_DIM, (h + 1) * DN_DIM)
        q[h], k[h], v[h] = q_ref[:, sl], k_ref[:, sl], v_ref[:, sl]
        k16[h] = k[h].astype(BF16)
        dec[h] = jnp.exp(jnp.where(tri, gc_c[h] - gct_ref[col0 + h:col0 + h + 1, :], NEG))
    for h in heads:
        kk = lax.dot_general(k16[h], k16[h], contract_last, preferred_element_type=F32)
        m[h] = jnp.where(offdiag, kk * beta_c[h] * dec[h], 0.0)
        qk16[h] = (lax.dot_general(q[h].astype(BF16), k16[h], contract_last, preferred_element_type=F32)
                   * dec[h]).astype(BF16)

    for h in heads:
        p[h] = jnp.where(same16, -m[h], 0.0)
        e[h] = p[h]
    p16 = [p[h].astype(BF16) for h in heads]
    for _ in range(3):
        for h in heads:
            p[h] = dot(p16[h], p16[h])
            p16[h] = p[h].astype(BF16)
        for h in heads:
            e[h] = e[h] + p[h] + dot(e[h].astype(BF16), p16[h])
    for inner, outer in ((same16, same32), (same32, None)):
        y = [None] * DN_HEADS
        for h in heads:
            cm = jnp.where(~inner if outer is None else (outer & ~inner), m[h], 0.0)
            y[h] = cm + dot(cm.astype(BF16), e[h].astype(BF16))
        for h in heads:
            e[h] = e[h] - y[h] - dot(e[h].astype(BF16), y[h].astype(BF16))

    u, w16, qg16, kd16, s = ([None] * DN_HEADS for _ in range(5))
    for h in heads:
        eg = jnp.exp(gc_c[h])
        rhs = jnp.concatenate([v[h] * beta_c[h], k[h] * (beta_c[h] * eg)], axis=1)
        uw = rhs + dot(e[h].astype(BF16), rhs.astype(BF16))
        u[h] = uw[:, :DN_DIM]
        w16[h] = uw[:, DN_DIM:].astype(BF16)
        qg16[h] = (q[h] * eg).astype(BF16)
        kd16[h] = (k[h] * jnp.exp(gt_c[h] - gc_c[h])).astype(BF16)
        s[h] = state_ref[h]

    outs = [[None] * nc for _ in heads]
    for ci in (range(nc - 1, -1, -1) if bwd else range(nc)):
        rows = slice(ci * DN_CHUNK, (ci + 1) * DN_CHUNK)
        ws = [dot(jnp.concatenate([w16[h][rows], qg16[h][rows]], axis=0), s[h].astype(BF16)) for h in heads]
        vn16 = [(u[h][rows] - ws[h][:DN_CHUNK]).astype(BF16) for h in heads]
        for h in heads:
            outs[h][ci] = ws[h][DN_CHUNK:] + dot(qk16[h][rows, ci * DN_CHUNK:(ci + 1) * DN_CHUNK], vn16[h])
        for h in heads:
            gl = jnp.exp(gt_c[h][ci * DN_CHUNK:ci * DN_CHUNK + 1, :])
            s[h] = s[h] * gl + lax.dot_general(kd16[h][rows], vn16[h], contract_first, preferred_element_type=F32)
    for h in heads:
        state_ref[h] = s[h]
        sl = slice(h * DN_DIM, (h + 1) * DN_DIM)
        o = jnp.concatenate(outs[h], axis=0)
        if fuse:
            o = o + other_ref[:, sl]
            o = o * lax.rsqrt(jnp.mean(o * o, axis=-1, keepdims=True) + NORM_EPS) * nw_ref[...]
            gate = gate_ref[:, sl].astype(F32)
            o = o * (gate * _sigmoid(gate))
        o_ref[:, sl] = o.astype(o_ref.dtype)

    @pl.when(i == nblk - 1)
    def _():
        sfin_ref[...] = state_ref[...]


def gdn_scan(qkv, gc, gt, beta, gct, s0, bwd, row0, l, fuse=None):
    nblk = l // DN_SC
    blk0 = row0 // DN_SC

    def rel(i):
        return nblk - 1 - i if bwd else i

    def b(i):
        return blk0 + rel(i)

    def part_spec(part):
        return pl.BlockSpec((DN_SC, DN_WIDTH), lambda i: (b(i), part))

    col = pl.BlockSpec((DN_SC, N_GB), lambda i: (b(i), 0))
    st = pl.BlockSpec((DN_HEADS, DN_DIM, DN_DIM), lambda i: (0, 0, 0))
    out_blk = pl.BlockSpec((DN_SC, DN_WIDTH), lambda i: (rel(i), 0))
    extra_specs, extra_args = [], []
    if fuse is not None:
        o_other, z, onorm_w = fuse
        extra_specs = [out_blk, part_spec(3), pl.BlockSpec((1, DN_DIM), lambda i: (0, 0))]
        extra_args = [o_other, z, onorm_w.reshape(1, DN_DIM)]
    return pl.pallas_call(
        functools.partial(_gdn_kernel, bwd=bwd, nblk=nblk, fuse=fuse is not None),
        grid=(nblk,),
        in_specs=[part_spec(0), part_spec(1), part_spec(2), col, col, col,
                  pl.BlockSpec((N_GB, DN_SC), lambda i: (0, b(i))), st] + extra_specs,
        out_specs=[out_blk, st],
        out_shape=[jax.ShapeDtypeStruct((l, DN_WIDTH), F32 if fuse is None else BF16),
                   jax.ShapeDtypeStruct((DN_HEADS, DN_DIM, DN_DIM), F32)],
        scratch_shapes=[pltpu.VMEM((DN_HEADS, DN_DIM, DN_DIM), F32)],
        compiler_params=_cparams("arbitrary"),
        name="gdn_scan_bwd" if bwd else "gdn_scan_fwd",
    )(qkv, qkv, qkv, gc, gt, beta, gct, s0, *extra_args)


def gated_deltanet(z, conv_w, a_log, dt_bias, onorm_w):
    a0, b0 = EVEN_IN - 2 * N_GB, EVEN_IN - N_GB
    scal = dn_scalars(z[:, a0:b0].astype(F32), z[:, b0:b0 + N_GB].astype(F32), a_log, dt_bias)
    qkv = dn_prep(z, conv_w)
    s0 = jnp.zeros((DN_HEADS, DN_DIM, DN_DIM), F32)
    oc_f, s_ctx = gdn_scan(qkv, *scal, s0, False, SEQ, CTX_LEN)
    ol_f, _ = gdn_scan(qkv, *scal, s_ctx, False, 0, SEQ)
    y_ctx, s_ctx = gdn_scan(qkv, *scal, s0, True, SEQ, CTX_LEN, fuse=(oc_f, z, onorm_w))
    y_lat, _ = gdn_scan(qkv, *scal, s_ctx, True, 0, SEQ, fuse=(ol_f, z, onorm_w))
    return y_lat, y_ctx


HY_SHORT = 3
HY_EMB = 1 + 2 * HY_BANDS
HY_EMB_PAD = 40
HY_TB = 256
FFT_B = 128
HY_CB = 32
HY_COL0 = 4


def _hy_pre_kernel(x0_ref, x0p_ref, x0n_ref, x1_ref, x1p_ref, x1n_ref, v_ref, vp_ref, vn_ref, w_ref, b_ref,
                   u_ref, x0c_ref, *, nblk, transpose_u):
    first = pl.program_id(0) == 0
    last = pl.program_id(0) == nblk - 1
    w = HY_WIDTH
    x0 = _short_conv(x0_ref, x0p_ref, x0n_ref, w_ref, 0, first, last) + b_ref[:, :w]
    x1 = _short_conv(x1_ref, x1p_ref, x1n_ref, w_ref, w, first, last) + b_ref[:, w:2 * w]
    v = _short_conv(v_ref, vp_ref, vn_ref, w_ref, 2 * w, first, last) + b_ref[:, 2 * w:]
    u = v * x1
    x0c_ref[...] = x0.astype(x0c_ref.dtype)
    u_ref[...] = u.T if transpose_u else u


def hyena_pre(z, row0, l, short_w, short_b, transpose_u):
    tb = min(HY_TB, l)
    nblk = l // tb
    per = tb // HALO
    blk0 = row0 // tb
    last_halo = z.shape[0] // HALO - 1
    w = HY_WIDTH

    def part_specs(part):
        cb = HY_COL0 + part
        return [pl.BlockSpec((tb, w), lambda i: (blk0 + i, cb)),
                pl.BlockSpec((HALO, w), lambda i: (jnp.maximum((blk0 + i) * per - 1, 0), cb)),
                pl.BlockSpec((HALO, w), lambda i: (jnp.minimum((blk0 + i + 1) * per, last_halo), cb))]

    u_shape, u_spec = ((w, l), pl.BlockSpec((w, tb), lambda i: (0, i))) if transpose_u else (
        (l, w), pl.BlockSpec((tb, w), lambda i: (i, 0)))
    return pl.pallas_call(
        functools.partial(_hy_pre_kernel, nblk=nblk, transpose_u=transpose_u),
        grid=(nblk,),
        in_specs=part_specs(0) + part_specs(1) + part_specs(2) + [
            pl.BlockSpec((HY_SHORT, HY_IN), lambda i: (0, 0)), pl.BlockSpec((1, HY_IN), lambda i: (0, 0))],
        out_specs=[u_spec, pl.BlockSpec((tb, w), lambda i: (i, 0))],
        out_shape=[jax.ShapeDtypeStruct(u_shape, F32), jax.ShapeDtypeStruct((l, w), BF16)],
        compiler_params=_cparams("parallel"),
        name="hyena_pre",
    )(z, z, z, z, z, z, z, z, z, short_w, short_b.reshape(1, HY_IN))


def _hy_filter_kernel(w1t_ref, b1_ref, w2t_ref, b2_ref, w3t_ref, b3_ref, w4t_ref, fr_ref, band_ref, dl_ref,
                      f_ref, hb0_ref, *, l, fb):
    hi = lax.Precision.HIGHEST
    j = pl.program_id(0)
    second = j >= l // fb
    n = j * fb + lax.broadcasted_iota(jnp.int32, (1, fb), 1)
    pos = jnp.where(second, 2 * l - n, n).astype(F32)
    t = pos / max(l - 1, 1)
    wpos = 2.0 * math.pi * pos / l
    arg = band_ref[...] * wpos
    row = lax.broadcasted_iota(jnp.int32, (HY_EMB_PAD, fb), 0)
    feat = jnp.where(row == 0, t, jnp.where(row <= HY_BANDS, jnp.cos(arg),
                                            jnp.where(row <= 2 * HY_BANDS, -jnp.sin(arg), 0.0)))
    fr = fr_ref[...]
    h = jnp.sin(fr * (jnp.dot(w1t_ref[...], feat, precision=hi) + b1_ref[...]))
    h = jnp.sin(fr * (jnp.dot(w2t_ref[...], h, precision=hi) + b2_ref[...]))
    h = jnp.sin(fr * (jnp.dot(w3t_ref[...], h, precision=hi) + b3_ref[...]))
    window = jnp.exp(-t * dl_ref[...])
    half = pl.multiple_of(jnp.where(second, HY_WIDTH, 0), HY_WIDTH)
    h16 = h.astype(BF16)
    f = jnp.dot(w4t_ref[pl.ds(half, HY_WIDTH), :].astype(BF16), h16, preferred_element_type=F32) * window
    f_ref[...] = jnp.where(n == l, 0.0, f)

    @pl.when(j == 0)
    def _():
        hb0_ref[...] = (jnp.dot(w4t_ref[HY_WIDTH:, :].astype(BF16), h16[:, :LANES], preferred_element_type=F32)
                        * window[:, :LANES])


def hyena_filter(l, w1, b1, w2, b2, w3, b3, w4, freq):
    fb = min(1024, l)
    colv = lambda v: v.reshape(-1, 1)
    bands = np.zeros((HY_EMB_PAD, 1), np.float32)
    bands[1:1 + HY_BANDS, 0] = bands[1 + HY_BANDS:HY_EMB, 0] = np.linspace(1e-4, HY_BANDS - 1, HY_BANDS,
                                                                            dtype=np.float32)
    min_decay = math.log(HY_DECAY_TARGET) / HY_WEAK_DECAY_PCT
    max_decay = math.log(HY_DECAY_TARGET) / HY_STRONG_DECAY_PCT
    deltas = np.abs(np.linspace(min_decay, max_decay, HY_WIDTH, dtype=np.float32)).reshape(-1, 1)
    w1t = jnp.pad(w1.T, ((0, 0), (0, HY_EMB_PAD - HY_EMB)))
    full = lambda a: pl.BlockSpec(a.shape, lambda j: (0,) * a.ndim)
    args = (w1t, colv(b1), w2.T, colv(b2), w3.T, colv(b3), w4.T, colv(freq), jnp.asarray(bands), jnp.asarray(deltas))
    filt, hb0 = pl.pallas_call(
        functools.partial(_hy_filter_kernel, l=l, fb=fb),
        grid=(2 * l // fb,),
        in_specs=[full(a) for a in args],
        out_specs=[pl.BlockSpec((HY_WIDTH, fb), lambda j: (0, j)), pl.BlockSpec((HY_WIDTH, LANES), lambda j: (0, 0))],
        out_shape=[jax.ShapeDtypeStruct((HY_WIDTH, 2 * l), F32), jax.ShapeDtypeStruct((HY_WIDTH, LANES), F32)],
        compiler_params=_cparams("arbitrary"),
        name="hyena_filter",
    )(*args)
    return filt, hb0[:, :1]


def _dft_constants():
    b = FFT_B
    n = b * b
    idx = np.arange(b)
    ang = 2.0 * np.pi * np.outer(idx, idx) / b
    c, s = np.cos(ang), np.sin(ang)
    tw = 2.0 * np.pi * np.outer(idx, idx) / n
    fwd_b = np.concatenate([c, -s], axis=0)
    cs = np.concatenate([c, s], axis=1)
    inv_b = np.concatenate([c[:b // 2], -s[:b // 2]], axis=1) / n
    return (jnp.asarray(fwd_b, BF16), jnp.asarray(cs, BF16), jnp.asarray(np.cos(tw), F32),
            jnp.asarray(-np.sin(tw), F32), jnp.asarray(inv_b, BF16))


def _hy_conv_kernel(u_ref, f_ref, fwd_ref, cs_ref, tr_ref, ti_ref, inv_ref, y_ref, ur_s, ui_s, fr_s, fi_s):
    cb = u_ref.shape[0]
    b = FFT_B
    m = cb * b
    dot = functools.partial(jnp.dot, preferred_element_type=F32)
    fwd = fwd_ref[...]
    fwd_half = fwd[:, :b // 2]
    tr, ti = tr_ref[...], ti_ref[...]

    def first_stage(c, carry):
        for src, lhs, re_s, im_s in ((u_ref, fwd_half, ur_s, ui_s), (f_ref, fwd, fr_s, fi_s)):
            p = dot(lhs, src[c].astype(BF16))
            pr, pi = p[:b], p[b:]
            re_s[c] = (pr * tr - pi * ti).astype(BF16)
            im_s[c] = (pr * ti + pi * tr).astype(BF16)
        return carry

    lax.fori_loop(0, cb, first_stage, 0, unroll=True)

    cs = cs_ref[...]

    def times_cs(re, im):
        big = dot(jnp.concatenate([re, im], axis=0), cs)
        return big[:m, :b], big[:m, b:], big[m:, :b], big[m:, b:]

    def second_stage(re_s, im_s):
        rc, rs, ic, is_ = times_cs(re_s[...].reshape(m, b), im_s[...].reshape(m, b))
        return rc + is_, ic - rs

    xr, xi = second_stage(ur_s, ui_s)
    hr, hi = second_stage(fr_s, fi_s)
    zr = (xr * hr - xi * hi).astype(BF16)
    zi = (xr * hi + xi * hr).astype(BF16)
    rc, rs, ic, is_ = times_cs(zr, zi)
    gr = (rc - is_).reshape(cb, b, b)
    gi = (rs + ic).reshape(cb, b, b)
    ur_s[...] = (gr * tr + gi * ti).astype(BF16)
    ui_s[...] = (gi * tr - gr * ti).astype(BF16)
    inv = inv_ref[...]

    def last_stage(c, carry):
        y_ref[c] = dot(inv, jnp.concatenate([ur_s[c], ui_s[c]], axis=0))
        return carry

    lax.fori_loop(0, cb, last_stage, 0, unroll=True)


def hyena_long_conv(u_t, filt_t):
    c, l = u_t.shape
    b = FFT_B
    consts = _dft_constants()
    full = lambda a: pl.BlockSpec(a.shape, lambda i: (0,) * a.ndim)
    y = pl.pallas_call(
        _hy_conv_kernel,
        grid=(c // HY_CB,),
        in_specs=[pl.BlockSpec((HY_CB, b // 2, b), lambda i: (i, 0, 0)),
                  pl.BlockSpec((HY_CB, b, b), lambda i: (i, 0, 0))] + [full(a) for a in consts],
        out_specs=pl.BlockSpec((HY_CB, b // 2, b), lambda i: (i, 0, 0)),
        out_shape=jax.ShapeDtypeStruct((c, b // 2, b), F32),
        scratch_shapes=[pltpu.VMEM((HY_CB, b, b), BF16) for _ in range(4)],
        compiler_params=_cparams("parallel"),
        name="hyena_long_conv",
    )(u_t.reshape(c, b // 2, b), filt_t.reshape(c, b, b), *consts)
    return y.reshape(c, l)


def _hy_ctx_conv_kernel(u_ref, f_ref, fwdu_ref, fwdf_ref, inv_ref, y_ref):
    dot = functools.partial(jnp.dot, preferred_element_type=F32)
    n = f_ref.shape[0]
    x = dot(fwdu_ref[...], u_ref[...].astype(BF16))
    h = dot(fwdf_ref[...], f_ref[...].astype(BF16))
    xr, xi, hr, hi = x[:n], x[n:], h[:n], h[n:]
    z = jnp.concatenate([xr * hr - xi * hi, xr * hi + xi * hr], axis=0).astype(BF16)
    y_ref[...] = dot(inv_ref[...], z)


def hyena_ctx_conv(u, filt):
    l, c = u.shape
    n = 2 * l
    idx = np.arange(n)
    ang = 2.0 * np.pi * np.outer(idx, idx) / n
    cm, sm = np.cos(ang), np.sin(ang)
    fwdf = np.concatenate([cm, -sm], axis=0)
    inv = np.concatenate([cm[:l], -sm[:l]], axis=1) / n
    consts = (jnp.asarray(fwdf[:, :l], BF16), jnp.asarray(fwdf, BF16), jnp.asarray(inv, BF16))
    tc = 256
    full = lambda a: pl.BlockSpec(a.shape, lambda i: (0,) * a.ndim)
    return pl.pallas_call(
        _hy_ctx_conv_kernel,
        grid=(c // tc,),
        in_specs=[pl.BlockSpec((l, tc), lambda i: (0, i)), pl.BlockSpec((n, tc), lambda i: (0, i))]
        + [full(a) for a in consts],
        out_specs=pl.BlockSpec((l, tc), lambda i: (0, i)),
        out_shape=jax.ShapeDtypeStruct((l, c), F32),
        compiler_params=_cparams("parallel"),
        name="hyena_ctx_conv",
    )(u, filt, *consts)


def _hy_post_kernel(y_ref, u_ref, x0_ref, b_ref, o_ref, *, transposed):
    w = y_ref[...] + u_ref[...] * b_ref[...]
    if transposed:
        w = w.T
    o_ref[...] = (w * x0_ref[...]).astype(o_ref.dtype)


def hyena_post(y, u, x0, bias, transposed):
    l, w = x0.shape
    tb = min(HY_TB, l)
    tm = pl.BlockSpec((tb, w), lambda i: (i, 0))
    yu = pl.BlockSpec((w, tb), lambda i: (0, i)) if transposed else tm
    return pl.pallas_call(
        functools.partial(_hy_post_kernel, transposed=transposed),
        grid=(l // tb,),
        in_specs=[yu, yu, tm, pl.BlockSpec(bias.shape, lambda i: (0, 0))],
        out_specs=tm,
        out_shape=jax.ShapeDtypeStruct((l, w), BF16),
        compiler_params=_cparams("parallel"),
        name="hyena_post",
    )(y, u, x0, bias)


def hyena(z, row0, l, short_w, short_b, filt, bias):
    filt_t, hb0 = hyena_filter(l, *filt)
    if l == FFT_B * FFT_B // 2:
        u_t, x0 = hyena_pre(z, row0, l, short_w, short_b, True)
        y_t = hyena_long_conv(u_t, filt_t)
        return hyena_post(y_t, u_t, x0, bias.reshape(-1, 1) + hb0, True)
    u, x0 = hyena_pre(z, row0, l, short_w, short_b, False)
    y = hyena_ctx_conv(u, filt_t.T)
    return hyena_post(y, u, x0, (bias.reshape(-1, 1) + hb0).T, False)


def kernel(x, c, ctx, c_ctx, w_mod, b_mod, norm1_w, norm2_w, ffn_w_gate, ffn_w_up, ffn_w_down, even_w_in, even_w_out, dn_conv_w, dn_a_log, dn_dt_bias, dn_onorm_w, hy_short_w, hy_short_b, hy_f_w1, hy_f_b1, hy_f_w2, hy_f_b2, hy_f_w3, hy_f_b3, hy_f_w4, hy_f_freq, hy_bias, na_w_qkv, na_rpb, na_w_out, final_norm_w):
    d = D_MODEL
    x_src = (x[0], ctx[0], 0)
    mod = modulation(jnp.concatenate([c, c_ctx[None]], axis=0), w_mod, b_mod)

    for layer in range(DEPTH):
        need_ctx = layer < DEPTH - 1
        rows = ALL_ROWS if need_ctx else SEQ
        sh1, sc1, g1, sh2, sc2, g2 = (mod[layer, :, None, j * d:(j + 1) * d] for j in range(N_MOD))
        i = layer // 2
        h = norm_mod(x_src, norm1_w[layer][None], sc1, sh1)
        if layer % 2 == 0:
            w_in = even_w_in[i]
            w_in = jnp.concatenate([w_in[:, :4 * DN_WIDTH], w_in[:, DN_IN:], w_in[:, 4 * DN_WIDTH:DN_IN],
                                    jnp.zeros((d, EVEN_PAD - EVEN_IN), F32)], axis=1).astype(BF16)
            z = projection(h, w_in, EVEN_TN, BF16)
            dn_l, dn_c = gated_deltanet(z, dn_conv_w[i], dn_a_log[i], dn_dt_bias[i], dn_onorm_w[i])
            filt = (hy_f_w1[i], hy_f_b1[i], hy_f_w2[i], hy_f_b2[i], hy_f_w3[i], hy_f_b3[i], hy_f_w4[i],
                    hy_f_freq[i])
            y_lat = (dn_l, hyena(z, 0, SEQ, hy_short_w[i], hy_short_b[i], filt, hy_bias[i]))
            y_ctx = (dn_c, hyena(z, SEQ, CTX_LEN, hy_short_w[i], hy_short_b[i], filt, hy_bias[i])) if need_ctx else y_lat
            w_out = even_w_out
        else:
            z = projection(h, na_w_qkv, QKV_TN, BF16, layer=i, scaled_cols=D_MODEL, scale=NA_DIM ** -0.5)
            ya = na_attention(z, na_bias_tiles(na_rpb[i]))
            y_lat = (ya, ya)
            if need_ctx:
                yc = ctx_attention(z)
                y_ctx = (yc, yc)
            else:
                y_ctx = y_lat
            w_out = na_w_out
        xa, h = proj_residual(y_lat, y_ctx, w_out, i, x_src, rows, g1, norm2_w[layer][None], sc2, sh2)
        xa = ffn_block(xa, h, rows, g2, ffn_w_gate, ffn_w_up, ffn_w_down, layer)
        x_src = (xa, xa, SEQ)
    return final_norm(xa, SEQ, final_norm_w[None])[None]
```

```python
import functools
import math

import jax
import jax.numpy as jnp
import numpy as np
from jax import lax
from jax.experimental import pallas as pl
from jax.experimental.pallas import tpu as pltpu

D_MODEL = 2048
SEQ = 8192
DEPTH = 4
GRID_W = 64
ROWS = SEQ // GRID_W
CTX_LEN = 256
ALL_ROWS = SEQ + CTX_LEN
NORM_EPS = 1e-6
N_MOD = 6

DN_HEADS = 8
DN_DIM = 128
DN_WIDTH = DN_HEADS * DN_DIM
DN_CONV = 5
DN_CHUNK = 64
DN_IN = 4 * DN_WIDTH + 4 * DN_HEADS

HY_WIDTH = D_MODEL - DN_WIDTH
HY_BANDS = 16
HY_DECAY_TARGET = 1e-2
HY_STRONG_DECAY_PCT = 0.3
HY_WEAK_DECAY_PCT = 1.5
HY_IN = 3 * HY_WIDTH
EVEN_IN = DN_IN + HY_IN

NA_HEADS = D_MODEL // 128
NA_DIM = 128
WIN_R = 8
WIN_C = 16
FFN_HIDDEN = -(-8 * D_MODEL // (3 * 256)) * 256

LANES = 128
VMEM_LIMIT = 56 * 1024 * 1024
NEG = -1e30

F32 = jnp.float32
BF16 = jnp.bfloat16


def _cparams(*sem):
    return pltpu.CompilerParams(dimension_semantics=sem, vmem_limit_bytes=VMEM_LIMIT)


def _sigmoid(x):
    return 1.0 / (1.0 + jnp.exp(-x))


MOD_TK = 256


def _mod_kernel(c_ref, w_ref, b_ref, o_ref):
    k = pl.program_id(1)
    n = o_ref.shape[2]

    @pl.when(k == 0)
    def _():
        o_ref[0] = jnp.broadcast_to(b_ref[0], (2, n))

    rows = []
    for r in range(2):
        c = c_ref[r, pl.ds(pl.multiple_of(k * MOD_TK, MOD_TK), MOD_TK), :]
        a = c * _sigmoid(c)
        cols = [jnp.sum(w_ref[0, :, j * LANES:(j + 1) * LANES] * a, axis=0, keepdims=True)
                for j in range(n // LANES)]
        rows.append(jnp.concatenate(cols, axis=1))
    o_ref[0] += jnp.concatenate(rows, axis=0)


def modulation(c_pair, w_mod, b_mod):
    n = N_MOD * D_MODEL
    cb = jnp.broadcast_to(c_pair[:, :, None], (2, D_MODEL, LANES))
    return pl.pallas_call(
        _mod_kernel,
        grid=(DEPTH, D_MODEL // MOD_TK),
        in_specs=[pl.BlockSpec((2, D_MODEL, LANES), lambda l, k: (0, 0, 0)),
                  pl.BlockSpec((1, MOD_TK, n), lambda l, k: (l, k, 0)),
                  pl.BlockSpec((1, 1, n), lambda l, k: (l, 0, 0))],
        out_specs=pl.BlockSpec((1, 2, n), lambda l, k: (l, 0, 0)),
        out_shape=jax.ShapeDtypeStruct((DEPTH, 2, n), F32),
        compiler_params=_cparams("parallel", "arbitrary"),
        name="modulation",
    )(cb, w_mod, b_mod.reshape(DEPTH, 1, n))


ROW_TILE = 256


def _kind_spec(d):
    return pl.BlockSpec((None, 1, d), lambda i: (jnp.where(i >= SEQ // ROW_TILE, 1, 0), 0, 0))


def _row_source_specs(src, width, col=0):
    n_lat = SEQ // ROW_TILE
    ctx_blk = src[2] // ROW_TILE
    return [pl.BlockSpec((ROW_TILE, width), lambda i: (jnp.minimum(i, n_lat - 1), col)),
            pl.BlockSpec((ROW_TILE, width), lambda i: (ctx_blk, col))]


def _pick_rows(lat_ref, ctx_ref):
    return jnp.where(pl.program_id(0) < SEQ // ROW_TILE, lat_ref[...], ctx_ref[...])


def _norm_mod_kernel(xl_ref, xc_ref, nw_ref, sc_ref, sh_ref, h_ref):
    x = _pick_rows(xl_ref, xc_ref)
    y = x * lax.rsqrt(jnp.mean(x * x, axis=-1, keepdims=True) + NORM_EPS)
    h_ref[...] = (y * nw_ref[...] * (1.0 + sc_ref[...]) + sh_ref[...]).astype(h_ref.dtype)


def norm_mod(x_src, nw, sc, sh):
    d = x_src[0].shape[1]
    return pl.pallas_call(
        _norm_mod_kernel,
        grid=(ALL_ROWS // ROW_TILE,),
        in_specs=_row_source_specs(x_src, d) + [pl.BlockSpec((1, d), lambda i: (0, 0)), _kind_spec(d), _kind_spec(d)],
        out_specs=pl.BlockSpec((ROW_TILE, d), lambda i: (i, 0)),
        out_shape=jax.ShapeDtypeStruct((ALL_ROWS, d), BF16),
        compiler_params=_cparams("parallel"),
        name="norm_mod",
    )(x_src[0], x_src[1], nw, sc, sh)


ROW_TILES = 8
FFN_UP_ROW_TILES = 4
PROJ_ROW_TILES = 8
EVEN_TN = 19 * LANES
EVEN_PAD = 3 * EVEN_TN
QKV_TN = 1024
assert EVEN_PAD >= EVEN_IN


def _weight_spec(w, tn, layer):
    d = w.shape[-2]
    if layer is None:
        return pl.BlockSpec((d, tn), lambda j, i: (0, j))
    return pl.BlockSpec((None, d, tn), lambda j, i: (layer, 0, j))


def _proj_kernel(h_ref, w_ref, o_ref, *scratch, scaled_cols, scale):
    if scratch:
        wb_ref, = scratch

        @pl.when(pl.program_id(1) == 0)
        def _():
            wb_ref[...] = w_ref[...].astype(BF16)
    else:
        wb_ref = w_ref
    y = jnp.dot(h_ref[...], wb_ref[...], preferred_element_type=F32)
    if scaled_cols:
        y = y * jnp.where(pl.program_id(0) * o_ref.shape[1] < scaled_cols, scale, 1.0)
    o_ref[...] = y.astype(o_ref.dtype)


def projection(h, w, tn, out_dtype, layer=None, scaled_cols=0, scale=1.0):
    m, d = h.shape
    n = w.shape[-1]
    tm = m // PROJ_ROW_TILES
    assert n % tn == 0 and scaled_cols % tn == 0
    scratch = [] if w.dtype == BF16 else [pltpu.VMEM((d, tn), BF16)]
    return pl.pallas_call(
        functools.partial(_proj_kernel, scaled_cols=scaled_cols, scale=scale),
        grid=(n // tn, PROJ_ROW_TILES),
        in_specs=[pl.BlockSpec((tm, d), lambda j, i: (i, 0)), _weight_spec(w, tn, layer)],
        out_specs=pl.BlockSpec((tm, tn), lambda j, i: (i, j)),
        out_shape=jax.ShapeDtypeStruct((m, n), out_dtype),
        scratch_shapes=scratch,
        compiler_params=_cparams("parallel", "arbitrary"),
        name="projection",
    )(h, w)


FFN_TF = 512
FFN_TN = 256


def _ffn_up_kernel(h_ref, wg_ref, wu_ref, a_ref, wgb_ref, wub_ref):
    @pl.when(pl.program_id(1) == 0)
    def _():
        wgb_ref[...] = wg_ref[...].astype(BF16)
        wub_ref[...] = wu_ref[...].astype(BF16)

    h = h_ref[...]
    gate = jnp.dot(h, wgb_ref[...], preferred_element_type=F32)
    up = jnp.dot(h, wub_ref[...], preferred_element_type=F32)
    a_ref[...] = (gate * _sigmoid(gate) * up).astype(a_ref.dtype)


def _ffn_down_kernel(a_ref, wd_ref, x_ref, g_ref, o_ref):
    tm = x_ref.shape[0]
    y = jnp.dot(a_ref[...], wd_ref[...].astype(BF16), preferred_element_type=F32)
    row = pl.program_id(0) * tm + lax.broadcasted_iota(jnp.int32, (tm, 1), 0)
    o_ref[...] = x_ref[...] + jnp.where(row < SEQ, g_ref[0], g_ref[1]) * y


def ffn_block(x, h, rows, g, wg, wu, wd, layer):
    m, d = x.shape
    hid = wg.shape[2]
    tm = rows // ROW_TILES
    tu = rows // FFN_UP_ROW_TILES
    a = pl.pallas_call(
        _ffn_up_kernel,
        grid=(hid // FFN_TF, FFN_UP_ROW_TILES),
        in_specs=[pl.BlockSpec((tu, d), lambda f, i: (i, 0)),
                  _weight_spec(wg, FFN_TF, layer), _weight_spec(wu, FFN_TF, layer)],
        out_specs=pl.BlockSpec((tu, FFN_TF), lambda f, i: (i, f)),
        out_shape=jax.ShapeDtypeStruct((m, hid), BF16),
        scratch_shapes=[pltpu.VMEM((d, FFN_TF), BF16), pltpu.VMEM((d, FFN_TF), BF16)],
        compiler_params=_cparams("parallel", "arbitrary"),
        name="ffn_up",
    )(h, wg, wu)
    return pl.pallas_call(
        _ffn_down_kernel,
        grid=(ROW_TILES, d // FFN_TN),
        in_specs=[pl.BlockSpec((tm, hid), lambda i, j: (i, 0)),
                  pl.BlockSpec((None, hid, FFN_TN), lambda i, j: (layer, 0, j)),
                  pl.BlockSpec((tm, FFN_TN), lambda i, j: (i, j)),
                  pl.BlockSpec((2, 1, FFN_TN), lambda i, j: (0, 0, j))],
        out_specs=pl.BlockSpec((tm, FFN_TN), lambda i, j: (i, j)),
        out_shape=jax.ShapeDtypeStruct((m, d), F32),
        compiler_params=_cparams("parallel", "arbitrary"),
        name="ffn_down",
    )(a, wd, x, g)


def _proj_res_kernel(yal_ref, yac_ref, ybl_ref, ybc_ref, xl_ref, xc_ref, w_ref, g_ref, nw_ref, sc_ref, sh_ref,
                     o_ref, h_ref, wb_ref):
    @pl.when(pl.program_id(0) == 0)
    def _():
        wb_ref[...] = w_ref[...].astype(BF16)

    ya = _pick_rows(yal_ref, yac_ref)
    yb = _pick_rows(ybl_ref, ybc_ref)
    ka = ya.shape[1]
    y = (jnp.dot(ya, wb_ref[:ka, :], preferred_element_type=F32)
         + jnp.dot(yb, wb_ref[ka:, :], preferred_element_type=F32))
    x = _pick_rows(xl_ref, xc_ref) + g_ref[...] * y
    o_ref[...] = x
    xn = x * lax.rsqrt(jnp.mean(x * x, axis=-1, keepdims=True) + NORM_EPS)
    h_ref[...] = (xn * nw_ref[...] * (1.0 + sc_ref[...]) + sh_ref[...]).astype(h_ref.dtype)


def proj_residual(y_lat, y_ctx, w, layer, x_src, rows, g, nw, sc, sh):
    d = x_src[0].shape[1]
    half = d // 2
    second = [1 if pair[1] is pair[0] else 0 for pair in (y_lat, y_ctx)]
    assert second[0] == second[1]
    row = pl.BlockSpec((ROW_TILE, d), lambda i: (i, 0))
    return pl.pallas_call(
        _proj_res_kernel,
        grid=(rows // ROW_TILE,),
        in_specs=_row_source_specs((y_lat[0], y_ctx[0], 0), half) + _row_source_specs((y_lat[1], y_ctx[1], 0), half,
                                                                                      second[0])
        + _row_source_specs(x_src, d) + [
            pl.BlockSpec((None, d, d), lambda i: (layer, 0, 0), pipeline_mode=pl.Buffered(1)),
            _kind_spec(d), pl.BlockSpec((1, d), lambda i: (0, 0)), _kind_spec(d), _kind_spec(d)],
        out_specs=[row, row],
        out_shape=[jax.ShapeDtypeStruct((ALL_ROWS, d), F32), jax.ShapeDtypeStruct((ALL_ROWS, d), BF16)],
        scratch_shapes=[pltpu.VMEM((d, d), BF16)],
        compiler_params=_cparams("arbitrary"),
        name="proj_residual",
    )(y_lat[0], y_ctx[0], y_lat[1], y_ctx[1], x_src[0], x_src[1], w, g, nw, sc, sh)


def _final_norm_kernel(x_ref, w_ref, o_ref):
    x = x_ref[...]
    o_ref[...] = x * lax.rsqrt(jnp.mean(x * x, axis=-1, keepdims=True) + NORM_EPS) * w_ref[...]


def final_norm(x, rows, w):
    d = x.shape[1]
    tm = 512
    return pl.pallas_call(
        _final_norm_kernel,
        grid=(rows // tm,),
        in_specs=[pl.BlockSpec((tm, d), lambda i: (i, 0)), pl.BlockSpec((1, d), lambda i: (0, 0))],
        out_specs=pl.BlockSpec((tm, d), lambda i: (i, 0)),
        out_shape=jax.ShapeDtypeStruct((rows, d), F32),
        compiler_params=_cparams("parallel"),
        name="final_norm",
    )(x, w)


NA_R = 4
NA_KD = (NA_R + WIN_R) // 2
NA_T = 2 * WIN_R
NA_HP = 4


def _na_bias_kernel(rpb_ref, o_ref):
    c = lax.broadcasted_iota(jnp.int32, (GRID_W, LANES), 0)
    lane = lax.broadcasted_iota(jnp.int32, (GRID_W, LANES), 1)
    left = lane < GRID_W
    kc = jnp.where(left, lane, lane - GRID_W)
    cs = jnp.clip(c - WIN_C // 2, 0, GRID_W - WIN_C)
    inwin = (kc >= cs) & (kc < cs + WIN_C)
    n_ro = 2 * WIN_R - 1

    def toeplitz(ro):
        if ro < 0 or ro >= n_ro:
            return jnp.full((GRID_W, LANES), NEG, F32)
        row = jnp.broadcast_to(rpb_ref[0, ro:ro + 1, :], (GRID_W, LANES))
        return pltpu.roll(pltpu.roll(row, LANES - (WIN_C - 1), 1), 0, 1, stride=1, stride_axis=0)

    for t in range(NA_T):
        tile = jnp.where(left, toeplitz(t - 1), pltpu.roll(toeplitz(t), GRID_W, 1))
        o_ref[0, t] = jnp.where(inwin, tile, NEG)


def na_bias_tiles(rpb):
    padded = jnp.pad(rpb, ((0, 0), (0, 1), (0, LANES - rpb.shape[2])))
    return pl.pallas_call(
        _na_bias_kernel,
        grid=(NA_HEADS,),
        in_specs=[pl.BlockSpec((1, NA_T, LANES), lambda h: (h, 0, 0))],
        out_specs=pl.BlockSpec((1, NA_T, GRID_W, LANES), lambda h: (h, 0, 0, 0)),
        out_shape=jax.ShapeDtypeStruct((NA_HEADS, NA_T, GRID_W, LANES), F32),
        compiler_params=_cparams("parallel"),
        name="na_bias_tiles",
    )(padded)


def _na_kernel(q_ref, k_ref, v_ref, kc_ref, vc_ref, tt_ref, o_ref):
    r0 = pl.program_id(1) * NA_R
    ks = jnp.clip(r0 - WIN_R // 2, 0, ROWS - 2 * NA_KD)
    span = pl.ds(pl.multiple_of(ks * GRID_W, LANES), NA_KD * LANES)
    heads = range(NA_HP)
    cols = [slice(h * NA_DIM, (h + 1) * NA_DIM) for h in heads]
    contract_last = (((1,), (1,)), ((), ()))
    q = [q_ref[:, cols[h]] for h in heads]
    s_loc = [lax.dot_general(q[h], k_ref[span, cols[h]], contract_last, preferred_element_type=F32) for h in heads]
    s_ctx = [lax.dot_general(q[h], kc_ref[:, cols[h]], contract_last, preferred_element_type=F32) for h in heads]

    left = lax.broadcasted_iota(jnp.int32, (GRID_W, LANES), 1) < GRID_W
    tile_sel = []
    for qi in range(NA_R):
        r = r0 + qi
        rs = jnp.clip(r - WIN_R // 2, 0, ROWS - WIN_R)
        for dj in range(NA_KD):
            kl = ks + 2 * dj
            vl = ((kl >= rs) & (kl < rs + WIN_R)).astype(jnp.int32)
            vr = ((kl + 1 >= rs) & (kl + 1 < rs + WIN_R)).astype(jnp.int32)
            tile_sel.append((jnp.clip(kl - r + WIN_R, 0, NA_T - 1), jnp.where(left, vl, vr) > 0))
    for h in heads:
        rows = [jnp.concatenate([jnp.where(valid, tt_ref[h, t], NEG)
                                 for t, valid in tile_sel[qi * NA_KD:(qi + 1) * NA_KD]], axis=1)
                for qi in range(NA_R)]
        s_loc[h] = s_loc[h] + jnp.concatenate(rows, axis=0)

    m = [jnp.maximum(jnp.max(s_loc[h], axis=-1, keepdims=True), jnp.max(s_ctx[h], axis=-1, keepdims=True))
         for h in heads]
    p_loc = [jnp.exp(s_loc[h] - m[h]) for h in heads]
    p_ctx = [jnp.exp(s_ctx[h] - m[h]) for h in heads]
    denom = [jnp.sum(p_loc[h], axis=-1, keepdims=True) + jnp.sum(p_ctx[h], axis=-1, keepdims=True) for h in heads]
    o = [jnp.dot(p_loc[h].astype(BF16), v_ref[span, cols[h]], preferred_element_type=F32)
         + jnp.dot(p_ctx[h].astype(BF16), vc_ref[:, cols[h]], preferred_element_type=F32) for h in heads]
    for h in heads:
        o_ref[:, cols[h]] = (o[h] / denom[h]).astype(o_ref.dtype)


def na_attention(z, tiles):
    tq = NA_R * GRID_W
    w = NA_HP * NA_DIM
    ng = NA_HEADS // NA_HP
    return pl.pallas_call(
        _na_kernel,
        grid=(ng, ROWS // NA_R),
        in_specs=[pl.BlockSpec((tq, w), lambda g, i: (i, g)),
                  pl.BlockSpec((SEQ, w), lambda g, i: (0, ng + g)),
                  pl.BlockSpec((SEQ, w), lambda g, i: (0, 2 * ng + g)),
                  pl.BlockSpec((CTX_LEN, w), lambda g, i: (SEQ // CTX_LEN, ng + g)),
                  pl.BlockSpec((CTX_LEN, w), lambda g, i: (SEQ // CTX_LEN, 2 * ng + g)),
                  pl.BlockSpec((NA_HP, NA_T, GRID_W, LANES), lambda g, i: (g, 0, 0, 0))],
        out_specs=pl.BlockSpec((tq, w), lambda g, i: (i, g)),
        out_shape=jax.ShapeDtypeStruct((SEQ, D_MODEL), BF16),
        compiler_params=_cparams("parallel", "arbitrary"),
        name="na_attention",
    )(z, z, z, z, z, tiles)


def _ctx_attn_kernel(q_ref, k_ref, v_ref, o_ref):
    s = lax.dot_general(q_ref[...], k_ref[...], (((1,), (1,)), ((), ())), preferred_element_type=F32)
    p = jnp.exp(s - jnp.max(s, axis=-1, keepdims=True))
    o = jnp.dot(p.astype(BF16), v_ref[...], preferred_element_type=F32)
    o_ref[...] = (o / jnp.sum(p, axis=-1, keepdims=True)).astype(o_ref.dtype)


def ctx_attention(z):
    nh = NA_HEADS
    blk = lambda off: pl.BlockSpec((CTX_LEN, NA_DIM), lambda h: (SEQ // CTX_LEN, off + h))
    return pl.pallas_call(
        _ctx_attn_kernel,
        grid=(nh,),
        in_specs=[blk(0), blk(nh), blk(2 * nh)],
        out_specs=pl.BlockSpec((CTX_LEN, NA_DIM), lambda h: (0, h)),
        out_shape=jax.ShapeDtypeStruct((CTX_LEN, D_MODEL), BF16),
        compiler_params=_cparams("parallel"),
        name="ctx_attention",
    )(z, z, z)


DN_SC = 256
HALO = 16
N_GB = 2 * DN_HEADS


def _softplus(x):
    return jnp.maximum(x, 0.0) + jnp.log1p(jnp.exp(-jnp.abs(x)))


def _short_conv(main_ref, prev_ref, next_ref, w_ref, col0, first, last):
    ksize = w_ref.shape[0]
    tb, width = main_ref.shape
    prev = jnp.where(first, 0.0, prev_ref[...].astype(F32))
    nxt = jnp.where(last, 0.0, next_ref[...].astype(F32))
    xf = jnp.concatenate([prev, main_ref[...].astype(F32), nxt], axis=0)
    n = xf.shape[0]
    acc = None
    for j in range(ksize):
        sh = (ksize // 2 - j) % n
        xs = xf if sh == 0 else pltpu.roll(xf, sh, 0)
        term = xs[HALO:HALO + tb] * w_ref[j:j + 1, col0:col0 + width]
        acc = term if acc is None else acc + term
    return acc


def _chunk_masks(n):
    r = lax.broadcasted_iota(jnp.int32, (n, n), 0)
    c = lax.broadcasted_iota(jnp.int32, (n, n), 1)
    return r, c, (r // DN_CHUNK) == (c // DN_CHUNK)


def _dn_scalars_kernel(za_ref, zb_ref, zat_ref, alog_ref, dtb_ref, alogt_ref, dtbt_ref,
                       gc_ref, gt_ref, beta_ref, gct_ref):
    hi = lax.Precision.HIGHEST
    g = -jnp.exp(alog_ref[...]) * _softplus(za_ref[...] + dtb_ref[...])
    beta_ref[...] = _sigmoid(zb_ref[...])
    r, c, same = _chunk_masks(DN_SC)
    lo = (same & (c <= r)).astype(F32)
    up = (same & (c >= r)).astype(F32)
    fwd_col = lax.broadcasted_iota(jnp.int32, (DN_SC, N_GB), 1) < DN_HEADS
    gc_ref[...] = jnp.where(fwd_col, jnp.dot(lo, g, precision=hi), jnp.dot(up, g, precision=hi))
    gt_ref[...] = jnp.dot(same.astype(F32), g, precision=hi)
    g_t = -jnp.exp(alogt_ref[...]) * _softplus(zat_ref[...] + dtbt_ref[...])
    fwd_row = lax.broadcasted_iota(jnp.int32, (N_GB, DN_SC), 0) < DN_HEADS
    gct_ref[...] = jnp.where(fwd_row, jnp.dot(g_t, up, precision=hi), jnp.dot(g_t, lo, precision=hi))


def dn_scalars(za, zb, a_log, dt_bias):
    l = za.shape[0]
    col = pl.BlockSpec((DN_SC, N_GB), lambda i: (i, 0))
    row = pl.BlockSpec((N_GB, DN_SC), lambda i: (0, i))
    prow = pl.BlockSpec((1, N_GB), lambda i: (0, 0))
    pcol = pl.BlockSpec((N_GB, 1), lambda i: (0, 0))
    cshape = jax.ShapeDtypeStruct((l, N_GB), F32)
    return pl.pallas_call(
        _dn_scalars_kernel,
        grid=(l // DN_SC,),
        in_specs=[col, col, row, prow, prow, pcol, pcol],
        out_specs=[col, col, col, row],
        out_shape=[cshape, cshape, cshape, jax.ShapeDtypeStruct((N_GB, l), F32)],
        compiler_params=_cparams("parallel"),
        name="dn_scalars",
    )(za, zb, za.T, a_log.reshape(1, N_GB), dt_bias.reshape(1, N_GB), a_log.reshape(N_GB, 1),
      dt_bias.reshape(N_GB, 1))


def _dn_prep_kernel(zq_ref, zqp_ref, zqn_ref, zk_ref, zkp_ref, zkn_ref, zv_ref, zvp_ref, zvn_ref, cw_ref, o_ref):
    i = pl.program_id(0)
    n_lat = SEQ // DN_SC
    first = (i == 0) | (i == n_lat)
    last = (i == n_lat - 1) | (i == pl.num_programs(0) - 1)
    parts = ((zq_ref, zqp_ref, zqn_ref), (zk_ref, zkp_ref, zkn_ref), (zv_ref, zvp_ref, zvn_ref))
    for part, refs in enumerate(parts):
        acc = _short_conv(*refs, cw_ref, part * DN_WIDTH, first, last)
        x = acc * _sigmoid(acc)
        for h in range(DN_HEADS):
            xh = x[:, h * DN_DIM:(h + 1) * DN_DIM]
            if part < 2:
                xh = xh * (lax.rsqrt(jnp.sum(xh * xh, axis=-1, keepdims=True) + NORM_EPS)
                           * (DN_DIM ** -0.5 if part == 0 else 1.0))
            o_ref[:, part * DN_WIDTH + h * DN_DIM:part * DN_WIDTH + (h + 1) * DN_DIM] = xh


def dn_prep(z, conv_w):
    m = z.shape[0]
    per = DN_SC // HALO
    last_halo = m // HALO - 1

    def part_specs(part):
        return [pl.BlockSpec((DN_SC, DN_WIDTH), lambda i: (i, part)),
                pl.BlockSpec((HALO, DN_WIDTH), lambda i: (jnp.maximum(i * per - 1, 0), part)),
                pl.BlockSpec((HALO, DN_WIDTH), lambda i: (jnp.minimum((i + 1) * per, last_halo), part))]

    return pl.pallas_call(
        _dn_prep_kernel,
        grid=(m // DN_SC,),
        in_specs=part_specs(0) + part_specs(1) + part_specs(2) + [
            pl.BlockSpec((DN_CONV, 3 * DN_WIDTH), lambda i: (0, 0))],
        out_specs=pl.BlockSpec((DN_SC, 3 * DN_WIDTH), lambda i: (i, 0)),
        out_shape=jax.ShapeDtypeStruct((m, 3 * DN_WIDTH), F32),
        compiler_params=_cparams("parallel"),
        name="dn_prep",
    )(z, z, z, z, z, z, z, z, z, conv_w)


def _gdn_kernel(q_ref, k_ref, v_ref, gc_ref, gt_ref, beta_ref, gct_ref, s0_ref, *rest, bwd, nblk, fuse):
    if fuse:
        other_ref, gate_ref, nw_ref, o_ref, sfin_ref, state_ref = rest
    else:
        o_ref, sfin_ref, state_ref = rest
    i = pl.program_id(0)

    @pl.when(i == 0)
    def _():
        state_ref[...] = s0_ref[...]

    r, c, same64 = _chunk_masks(DN_SC)
    tri = same64 & ((c >= r) if bwd else (c <= r))
    offdiag = r != c
    same16 = (r // 16) == (c // 16)
    same32 = (r // 32) == (c // 32)
    contract_last = (((1,), (1,)), ((), ()))
    contract_first = (((0,), (0,)), ((), ()))
    nc = DN_SC // DN_CHUNK
    dot = functools.partial(jnp.dot, preferred_element_type=F32)

    def dot3(a, b):
        a_hi, b_hi = a.astype(BF16), b.astype(BF16)
        a_lo = (a - a_hi.astype(F32)).astype(BF16)
        b_lo = (b - b_hi.astype(F32)).astype(BF16)
        return dot(a_hi, b_hi) + dot(a_hi, b_lo) + dot(a_lo, b_hi)

    heads = range(DN_HEADS)
    col0 = DN_HEADS if bwd else 0
    gc_c = [gc_ref[:, col0 + h:col0 + h + 1] for h in heads]
    gt_c = [gt_ref[:, col0 + h:col0 + h + 1] for h in heads]
    beta_c = [beta_ref[:, col0 + h:col0 + h + 1] for h in heads]
    q, k, v, k16, dec, qk16, m, p, e = ([None] * DN_HEADS for _ in range(9))
    for h in heads:
        sl = slice(h * DN_DIM, (h + 1) * DN_DIM)
        q[h], k[h], v[h] = q_ref[:, sl], k_ref[:, sl], v_ref[:, sl]
        k16[h] = k[h].astype(BF16)
        dec[h] = jnp.exp(jnp.where(tri, gc_c[h] - gct_ref[col0 + h:col0 + h + 1, :], NEG))
    for h in heads:
        kk = lax.dot_general(k16[h], k16[h], contract_last, preferred_element_type=F32)
        m[h] = jnp.where(offdiag, kk * beta_c[h] * dec[h], 0.0)
        qk16[h] = (lax.dot_general(q[h].astype(BF16), k16[h], contract_last, preferred_element_type=F32)
                   * dec[h]).astype(BF16)

    for h in heads:
        p[h] = jnp.where(same16, -m[h], 0.0)
        e[h] = p[h]
    for _ in range(3):
        for h in heads:
            p[h] = dot3(p[h], p[h])
        for h in heads:
            e[h] = e[h] + p[h] + dot3(e[h], p[h])
    for inner, outer in ((same16, same32), (same32, None)):
        y = [None] * DN_HEADS
        for h in heads:
            cm = jnp.where(~inner if outer is None else (outer & ~inner), m[h], 0.0)
            y[h] = cm + dot3(cm, e[h])
        for h in heads:
            e[h] = e[h] - y[h] - dot3(e[h], y[h])

    u, w16, qg16, kd16, s = ([None] * DN_HEADS for _ in range(5))
    for h in heads:
        eg = jnp.exp(gc_c[h])
        rhs = jnp.concatenate([v[h] * beta_c[h], k[h] * (beta_c[h] * eg)], axis=1)
        uw = rhs + dot3(e[h], rhs)
        u[h] = uw[:, :DN_DIM]
        w16[h] = uw[:, DN_DIM:].astype(BF16)
        qg16[h] = (q[h] * eg).astype(BF16)
        kd16[h] = (k[h] * jnp.exp(gt_c[h] - gc_c[h])).astype(BF16)
        s[h] = state_ref[h]

    outs = [[None] * nc for _ in heads]
    for ci in (range(nc - 1, -1, -1) if bwd else range(nc)):
        rows = slice(ci * DN_CHUNK, (ci + 1) * DN_CHUNK)
        ws = [dot(jnp.concatenate([w16[h][rows], qg16[h][rows]], axis=0), s[h].astype(BF16)) for h in heads]
        vn16 = [(u[h][rows] - ws[h][:DN_CHUNK]).astype(BF16) for h in heads]
        for h in heads:
            outs[h][ci] = ws[h][DN_CHUNK:] + dot(qk16[h][rows, ci * DN_CHUNK:(ci + 1) * DN_CHUNK], vn16[h])
        for h in heads:
            gl = jnp.exp(gt_c[h][ci * DN_CHUNK:ci * DN_CHUNK + 1, :])
            s[h] = s[h] * gl + lax.dot_general(kd16[h][rows], vn16[h], contract_first, preferred_element_type=F32)
    for h in heads:
        state_ref[h] = s[h]
        sl = slice(h * DN_DIM, (h + 1) * DN_DIM)
        o = jnp.concatenate(outs[h], axis=0)
        if fuse:
            o = o + other_ref[:, sl]
            o = o * lax.rsqrt(jnp.mean(o * o, axis=-1, keepdims=True) + NORM_EPS) * nw_ref[...]
            gate = gate_ref[:, sl].astype(F32)
            o = o * (gate * _sigmoid(gate))
        o_ref[:, sl] = o.astype(o_ref.dtype)

    @pl.when(i == nblk - 1)
    def _():
        sfin_ref[...] = state_ref[...]


def gdn_scan(qkv, gc, gt, beta, gct, s0, bwd, row0, l, fuse=None):
    nblk = l // DN_SC
    blk0 = row0 // DN_SC

    def rel(i):
        return nblk - 1 - i if bwd else i

    def b(i):
        return blk0 + rel(i)

    def part_spec(part):
        return pl.BlockSpec((DN_SC, DN_WIDTH), lambda i: (b(i), part))

    col = pl.BlockSpec((DN_SC, N_GB), lambda i: (b(i), 0))
    st = pl.BlockSpec((DN_HEADS, DN_DIM, DN_DIM), lambda i: (0, 0, 0))
    out_blk = pl.BlockSpec((DN_SC, DN_WIDTH), lambda i: (rel(i), 0))
    extra_specs, extra_args = [], []
    if fuse is not None:
        o_other, z, onorm_w = fuse
        extra_specs = [out_blk, part_spec(3), pl.BlockSpec((1, DN_DIM), lambda i: (0, 0))]
        extra_args = [o_other, z, onorm_w.reshape(1, DN_DIM)]
    return pl.pallas_call(
        functools.partial(_gdn_kernel, bwd=bwd, nblk=nblk, fuse=fuse is not None),
        grid=(nblk,),
        in_specs=[part_spec(0), part_spec(1), part_spec(2), col, col, col,
                  pl.BlockSpec((N_GB, DN_SC), lambda i: (0, b(i))), st] + extra_specs,
        out_specs=[out_blk, st],
        out_shape=[jax.ShapeDtypeStruct((l, DN_WIDTH), F32 if fuse is None else BF16),
                   jax.ShapeDtypeStruct((DN_HEADS, DN_DIM, DN_DIM), F32)],
        scratch_shapes=[pltpu.VMEM((DN_HEADS, DN_DIM, DN_DIM), F32)],
        compiler_params=_cparams("arbitrary"),
        name="gdn_scan_bwd" if bwd else "gdn_scan_fwd",
    )(qkv, qkv, qkv, gc, gt, beta, gct, s0, *extra_args)


def gated_deltanet(z, conv_w, a_log, dt_bias, onorm_w):
    a0, b0 = EVEN_IN - 2 * N_GB, EVEN_IN - N_GB
    scal = dn_scalars(z[:, a0:b0].astype(F32), z[:, b0:b0 + N_GB].astype(F32), a_log, dt_bias)
    qkv = dn_prep(z, conv_w)
    s0 = jnp.zeros((DN_HEADS, DN_DIM, DN_DIM), F32)
    oc_f, s_ctx = gdn_scan(qkv, *scal, s0, False, SEQ, CTX_LEN)
    ol_f, _ = gdn_scan(qkv, *scal, s_ctx, False, 0, SEQ)
    y_ctx, s_ctx = gdn_scan(qkv, *scal, s0, True, SEQ, CTX_LEN, fuse=(oc_f, z, onorm_w))
    y_lat, _ = gdn_scan(qkv, *scal, s_ctx, True, 0, SEQ, fuse=(ol_f, z, onorm_w))
    return y_lat, y_ctx


HY_SHORT = 3
HY_EMB = 1 + 2 * HY_BANDS
HY_EMB_PAD = 40
HY_TB = 256
FFT_B = 128
HY_CB = 32
HY_COL0 = 4


def _hy_pre_kernel(x0_ref, x0p_ref, x0n_ref, x1_ref, x1p_ref, x1n_ref, v_ref, vp_ref, vn_ref, w_ref, b_ref,
                   u_ref, x0c_ref, *, nblk, transpose_u):
    first = pl.program_id(0) == 0
    last = pl.program_id(0) == nblk - 1
    w = HY_WIDTH
    x0 = _short_conv(x0_ref, x0p_ref, x0n_ref, w_ref, 0, first, last) + b_ref[:, :w]
    x1 = _short_conv(x1_ref, x1p_ref, x1n_ref, w_ref, w, first, last) + b_ref[:, w:2 * w]
    v = _short_conv(v_ref, vp_ref, vn_ref, w_ref, 2 * w, first, last) + b_ref[:, 2 * w:]
    u = v * x1
    x0c_ref[...] = x0
    u_ref[...] = u.T if transpose_u else u


def hyena_pre(z, row0, l, short_w, short_b, transpose_u):
    tb = min(HY_TB, l)
    nblk = l // tb
    per = tb // HALO
    blk0 = row0 // tb
    last_halo = z.shape[0] // HALO - 1
    w = HY_WIDTH

    def part_specs(part):
        cb = HY_COL0 + part
        return [pl.BlockSpec((tb, w), lambda i: (blk0 + i, cb)),
                pl.BlockSpec((HALO, w), lambda i: (jnp.maximum((blk0 + i) * per - 1, 0), cb)),
                pl.BlockSpec((HALO, w), lambda i: (jnp.minimum((blk0 + i + 1) * per, last_halo), cb))]

    u_shape, u_spec = ((w, l), pl.BlockSpec((w, tb), lambda i: (0, i))) if transpose_u else (
        (l, w), pl.BlockSpec((tb, w), lambda i: (i, 0)))
    return pl.pallas_call(
        functools.partial(_hy_pre_kernel, nblk=nblk, transpose_u=transpose_u),
        grid=(nblk,),
        in_specs=part_specs(0) + part_specs(1) + part_specs(2) + [
            pl.BlockSpec((HY_SHORT, HY_IN), lambda i: (0, 0)), pl.BlockSpec((1, HY_IN), lambda i: (0, 0))],
        out_specs=[u_spec, pl.BlockSpec((tb, w), lambda i: (i, 0))],
        out_shape=[jax.ShapeDtypeStruct(u_shape, F32), jax.ShapeDtypeStruct((l, w), F32)],
        compiler_params=_cparams("parallel"),
        name="hyena_pre",
    )(z, z, z, z, z, z, z, z, z, short_w, short_b.reshape(1, HY_IN))


def _hy_filter_kernel(w1t_ref, b1_ref, w2t_ref, b2_ref, w3t_ref, b3_ref, w4t_ref, fr_ref, band_ref, dl_ref,
                      f_ref, hb0_ref, *, l, fb):
    hi = lax.Precision.HIGHEST
    j = pl.program_id(0)
    second = j >= l // fb
    n = j * fb + lax.broadcasted_iota(jnp.int32, (1, fb), 1)
    pos = jnp.where(second, 2 * l - n, n).astype(F32)
    t = pos / max(l - 1, 1)
    wpos = 2.0 * math.pi * pos / l
    arg = band_ref[...] * wpos
    row = lax.broadcasted_iota(jnp.int32, (HY_EMB_PAD, fb), 0)
    feat = jnp.where(row == 0, t, jnp.where(row <= HY_BANDS, jnp.cos(arg),
                                            jnp.where(row <= 2 * HY_BANDS, -jnp.sin(arg), 0.0)))
    fr = fr_ref[...]
    h = jnp.sin(fr * (jnp.dot(w1t_ref[...], feat, precision=hi) + b1_ref[...]))
    h = jnp.sin(fr * (jnp.dot(w2t_ref[...], h, precision=hi) + b2_ref[...]))
    h = jnp.sin(fr * (jnp.dot(w3t_ref[...], h, precision=hi) + b3_ref[...]))
    window = jnp.exp(-t * dl_ref[...])
    half = pl.multiple_of(jnp.where(second, HY_WIDTH, 0), HY_WIDTH)
    h16 = h.astype(BF16)
    f = jnp.dot(w4t_ref[pl.ds(half, HY_WIDTH), :].astype(BF16), h16, preferred_element_type=F32) * window
    f_ref[...] = jnp.where(n == l, 0.0, f)

    @pl.when(j == 0)
    def _():
        hb0_ref[...] = (jnp.dot(w4t_ref[HY_WIDTH:, :].astype(BF16), h16[:, :LANES], preferred_element_type=F32)
                        * window[:, :LANES])


def hyena_filter(l, w1, b1, w2, b2, w3, b3, w4, freq):
    fb = min(1024, l)
    colv = lambda v: v.reshape(-1, 1)
    bands = np.zeros((HY_EMB_PAD, 1), np.float32)
    bands[1:1 + HY_BANDS, 0] = bands[1 + HY_BANDS:HY_EMB, 0] = np.linspace(1e-4, HY_BANDS - 1, HY_BANDS,
                                                                            dtype=np.float32)
    min_decay = math.log(HY_DECAY_TARGET) / HY_WEAK_DECAY_PCT
    max_decay = math.log(HY_DECAY_TARGET) / HY_STRONG_DECAY_PCT
    deltas = np.abs(np.linspace(min_decay, max_decay, HY_WIDTH, dtype=np.float32)).reshape(-1, 1)
    w1t = jnp.pad(w1.T, ((0, 0), (0, HY_EMB_PAD - HY_EMB)))
    full = lambda a: pl.BlockSpec(a.shape, lambda j: (0,) * a.ndim)
    args = (w1t, colv(b1), w2.T, colv(b2), w3.T, colv(b3), w4.T, colv(freq), jnp.asarray(bands), jnp.asarray(deltas))
    filt, hb0 = pl.pallas_call(
        functools.partial(_hy_filter_kernel, l=l, fb=fb),
        grid=(2 * l // fb,),
        in_specs=[full(a) for a in args],
        out_specs=[pl.BlockSpec((HY_WIDTH, fb), lambda j: (0, j)), pl.BlockSpec((HY_WIDTH, LANES), lambda j: (0, 0))],
        out_shape=[jax.ShapeDtypeStruct((HY_WIDTH, 2 * l), F32), jax.ShapeDtypeStruct((HY_WIDTH, LANES), F32)],
        compiler_params=_cparams("arbitrary"),
        name="hyena_filter",
    )(*args)
    return filt, hb0[:, :1]


def _dft_constants():
    b = FFT_B
    n = b * b
    idx = np.arange(b)
    ang = 2.0 * np.pi * np.outer(idx, idx) / b
    c, s = np.cos(ang), np.sin(ang)
    tw = 2.0 * np.pi * np.outer(idx, idx) / n
    fwd_b = np.concatenate([c, -s], axis=0)
    cs = np.concatenate([c, s], axis=1)
    inv_b = np.concatenate([c[:b // 2], -s[:b // 2]], axis=1) / n
    return (jnp.asarray(fwd_b, BF16), jnp.asarray(cs, BF16), jnp.asarray(np.cos(tw), F32),
            jnp.asarray(-np.sin(tw), F32), jnp.asarray(inv_b, BF16))


def _hy_conv_kernel(u_ref, f_ref, fwd_ref, cs_ref, tr_ref, ti_ref, inv_ref, y_ref, ur_s, ui_s, fr_s, fi_s):
    cb = u_ref.shape[0]
    b = FFT_B
    m = cb * b
    dot = functools.partial(jnp.dot, preferred_element_type=F32)
    fwd = fwd_ref[...]
    fwd_half = fwd[:, :b // 2]
    tr, ti = tr_ref[...], ti_ref[...]

    def first_stage(c, carry):
        for src, lhs, re_s, im_s in ((u_ref, fwd_half, ur_s, ui_s), (f_ref, fwd, fr_s, fi_s)):
            p = dot(lhs, src[c].astype(BF16))
            pr, pi = p[:b], p[b:]
            re_s[c] = (pr * tr - pi * ti).astype(BF16)
            im_s[c] = (pr * ti + pi * tr).astype(BF16)
        return carry

    lax.fori_loop(0, cb, first_stage, 0, unroll=True)

    cs = cs_ref[...]

    def times_cs(re, im):
        big = dot(jnp.concatenate([re, im], axis=0), cs)
        return big[:m, :b], big[:m, b:], big[m:, :b], big[m:, b:]

    def second_stage(re_s, im_s):
        rc, rs, ic, is_ = times_cs(re_s[...].reshape(m, b), im_s[...].reshape(m, b))
        return rc + is_, ic - rs

    xr, xi = second_stage(ur_s, ui_s)
    hr, hi = second_stage(fr_s, fi_s)
    zr = (xr * hr - xi * hi).astype(BF16)
    zi = (xr * hi + xi * hr).astype(BF16)
    rc, rs, ic, is_ = times_cs(zr, zi)
    gr = (rc - is_).reshape(cb, b, b)
    gi = (rs + ic).reshape(cb, b, b)
    ur_s[...] = (gr * tr + gi * ti).astype(BF16)
    ui_s[...] = (gi * tr - gr * ti).astype(BF16)
    inv = inv_ref[...]

    def last_stage(c, carry):
        y_ref[c] = dot(inv, jnp.concatenate([ur_s[c], ui_s[c]], axis=0))
        return carry

    lax.fori_loop(0, cb, last_stage, 0, unroll=True)


def hyena_long_conv(u_t, filt_t):
    c, l = u_t.shape
    b = FFT_B
    consts = _dft_constants()
    full = lambda a: pl.BlockSpec(a.shape, lambda i: (0,) * a.ndim)
    y = pl.pallas_call(
        _hy_conv_kernel,
        grid=(c // HY_CB,),
        in_specs=[pl.BlockSpec((HY_CB, b // 2, b), lambda i: (i, 0, 0)),
                  pl.BlockSpec((HY_CB, b, b), lambda i: (i, 0, 0))] + [full(a) for a in consts],
        out_specs=pl.BlockSpec((HY_CB, b // 2, b), lambda i: (i, 0, 0)),
        out_shape=jax.ShapeDtypeStruct((c, b // 2, b), F32),
        scratch_shapes=[pltpu.VMEM((HY_CB, b, b), BF16) for _ in range(4)],
        compiler_params=_cparams("parallel"),
        name="hyena_long_conv",
    )(u_t.reshape(c, b // 2, b), filt_t.reshape(c, b, b), *consts)
    return y.reshape(c, l)


def _hy_ctx_conv_kernel(u_ref, f_ref, fwdu_ref, fwdf_ref, inv_ref, y_ref):
    dot = functools.partial(jnp.dot, preferred_element_type=F32)
    n = f_ref.shape[0]
    x = dot(fwdu_ref[...], u_ref[...].astype(BF16))
    h = dot(fwdf_ref[...], f_ref[...].astype(BF16))
    xr, xi, hr, hi = x[:n], x[n:], h[:n], h[n:]
    z = jnp.concatenate([xr * hr - xi * hi, xr * hi + xi * hr], axis=0).astype(BF16)
    y_ref[...] = dot(inv_ref[...], z)


def hyena_ctx_conv(u, filt):
    l, c = u.shape
    n = 2 * l
    idx = np.arange(n)
    ang = 2.0 * np.pi * np.outer(idx, idx) / n
    cm, sm = np.cos(ang), np.sin(ang)
    fwdf = np.concatenate([cm, -sm], axis=0)
    inv = np.concatenate([cm[:l], -sm[:l]], axis=1) / n
    consts = (jnp.asarray(fwdf[:, :l], BF16), jnp.asarray(fwdf, BF16), jnp.asarray(inv, BF16))
    tc = 256
    full = lambda a: pl.BlockSpec(a.shape, lambda i: (0,) * a.ndim)
    return pl.pallas_call(
        _hy_ctx_conv_kernel,
        grid=(c // tc,),
        in_specs=[pl.BlockSpec((l, tc), lambda i: (0, i)), pl.BlockSpec((n, tc), lambda i: (0, i))]
        + [full(a) for a in consts],
        out_specs=pl.BlockSpec((l, tc), lambda i: (0, i)),
        out_shape=jax.ShapeDtypeStruct((l, c), F32),
        compiler_params=_cparams("parallel"),
        name="hyena_ctx_conv",
    )(u, filt, *consts)


def _hy_post_kernel(y_ref, u_ref, x0_ref, b_ref, o_ref, *, transposed):
    w = y_ref[...] + u_ref[...] * b_ref[...]
    if transposed:
        w = w.T
    o_ref[...] = (w * x0_ref[...]).astype(o_ref.dtype)


def hyena_post(y, u, x0, bias, transposed):
    l, w = x0.shape
    tb = min(HY_TB, l)
    tm = pl.BlockSpec((tb, w), lambda i: (i, 0))
    yu = pl.BlockSpec((w, tb), lambda i: (0, i)) if transposed else tm
    return pl.pallas_call(
        functools.partial(_hy_post_kernel, transposed=transposed),
        grid=(l // tb,),
        in_specs=[yu, yu, tm, pl.BlockSpec(bias.shape, lambda i: (0, 0))],
        out_specs=tm,
        out_shape=jax.ShapeDtypeStruct((l, w), BF16),
        compiler_params=_cparams("parallel"),
        name="hyena_post",
    )(y, u, x0, bias)


def hyena(z, row0, l, short_w, short_b, filt, bias):
    filt_t, hb0 = hyena_filter(l, *filt)
    if l == FFT_B * FFT_B // 2:
        u_t, x0 = hyena_pre(z, row0, l, short_w, short_b, True)
        y_t = hyena_long_conv(u_t, filt_t)
        return hyena_post(y_t, u_t, x0, bias.reshape(-1, 1) + hb0, True)
    u, x0 = hyena_pre(z, row0, l, short_w, short_b, False)
    y = hyena_ctx_conv(u, filt_t.T)
    return hyena_post(y, u, x0, (bias.reshape(-1, 1) + hb0).T, False)


def kernel(x, c, ctx, c_ctx, w_mod, b_mod, norm1_w, norm2_w, ffn_w_gate, ffn_w_up, ffn_w_down, even_w_in, even_w_out, dn_conv_w, dn_a_log, dn_dt_bias, dn_onorm_w, hy_short_w, hy_short_b, hy_f_w1, hy_f_b1, hy_f_w2, hy_f_b2, hy_f_w3, hy_f_b3, hy_f_w4, hy_f_freq, hy_bias, na_w_qkv, na_rpb, na_w_out, final_norm_w):
    d = D_MODEL
    x_src = (x[0], ctx[0], 0)
    mod = modulation(jnp.concatenate([c, c_ctx[None]], axis=0), w_mod, b_mod)

    for layer in range(DEPTH):
        need_ctx = layer < DEPTH - 1
        rows = ALL_ROWS if need_ctx else SEQ
        sh1, sc1, g1, sh2, sc2, g2 = (mod[layer, :, None, j * d:(j + 1) * d] for j in range(N_MOD))
        i = layer // 2
        h = norm_mod(x_src, norm1_w[layer][None], sc1, sh1)
        if layer % 2 == 0:
            w_in = even_w_in[i]
            w_in = jnp.concatenate([w_in[:, :4 * DN_WIDTH], w_in[:, DN_IN:], w_in[:, 4 * DN_WIDTH:DN_IN],
                                    jnp.zeros((d, EVEN_PAD - EVEN_IN), F32)], axis=1).astype(BF16)
            z = projection(h, w_in, EVEN_TN, BF16)
            dn_l, dn_c = gated_deltanet(z, dn_conv_w[i], dn_a_log[i], dn_dt_bias[i], dn_onorm_w[i])
            filt = (hy_f_w1[i], hy_f_b1[i], hy_f_w2[i], hy_f_b2[i], hy_f_w3[i], hy_f_b3[i], hy_f_w4[i],
                    hy_f_freq[i])
            y_lat = (dn_l, hyena(z, 0, SEQ, hy_short_w[i], hy_short_b[i], filt, hy_bias[i]))
            y_ctx = (dn_c, hyena(z, SEQ, CTX_LEN, hy_short_w[i], hy_short_b[i], filt, hy_bias[i])) if need_ctx else y_lat
            w_out = even_w_out
        else:
            z = projection(h, na_w_qkv, QKV_TN, BF16, layer=i, scaled_cols=D_MODEL, scale=NA_DIM ** -0.5)
            ya = na_attention(z, na_bias_tiles(na_rpb[i]))
            y_lat = (ya, ya)
            if need_ctx:
                yc = ctx_attention(z)
                y_ctx = (yc, yc)
            else:
                y_ctx = y_lat
            w_out = na_w_out
        xa, h = proj_residual(y_lat, y_ctx, w_out, i, x_src, rows, g1, norm2_w[layer][None], sc2, sh2)
        xa = ffn_block(xa, h, rows, g2, ffn_w_gate, ffn_w_up, ffn_w_down, layer)
        x_src = (xa, xa, SEQ)
    return final_norm(xa, SEQ, final_norm_w[None])[None]
```
